```python
import jax, jax.numpy as jnp
from jax import lax
import numpy as np

D_MODEL = 2048
BATCH = 8
SEQ = 2048
DEPTH = 4

MEM_LEN = 256
HEAD_DIM = 128
RET_HEADS = 4
RET_WIDTH = RET_HEADS * HEAD_DIM
RET_CHUNK = 128
MLA_HEADS = 4
MLA_NOPE = 128
MLA_ROPE = 64
MLA_V = 128
MLA_Q_RANK = 384
MLA_KV_RANK = 128
MLA_WIDTH = MLA_HEADS * MLA_V
ATTN_BLOCK = 128
MLSTM_HEADS = 4
MLSTM_WIDTH = MLSTM_HEADS * HEAD_DIM
MLSTM_CHUNK = 128
CONV_WIDTH = 4
MEM_HEADS = 4
MEM_WIDTH = MEM_HEADS * HEAD_DIM
MIX_WIDTH = RET_WIDTH + MLA_WIDTH + MLSTM_WIDTH + MEM_WIDTH

ROPE_THETA = 10000.0
LN_EPS = 1e-5
RMS_EPS = 1e-6
DEEPNORM_ALPHA = (2 * DEPTH) ** 0.25
DEEPNORM_BETA = (8 * DEPTH) ** -0.25

IN_SPLITS = (RET_WIDTH, RET_WIDTH, RET_WIDTH,
             MLA_Q_RANK, MLA_KV_RANK, MLA_ROPE,
             MLSTM_WIDTH, MLSTM_WIDTH, MLSTM_WIDTH,
             MLSTM_HEADS, MLSTM_HEADS,
             MEM_WIDTH,
             MIX_WIDTH)
IN_WIDTH = sum(IN_SPLITS)

kernel_name = "hymba_ret_mla_mlstm_deepnorm"


def _split_cols(h, sizes):
    idx = np.cumsum(sizes)[:-1].tolist()
    return jnp.split(h, idx, axis=-1)


def _heads(t, n):
    b, s, _ = t.shape
    return t.reshape(b, s, n, -1).transpose(0, 2, 1, 3)


def _merge(t):
    b, n, s, d = t.shape
    return t.transpose(0, 2, 1, 3).reshape(b, s, n * d)


def _to_chunks(t, size):
    b, h, s = t.shape[:3]
    t = t.reshape(b, h, s // size, size, *t.shape[3:])
    return jnp.moveaxis(t, 2, 0)


def _from_chunks(t):
    n, b, h, size = t.shape[:4]
    return jnp.moveaxis(t, 0, 2).reshape(b, h, n * size, *t.shape[4:])


def _layernorm(x, g, b):
    xf = x.astype(jnp.float32)
    mu = xf.mean(-1, keepdims=True)
    var = jnp.mean(jnp.square(xf - mu), -1, keepdims=True)
    y = (xf - mu) * lax.rsqrt(var + LN_EPS) * g.astype(jnp.float32) + b.astype(jnp.float32)
    return y.astype(x.dtype)


def _rmsnorm(x, g):
    xf = x.astype(jnp.float32)
    y = xf * lax.rsqrt(jnp.mean(jnp.square(xf), -1, keepdims=True) + RMS_EPS) * g.astype(jnp.float32)
    return y.astype(x.dtype)


def _head_norm(h, g):
    n, d = h.shape[1], h.shape[3]
    mu = h.mean(-1, keepdims=True)
    var = jnp.mean(jnp.square(h - mu), -1, keepdims=True)
    return (h - mu) * lax.rsqrt(var + LN_EPS) * g.astype(jnp.float32).reshape(n, 1, d)


def _rope(t, positions):
    half = t.shape[-1] // 2
    freqs = ROPE_THETA ** (-jnp.arange(half, dtype=jnp.float32) / half)
    ang = positions.astype(jnp.float32)[:, None, :, None] * freqs
    cos, sin = jnp.cos(ang), jnp.sin(ang)
    tf = t.astype(jnp.float32)
    t1, t2 = tf[..., :half], tf[..., half:]
    return jnp.concatenate([t1 * cos - t2 * sin, t2 * cos + t1 * sin], -1).astype(t.dtype)


def _retention(q, k, v, positions):
    b, n, s, d = q.shape
    q = _rope(q, positions)
    k = _rope(k, positions) * d ** -0.5
    log_g = jnp.log1p(-jnp.exp2(-5.0 - jnp.arange(n, dtype=jnp.float32)))
    size = RET_CHUNK
    idx = jnp.arange(size, dtype=jnp.float32)
    rel = idx[:, None] - idx[None, :]
    decay_in = jnp.where(rel >= 0, jnp.exp(log_g[:, None, None] * jnp.maximum(rel, 0.0)), 0.0)
    decay_q = jnp.exp(log_g[:, None] * (idx + 1.0))[..., None]
    decay_k = jnp.exp(log_g[:, None] * (size - 1.0 - idx))[..., None]
    decay_chunk = jnp.exp(log_g * size)[:, None, None]

    def step(state, inp):
        qc, kc, vc = inp
        sc = jnp.einsum('bhqd,bhkd->bhqk', qc, kc) * decay_in
        out = jnp.einsum('bhqk,bhkd->bhqd', sc, vc) + jnp.einsum('bhqd,bhde->bhqe', qc, state) * decay_q
        state = decay_chunk * state + jnp.einsum('bhkd,bhke->bhde', kc * decay_k, vc)
        return state, out

    state0 = jnp.zeros((b, n, d, v.shape[-1]), jnp.float32)
    _, out = lax.scan(step, state0, (_to_chunks(q, size), _to_chunks(k, size), _to_chunks(v, size)))
    return _from_chunks(out)


def _causal_attention(q, k, v):
    b, n, s, d = q.shape
    nb = s // ATTN_BLOCK
    qb = jnp.moveaxis(q.reshape(b, n, nb, ATTN_BLOCK, d), 2, 0)
    kpos = jnp.arange(s)

    def one_block(args):
        qblk, bi = args
        sc = jnp.einsum('bhqd,bhkd->bhqk', qblk, k).astype(jnp.float32)
        qpos = bi * ATTN_BLOCK + jnp.arange(ATTN_BLOCK)
        sc = jnp.where(kpos[None, :] <= qpos[:, None], sc, -jnp.inf)
        p = jax.nn.softmax(sc, axis=-1).astype(v.dtype)
        return jnp.einsum('bhqk,bhkd->bhqd', p, v)

    out = lax.map(one_block, (qb, jnp.arange(nb)))
    return _from_chunks(out)


def _mla(c_q, c_kv, k_pe, q_norm_g, w_uq, kv_norm_g, w_ukv, positions):
    b, s, _ = c_q.shape
    q = (_rmsnorm(c_q, q_norm_g) @ w_uq).reshape(b, s, MLA_HEADS, MLA_NOPE + MLA_ROPE).transpose(0, 2, 1, 3)
    q = jnp.concatenate([q[..., :MLA_NOPE], _rope(q[..., MLA_NOPE:], positions)], -1)
    kv = (_rmsnorm(c_kv, kv_norm_g) @ w_ukv).reshape(b, s, MLA_HEADS, MLA_NOPE + MLA_V).transpose(0, 2, 1, 3)
    k_rot = _rope(k_pe[:, None], positions)
    k = jnp.concatenate([kv[..., :MLA_NOPE], jnp.broadcast_to(k_rot, (b, MLA_HEADS, s, MLA_ROPE))], -1)
    v = kv[..., MLA_NOPE:]
    return _causal_attention(q * (MLA_NOPE + MLA_ROPE) ** -0.5, k, v)


def _causal_conv(t, w, bias):
    kw, c = w.shape
    tp = jnp.pad(t, ((0, 0), (kw - 1, 0), (0, 0)))
    y = lax.conv_general_dilated(tp, w[:, None, :], window_strides=(1,), padding='VALID',
                                 dimension_numbers=('NWC', 'WIO', 'NWC'), feature_group_count=c)
    return y + bias


def _mlstm(q, k, v, i_pre, f_pre):
    b, n, s, d = q.shape
    size = MLSTM_CHUNK
    k = k * d ** -0.5
    log_f = jax.nn.log_sigmoid(f_pre)
    causal = jnp.tril(jnp.ones((size, size), bool))

    def step(carry, inp):
        c_st, n_st, m_st = carry
        qc, kc, vc, ic, fc = inp
        cum = jnp.cumsum(fc, -1)
        log_d = jnp.where(causal, cum[..., :, None] - cum[..., None, :] + ic[..., None, :], -jnp.inf)
        inter = cum + m_st[..., None]
        m_t = jnp.maximum(inter, log_d.max(-1))
        sc = jnp.einsum('bhqd,bhkd->bhqk', qc, kc) * jnp.exp(log_d - m_t[..., None])
        w_inter = jnp.exp(inter - m_t)
        num = jnp.einsum('bhqk,bhkd->bhqd', sc, vc) + w_inter[..., None] * jnp.einsum('bhqd,bhde->bhqe', qc, c_st)
        den = sc.sum(-1) + w_inter * jnp.einsum('bhqd,bhd->bhq', qc, n_st)
        h = num / jnp.maximum(jnp.abs(den), jnp.exp(-m_t))[..., None]
        cum_last = cum[..., -1]
        log_wk = cum_last[..., None] - cum + ic
        m_new = jnp.maximum(cum_last + m_st, log_wk.max(-1))
        wk = jnp.exp(log_wk - m_new[..., None])
        decay = jnp.exp(cum_last + m_st - m_new)
        c_st = decay[..., None, None] * c_st + jnp.einsum('bhkd,bhke->bhde', kc * wk[..., None], vc)
        n_st = decay[..., None] * n_st + jnp.einsum('bhkd,bhk->bhd', kc, wk)
        return (c_st, n_st, m_new), h

    carry0 = (jnp.zeros((b, n, d, v.shape[-1]), jnp.float32), jnp.zeros((b, n, d), jnp.float32),
              jnp.zeros((b, n), jnp.float32))
    _, h = lax.scan(step, carry0, (_to_chunks(q, size), _to_chunks(k, size), _to_chunks(v, size),
                                   _to_chunks(i_pre, size), _to_chunks(log_f, size)))
    return _from_chunks(h)


def _mem_attention(q, mem, w_mem_kv):
    k, v = jnp.split(mem @ w_mem_kv, 2, axis=-1)
    qh = _heads(q, MEM_HEADS) * HEAD_DIM ** -0.5
    kh, vh = _heads(k, MEM_HEADS), _heads(v, MEM_HEADS)
    sc = jnp.einsum('bhqd,bhkd->bhqk', qh, kh).astype(jnp.float32)
    p = jax.nn.softmax(sc, axis=-1).astype(vh.dtype)
    return jnp.einsum('bhqk,bhkd->bhqd', p, vh)


def _layer(x, mem, positions, w_in, ret_norm_g, mla_q_norm_g, mla_w_uq, mla_kv_norm_g, mla_w_ukv,
           ml_conv_w, ml_conv_b, ml_w_q, ml_w_k, ml_i_bias, ml_f_bias, ml_skip, ml_norm_g,
           w_mem_kv, w_out, ln_g, ln_b):
    f32 = jnp.float32
    b, s, _ = x.shape
    (r_q, r_k, r_v, a_cq, a_ckv, a_kpe, l_x, l_v, l_o, l_i, l_f, c_q, z) = _split_cols(x @ w_in, IN_SPLITS)

    ret = _retention(_heads(r_q, RET_HEADS).astype(f32), _heads(r_k, RET_HEADS).astype(f32),
                     _heads(r_v, RET_HEADS).astype(f32), positions)
    ret = _merge(_head_norm(ret, ret_norm_g))

    mla = _merge(_mla(a_cq, a_ckv, a_kpe, mla_q_norm_g, mla_w_uq, mla_kv_norm_g, mla_w_ukv,
                      positions)).astype(f32)

    xc = jax.nn.silu(_causal_conv(l_x, ml_conv_w, ml_conv_b))
    xc_h = xc.reshape(b, s, MLSTM_HEADS, HEAD_DIM)
    lq = jnp.einsum('bshd,hde->bhse', xc_h, ml_w_q).astype(f32)
    lk = jnp.einsum('bshd,hde->bhse', xc_h, ml_w_k).astype(f32)
    lv = _heads(l_v, MLSTM_HEADS).astype(f32)
    i_pre = jnp.swapaxes(l_i + ml_i_bias, 1, 2).astype(f32)
    f_pre = jnp.swapaxes(l_f + ml_f_bias, 1, 2).astype(f32)
    cell = _mlstm(lq, lk, lv, i_pre, f_pre) * jax.nn.sigmoid(_heads(l_o, MLSTM_HEADS).astype(f32))
    mls = _merge(_head_norm(cell, ml_norm_g)) + ml_skip.astype(f32) * xc.astype(f32)

    memo = _merge(_mem_attention(c_q, mem, w_mem_kv)).astype(f32)

    y = jnp.concatenate([ret, mla, mls, memo], axis=-1) * jax.nn.silu(z.astype(f32))
    y = y.astype(x.dtype) @ w_out
    return _layernorm(DEEPNORM_ALPHA * x + y, ln_g, ln_b)


def setup_inputs(seed: int = 0) -> dict:
    key = jax.random.key(seed)
    ks = jax.random.split(key, 24)
    f32 = jnp.float32

    def nrm(k, shape, std):
        return jax.random.normal(k, shape, f32) * std

    def gain(k, shape):
        return 1.0 + 0.02 * jax.random.normal(k, shape, f32)

    L = DEPTH
    return {
        "x": nrm(ks[0], (BATCH, SEQ, D_MODEL), 1.0),
        "mem": nrm(ks[1], (BATCH, MEM_LEN, D_MODEL), 1.0),
        "positions": jnp.tile(jnp.arange(SEQ, dtype=jnp.int32)[None, :], (BATCH, 1)),
        "w_in": nrm(ks[2], (L, D_MODEL, IN_WIDTH), D_MODEL ** -0.5),
        "ret_norm_g": gain(ks[3], (L, RET_WIDTH)),
        "mla_q_norm_g": gain(ks[4], (L, MLA_Q_RANK)),
        "mla_w_uq": nrm(ks[5], (L, MLA_Q_RANK, MLA_HEADS * (MLA_NOPE + MLA_ROPE)), MLA_Q_RANK ** -0.5),
        "mla_kv_norm_g": gain(ks[6], (L, MLA_KV_RANK)),
        "mla_w_ukv": nrm(ks[7], (L, MLA_KV_RANK, MLA_HEADS * (MLA_NOPE + MLA_V)), MLA_KV_RANK ** -0.5),
        "ml_conv_w": nrm(ks[8], (L, CONV_WIDTH, MLSTM_WIDTH), CONV_WIDTH ** -0.5),
        "ml_conv_b": nrm(ks[9], (L, MLSTM_WIDTH), 0.02),
        "ml_w_q": nrm(ks[10], (L, MLSTM_HEADS, HEAD_DIM, HEAD_DIM), HEAD_DIM ** -0.5),
        "ml_w_k": nrm(ks[11], (L, MLSTM_HEADS, HEAD_DIM, HEAD_DIM), HEAD_DIM ** -0.5),
        "ml_i_bias": nrm(ks[12], (L, MLSTM_HEADS), 0.1),
        "ml_f_bias": jnp.linspace(3.0, 6.0, MLSTM_HEADS, dtype=f32)[None, :] + nrm(ks[13], (L, MLSTM_HEADS), 0.1),
        "ml_skip": gain(ks[14], (L, MLSTM_WIDTH)),
        "ml_norm_g": gain(ks[15], (L, MLSTM_WIDTH)),
        "w_mem_kv": nrm(ks[16], (L, D_MODEL, 2 * MEM_WIDTH), D_MODEL ** -0.5),
        "w_out": nrm(ks[17], (L, MIX_WIDTH, D_MODEL), DEEPNORM_BETA * MIX_WIDTH ** -0.5),
        "ln_g": gain(ks[18], (L, D_MODEL)),
        "ln_b": nrm(ks[19], (L, D_MODEL), 0.02),
    }


def reference(x, mem, positions, w_in, ret_norm_g, mla_q_norm_g, mla_w_uq, mla_kv_norm_g, mla_w_ukv,
              ml_conv_w, ml_conv_b, ml_w_q, ml_w_k, ml_i_bias, ml_f_bias, ml_skip, ml_norm_g,
              w_mem_kv, w_out, ln_g, ln_b):
    for l in range(DEPTH):
        x = _layer(x, mem, positions, w_in[l], ret_norm_g[l], mla_q_norm_g[l], mla_w_uq[l],
                   mla_kv_norm_g[l], mla_w_ukv[l], ml_conv_w[l], ml_conv_b[l], ml_w_q[l], ml_w_k[l],
                   ml_i_bias[l], ml_f_bias[l], ml_skip[l], ml_norm_g[l], w_mem_kv[l], w_out[l],
                   ln_g[l], ln_b[l])
    return x
```

```python
import functools

import numpy as np
import jax
import jax.numpy as jnp
from jax import lax
from jax.experimental import pallas as pl
from jax.experimental.pallas import tpu as pltpu

F32 = jnp.float32
BF16 = jnp.bfloat16

D_MODEL = 2048
BATCH = 8
SEQ = 2048
DEPTH = 4
MEM_LEN = 256
HEAD_DIM = 128
HEADS = 4
GROUP_WIDTH = HEADS * HEAD_DIM
CHUNK = 128
MLA_NOPE = 128
MLA_ROPE = 64
MLA_Q_RANK = 384
MLA_KV_RANK = 128
CONV_WIDTH = 4
MIX_WIDTH = 4 * GROUP_WIDTH
ROPE_THETA = 10000.0
LN_EPS = 1e-5
RMS_EPS = 1e-6
DEEPNORM_ALPHA = (2 * DEPTH) ** 0.25
IN_SPLITS = (512, 512, 512, MLA_Q_RANK, MLA_KV_RANK, MLA_ROPE, 512, 512, 512, HEADS, HEADS, 512, MIX_WIDTH)

LANE = 128
ROWS = BATCH * SEQ

BLK_RQ, BLK_RK, BLK_RV = 0, 4, 8
BLK_CQ, BLK_CKV, BLK_KPE = 12, 15, 16
BLK_LX, BLK_LV, BLK_LO = 17, 21, 25
BLK_MQ = 29
BLK_Z = 33
NBLK = 50
GATE_I_LANE, GATE_F_LANE = 32, 36

TM_IN, TN_IN = 1024, 1280
TM_OUT = 512
TS_PREP = 512
TQ_ATT = 512
TQ_MEM = 1024
VMEM_LIMIT = 56 * 1024 * 1024


def _cparams(sem):
    return pltpu.CompilerParams(dimension_semantics=sem, vmem_limit_bytes=VMEM_LIMIT)


def _silu(z):
    return z * jax.nn.sigmoid(z)


def _log_sigmoid(x):
    return jnp.minimum(x, 0.0) - jnp.log1p(jnp.exp(-jnp.abs(x)))


def _tables_kernel(pos_ref, c_ref, cr_ref, sr_ref, cm_ref, sm_ref):
    pos = pos_ref[0].astype(F32)
    ang_r = pos * c_ref[0:1, :]
    cr_ref[0] = jnp.cos(ang_r)
    sr_ref[0] = jnp.sin(ang_r) * c_ref[1:2, :]
    ang_m = pos * c_ref[2:3, :]
    cm_ref[0] = jnp.cos(ang_m) * c_ref[3:4, :]
    sm_ref[0] = jnp.sin(ang_m) * c_ref[4:5, :]


def _rope_tables(positions):
    half_r = HEAD_DIM // 2
    fr = ROPE_THETA ** (-jnp.arange(half_r, dtype=F32) / half_r)
    half_m = MLA_ROPE // 2
    fm = ROPE_THETA ** (-jnp.arange(half_m, dtype=F32) / half_m)
    z32 = jnp.zeros((half_m,), F32)
    o32 = jnp.ones((half_m,), F32)
    rows = [
        jnp.concatenate([fr, fr]),
        jnp.concatenate([-jnp.ones((half_r,), F32), jnp.ones((half_r,), F32)]),
        jnp.concatenate([fm, z32, fm, z32]),
        jnp.concatenate([o32, z32, o32, z32]),
        jnp.concatenate([-o32, z32, o32, z32]),
    ]
    consts = jnp.concatenate([jnp.stack(rows), jnp.zeros((3, LANE), F32)], axis=0)
    ts = 512
    tab = jax.ShapeDtypeStruct((BATCH, SEQ, LANE), F32)
    spec = pl.BlockSpec((1, ts, LANE), lambda b, i: (b, i, 0))
    return pl.pallas_call(
        _tables_kernel,
        out_shape=(tab, tab, tab, tab),
        grid=(BATCH, SEQ // ts),
        in_specs=[pl.BlockSpec((1, ts, 1), lambda b, i: (b, i, 0)),
                  pl.BlockSpec((8, LANE), lambda b, i: (0, 0))],
        out_specs=(spec, spec, spec, spec),
        compiler_params=_cparams(("parallel", "parallel")),
        name="rope_tables",
    )(positions.reshape(BATCH, SEQ, 1), consts)


def _matmul_kernel(a_ref, w_ref, o_ref):
    o_ref[...] = jnp.dot(a_ref[...].astype(BF16), w_ref[...], preferred_element_type=F32).astype(o_ref.dtype)


def _mem_kv(mem, w_mem_kv):
    w = jnp.transpose(w_mem_kv, (1, 0, 2)).reshape(D_MODEL, DEPTH * 2 * GROUP_WIDTH).astype(BF16)
    a = mem.reshape(BATCH * MEM_LEN, D_MODEL)
    tm, tn = 512, 1024
    return pl.pallas_call(
        _matmul_kernel,
        out_shape=jax.ShapeDtypeStruct((BATCH * MEM_LEN, DEPTH * 2 * GROUP_WIDTH), BF16),
        grid=(BATCH * MEM_LEN // tm, DEPTH * 2 * GROUP_WIDTH // tn),
        in_specs=[pl.BlockSpec((tm, D_MODEL), lambda i, j: (i, 0)),
                  pl.BlockSpec((D_MODEL, tn), lambda i, j: (0, j))],
        out_specs=pl.BlockSpec((tm, tn), lambda i, j: (i, j)),
        compiler_params=_cparams(("parallel", "parallel")),
        name="mem_kv",
    )(a, w)


def _inproj_kernel(x_ref, w_ref, o_ref, xb_ref):
    @pl.when(pl.program_id(1) == 0)
    def _():
        xb_ref[...] = x_ref[...].astype(BF16)

    xb = xb_ref[...]
    for k2 in range(TN_IN // 256):
        r = jnp.dot(xb, w_ref[:, k2 * 256:(k2 + 1) * 256], preferred_element_type=F32)
        o_ref[2 * k2] = r[:, :LANE]
        o_ref[2 * k2 + 1] = r[:, LANE:]


def _inproj(x2d, w_p):
    nb = TN_IN // LANE
    return pl.pallas_call(
        _inproj_kernel,
        out_shape=jax.ShapeDtypeStruct((NBLK, ROWS, LANE), F32),
        grid=(ROWS // TM_IN, NBLK * LANE // TN_IN),
        in_specs=[pl.BlockSpec((TM_IN, D_MODEL), lambda i, j: (i, 0)),
                  pl.BlockSpec((D_MODEL, TN_IN), lambda i, j: (0, j))],
        out_specs=pl.BlockSpec((nb, TM_IN, LANE), lambda i, j: (j, i, 0)),
        scratch_shapes=[pltpu.VMEM((TM_IN, D_MODEL), BF16)],
        compiler_params=_cparams(("parallel", "arbitrary")),
        name="inproj",
    )(x2d, w_p)


def _prep_w_in(w_in):
    idx = np.cumsum(IN_SPLITS)[:-1].tolist()
    (r_q, r_k, r_v, a_cq, a_ckv, a_kpe, l_x, l_v, l_o, l_i, l_f, c_q, z) = jnp.split(w_in, idx, axis=-1)
    half = MLA_ROPE // 2
    lead = w_in.shape[:-1]
    blk = jnp.concatenate([
        a_kpe[..., :half], l_i, l_f, jnp.zeros(lead + (64 - half - 2 * HEADS,), w_in.dtype),
        a_kpe[..., half:], jnp.zeros(lead + (64 - half,), w_in.dtype)], axis=-1)
    pad = jnp.zeros(lead + (LANE,), w_in.dtype)
    return jnp.concatenate([r_q, r_k, r_v, a_cq, a_ckv, blk, l_x, l_v, l_o, c_q, z, pad], axis=-1).astype(BF16)


def _ret_kernel(q_ref, k_ref, v_ref, z_ref, cos_ref, sin_ref, din_ref, dq_ref, dk_ref, dc_ref, g_ref,
                o_ref, st_ref):
    st_ref[...] = jnp.zeros_like(st_ref)
    din = din_ref[0]
    dq = dq_ref[0]
    dk = dk_ref[0]
    dc = dc_ref[0]
    g = g_ref[0]

    def body(c, carry):
        sl = pl.ds(pl.multiple_of(c * CHUNK, CHUNK), CHUNK)
        cs = cos_ref[0, sl, :]
        sn = sin_ref[0, sl, :]
        q = q_ref[0, sl, :]
        q = q * cs + pltpu.roll(q, HEAD_DIM // 2, 1) * sn
        k = k_ref[0, sl, :]
        k = (k * cs + pltpu.roll(k, HEAD_DIM // 2, 1) * sn) * HEAD_DIM ** -0.5
        qb = q.astype(BF16)
        kb = k.astype(BF16)
        vb = v_ref[0, sl, :].astype(BF16)
        sc = lax.dot_general(qb, kb, (((1,), (1,)), ((), ())), preferred_element_type=F32) * din
        st = st_ref[...]
        out = (jnp.dot(sc.astype(BF16), vb, preferred_element_type=F32)
               + jnp.dot(qb, st.astype(BF16), preferred_element_type=F32) * dq)
        kdt = (k * dk).T.astype(BF16)
        st_ref[...] = dc * st + jnp.dot(kdt, vb, preferred_element_type=F32)
        mu = jnp.mean(out, axis=-1, keepdims=True)
        var = jnp.mean(jnp.square(out - mu), axis=-1, keepdims=True)
        hn = (out - mu) * lax.rsqrt(var + LN_EPS) * g
        o_ref[0, sl, :] = (hn * _silu(z_ref[0, sl, :])).astype(BF16)
        return carry

    lax.fori_loop(0, SEQ // CHUNK, body, 0)


def _ret_consts():
    n, size = HEADS, CHUNK
    log_g = jnp.log1p(-jnp.exp2(-5.0 - jnp.arange(n, dtype=F32)))
    idx = jnp.arange(size, dtype=F32)
    rel = idx[:, None] - idx[None, :]
    decay_in = jnp.where(rel >= 0, jnp.exp(log_g[:, None, None] * jnp.maximum(rel, 0.0)), 0.0)
    decay_q = jnp.exp(log_g[:, None] * (idx + 1.0))[..., None]
    decay_k = jnp.exp(log_g[:, None] * (size - 1.0 - idx))[..., None]
    decay_chunk = jnp.exp(log_g * size)[:, None, None]
    full = (n, size, HEAD_DIM)
    return (decay_in, jnp.broadcast_to(decay_q, full), jnp.broadcast_to(decay_k, full),
            jnp.broadcast_to(decay_chunk, full))


def _head_spec(base):
    return pl.BlockSpec((1, SEQ, LANE), lambda b, h: (base + h, b, 0))


def _per_head(shape):
    nd = len(shape)
    return pl.BlockSpec((1,) + tuple(shape[1:]), lambda b, h: (h,) + (0,) * (nd - 1))


def _retention(hb, cos_r, sin_r, consts, norm_g):
    din, dq, dk, dc = consts
    g = norm_g.reshape(HEADS, 1, HEAD_DIM)
    tab = pl.BlockSpec((1, SEQ, LANE), lambda b, h: (b, 0, 0))
    return pl.pallas_call(
        _ret_kernel,
        out_shape=jax.ShapeDtypeStruct((HEADS, ROWS, LANE), BF16),
        grid=(BATCH, HEADS),
        in_specs=[_head_spec(BLK_RQ), _head_spec(BLK_RK), _head_spec(BLK_RV), _head_spec(BLK_Z),
                  tab, tab, _per_head(din.shape), _per_head(dq.shape), _per_head(dk.shape),
                  _per_head(dc.shape), _per_head(g.shape)],
        out_specs=pl.BlockSpec((1, SEQ, LANE), lambda b, h: (h, b, 0)),
        scratch_shapes=[pltpu.VMEM((HEAD_DIM, HEAD_DIM), F32)],
        compiler_params=_cparams(("parallel", "parallel")),
        name="retention",
    )(hb, hb, hb, hb, cos_r, sin_r, din, dq, dk, dc, g)


CONV_PAD = 8


def _mlstm_kernel(lx_ref, lv_ref, lo_ref, gt_ref, z_ref, cw_ref, cb_ref, wq_ref, wk_ref, gb_ref,
                  si_ref, sf_ref, skip_ref, g_ref, o_ref,
                  xp_ref, xc_ref, q_ref, k_ref, c_ref, n_ref, m_ref):
    xp_ref[0:CONV_PAD, :] = jnp.zeros((CONV_PAD, HEAD_DIM), F32)
    xp_ref[CONV_PAD:, :] = lx_ref[0]
    piece = 512
    for p in range(SEQ // piece):
        acc = jnp.zeros((piece, HEAD_DIM), F32) + cb_ref[0]
        for j in range(CONV_WIDTH):
            off = CONV_PAD - (CONV_WIDTH - 1) + j + p * piece
            acc = acc + xp_ref[off:off + piece, :] * cw_ref[0, j:j + 1, :]
        xc = _silu(acc)
        xc_ref[p * piece:(p + 1) * piece, :] = xc
        xcb = xc.astype(BF16)
        q_ref[p * piece:(p + 1) * piece, :] = jnp.dot(xcb, wq_ref[0], preferred_element_type=F32)
        k_ref[p * piece:(p + 1) * piece, :] = (
            jnp.dot(xcb, wk_ref[0], preferred_element_type=F32) * HEAD_DIM ** -0.5)

    c_ref[...] = jnp.zeros_like(c_ref)
    n_ref[...] = jnp.zeros_like(n_ref)
    m_ref[...] = jnp.zeros_like(m_ref)

    row = lax.broadcasted_iota(jnp.int32, (CHUNK, CHUNK), 0)
    col = lax.broadcasted_iota(jnp.int32, (CHUNK, CHUNK), 1)
    causal = col <= row
    upper = row <= col
    eye = col == row
    sel_i = si_ref[0]
    sel_f = sf_ref[0]
    gbias = gb_ref[0]
    skip = skip_ref[0]
    g = g_ref[0]

    def to_row(colv):
        return jnp.sum(jnp.where(eye, colv, 0.0), axis=0, keepdims=True)

    def body(c, carry):
        sl = pl.ds(pl.multiple_of(c * CHUNK, CHUNK), CHUNK)
        gates = gt_ref[0, sl, :] + gbias
        i_col = jnp.sum(gates * sel_i, axis=-1, keepdims=True)
        f_col = _log_sigmoid(jnp.sum(gates * sel_f, axis=-1, keepdims=True))
        i_row = to_row(i_col)
        f_row = to_row(f_col)
        cum_col = jnp.sum(jnp.where(causal, f_row, 0.0), axis=-1, keepdims=True)
        cum_row = jnp.sum(jnp.where(upper, f_col, 0.0), axis=0, keepdims=True)
        cum_last = jnp.sum(f_row, axis=-1, keepdims=True)
        m_st = m_ref[...]

        log_d = jnp.where(causal, cum_col - cum_row + i_row, -jnp.inf)
        inter = cum_col + m_st
        m_t = jnp.maximum(inter, jnp.max(log_d, axis=-1, keepdims=True))
        q = q_ref[sl, :]
        k = k_ref[sl, :]
        qb = q.astype(BF16)
        kb = k.astype(BF16)
        vb = lv_ref[0, sl, :].astype(BF16)
        sc = lax.dot_general(qb, kb, (((1,), (1,)), ((), ())), preferred_element_type=F32) * jnp.exp(log_d - m_t)
        w_inter = jnp.exp(inter - m_t)
        c_st = c_ref[...]
        n_st = n_ref[...]
        num = (jnp.dot(sc.astype(BF16), vb, preferred_element_type=F32)
               + w_inter * jnp.dot(qb, c_st.astype(BF16), preferred_element_type=F32))
        den = jnp.sum(sc, axis=-1, keepdims=True) + w_inter * jnp.sum(q * n_st, axis=-1, keepdims=True)
        hcell = num / jnp.maximum(jnp.abs(den), jnp.exp(-m_t))

        log_wk_row = cum_last - cum_row + i_row
        m_new = jnp.maximum(cum_last + m_st, jnp.max(log_wk_row, axis=-1, keepdims=True))
        wk_col = jnp.exp(cum_last - cum_col + i_col - m_new)
        decay = jnp.exp(cum_last + m_st - m_new)
        kw = k * wk_col
        c_ref[...] = decay * c_st + jnp.dot(kw.T.astype(BF16), vb, preferred_element_type=F32)
        n_ref[...] = decay * n_st + jnp.sum(kw, axis=0, keepdims=True)
        m_ref[...] = m_new

        cell = hcell * jax.nn.sigmoid(lo_ref[0, sl, :])
        mu = jnp.mean(cell, axis=-1, keepdims=True)
        var = jnp.mean(jnp.square(cell - mu), axis=-1, keepdims=True)
        mls = (cell - mu) * lax.rsqrt(var + LN_EPS) * g + skip * xc_ref[sl, :]
        o_ref[0, sl, :] = (mls * _silu(z_ref[0, sl, :])).astype(BF16)
        return carry

    lax.fori_loop(0, SEQ // CHUNK, body, 0)


def _mlstm(hb, conv_w, conv_b, w_q, w_k, i_bias, f_bias, skip, norm_g):
    cw = conv_w.reshape(CONV_WIDTH, HEADS, HEAD_DIM).transpose(1, 0, 2)
    cb = conv_b.reshape(HEADS, 1, HEAD_DIM)
    lanes = jnp.arange(LANE)
    gb = jnp.zeros((LANE,), F32).at[GATE_I_LANE:GATE_I_LANE + HEADS].set(i_bias)
    gb = gb.at[GATE_F_LANE:GATE_F_LANE + HEADS].set(f_bias)
    gb = jnp.broadcast_to(gb, (HEADS, 1, LANE))
    hid = jnp.arange(HEADS)[:, None]
    sel_i = (lanes[None, :] == GATE_I_LANE + hid).astype(F32).reshape(HEADS, 1, LANE)
    sel_f = (lanes[None, :] == GATE_F_LANE + hid).astype(F32).reshape(HEADS, 1, LANE)
    sk = skip.reshape(HEADS, 1, HEAD_DIM)
    g = norm_g.reshape(HEADS, 1, HEAD_DIM)
    wq = w_q.astype(BF16)
    wk = w_k.astype(BF16)
    return pl.pallas_call(
        _mlstm_kernel,
        out_shape=jax.ShapeDtypeStruct((HEADS, ROWS, LANE), BF16),
        grid=(BATCH, HEADS),
        in_specs=[_head_spec(BLK_LX), _head_spec(BLK_LV), _head_spec(BLK_LO),
                  pl.BlockSpec((1, SEQ, LANE), lambda b, h: (BLK_KPE, b, 0)),
                  _head_spec(BLK_Z + 2 * HEADS),
                  _per_head(cw.shape), _per_head(cb.shape), _per_head(wq.shape), _per_head(wk.shape),
                  _per_head(gb.shape), _per_head(sel_i.shape), _per_head(sel_f.shape),
                  _per_head(sk.shape), _per_head(g.shape)],
        out_specs=pl.BlockSpec((1, SEQ, LANE), lambda b, h: (h, b, 0)),
        scratch_shapes=[pltpu.VMEM((SEQ + CONV_PAD, HEAD_DIM), F32),
                        pltpu.VMEM((SEQ, HEAD_DIM), F32),
                        pltpu.VMEM((SEQ, HEAD_DIM), F32),
                        pltpu.VMEM((SEQ, HEAD_DIM), F32),
                        pltpu.VMEM((HEAD_DIM, HEAD_DIM), F32),
                        pltpu.VMEM((1, HEAD_DIM), F32),
                        pltpu.VMEM((1, 1), F32)],
        compiler_params=_cparams(("parallel", "parallel")),
        name="mlstm",
    )(hb, hb, hb, hb, hb, cw, cb, wq, wk, gb, sel_i, sel_f, sk, g)


def _mla_prep_kernel(cq_ref, ckv_ref, kpe_ref, cm_ref, sm_ref, qg_ref, wuq_ref, kvg_ref, wukv_ref,
                     q_out, k_out, v_out):
    scale = (MLA_NOPE + MLA_ROPE) ** -0.5
    cs = cm_ref[0]
    sn = sm_ref[0]
    cq = jnp.concatenate([cq_ref[0], cq_ref[1], cq_ref[2]], axis=-1)
    qn = cq * lax.rsqrt(jnp.mean(jnp.square(cq), axis=-1, keepdims=True) + RMS_EPS) * qg_ref[...]
    q = jnp.dot(qn.astype(BF16), wuq_ref[...], preferred_element_type=F32)
    ckv = ckv_ref[0]
    kvn = ckv * lax.rsqrt(jnp.mean(jnp.square(ckv), axis=-1, keepdims=True) + RMS_EPS) * kvg_ref[...]
    kv = jnp.dot(kvn.astype(BF16), wukv_ref[...], preferred_element_type=F32)
    kpe = kpe_ref[0]
    krot = (kpe * cs + pltpu.roll(kpe, LANE // 2, 1) * sn).astype(BF16)
    for h in range(HEADS):
        qr = q[:, GROUP_WIDTH + h * LANE:GROUP_WIDTH + (h + 1) * LANE]
        qr = qr * cs + pltpu.roll(qr, LANE // 2, 1) * sn
        q_out[0, h, :, 0:LANE] = (q[:, h * LANE:(h + 1) * LANE] * scale).astype(BF16)
        q_out[0, h, :, LANE:2 * LANE] = (qr * scale).astype(BF16)
        k_out[0, h, :, 0:LANE] = kv[:, h * LANE:(h + 1) * LANE].astype(BF16)
        k_out[0, h, :, LANE:2 * LANE] = krot
        v_out[0, h] = kv[:, GROUP_WIDTH + h * LANE:GROUP_WIDTH + (h + 1) * LANE].astype(BF16)


def _prep_mla_weights(w_uq, w_ukv):
    lead = w_uq.shape[:-1]
    wq = w_uq.reshape(lead + (HEADS, MLA_NOPE + MLA_ROPE))
    nope = wq[..., :MLA_NOPE].reshape(lead + (GROUP_WIDTH,))
    half = MLA_ROPE // 2
    zeros = jnp.zeros(lead + (HEADS, 64 - half), w_uq.dtype)
    rope = jnp.concatenate([wq[..., MLA_NOPE:MLA_NOPE + half], zeros, wq[..., MLA_NOPE + half:], zeros], axis=-1)
    wq_p = jnp.concatenate([nope, rope.reshape(lead + (HEADS * LANE,))], axis=-1).astype(BF16)
    lead = w_ukv.shape[:-1]
    wkv = w_ukv.reshape(lead + (HEADS, MLA_NOPE + HEAD_DIM))
    wkv_p = jnp.concatenate([wkv[..., :MLA_NOPE].reshape(lead + (GROUP_WIDTH,)),
                             wkv[..., MLA_NOPE:].reshape(lead + (GROUP_WIDTH,))], axis=-1).astype(BF16)
    return wq_p, wkv_p


def _mla_prep(hb, cos_m, sin_m, q_norm_g, wuq_p, kv_norm_g, wukv_p):
    ns = SEQ // TS_PREP
    tab = pl.BlockSpec((1, TS_PREP, LANE), lambda b, i: (b, i, 0))
    full = lambda shape: pl.BlockSpec(shape, lambda b, i: (0,) * len(shape))
    qk_shape = jax.ShapeDtypeStruct((BATCH, HEADS, SEQ, 2 * LANE), BF16)
    v_shape = jax.ShapeDtypeStruct((BATCH, HEADS, SEQ, LANE), BF16)
    qg = q_norm_g.reshape(1, MLA_Q_RANK)
    kvg = kv_norm_g.reshape(1, MLA_KV_RANK)
    return pl.pallas_call(
        _mla_prep_kernel,
        out_shape=(qk_shape, qk_shape, v_shape),
        grid=(BATCH, ns),
        in_specs=[pl.BlockSpec((3, TS_PREP, LANE), lambda b, i: (BLK_CQ // 3, b * ns + i, 0)),
                  pl.BlockSpec((1, TS_PREP, LANE), lambda b, i: (BLK_CKV, b * ns + i, 0)),
                  pl.BlockSpec((1, TS_PREP, LANE), lambda b, i: (BLK_KPE, b * ns + i, 0)),
                  tab, tab, full(qg.shape), full(wuq_p.shape), full(kvg.shape), full(wukv_p.shape)],
        out_specs=(pl.BlockSpec((1, HEADS, TS_PREP, 2 * LANE), lambda b, i: (b, 0, i, 0)),
                   pl.BlockSpec((1, HEADS, TS_PREP, 2 * LANE), lambda b, i: (b, 0, i, 0)),
                   pl.BlockSpec((1, HEADS, TS_PREP, LANE), lambda b, i: (b, 0, i, 0))),
        compiler_params=_cparams(("parallel", "parallel")),
        name="mla_prep",
    )(hb, hb, hb, cos_m, sin_m, qg, wuq_p, kvg, wukv_p)


def _mla_attn_kernel(q_ref, k_ref, v_ref, z_ref, o_ref):
    qi = pl.program_id(2)
    q = q_ref[0, 0]

    def block(j, carry, masked):
        m, l, acc = carry
        sl = pl.ds(pl.multiple_of(j * TQ_ATT, TQ_ATT), TQ_ATT)
        s = lax.dot_general(q, k_ref[0, 0, sl, :], (((1,), (1,)), ((), ())), preferred_element_type=F32)
        if masked:
            r = lax.broadcasted_iota(jnp.int32, s.shape, 0)
            c = lax.broadcasted_iota(jnp.int32, s.shape, 1)
            s = jnp.where(c <= r, s, -jnp.inf)
        m_new = jnp.maximum(m, jnp.max(s, axis=-1, keepdims=True))
        a = jnp.exp(m - m_new)
        p = jnp.exp(s - m_new)
        l = a * l + jnp.sum(p, axis=-1, keepdims=True)
        acc = a * acc + jnp.dot(p.astype(BF16), v_ref[0, 0, sl, :], preferred_element_type=F32)
        return m_new, l, acc

    init = (jnp.full((TQ_ATT, 1), -jnp.inf, F32), jnp.zeros((TQ_ATT, 1), F32), jnp.zeros((TQ_ATT, LANE), F32))
    carry = lax.fori_loop(0, qi, lambda j, cr: block(j, cr, False), init)
    m, l, acc = block(qi, carry, True)
    o_ref[0] = (acc / l * _silu(z_ref[0])).astype(BF16)


def _mla_attn(q, k, v, hb):
    nq = SEQ // TQ_ATT
    return pl.pallas_call(
        _mla_attn_kernel,
        out_shape=jax.ShapeDtypeStruct((HEADS, ROWS, LANE), BF16),
        grid=(BATCH, HEADS, nq),
        in_specs=[pl.BlockSpec((1, 1, TQ_ATT, 2 * LANE), lambda b, h, i: (b, h, i, 0)),
                  pl.BlockSpec((1, 1, SEQ, 2 * LANE), lambda b, h, i: (b, h, 0, 0)),
                  pl.BlockSpec((1, 1, SEQ, LANE), lambda b, h, i: (b, h, 0, 0)),
                  pl.BlockSpec((1, TQ_ATT, LANE), lambda b, h, i: (BLK_Z + HEADS + h, b * nq + i, 0))],
        out_specs=pl.BlockSpec((1, TQ_ATT, LANE), lambda b, h, i: (h, b * nq + i, 0)),
        compiler_params=_cparams(("parallel", "parallel", "parallel")),
        name="mla_attn",
    )(q, k, v, hb)


def _mem_attn_kernel(q_ref, k_ref, v_ref, z_ref, o_ref):
    q = (q_ref[0] * HEAD_DIM ** -0.5).astype(BF16)
    s = lax.dot_general(q, k_ref[...], (((1,), (1,)), ((), ())), preferred_element_type=F32)
    e = jnp.exp(s - jnp.max(s, axis=-1, keepdims=True))
    p = e / jnp.sum(e, axis=-1, keepdims=True)
    out = jnp.dot(p.astype(BF16), v_ref[...], preferred_element_type=F32)
    o_ref[0] = (out * _silu(z_ref[0])).astype(BF16)


def _mem_attn(hb, kvm, layer):
    nq = SEQ // TQ_MEM
    kbase = layer * 2 * HEADS
    return pl.pallas_call(
        _mem_attn_kernel,
        out_shape=jax.ShapeDtypeStruct((HEADS, ROWS, LANE), BF16),
        grid=(BATCH, HEADS, nq),
        in_specs=[pl.BlockSpec((1, TQ_MEM, LANE), lambda b, h, i: (BLK_MQ + h, b * nq + i, 0)),
                  pl.BlockSpec((MEM_LEN, LANE), lambda b, h, i: (b, kbase + h)),
                  pl.BlockSpec((MEM_LEN, LANE), lambda b, h, i: (b, kbase + HEADS + h)),
                  pl.BlockSpec((1, TQ_MEM, LANE), lambda b, h, i: (BLK_Z + 3 * HEADS + h, b * nq + i, 0))],
        out_specs=pl.BlockSpec((1, TQ_MEM, LANE), lambda b, h, i: (h, b * nq + i, 0)),
        compiler_params=_cparams(("parallel", "parallel", "parallel")),
        name="mem_attn",
    )(hb, kvm, kvm, hb)


def _out_kernel(ya_ref, yb_ref, yc_ref, yd_ref, w_ref, x_ref, g_ref, b_ref, o_ref):
    parts = [ref[h] for ref in (ya_ref, yb_ref, yc_ref, yd_ref) for h in range(HEADS)]
    y = jnp.concatenate(parts, axis=-1)
    r = DEEPNORM_ALPHA * x_ref[...] + jnp.dot(y, w_ref[...], preferred_element_type=F32)
    mu = jnp.mean(r, axis=-1, keepdims=True)
    var = jnp.mean(jnp.square(r - mu), axis=-1, keepdims=True)
    o_ref[...] = (r - mu) * lax.rsqrt(var + LN_EPS) * g_ref[...] + b_ref[...]


def _outproj(ya, yb, yc, yd, w_out, x2d, ln_g, ln_b):
    yspec = pl.BlockSpec((HEADS, TM_OUT, LANE), lambda i: (0, i, 0))
    vec = pl.BlockSpec((1, D_MODEL), lambda i: (0, 0))
    return pl.pallas_call(
        _out_kernel,
        out_shape=jax.ShapeDtypeStruct((ROWS, D_MODEL), F32),
        grid=(ROWS // TM_OUT,),
        in_specs=[yspec, yspec, yspec, yspec,
                  pl.BlockSpec((MIX_WIDTH, D_MODEL), lambda i: (0, 0)),
                  pl.BlockSpec((TM_OUT, D_MODEL), lambda i: (i, 0)), vec, vec],
        out_specs=pl.BlockSpec((TM_OUT, D_MODEL), lambda i: (i, 0)),
        compiler_params=_cparams(("parallel",)),
        name="outproj_ln",
    )(ya, yb, yc, yd, w_out, x2d, ln_g.reshape(1, D_MODEL), ln_b.reshape(1, D_MODEL))


def kernel(x, mem, positions, w_in, ret_norm_g, mla_q_norm_g, mla_w_uq, mla_kv_norm_g, mla_w_ukv, ml_conv_w, ml_conv_b, ml_w_q, ml_w_k, ml_i_bias, ml_f_bias, ml_skip, ml_norm_g, w_mem_kv, w_out, ln_g, ln_b):
    assert x.shape == (BATCH, SEQ, D_MODEL) and mem.shape == (BATCH, MEM_LEN, D_MODEL)
    cos_r, sin_r, cos_m, sin_m = _rope_tables(positions)
    kvm = _mem_kv(mem, w_mem_kv)
    w_in_p = _prep_w_in(w_in)
    wuq_p, wukv_p = _prep_mla_weights(mla_w_uq, mla_w_ukv)
    w_out_b = w_out.astype(BF16)
    ret_consts = _ret_consts()

    x2d = x.reshape(ROWS, D_MODEL)
    for l in range(DEPTH):
        hb = _inproj(x2d, w_in_p[l])
        ya = _retention(hb, cos_r, sin_r, ret_consts, ret_norm_g[l])
        q, k, v = _mla_prep(hb, cos_m, sin_m, mla_q_norm_g[l], wuq_p[l], mla_kv_norm_g[l], wukv_p[l])
        yb = _mla_attn(q, k, v, hb)
        yc = _mlstm(hb, ml_conv_w[l], ml_conv_b[l], ml_w_q[l], ml_w_k[l], ml_i_bias[l], ml_f_bias[l],
                    ml_skip[l], ml_norm_g[l])
        yd = _mem_attn(hb, kvm, l)
        x2d = _outproj(ya, yb, yc, yd, w_out_b[l], x2d, ln_g[l], ln_b[l])
    return x2d.reshape(BATCH, SEQ, D_MODEL)
```

```python
import numpy as np
import jax
import jax.numpy as jnp
from jax import lax
from jax.experimental import pallas as pl
from jax.experimental.pallas import tpu as pltpu

F32 = jnp.float32
BF16 = jnp.bfloat16

D_MODEL = 2048
BATCH = 8
SEQ = 2048
DEPTH = 4
MEM_LEN = 256
HEAD_DIM = 128
HEADS = 4
GROUP_WIDTH = HEADS * HEAD_DIM
MLA_NOPE = 128
MLA_ROPE = 64
MLA_Q_RANK = 384
MLA_KV_RANK = 128
CONV_WIDTH = 4
MIX_WIDTH = 4 * GROUP_WIDTH
ROPE_THETA = 10000.0
LN_EPS = 1e-5
RMS_EPS = 1e-6
DEEPNORM_ALPHA = (2 * DEPTH) ** 0.25
IN_SPLITS = (512, 512, 512, MLA_Q_RANK, MLA_KV_RANK, MLA_ROPE, 512, 512, 512, HEADS, HEADS, 512, MIX_WIDTH)

LANE = 128
ROWS = BATCH * SEQ

BLK_RQ, BLK_RK, BLK_RV = 0, 4, 8
BLK_LX, BLK_LV, BLK_LO = 12, 16, 20
BLK_MQ = 24
BLK_Z = 28
BLK_CQ, BLK_CKV = 44, 47
BLK_KPE = 48
NBLK = 50
GATE_I_LANE, GATE_F_LANE = 32, 36

TM_IN, TN_IN = 1024, 1280
TM_OUT = 512
TS_PREP = 512
TQ_ATT = 512
TQ_MEM = 1024
RCHUNK = 512
CONV_PAD = 8
VMEM_LIMIT = 56 * 1024 * 1024


def _cparams(sem):
    return pltpu.CompilerParams(dimension_semantics=sem, vmem_limit_bytes=VMEM_LIMIT)


def _silu(z):
    return z * jax.nn.sigmoid(z)


def _log_sigmoid(x):
    return jnp.minimum(x, 0.0) - jnp.log1p(jnp.exp(-jnp.abs(x)))


def _split3(x):
    x1 = x.astype(BF16)
    r1 = x - x1.astype(F32)
    x2 = r1.astype(BF16)
    x3 = (r1 - x2.astype(F32)).astype(BF16)
    return x1, x2, x3


def _dot(a, b):
    return jnp.dot(a, b, preferred_element_type=F32)


def _dot_nt(a, b):
    return lax.dot_general(a, b, (((1,), (1,)), ((), ())), preferred_element_type=F32)


def _tables_kernel(pos_ref, c_ref, cr_ref, sr_ref, cm_ref, sm_ref):
    pos = pos_ref[0].astype(F32)
    ang_r = pos * c_ref[0:1, :]
    cr_ref[0] = jnp.cos(ang_r)
    sr_ref[0] = jnp.sin(ang_r) * c_ref[1:2, :]
    ang_m = pos * c_ref[2:3, :]
    cm_ref[0] = jnp.cos(ang_m) * c_ref[3:4, :]
    sm_ref[0] = jnp.sin(ang_m) * c_ref[4:5, :]


def _rope_tables(positions):
    half_r = HEAD_DIM // 2
    fr = ROPE_THETA ** (-jnp.arange(half_r, dtype=F32) / half_r)
    half_m = MLA_ROPE // 2
    fm = ROPE_THETA ** (-jnp.arange(half_m, dtype=F32) / half_m)
    z32 = jnp.zeros((half_m,), F32)
    o32 = jnp.ones((half_m,), F32)
    rows = [
        jnp.concatenate([fr, fr]),
        jnp.concatenate([-jnp.ones((half_r,), F32), jnp.ones((half_r,), F32)]),
        jnp.concatenate([fm, z32, fm, z32]),
        jnp.concatenate([o32, z32, o32, z32]),
        jnp.concatenate([-o32, z32, o32, z32]),
    ]
    consts = jnp.concatenate([jnp.stack(rows), jnp.zeros((3, LANE), F32)], axis=0)
    ts = 512
    tab = jax.ShapeDtypeStruct((BATCH, SEQ, LANE), F32)
    spec = pl.BlockSpec((1, ts, LANE), lambda b, i: (b, i, 0))
    return pl.pallas_call(
        _tables_kernel,
        out_shape=(tab, tab, tab, tab),
        grid=(BATCH, SEQ // ts),
        in_specs=[pl.BlockSpec((1, ts, 1), lambda b, i: (b, i, 0)),
                  pl.BlockSpec((8, LANE), lambda b, i: (0, 0))],
        out_specs=(spec, spec, spec, spec),
        compiler_params=_cparams(("parallel", "parallel")),
        name="rope_tables",
    )(positions.reshape(BATCH, SEQ, 1), consts)


def _matmul_kernel(a_ref, w_ref, o_ref):
    o_ref[...] = _dot(a_ref[...].astype(BF16), w_ref[...]).astype(o_ref.dtype)


def _mem_kv(mem, w_mem_kv):
    w = jnp.transpose(w_mem_kv, (1, 0, 2)).reshape(D_MODEL, DEPTH * 2 * GROUP_WIDTH).astype(BF16)
    a = mem.reshape(BATCH * MEM_LEN, D_MODEL)
    tm, tn = 512, 1024
    return pl.pallas_call(
        _matmul_kernel,
        out_shape=jax.ShapeDtypeStruct((BATCH * MEM_LEN, DEPTH * 2 * GROUP_WIDTH), BF16),
        grid=(BATCH * MEM_LEN // tm, DEPTH * 2 * GROUP_WIDTH // tn),
        in_specs=[pl.BlockSpec((tm, D_MODEL), lambda i, j: (i, 0)),
                  pl.BlockSpec((D_MODEL, tn), lambda i, j: (0, j))],
        out_specs=pl.BlockSpec((tm, tn), lambda i, j: (i, j)),
        compiler_params=_cparams(("parallel", "parallel")),
        name="mem_kv",
    )(a, w)


def _inproj_kernel(x_ref, w_ref, o_ref, xb_ref):
    @pl.when(pl.program_id(1) == 0)
    def _():
        xb_ref[...] = x_ref[...].astype(BF16)

    xb = xb_ref[...]
    for k2 in range(TN_IN // 256):
        r = _dot(xb, w_ref[:, k2 * 256:(k2 + 1) * 256])
        o_ref[2 * k2] = r[:, :LANE]
        o_ref[2 * k2 + 1] = r[:, LANE:]


def _inproj(x2d, w_p, layer):
    nb = TN_IN // LANE
    return pl.pallas_call(
        _inproj_kernel,
        out_shape=jax.ShapeDtypeStruct((NBLK, ROWS, LANE), F32),
        grid=(ROWS // TM_IN, NBLK * LANE // TN_IN),
        in_specs=[pl.BlockSpec((TM_IN, D_MODEL), lambda i, j: (i, 0)),
                  pl.BlockSpec((None, D_MODEL, TN_IN), lambda i, j: (layer, 0, j))],
        out_specs=pl.BlockSpec((nb, TM_IN, LANE), lambda i, j: (j, i, 0)),
        scratch_shapes=[pltpu.VMEM((TM_IN, D_MODEL), BF16)],
        compiler_params=_cparams(("parallel", "arbitrary")),
        name="inproj",
    )(x2d, w_p)


def _prep_w_in(w_in):
    idx = np.cumsum(IN_SPLITS)[:-1].tolist()
    (r_q, r_k, r_v, a_cq, a_ckv, a_kpe, l_x, l_v, l_o, l_i, l_f, c_q, z) = jnp.split(w_in, idx, axis=-1)
    half = MLA_ROPE // 2
    lead = w_in.shape[:-1]
    blk = jnp.concatenate([
        a_kpe[..., :half], l_i, l_f, jnp.zeros(lead + (64 - half - 2 * HEADS,), w_in.dtype),
        a_kpe[..., half:], jnp.zeros(lead + (64 - half,), w_in.dtype)], axis=-1)
    pad = jnp.zeros(lead + (LANE,), w_in.dtype)
    return jnp.concatenate([r_q, r_k, r_v, l_x, l_v, l_o, c_q, z, a_cq, a_ckv, blk, pad], axis=-1).astype(BF16)


def _group_spec(base, rows):
    nchunk = SEQ // rows
    return pl.BlockSpec((HEADS, rows, LANE), lambda b, c: (base // HEADS, b * nchunk + c, 0))


def _layer_spec(shape, layer):
    nd = len(shape) - 1
    return pl.BlockSpec((None,) + tuple(shape[1:]), lambda *_: (layer,) + (0,) * nd)


def _const_spec(shape):
    nd = len(shape)
    return pl.BlockSpec(tuple(shape), lambda *_: (0,) * nd)


def _head_norm_gate(h, g, z):
    mu = jnp.mean(h, axis=-1, keepdims=True)
    var = jnp.mean(jnp.square(h - mu), axis=-1, keepdims=True)
    return (h - mu) * lax.rsqrt(var + LN_EPS) * g, _silu(z)


def _ret_kernel(q_ref, k_ref, v_ref, z_ref, cos_ref, sin_ref, gq_ref, gk_ref, gc_ref, g_ref, o_ref, st_ref):
    @pl.when(pl.program_id(1) == 0)
    def _():
        st_ref[...] = jnp.zeros_like(st_ref)

    cs = cos_ref[0]
    sn = sin_ref[0]
    row = lax.broadcasted_iota(jnp.int32, (RCHUNK, RCHUNK), 0)
    col = lax.broadcasted_iota(jnp.int32, (RCHUNK, RCHUNK), 1)
    causal = col <= row
    for h in range(HEADS):
        q = q_ref[h]
        q = (q * cs + pltpu.roll(q, HEAD_DIM // 2, 1) * sn) * gq_ref[h]
        k = k_ref[h]
        k = (k * cs + pltpu.roll(k, HEAD_DIM // 2, 1) * sn) * HEAD_DIM ** -0.5 * gk_ref[h]
        qb = q.astype(BF16)
        kb = k.astype(BF16)
        vb = v_ref[h].astype(BF16)
        sc = jnp.where(causal, _dot_nt(qb, kb), 0.0)
        st = st_ref[h]
        out = _dot(sc.astype(BF16), vb) + _dot(qb, st.astype(BF16))
        st_ref[h] = gc_ref[h] * (st + _dot(k.T.astype(BF16), vb))
        hn, gate = _head_norm_gate(out, g_ref[h], z_ref[h])
        o_ref[h] = (hn * gate).astype(BF16)


def _ret_consts():
    log_g = jnp.log1p(-jnp.exp2(-5.0 - jnp.arange(HEADS, dtype=F32)))
    idx = jnp.arange(RCHUNK, dtype=F32)
    full = (HEADS, RCHUNK, HEAD_DIM)
    gq = jnp.broadcast_to(jnp.exp(log_g[:, None] * (idx + 1.0))[..., None], full)
    gk = jnp.broadcast_to(jnp.exp(-log_g[:, None] * (idx + 1.0))[..., None], full)
    gc = jnp.broadcast_to(jnp.exp(log_g * RCHUNK)[:, None, None], (HEADS, 1, HEAD_DIM))
    return gq, gk, gc


def _retention(hb, cos_r, sin_r, consts, norm_g, layer):
    gq, gk, gc = consts
    nchunk = SEQ // RCHUNK
    tab = pl.BlockSpec((1, RCHUNK, LANE), lambda b, c: (b, c, 0))
    return pl.pallas_call(
        _ret_kernel,
        out_shape=jax.ShapeDtypeStruct((HEADS, ROWS, LANE), BF16),
        grid=(BATCH, nchunk),
        in_specs=[_group_spec(BLK_RQ, RCHUNK), _group_spec(BLK_RK, RCHUNK), _group_spec(BLK_RV, RCHUNK),
                  _group_spec(BLK_Z, RCHUNK), tab, tab,
                  _const_spec(gq.shape), _const_spec(gk.shape), _const_spec(gc.shape),
                  _layer_spec(norm_g.shape, layer)],
        out_specs=pl.BlockSpec((HEADS, RCHUNK, LANE), lambda b, c: (0, b * nchunk + c, 0)),
        scratch_shapes=[pltpu.VMEM((HEADS, HEAD_DIM, HEAD_DIM), F32)],
        compiler_params=_cparams(("parallel", "arbitrary")),
        name="retention",
    )(hb, hb, hb, hb, cos_r, sin_r, gq, gk, gc, norm_g)


def _mlstm_kernel(lx_ref, lv_ref, lo_ref, gt_ref, z_ref, cw_ref, cb_ref, wq_ref, wk_ref, gb_ref,
                  skip_ref, g_ref, o_ref, xp_ref, c_ref, n_ref, m_ref):
    @pl.when(pl.program_id(1) == 0)
    def _():
        xp_ref[:, 0:CONV_PAD, :] = jnp.zeros((HEADS, CONV_PAD, HEAD_DIM), F32)
        c_ref[...] = jnp.zeros_like(c_ref)
        n_ref[...] = jnp.zeros_like(n_ref)
        m_ref[...] = jnp.zeros_like(m_ref)

    n = RCHUNK
    row = lax.broadcasted_iota(jnp.int32, (n, n), 0)
    col = lax.broadcasted_iota(jnp.int32, (n, n), 1)
    causal = col <= row
    tril = jnp.where(causal, 1.0, 0.0).astype(BF16)
    r128 = lax.broadcasted_iota(jnp.int32, (LANE, LANE), 0)
    c128 = lax.broadcasted_iota(jnp.int32, (LANE, LANE), 1)
    ident = jnp.where(r128 == c128, 1.0, 0.0).astype(BF16)

    lane = lax.broadcasted_iota(jnp.int32, (n, LANE), 1)
    x = gt_ref[0] + gb_ref[...]
    x = jnp.where((lane >= GATE_F_LANE) & (lane < GATE_F_LANE + HEADS), _log_sigmoid(x), x)
    x1, x2, x3 = _split3(x)
    cum = _dot(tril, x1) + _dot(tril, x2) + _dot(tril, x3)
    rt = x - pltpu.roll(cum, LANE - (GATE_F_LANE - GATE_I_LANE), 1)
    r1, r2, r3 = _split3(rt)
    rtt = _dot_nt(ident, r1) + _dot_nt(ident, r2) + _dot_nt(ident, r3)

    for h in range(HEADS):
        xp_ref[h, CONV_PAD:, :] = lx_ref[h]
        acc = jnp.zeros((n, HEAD_DIM), F32) + cb_ref[h]
        for j in range(CONV_WIDTH):
            off = CONV_PAD - (CONV_WIDTH - 1) + j
            acc = acc + xp_ref[h, off:off + n, :] * cw_ref[h, j:j + 1, :]
        xp_ref[h, 0:CONV_PAD, :] = lx_ref[h, n - CONV_PAD:n, :]
        xc = _silu(acc)
        xcb = xc.astype(BF16)
        q = _dot(xcb, wq_ref[h])
        k = _dot(xcb, wk_ref[h]) * HEAD_DIM ** -0.5
        qb = q.astype(BF16)
        kb = k.astype(BF16)
        vb = lv_ref[h].astype(BF16)

        li = GATE_I_LANE + h
        lf = GATE_F_LANE + h
        r_row = rtt[li:li + 1, :]
        r_col = rt[:, li:li + 1]
        cum_col = cum[:, lf:lf + 1]
        m_st = m_ref[h]
        rm = jnp.where(causal, r_row, -jnp.inf)
        u = jnp.maximum(jnp.max(rm, axis=-1, keepdims=True), m_st)
        sc = _dot_nt(qb, kb) * jnp.exp(rm - u)
        w_inter = jnp.exp(m_st - u)
        c_st = c_ref[h]
        n_st = n_ref[h]
        num = _dot(sc.astype(BF16), vb) + w_inter * _dot(qb, c_st.astype(BF16))
        den = jnp.sum(sc, axis=-1, keepdims=True) + w_inter * jnp.sum(q * n_st, axis=-1, keepdims=True)
        hcell = num / jnp.maximum(jnp.abs(den), jnp.exp(-(cum_col + u)))

        u_last = u[n - 1:n, :]
        decay = jnp.exp(m_st - u_last)
        kw = k * jnp.exp(r_col - u_last)
        c_ref[h] = decay * c_st + _dot(kw.T.astype(BF16), vb)
        n_ref[h] = decay * n_st + jnp.sum(kw, axis=0, keepdims=True)
        m_ref[h] = cum_col[n - 1:n, :] + u_last

        cell = hcell * jax.nn.sigmoid(lo_ref[h])
        hn, gate = _head_norm_gate(cell, g_ref[h], z_ref[h])
        o_ref[h] = ((hn + skip_ref[h] * xc) * gate).astype(BF16)


def _prep_mlstm(conv_w, conv_b, w_q, w_k, i_bias, f_bias, skip, norm_g):
    nl = conv_w.shape[0]
    cw = conv_w.reshape(nl, CONV_WIDTH, HEADS, HEAD_DIM).transpose(0, 2, 1, 3)
    cb = conv_b.reshape(nl, HEADS, 1, HEAD_DIM)
    zeros = lambda w: jnp.zeros((nl, w), F32)
    gb = jnp.concatenate([zeros(GATE_I_LANE), i_bias, f_bias, zeros(LANE - GATE_F_LANE - HEADS)], axis=-1)
    return (cw, cb, w_q.astype(BF16), w_k.astype(BF16), gb.reshape(nl, 1, LANE),
            skip.reshape(nl, HEADS, 1, HEAD_DIM), norm_g.reshape(nl, HEADS, 1, HEAD_DIM))


def _mlstm(hb, params, layer):
    nchunk = SEQ // RCHUNK
    return pl.pallas_call(
        _mlstm_kernel,
        out_shape=jax.ShapeDtypeStruct((HEADS, ROWS, LANE), BF16),
        grid=(BATCH, nchunk),
        in_specs=[_group_spec(BLK_LX, RCHUNK), _group_spec(BLK_LV, RCHUNK), _group_spec(BLK_LO, RCHUNK),
                  pl.BlockSpec((1, RCHUNK, LANE), lambda b, c: (BLK_KPE, b * nchunk + c, 0)),
                  _group_spec(BLK_Z + 2 * HEADS, RCHUNK)] + [_layer_spec(p.shape, layer) for p in params],
        out_specs=pl.BlockSpec((HEADS, RCHUNK, LANE), lambda b, c: (0, b * nchunk + c, 0)),
        scratch_shapes=[pltpu.VMEM((HEADS, RCHUNK + CONV_PAD, HEAD_DIM), F32),
                        pltpu.VMEM((HEADS, HEAD_DIM, HEAD_DIM), F32),
                        pltpu.VMEM((HEADS, 1, HEAD_DIM), F32),
                        pltpu.VMEM((HEADS, 1, 1), F32)],
        compiler_params=_cparams(("parallel", "arbitrary")),
        name="mlstm",
    )(hb, hb, hb, hb, hb, *params)


def _mla_prep_kernel(lat_ref, kpe_ref, cm_ref, sm_ref, qg_ref, wuq_ref, kvg_ref, wukv_ref,
                     q_out, k_out, v_out):
    scale = (MLA_NOPE + MLA_ROPE) ** -0.5
    cs = cm_ref[0]
    sn = sm_ref[0]
    cq = jnp.concatenate([lat_ref[0], lat_ref[1], lat_ref[2]], axis=-1)
    qn = cq * lax.rsqrt(jnp.mean(jnp.square(cq), axis=-1, keepdims=True) + RMS_EPS) * qg_ref[...]
    q = _dot(qn.astype(BF16), wuq_ref[...])
    ckv = lat_ref[3]
    kvn = ckv * lax.rsqrt(jnp.mean(jnp.square(ckv), axis=-1, keepdims=True) + RMS_EPS) * kvg_ref[...]
    kv = _dot(kvn.astype(BF16), wukv_ref[...])
    kpe = kpe_ref[0]
    krot = (kpe * cs + pltpu.roll(kpe, LANE // 2, 1) * sn).astype(BF16)
    for h in range(HEADS):
        qr = q[:, GROUP_WIDTH + h * LANE:GROUP_WIDTH + (h + 1) * LANE]
        qr = qr * cs + pltpu.roll(qr, LANE // 2, 1) * sn
        q_out[0, h, :, 0:LANE] = (q[:, h * LANE:(h + 1) * LANE] * scale).astype(BF16)
        q_out[0, h, :, LANE:2 * LANE] = (qr * scale).astype(BF16)
        k_out[0, h, :, 0:LANE] = kv[:, h * LANE:(h + 1) * LANE].astype(BF16)
        k_out[0, h, :, LANE:2 * LANE] = krot
        v_out[0, h] = kv[:, GROUP_WIDTH + h * LANE:GROUP_WIDTH + (h + 1) * LANE].astype(BF16)


def _prep_mla_weights(w_uq, w_ukv):
    lead = w_uq.shape[:-1]
    wq = w_uq.reshape(lead + (HEADS, MLA_NOPE + MLA_ROPE))
    nope = wq[..., :MLA_NOPE].reshape(lead + (GROUP_WIDTH,))
    half = MLA_ROPE // 2
    zeros = jnp.zeros(lead + (HEADS, 64 - half), w_uq.dtype)
    rope = jnp.concatenate([wq[..., MLA_NOPE:MLA_NOPE + half], zeros, wq[..., MLA_NOPE + half:], zeros], axis=-1)
    wq_p = jnp.concatenate([nope, rope.reshape(lead + (HEADS * LANE,))], axis=-1).astype(BF16)
    lead = w_ukv.shape[:-1]
    wkv = w_ukv.reshape(lead + (HEADS, MLA_NOPE + HEAD_DIM))
    wkv_p = jnp.concatenate([wkv[..., :MLA_NOPE].reshape(lead + (GROUP_WIDTH,)),
                             wkv[..., MLA_NOPE:].reshape(lead + (GROUP_WIDTH,))], axis=-1).astype(BF16)
    return wq_p, wkv_p


def _mla_prep(hb, cos_m, sin_m, qg, wuq_p, kvg, wukv_p, layer):
    ns = SEQ // TS_PREP
    tab = pl.BlockSpec((1, TS_PREP, LANE), lambda b, i: (b, i, 0))
    qk_shape = jax.ShapeDtypeStruct((BATCH, HEADS, SEQ, 2 * LANE), BF16)
    v_shape = jax.ShapeDtypeStruct((BATCH, HEADS, SEQ, LANE), BF16)
    return pl.pallas_call(
        _mla_prep_kernel,
        out_shape=(qk_shape, qk_shape, v_shape),
        grid=(BATCH, ns),
        in_specs=[pl.BlockSpec((HEADS, TS_PREP, LANE), lambda b, i: (BLK_CQ // HEADS, b * ns + i, 0)),
                  pl.BlockSpec((1, TS_PREP, LANE), lambda b, i: (BLK_KPE, b * ns + i, 0)),
                  tab, tab, _layer_spec(qg.shape, layer), _layer_spec(wuq_p.shape, layer),
                  _layer_spec(kvg.shape, layer), _layer_spec(wukv_p.shape, layer)],
        out_specs=(pl.BlockSpec((1, HEADS, TS_PREP, 2 * LANE), lambda b, i: (b, 0, i, 0)),
                   pl.BlockSpec((1, HEADS, TS_PREP, 2 * LANE), lambda b, i: (b, 0, i, 0)),
                   pl.BlockSpec((1, HEADS, TS_PREP, LANE), lambda b, i: (b, 0, i, 0))),
        compiler_params=_cparams(("parallel", "parallel")),
        name="mla_prep",
    )(hb, hb, cos_m, sin_m, qg, wuq_p, kvg, wukv_p)


def _mla_attn_kernel(q_ref, k_ref, v_ref, z_ref, o_ref):
    qi = pl.program_id(2)
    q = q_ref[0, 0]

    def block(j, carry, masked):
        m, l, acc = carry
        sl = pl.ds(pl.multiple_of(j * TQ_ATT, TQ_ATT), TQ_ATT)
        s = _dot_nt(q, k_ref[0, 0, sl, :])
        if masked:
            r = lax.broadcasted_iota(jnp.int32, s.shape, 0)
            c = lax.broadcasted_iota(jnp.int32, s.shape, 1)
            s = jnp.where(c <= r, s, -jnp.inf)
        m_new = jnp.maximum(m, jnp.max(s, axis=-1, keepdims=True))
        a = jnp.exp(m - m_new)
        p = jnp.exp(s - m_new)
        l = a * l + jnp.sum(p, axis=-1, keepdims=True)
        acc = a * acc + _dot(p.astype(BF16), v_ref[0, 0, sl, :])
        return m_new, l, acc

    init = (jnp.full((TQ_ATT, 1), -jnp.inf, F32), jnp.zeros((TQ_ATT, 1), F32), jnp.zeros((TQ_ATT, LANE), F32))
    carry = lax.fori_loop(0, qi, lambda j, cr: block(j, cr, False), init)
    m, l, acc = block(qi, carry, True)
    o_ref[0] = (acc / l * _silu(z_ref[0])).astype(BF16)


def _mla_attn(q, k, v, hb):
    nq = SEQ // TQ_ATT
    return pl.pallas_call(
        _mla_attn_kernel,
        out_shape=jax.ShapeDtypeStruct((HEADS, ROWS, LANE), BF16),
        grid=(BATCH, HEADS, nq),
        in_specs=[pl.BlockSpec((1, 1, TQ_ATT, 2 * LANE), lambda b, h, i: (b, h, i, 0)),
                  pl.BlockSpec((1, 1, SEQ, 2 * LANE), lambda b, h, i: (b, h, 0, 0)),
                  pl.BlockSpec((1, 1, SEQ, LANE), lambda b, h, i: (b, h, 0, 0)),
                  pl.BlockSpec((1, TQ_ATT, LANE), lambda b, h, i: (BLK_Z + HEADS + h, b * nq + i, 0))],
        out_specs=pl.BlockSpec((1, TQ_ATT, LANE), lambda b, h, i: (h, b * nq + i, 0)),
        compiler_params=_cparams(("parallel", "parallel", "parallel")),
        name="mla_attn",
    )(q, k, v, hb)


def _mem_attn_kernel(q_ref, k_ref, v_ref, z_ref, o_ref):
    q = (q_ref[0] * HEAD_DIM ** -0.5).astype(BF16)
    s = _dot_nt(q, k_ref[...])
    e = jnp.exp(s - jnp.max(s, axis=-1, keepdims=True))
    p = e / jnp.sum(e, axis=-1, keepdims=True)
    out = _dot(p.astype(BF16), v_ref[...])
    o_ref[0] = (out * _silu(z_ref[0])).astype(BF16)


def _mem_attn(hb, kvm, layer):
    nq = SEQ // TQ_MEM
    kbase = layer * 2 * HEADS
    return pl.pallas_call(
        _mem_attn_kernel,
        out_shape=jax.ShapeDtypeStruct((HEADS, ROWS, LANE), BF16),
        grid=(BATCH, HEADS, nq),
        in_specs=[pl.BlockSpec((1, TQ_MEM, LANE), lambda b, h, i: (BLK_MQ + h, b * nq + i, 0)),
                  pl.BlockSpec((MEM_LEN, LANE), lambda b, h, i: (b, kbase + h)),
                  pl.BlockSpec((MEM_LEN, LANE), lambda b, h, i: (b, kbase + HEADS + h)),
                  pl.BlockSpec((1, TQ_MEM, LANE), lambda b, h, i: (BLK_Z + 3 * HEADS + h, b * nq + i, 0))],
        out_specs=pl.BlockSpec((1, TQ_MEM, LANE), lambda b, h, i: (h, b * nq + i, 0)),
        compiler_params=_cparams(("parallel", "parallel", "parallel")),
        name="mem_attn",
    )(hb, kvm, kvm, hb)


def _out_kernel(ya_ref, yb_ref, yc_ref, yd_ref, w_ref, x_ref, g_ref, b_ref, o_ref):
    parts = [ref[h] for ref in (ya_ref, yb_ref, yc_ref, yd_ref) for h in range(HEADS)]
    y = jnp.concatenate(parts, axis=-1)
    r = DEEPNORM_ALPHA * x_ref[...] + _dot(y, w_ref[...])
    mu = jnp.mean(r, axis=-1, keepdims=True)
    var = jnp.mean(jnp.square(r - mu), axis=-1, keepdims=True)
    o_ref[...] = (r - mu) * lax.rsqrt(var + LN_EPS) * g_ref[...] + b_ref[...]


def _outproj(ya, yb, yc, yd, w_out, x2d, ln_g, ln_b, layer):
    yspec = pl.BlockSpec((HEADS, TM_OUT, LANE), lambda i: (0, i, 0))
    return pl.pallas_call(
        _out_kernel,
        out_shape=jax.ShapeDtypeStruct((ROWS, D_MODEL), F32),
        grid=(ROWS // TM_OUT,),
        in_specs=[yspec, yspec, yspec, yspec, _layer_spec(w_out.shape, layer),
                  pl.BlockSpec((TM_OUT, D_MODEL), lambda i: (i, 0)),
                  _layer_spec(ln_g.shape, layer), _layer_spec(ln_b.shape, layer)],
        out_specs=pl.BlockSpec((TM_OUT, D_MODEL), lambda i: (i, 0)),
        compiler_params=_cparams(("parallel",)),
        name="outproj_ln",
    )(ya, yb, yc, yd, w_out, x2d, ln_g, ln_b)


def kernel(x, mem, positions, w_in, ret_norm_g, mla_q_norm_g, mla_w_uq, mla_kv_norm_g, mla_w_ukv, ml_conv_w, ml_conv_b, ml_w_q, ml_w_k, ml_i_bias, ml_f_bias, ml_skip, ml_norm_g, w_mem_kv, w_out, ln_g, ln_b):
    assert x.shape == (BATCH, SEQ, D_MODEL) and mem.shape == (BATCH, MEM_LEN, D_MODEL)
    cos_r, sin_r, cos_m, sin_m = _rope_tables(positions)
    kvm = _mem_kv(mem, w_mem_kv)
    w_in_p = _prep_w_in(w_in)
    wuq_p, wukv_p = _prep_mla_weights(mla_w_uq, mla_w_ukv)
    w_out_b = w_out.astype(BF16)
    ret_consts = _ret_consts()
    ret_g = ret_norm_g.reshape(DEPTH, HEADS, 1, HEAD_DIM)
    mlstm_params = _prep_mlstm(ml_conv_w, ml_conv_b, ml_w_q, ml_w_k, ml_i_bias, ml_f_bias, ml_skip, ml_norm_g)
    qg = mla_q_norm_g.reshape(DEPTH, 1, MLA_Q_RANK)
    kvg = mla_kv_norm_g.reshape(DEPTH, 1, MLA_KV_RANK)
    lng = ln_g.reshape(DEPTH, 1, D_MODEL)
    lnb = ln_b.reshape(DEPTH, 1, D_MODEL)

    x2d = x.reshape(ROWS, D_MODEL)
    for l in range(DEPTH):
        hb = _inproj(x2d, w_in_p, l)
        ya = _retention(hb, cos_r, sin_r, ret_consts, ret_g, l)
        q, k, v = _mla_prep(hb, cos_m, sin_m, qg, wuq_p, kvg, wukv_p, l)
        yb = _mla_attn(q, k, v, hb)
        yc = _mlstm(hb, mlstm_params, l)
        yd = _mem_attn(hb, kvm, l)
        x2d = _outproj(ya, yb, yc, yd, w_out_b, x2d, lng, lnb, l)
    return x2d.reshape(BATCH, SEQ, D_MODEL)
```

```python
import numpy as np
import jax
import jax.numpy as jnp
from jax import lax
from jax.experimental import pallas as pl
from jax.experimental.pallas import tpu as pltpu

F32 = jnp.float32
BF16 = jnp.bfloat16

D_MODEL = 2048
BATCH = 8
SEQ = 2048
DEPTH = 4
MEM_LEN = 256
HEAD_DIM = 128
HEADS = 4
GROUP_WIDTH = HEADS * HEAD_DIM
MLA_NOPE = 128
MLA_ROPE = 64
MLA_Q_RANK = 384
MLA_KV_RANK = 128
CONV_WIDTH = 4
MIX_WIDTH = 4 * GROUP_WIDTH
ROPE_THETA = 10000.0
LN_EPS = 1e-5
RMS_EPS = 1e-6
DEEPNORM_ALPHA = (2 * DEPTH) ** 0.25
IN_SPLITS = (512, 512, 512, MLA_Q_RANK, MLA_KV_RANK, MLA_ROPE, 512, 512, 512, HEADS, HEADS, 512, MIX_WIDTH)

LANE = 128
ROWS = BATCH * SEQ

BLK_RQ, BLK_RK, BLK_RV = 0, 4, 8
BLK_LX, BLK_LV, BLK_LO = 12, 16, 20
BLK_MQ = 24
BLK_Z = 28
BLK_CQ, BLK_CKV = 44, 47
BLK_KPE = 48
NBLK = 50
GATE_I_LANE, GATE_F_LANE = 32, 36

TM_IN, TN_IN = 1024, 1280
TM_OUT = 512
TS_PREP = 512
TQ_ATT = 512
ATT_HEADS = 2
TQ_MEM = 1024
RCHUNK = 512
CONV_PAD = 8
STRIP = 64
LOG2E = 1.4426950408889634
VMEM_LIMIT = 56 * 1024 * 1024


def _cparams(sem):
    return pltpu.CompilerParams(dimension_semantics=sem, vmem_limit_bytes=VMEM_LIMIT)


def _silu(z):
    return z * jax.nn.sigmoid(z)


def _log_sigmoid(x):
    return jnp.minimum(x, 0.0) - jnp.log1p(jnp.exp(-jnp.abs(x)))


def _split3(x):
    x1 = x.astype(BF16)
    r1 = x - x1.astype(F32)
    x2 = r1.astype(BF16)
    x3 = (r1 - x2.astype(F32)).astype(BF16)
    return x1, x2, x3


def _dot(a, b):
    return jnp.dot(a, b, preferred_element_type=F32)


def _dot_nt(a, b):
    return lax.dot_general(a, b, (((1,), (1,)), ((), ())), preferred_element_type=F32)


def _tables_kernel(pos_ref, c_ref, cr_ref, sr_ref, cm_ref, sm_ref):
    pos = pos_ref[0].astype(F32)
    ang_r = pos * c_ref[0:1, :]
    cr_ref[0] = jnp.cos(ang_r)
    sr_ref[0] = jnp.sin(ang_r) * c_ref[1:2, :]
    ang_m = pos * c_ref[2:3, :]
    cm_ref[0] = jnp.cos(ang_m) * c_ref[3:4, :]
    sm_ref[0] = jnp.sin(ang_m) * c_ref[4:5, :]


def _rope_tables(positions):
    half_r = HEAD_DIM // 2
    fr = ROPE_THETA ** (-jnp.arange(half_r, dtype=F32) / half_r)
    half_m = MLA_ROPE // 2
    fm = ROPE_THETA ** (-jnp.arange(half_m, dtype=F32) / half_m)
    z32 = jnp.zeros((half_m,), F32)
    o32 = jnp.ones((half_m,), F32)
    rows = [
        jnp.concatenate([fr, fr]),
        jnp.concatenate([-jnp.ones((half_r,), F32), jnp.ones((half_r,), F32)]),
        jnp.concatenate([fm, z32, fm, z32]),
        jnp.concatenate([o32, z32, o32, z32]),
        jnp.concatenate([-o32, z32, o32, z32]),
    ]
    consts = jnp.concatenate([jnp.stack(rows), jnp.zeros((3, LANE), F32)], axis=0)
    ts = 512
    tab = jax.ShapeDtypeStruct((BATCH, SEQ, LANE), F32)
    spec = pl.BlockSpec((1, ts, LANE), lambda b, i: (b, i, 0))
    return pl.pallas_call(
        _tables_kernel,
        out_shape=(tab, tab, tab, tab),
        grid=(BATCH, SEQ // ts),
        in_specs=[pl.BlockSpec((1, ts, 1), lambda b, i: (b, i, 0)),
                  pl.BlockSpec((8, LANE), lambda b, i: (0, 0))],
        out_specs=(spec, spec, spec, spec),
        compiler_params=_cparams(("parallel", "parallel")),
        name="rope_tables",
    )(positions.reshape(BATCH, SEQ, 1), consts)


def _matmul_kernel(a_ref, w_ref, o_ref):
    o_ref[...] = _dot(a_ref[...].astype(BF16), w_ref[...]).astype(o_ref.dtype)


def _mem_kv(mem, w_mem_kv):
    w = jnp.transpose(w_mem_kv, (1, 0, 2)).reshape(D_MODEL, DEPTH * 2 * GROUP_WIDTH).astype(BF16)
    a = mem.reshape(BATCH * MEM_LEN, D_MODEL)
    tm, tn = 512, 1024
    return pl.pallas_call(
        _matmul_kernel,
        out_shape=jax.ShapeDtypeStruct((BATCH * MEM_LEN, DEPTH * 2 * GROUP_WIDTH), BF16),
        grid=(BATCH * MEM_LEN // tm, DEPTH * 2 * GROUP_WIDTH // tn),
        in_specs=[pl.BlockSpec((tm, D_MODEL), lambda i, j: (i, 0)),
                  pl.BlockSpec((D_MODEL, tn), lambda i, j: (0, j))],
        out_specs=pl.BlockSpec((tm, tn), lambda i, j: (i, j)),
        compiler_params=_cparams(("parallel", "parallel")),
        name="mem_kv",
    )(a, w)


def _inproj_kernel(x_ref, w_ref, o_ref, xb_ref):
    @pl.when(pl.program_id(1) == 0)
    def _():
        xb_ref[...] = x_ref[...].astype(BF16)

    xb = xb_ref[...]
    for k2 in range(TN_IN // 256):
        r = _dot(xb, w_ref[:, k2 * 256:(k2 + 1) * 256])
        o_ref[2 * k2] = r[:, :LANE]
        o_ref[2 * k2 + 1] = r[:, LANE:]


def _inproj(x2d, w_p, layer):
    nb = TN_IN // LANE
    return pl.pallas_call(
        _inproj_kernel,
        out_shape=jax.ShapeDtypeStruct((NBLK, ROWS, LANE), F32),
        grid=(ROWS // TM_IN, NBLK * LANE // TN_IN),
        in_specs=[pl.BlockSpec((TM_IN, D_MODEL), lambda i, j: (i, 0)),
                  pl.BlockSpec((None, D_MODEL, TN_IN), lambda i, j: (layer, 0, j))],
        out_specs=pl.BlockSpec((nb, TM_IN, LANE), lambda i, j: (j, i, 0)),
        scratch_shapes=[pltpu.VMEM((TM_IN, D_MODEL), BF16)],
        compiler_params=_cparams(("parallel", "arbitrary")),
        name="inproj",
    )(x2d, w_p)


def _prep_w_in(w_in):
    idx = np.cumsum(IN_SPLITS)[:-1].tolist()
    (r_q, r_k, r_v, a_cq, a_ckv, a_kpe, l_x, l_v, l_o, l_i, l_f, c_q, z) = jnp.split(w_in, idx, axis=-1)
    half = MLA_ROPE // 2
    lead = w_in.shape[:-1]
    blk = jnp.concatenate([
        a_kpe[..., :half], l_i, l_f, jnp.zeros(lead + (64 - half - 2 * HEADS,), w_in.dtype),
        a_kpe[..., half:], jnp.zeros(lead + (64 - half,), w_in.dtype)], axis=-1)
    pad = jnp.zeros(lead + (LANE,), w_in.dtype)
    return jnp.concatenate([r_q, r_k, r_v, l_x, l_v, l_o, c_q, z, a_cq, a_ckv, blk, pad], axis=-1).astype(BF16)


def _group_spec(base, rows):
    nchunk = SEQ // rows
    return pl.BlockSpec((HEADS, rows, LANE), lambda b, c: (base // HEADS, b * nchunk + c, 0))


def _layer_spec(shape, layer):
    nd = len(shape) - 1
    return pl.BlockSpec((None,) + tuple(shape[1:]), lambda *_: (layer,) + (0,) * nd)


def _const_spec(shape):
    nd = len(shape)
    return pl.BlockSpec(tuple(shape), lambda *_: (0,) * nd)


def _head_norm_gate(h, g, z):
    mu = jnp.mean(h, axis=-1, keepdims=True)
    var = jnp.mean(jnp.square(h - mu), axis=-1, keepdims=True)
    return (h - mu) * lax.rsqrt(var + LN_EPS) * g, _silu(z)


def _ret_kernel(q_ref, k_ref, v_ref, z_ref, cos_ref, sin_ref, gq_ref, gk_ref, gc_ref, g_ref, o_ref, st_ref):
    @pl.when(pl.program_id(1) == 0)
    def _():
        st_ref[...] = jnp.zeros_like(st_ref)

    cs = cos_ref[0]
    sn = sin_ref[0]
    row = lax.broadcasted_iota(jnp.int32, (RCHUNK, RCHUNK), 0)
    col = lax.broadcasted_iota(jnp.int32, (RCHUNK, RCHUNK), 1)
    causal = col <= row
    for h in range(HEADS):
        q = q_ref[h]
        q = (q * cs + pltpu.roll(q, HEAD_DIM // 2, 1) * sn) * gq_ref[h]
        k = k_ref[h]
        k = (k * cs + pltpu.roll(k, HEAD_DIM // 2, 1) * sn) * HEAD_DIM ** -0.5 * gk_ref[h]
        qb = q.astype(BF16)
        kb = k.astype(BF16)
        vb = v_ref[h].astype(BF16)
        sc = jnp.where(causal, _dot_nt(qb, kb), 0.0)
        st = st_ref[h]
        out = _dot(sc.astype(BF16), vb) + _dot(qb, st.astype(BF16))
        st_ref[h] = gc_ref[h] * (st + _dot(k.T.astype(BF16), vb))
        hn, gate = _head_norm_gate(out, g_ref[h], z_ref[h])
        o_ref[h] = (hn * gate).astype(BF16)


def _ret_consts():
    log_g = jnp.log1p(-jnp.exp2(-5.0 - jnp.arange(HEADS, dtype=F32)))
    idx = jnp.arange(RCHUNK, dtype=F32)
    full = (HEADS, RCHUNK, HEAD_DIM)
    gq = jnp.broadcast_to(jnp.exp(log_g[:, None] * (idx + 1.0))[..., None], full)
    gk = jnp.broadcast_to(jnp.exp(-log_g[:, None] * (idx + 1.0))[..., None], full)
    gc = jnp.broadcast_to(jnp.exp(log_g * RCHUNK)[:, None, None], (HEADS, 1, HEAD_DIM))
    return gq, gk, gc


def _retention(hb, cos_r, sin_r, consts, norm_g, layer):
    gq, gk, gc = consts
    nchunk = SEQ // RCHUNK
    tab = pl.BlockSpec((1, RCHUNK, LANE), lambda b, c: (b, c, 0))
    return pl.pallas_call(
        _ret_kernel,
        out_shape=jax.ShapeDtypeStruct((HEADS, ROWS, LANE), BF16),
        grid=(BATCH, nchunk),
        in_specs=[_group_spec(BLK_RQ, RCHUNK), _group_spec(BLK_RK, RCHUNK), _group_spec(BLK_RV, RCHUNK),
                  _group_spec(BLK_Z, RCHUNK), tab, tab,
                  _const_spec(gq.shape), _const_spec(gk.shape), _const_spec(gc.shape),
                  _layer_spec(norm_g.shape, layer)],
        out_specs=pl.BlockSpec((HEADS, RCHUNK, LANE), lambda b, c: (0, b * nchunk + c, 0)),
        scratch_shapes=[pltpu.VMEM((HEADS, HEAD_DIM, HEAD_DIM), F32)],
        compiler_params=_cparams(("parallel", "arbitrary")),
        name="retention",
    )(hb, hb, hb, hb, cos_r, sin_r, gq, gk, gc, norm_g)


def _mlstm_kernel(lx_ref, lv_ref, lo_ref, gt_ref, z_ref, cw_ref, cb_ref, wq_ref, wk_ref, gb_ref,
                  skip_ref, g_ref, o_ref, xp_ref, c_ref, n_ref, m_ref, s_ref, p_ref):
    @pl.when(pl.program_id(1) == 0)
    def _():
        xp_ref[:, 0:CONV_PAD, :] = jnp.zeros((HEADS, CONV_PAD, HEAD_DIM), F32)
        c_ref[...] = jnp.zeros_like(c_ref)
        n_ref[...] = jnp.zeros_like(n_ref)
        m_ref[...] = jnp.zeros_like(m_ref)

    n = RCHUNK
    row = lax.broadcasted_iota(jnp.int32, (n, n), 0)
    col = lax.broadcasted_iota(jnp.int32, (n, n), 1)
    tril = jnp.where(col <= row, 1.0, 0.0).astype(BF16)
    r128 = lax.broadcasted_iota(jnp.int32, (LANE, LANE), 0)
    c128 = lax.broadcasted_iota(jnp.int32, (LANE, LANE), 1)
    ident = jnp.where(r128 == c128, 1.0, 0.0).astype(BF16)
    srow = lax.broadcasted_iota(jnp.int32, (STRIP, LANE), 0)
    scol = lax.broadcasted_iota(jnp.int32, (STRIP, LANE), 1)

    lane = lax.broadcasted_iota(jnp.int32, (n, LANE), 1)
    x = gt_ref[0] + gb_ref[...]
    x = jnp.where((lane >= GATE_F_LANE) & (lane < GATE_F_LANE + HEADS), _log_sigmoid(x), x) * LOG2E
    x1, x2, x3 = _split3(x)
    cum = _dot(tril, x1) + _dot(tril, x2) + _dot(tril, x3)
    rt = x - pltpu.roll(cum, LANE - (GATE_F_LANE - GATE_I_LANE), 1)
    r1, r2, r3 = _split3(rt)
    rtt = _dot_nt(ident, r1) + _dot_nt(ident, r2) + _dot_nt(ident, r3)

    for h in range(HEADS):
        xp_ref[h, CONV_PAD:, :] = lx_ref[h]
        acc = jnp.zeros((n, HEAD_DIM), F32) + cb_ref[h]
        for j in range(CONV_WIDTH):
            off = CONV_PAD - (CONV_WIDTH - 1) + j
            acc = acc + xp_ref[h, off:off + n, :] * cw_ref[h, j:j + 1, :]
        xp_ref[h, 0:CONV_PAD, :] = lx_ref[h, n - CONV_PAD:n, :]
        xc = _silu(acc)
        xcb = xc.astype(BF16)
        q = _dot(xcb, wq_ref[h])
        k = _dot(xcb, wk_ref[h]) * HEAD_DIM ** -0.5
        qb = q.astype(BF16)
        kb = k.astype(BF16)
        vb = lv_ref[h].astype(BF16)
        s_ref[h] = _dot_nt(qb, kb)

        li = GATE_I_LANE + h
        lf = GATE_F_LANE + h
        r_row = rtt[li:li + 1, :]
        r_col = rt[:, li:li + 1]
        cum_col = cum[:, lf:lf + 1]
        m_st = m_ref[h]

        base = m_st
        us, dens = [], []
        for i in range(n // STRIP):
            rows = slice(i * STRIP, (i + 1) * STRIP)
            d0 = (i * STRIP // LANE) * LANE
            w = d0 + LANE
            if d0 > 0 and (i * STRIP) % LANE == 0:
                base = jnp.maximum(base, jnp.max(r_row[:, d0 - LANE:d0], axis=-1, keepdims=True))
            mask = (scol + d0) <= (srow + i * STRIP)
            rmd = jnp.where(mask, r_row[:, d0:w], -jnp.inf)
            u = jnp.maximum(jnp.max(rmd, axis=-1, keepdims=True), base)
            sd = s_ref[h, rows, d0:w] * jnp.exp2(rmd - u)
            den = jnp.sum(sd, axis=-1, keepdims=True)
            p_ref[h, rows, d0:w] = sd.astype(BF16)
            if d0 > 0:
                sl = s_ref[h, rows, 0:d0] * jnp.exp2(r_row[:, 0:d0] - u)
                den = den + jnp.sum(sl, axis=-1, keepdims=True)
                p_ref[h, rows, 0:d0] = sl.astype(BF16)
            if w < n:
                p_ref[h, rows, w:n] = jnp.zeros((STRIP, n - w), BF16)
            us.append(u)
            dens.append(den)
        u = jnp.concatenate(us, axis=0)
        dsum = jnp.concatenate(dens, axis=0)

        w_inter = jnp.exp2(m_st - u)
        c_st = c_ref[h]
        n_st = n_ref[h]
        num = _dot(p_ref[h], vb) + w_inter * _dot(qb, c_st.astype(BF16))
        den = dsum + w_inter * jnp.sum(q * n_st, axis=-1, keepdims=True)
        hcell = num / jnp.maximum(jnp.abs(den), jnp.exp2(-(cum_col + u)))

        u_last = u[n - 1:n, :]
        decay = jnp.exp2(m_st - u_last)
        kw = k * jnp.exp2(r_col - u_last)
        c_ref[h] = decay * c_st + _dot(kw.T.astype(BF16), vb)
        n_ref[h] = decay * n_st + jnp.sum(kw, axis=0, keepdims=True)
        m_ref[h] = cum_col[n - 1:n, :] + u_last

        cell = hcell * jax.nn.sigmoid(lo_ref[h])
        hn, gate = _head_norm_gate(cell, g_ref[h], z_ref[h])
        o_ref[h] = ((hn + skip_ref[h] * xc) * gate).astype(BF16)


def _prep_mlstm(conv_w, conv_b, w_q, w_k, i_bias, f_bias, skip, norm_g):
    nl = conv_w.shape[0]
    cw = conv_w.reshape(nl, CONV_WIDTH, HEADS, HEAD_DIM).transpose(0, 2, 1, 3)
    cb = conv_b.reshape(nl, HEADS, 1, HEAD_DIM)
    zeros = lambda w: jnp.zeros((nl, w), F32)
    gb = jnp.concatenate([zeros(GATE_I_LANE), i_bias, f_bias, zeros(LANE - GATE_F_LANE - HEADS)], axis=-1)
    return (cw, cb, w_q.astype(BF16), w_k.astype(BF16), gb.reshape(nl, 1, LANE),
            skip.reshape(nl, HEADS, 1, HEAD_DIM), norm_g.reshape(nl, HEADS, 1, HEAD_DIM))


def _mlstm(hb, params, layer):
    nchunk = SEQ // RCHUNK
    return pl.pallas_call(
        _mlstm_kernel,
        out_shape=jax.ShapeDtypeStruct((HEADS, ROWS, LANE), BF16),
        grid=(BATCH, nchunk),
        in_specs=[_group_spec(BLK_LX, RCHUNK), _group_spec(BLK_LV, RCHUNK), _group_spec(BLK_LO, RCHUNK),
                  pl.BlockSpec((1, RCHUNK, LANE), lambda b, c: (BLK_KPE, b * nchunk + c, 0)),
                  _group_spec(BLK_Z + 2 * HEADS, RCHUNK)] + [_layer_spec(p.shape, layer) for p in params],
        out_specs=pl.BlockSpec((HEADS, RCHUNK, LANE), lambda b, c: (0, b * nchunk + c, 0)),
        scratch_shapes=[pltpu.VMEM((HEADS, RCHUNK + CONV_PAD, HEAD_DIM), F32),
                        pltpu.VMEM((HEADS, HEAD_DIM, HEAD_DIM), F32),
                        pltpu.VMEM((HEADS, 1, HEAD_DIM), F32),
                        pltpu.VMEM((HEADS, 1, 1), F32),
                        pltpu.VMEM((HEADS, RCHUNK, RCHUNK), F32),
                        pltpu.VMEM((HEADS, RCHUNK, RCHUNK), BF16)],
        compiler_params=_cparams(("parallel", "arbitrary")),
        name="mlstm",
    )(hb, hb, hb, hb, hb, *params)


def _mla_prep_kernel(lat_ref, kpe_ref, cm_ref, sm_ref, qg_ref, wuq_ref, kvg_ref, wukv_ref,
                     q_out, k_out, v_out):
    scale = (MLA_NOPE + MLA_ROPE) ** -0.5 * LOG2E
    cs = cm_ref[0]
    sn = sm_ref[0]
    cq = jnp.concatenate([lat_ref[0], lat_ref[1], lat_ref[2]], axis=-1)
    qn = cq * lax.rsqrt(jnp.mean(jnp.square(cq), axis=-1, keepdims=True) + RMS_EPS) * qg_ref[...]
    q = _dot(qn.astype(BF16), wuq_ref[...])
    ckv = lat_ref[3]
    kvn = ckv * lax.rsqrt(jnp.mean(jnp.square(ckv), axis=-1, keepdims=True) + RMS_EPS) * kvg_ref[...]
    kv = _dot(kvn.astype(BF16), wukv_ref[...])
    kpe = kpe_ref[0]
    krot = (kpe * cs + pltpu.roll(kpe, LANE // 2, 1) * sn).astype(BF16)
    for h in range(HEADS):
        qr = q[:, GROUP_WIDTH + h * LANE:GROUP_WIDTH + (h + 1) * LANE]
        qr = qr * cs + pltpu.roll(qr, LANE // 2, 1) * sn
        q_out[0, h, :, 0:LANE] = (q[:, h * LANE:(h + 1) * LANE] * scale).astype(BF16)
        q_out[0, h, :, LANE:2 * LANE] = (qr * scale).astype(BF16)
        k_out[0, h, :, 0:LANE] = kv[:, h * LANE:(h + 1) * LANE].astype(BF16)
        k_out[0, h, :, LANE:2 * LANE] = krot
        v_out[0, h, :, 0:LANE] = kv[:, GROUP_WIDTH + h * LANE:GROUP_WIDTH + (h + 1) * LANE].astype(BF16)
        v_out[0, h, :, LANE:2 * LANE] = jnp.ones((TS_PREP, LANE), BF16)


def _prep_mla_weights(w_uq, w_ukv):
    lead = w_uq.shape[:-1]
    wq = w_uq.reshape(lead + (HEADS, MLA_NOPE + MLA_ROPE))
    nope = wq[..., :MLA_NOPE].reshape(lead + (GROUP_WIDTH,))
    half = MLA_ROPE // 2
    zeros = jnp.zeros(lead + (HEADS, 64 - half), w_uq.dtype)
    rope = jnp.concatenate([wq[..., MLA_NOPE:MLA_NOPE + half], zeros, wq[..., MLA_NOPE + half:], zeros], axis=-1)
    wq_p = jnp.concatenate([nope, rope.reshape(lead + (HEADS * LANE,))], axis=-1).astype(BF16)
    lead = w_ukv.shape[:-1]
    wkv = w_ukv.reshape(lead + (HEADS, MLA_NOPE + HEAD_DIM))
    wkv_p = jnp.concatenate([wkv[..., :MLA_NOPE].reshape(lead + (GROUP_WIDTH,)),
                             wkv[..., MLA_NOPE:].reshape(lead + (GROUP_WIDTH,))], axis=-1).astype(BF16)
    return wq_p, wkv_p


def _mla_prep(hb, cos_m, sin_m, qg, wuq_p, kvg, wukv_p, layer):
    ns = SEQ // TS_PREP
    tab = pl.BlockSpec((1, TS_PREP, LANE), lambda b, i: (b, i, 0))
    qk_shape = jax.ShapeDtypeStruct((BATCH, HEADS, SEQ, 2 * LANE), BF16)
    return pl.pallas_call(
        _mla_prep_kernel,
        out_shape=(qk_shape, qk_shape, qk_shape),
        grid=(BATCH, ns),
        in_specs=[pl.BlockSpec((HEADS, TS_PREP, LANE), lambda b, i: (BLK_CQ // HEADS, b * ns + i, 0)),
                  pl.BlockSpec((1, TS_PREP, LANE), lambda b, i: (BLK_KPE, b * ns + i, 0)),
                  tab, tab, _layer_spec(qg.shape, layer), _layer_spec(wuq_p.shape, layer),
                  _layer_spec(kvg.shape, layer), _layer_spec(wukv_p.shape, layer)],
        out_specs=(pl.BlockSpec((1, HEADS, TS_PREP, 2 * LANE), lambda b, i: (b, 0, i, 0)),
                   pl.BlockSpec((1, HEADS, TS_PREP, 2 * LANE), lambda b, i: (b, 0, i, 0)),
                   pl.BlockSpec((1, HEADS, TS_PREP, 2 * LANE), lambda b, i: (b, 0, i, 0))),
        compiler_params=_cparams(("parallel", "parallel")),
        name="mla_prep",
    )(hb, hb, cos_m, sin_m, qg, wuq_p, kvg, wukv_p)


def _attn_tile(nfull, q_ref, k_ref, v_ref, z_ref, o_ref, s_ref, p_ref, m_ref):
    kvlen = (nfull + 1) * TQ_ATT
    d0 = nfull * TQ_ATT
    srow = lax.broadcasted_iota(jnp.int32, (STRIP, LANE), 0)
    scol = lax.broadcasted_iota(jnp.int32, (STRIP, LANE), 1)

    def strip_blocks(g, i):
        rows = slice(i * STRIP, (i + 1) * STRIP)
        wd = -(-(i + 1) * STRIP // LANE) * LANE
        ncol = (d0 + wd) // LANE
        blks = [s_ref[g, rows, c * LANE:(c + 1) * LANE] for c in range(ncol)]
        blks[-1] = jnp.where(scol + (wd - LANE) <= srow + i * STRIP, blks[-1], -jnp.inf)
        return rows, ncol, blks

    for g in range(ATT_HEADS):
        q = q_ref[0, g]
        for j in range(nfull + 1):
            s_ref[g, :, j * TQ_ATT:(j + 1) * TQ_ATT] = _dot_nt(q, k_ref[0, g, j * TQ_ATT:(j + 1) * TQ_ATT, :])
    for g in range(ATT_HEADS):
        for i in range(TQ_ATT // STRIP):
            rows, ncol, blks = strip_blocks(g, i)
            mx = blks[0]
            for blk in blks[1:]:
                mx = jnp.maximum(mx, blk)
            m_ref[g, rows, :] = jnp.broadcast_to(jnp.max(mx, axis=-1, keepdims=True), (STRIP, LANE))
        for i in range(TQ_ATT // STRIP):
            rows, ncol, blks = strip_blocks(g, i)
            m = m_ref[g, rows, :]
            for c, blk in enumerate(blks):
                p_ref[g, rows, c * LANE:(c + 1) * LANE] = jnp.exp2(blk - m).astype(BF16)
            if ncol * LANE < kvlen:
                p_ref[g, rows, ncol * LANE:kvlen] = jnp.zeros((STRIP, kvlen - ncol * LANE), BF16)
        pv = _dot(p_ref[g, :, 0:kvlen], v_ref[0, g, 0:kvlen, :])
        o_ref[g] = (pv[:, :LANE] / pv[:, LANE:] * _silu(z_ref[g])).astype(BF16)


def _mla_attn_kernel(q_ref, k_ref, v_ref, z_ref, o_ref, s_ref, p_ref, m_ref):
    qi = pl.program_id(2)
    for nfull in range(SEQ // TQ_ATT):
        @pl.when(qi == nfull)
        def _(nfull=nfull):
            _attn_tile(nfull, q_ref, k_ref, v_ref, z_ref, o_ref, s_ref, p_ref, m_ref)


def _mla_attn(q, k, v, hb):
    nq = SEQ // TQ_ATT
    return pl.pallas_call(
        _mla_attn_kernel,
        out_shape=jax.ShapeDtypeStruct((HEADS, ROWS, LANE), BF16),
        grid=(BATCH, HEADS // ATT_HEADS, nq),
        in_specs=[pl.BlockSpec((1, ATT_HEADS, TQ_ATT, 2 * LANE), lambda b, h, i: (b, h, i, 0)),
                  pl.BlockSpec((1, ATT_HEADS, SEQ, 2 * LANE), lambda b, h, i: (b, h, 0, 0)),
                  pl.BlockSpec((1, ATT_HEADS, SEQ, 2 * LANE), lambda b, h, i: (b, h, 0, 0)),
                  pl.BlockSpec((ATT_HEADS, TQ_ATT, LANE),
                               lambda b, h, i: ((BLK_Z + HEADS) // ATT_HEADS + h, b * nq + i, 0))],
        out_specs=pl.BlockSpec((ATT_HEADS, TQ_ATT, LANE), lambda b, h, i: (h, b * nq + i, 0)),
        scratch_shapes=[pltpu.VMEM((ATT_HEADS, TQ_ATT, SEQ), F32), pltpu.VMEM((ATT_HEADS, TQ_ATT, SEQ), BF16),
                        pltpu.VMEM((ATT_HEADS, TQ_ATT, LANE), F32)],
        compiler_params=_cparams(("parallel", "parallel", "parallel")),
        name="mla_attn",
    )(q, k, v, hb)


def _mem_attn_kernel(q_ref, k_ref, v_ref, z_ref, o_ref):
    q = (q_ref[0] * HEAD_DIM ** -0.5).astype(BF16)
    s = _dot_nt(q, k_ref[...])
    e = jnp.exp(s - jnp.max(s, axis=-1, keepdims=True))
    p = e / jnp.sum(e, axis=-1, keepdims=True)
    out = _dot(p.astype(BF16), v_ref[...])
    o_ref[0] = (out * _silu(z_ref[0])).astype(BF16)


def _mem_attn(hb, kvm, layer):
    nq = SEQ // TQ_MEM
    kbase = layer * 2 * HEADS
    return pl.pallas_call(
        _mem_attn_kernel,
        out_shape=jax.ShapeDtypeStruct((HEADS, ROWS, LANE), BF16),
        grid=(BATCH, HEADS, nq),
        in_specs=[pl.BlockSpec((1, TQ_MEM, LANE), lambda b, h, i: (BLK_MQ + h, b * nq + i, 0)),
                  pl.BlockSpec((MEM_LEN, LANE), lambda b, h, i: (b, kbase + h)),
                  pl.BlockSpec((MEM_LEN, LANE), lambda b, h, i: (b, kbase + HEADS + h)),
                  pl.BlockSpec((1, TQ_MEM, LANE), lambda b, h, i: (BLK_Z + 3 * HEADS + h, b * nq + i, 0))],
        out_specs=pl.BlockSpec((1, TQ_MEM, LANE), lambda b, h, i: (h, b * nq + i, 0)),
        compiler_params=_cparams(("parallel", "parallel", "parallel")),
        name="mem_attn",
    )(hb, kvm, kvm, hb)


def _out_kernel(ya_ref, yb_ref, yc_ref, yd_ref, w_ref, x_ref, g_ref, b_ref, o_ref):
    parts = [ref[h] for ref in (ya_ref, yb_ref, yc_ref, yd_ref) for h in range(HEADS)]
    y = jnp.concatenate(parts, axis=-1)
    r = DEEPNORM_ALPHA * x_ref[...] + _dot(y, w_ref[...])
    mu = jnp.mean(r, axis=-1, keepdims=True)
    var = jnp.mean(jnp.square(r - mu), axis=-1, keepdims=True)
    o_ref[...] = (r - mu) * lax.rsqrt(var + LN_EPS) * g_ref[...] + b_ref[...]


def _outproj(ya, yb, yc, yd, w_out, x2d, ln_g, ln_b, layer):
    yspec = pl.BlockSpec((HEADS, TM_OUT, LANE), lambda i: (0, i, 0))
    return pl.pallas_call(
        _out_kernel,
        out_shape=jax.ShapeDtypeStruct((ROWS, D_MODEL), F32),
        grid=(ROWS // TM_OUT,),
        in_specs=[yspec, yspec, yspec, yspec, _layer_spec(w_out.shape, layer),
                  pl.BlockSpec((TM_OUT, D_MODEL), lambda i: (i, 0)),
                  _layer_spec(ln_g.shape, layer), _layer_spec(ln_b.shape, layer)],
        out_specs=pl.BlockSpec((TM_OUT, D_MODEL), lambda i: (i, 0)),
        compiler_params=_cparams(("parallel",)),
        name="outproj_ln",
    )(ya, yb, yc, yd, w_out, x2d, ln_g, ln_b)


def kernel(x, mem, positions, w_in, ret_norm_g, mla_q_norm_g, mla_w_uq, mla_kv_norm_g, mla_w_ukv, ml_conv_w, ml_conv_b, ml_w_q, ml_w_k, ml_i_bias, ml_f_bias, ml_skip, ml_norm_g, w_mem_kv, w_out, ln_g, ln_b):
    assert x.shape == (BATCH, SEQ, D_MODEL) and mem.shape == (BATCH, MEM_LEN, D_MODEL)
    cos_r, sin_r, cos_m, sin_m = _rope_tables(positions)
    kvm = _mem_kv(mem, w_mem_kv)
    w_in_p = _prep_w_in(w_in)
    wuq_p, wukv_p = _prep_mla_weights(mla_w_uq, mla_w_ukv)
    w_out_b = w_out.astype(BF16)
    ret_consts = _ret_consts()
    ret_g = ret_norm_g.reshape(DEPTH, HEADS, 1, HEAD_DIM)
    mlstm_params = _prep_mlstm(ml_conv_w, ml_conv_b, ml_w_q, ml_w_k, ml_i_bias, ml_f_bias, ml_skip, ml_norm_g)
    qg = mla_q_norm_g.reshape(DEPTH, 1, MLA_Q_RANK)
    kvg = mla_kv_norm_g.reshape(DEPTH, 1, MLA_KV_RANK)
    lng = ln_g.reshape(DEPTH, 1, D_MODEL)
    lnb = ln_b.reshape(DEPTH, 1, D_MODEL)

    x2d = x.reshape(ROWS, D_MODEL)
    for l in range(DEPTH):
        hb = _inproj(x2d, w_in_p, l)
        ya = _retention(hb, cos_r, sin_r, ret_consts, ret_g, l)
        q, k, v = _mla_prep(hb, cos_m, sin_m, qg, wuq_p, kvg, wukv_p, l)
        yb = _mla_attn(q, k, v, hb)
        yc = _mlstm(hb, mlstm_params, l)
        yd = _mem_attn(hb, kvm, l)
        x2d = _outproj(ya, yb, yc, yd, w_out_b, x2d, lng, lnb, l)
    return x2d.reshape(BATCH, SEQ, D_MODEL)
```

```python
import numpy as np
import jax
import jax.numpy as jnp
from jax import lax
from jax.experimental import pallas as pl
from jax.experimental.pallas import tpu as pltpu

F32 = jnp.float32
BF16 = jnp.bfloat16

D_MODEL = 2048
BATCH = 8
SEQ = 2048
DEPTH = 4
MEM_LEN = 256
HEAD_DIM = 128
HEADS = 4
GROUP_WIDTH = HEADS * HEAD_DIM
MLA_NOPE = 128
MLA_ROPE = 64
MLA_Q_RANK = 384
MLA_KV_RANK = 128
CONV_WIDTH = 4
MIX_WIDTH = 4 * GROUP_WIDTH
ROPE_THETA = 10000.0
LN_EPS = 1e-5
RMS_EPS = 1e-6
DEEPNORM_ALPHA = (2 * DEPTH) ** 0.25
IN_SPLITS = (512, 512, 512, MLA_Q_RANK, MLA_KV_RANK, MLA_ROPE, 512, 512, 512, HEADS, HEADS, 512, MIX_WIDTH)

LANE = 128
ROWS = BATCH * SEQ

BLK_RQ, BLK_RK, BLK_RV = 0, 4, 8
BLK_LX, BLK_LV, BLK_LO = 12, 16, 20
BLK_MQ = 24
BLK_Z = 28
BLK_CQ, BLK_CKV = 44, 47
BLK_KPE = 48
NBLK = 50
GATE_I_LANE, GATE_F_LANE = 32, 36

TM_IN, TN_IN = 1024, 1280
TM_OUT = 512
SUB_OUT = 256
TS_PREP = 512
TQ_ATT = 512
ATT_HEADS = 2
TQ_MEM = 512
RCHUNK = 512
CONV_PAD = 8
STRIP = 64
LOG2E = 1.4426950408889634
VMEM_LIMIT = 56 * 1024 * 1024


def _cparams(sem):
    return pltpu.CompilerParams(dimension_semantics=sem, vmem_limit_bytes=VMEM_LIMIT)


def _silu(z):
    return z * jax.nn.sigmoid(z)


def _log_sigmoid(x):
    return jnp.minimum(x, 0.0) - jnp.log1p(jnp.exp(-jnp.abs(x)))


def _split3(x):
    x1 = x.astype(BF16)
    r1 = x - x1.astype(F32)
    x2 = r1.astype(BF16)
    x3 = (r1 - x2.astype(F32)).astype(BF16)
    return x1, x2, x3


def _dot(a, b):
    return jnp.dot(a, b, preferred_element_type=F32)


def _dot_nt(a, b):
    return lax.dot_general(a, b, (((1,), (1,)), ((), ())), preferred_element_type=F32)


def _tables_kernel(pos_ref, c_ref, cr_ref, sr_ref, cm_ref, sm_ref):
    pos = pos_ref[0].astype(F32)
    ang_r = pos * c_ref[0:1, :]
    cr_ref[0] = jnp.cos(ang_r)
    sr_ref[0] = jnp.sin(ang_r) * c_ref[1:2, :]
    ang_m = pos * c_ref[2:3, :]
    cm_ref[0] = jnp.cos(ang_m) * c_ref[3:4, :]
    sm_ref[0] = jnp.sin(ang_m) * c_ref[4:5, :]


def _rope_tables(positions):
    half_r = HEAD_DIM // 2
    fr = ROPE_THETA ** (-jnp.arange(half_r, dtype=F32) / half_r)
    half_m = MLA_ROPE // 2
    fm = ROPE_THETA ** (-jnp.arange(half_m, dtype=F32) / half_m)
    z32 = jnp.zeros((half_m,), F32)
    o32 = jnp.ones((half_m,), F32)
    rows = [
        jnp.concatenate([fr, fr]),
        jnp.concatenate([-jnp.ones((half_r,), F32), jnp.ones((half_r,), F32)]),
        jnp.concatenate([fm, z32, fm, z32]),
        jnp.concatenate([o32, z32, o32, z32]),
        jnp.concatenate([-o32, z32, o32, z32]),
    ]
    consts = jnp.concatenate([jnp.stack(rows), jnp.zeros((3, LANE), F32)], axis=0)
    ts = 512
    tab = jax.ShapeDtypeStruct((BATCH, SEQ, LANE), F32)
    spec = pl.BlockSpec((1, ts, LANE), lambda b, i: (b, i, 0))
    return pl.pallas_call(
        _tables_kernel,
        out_shape=(tab, tab, tab, tab),
        grid=(BATCH, SEQ // ts),
        in_specs=[pl.BlockSpec((1, ts, 1), lambda b, i: (b, i, 0)),
                  pl.BlockSpec((8, LANE), lambda b, i: (0, 0))],
        out_specs=(spec, spec, spec, spec),
        compiler_params=_cparams(("parallel", "parallel")),
        name="rope_tables",
    )(positions.reshape(BATCH, SEQ, 1), consts)


def _matmul_kernel(a_ref, w_ref, o_ref):
    o_ref[...] = _dot(a_ref[...].astype(BF16), w_ref[...]).astype(o_ref.dtype)


def _mem_kv(mem, w_mem_kv):
    w = jnp.transpose(w_mem_kv, (1, 0, 2)).reshape(D_MODEL, DEPTH * 2 * GROUP_WIDTH).astype(BF16)
    a = mem.reshape(BATCH * MEM_LEN, D_MODEL)
    tm, tn = 512, 1024
    return pl.pallas_call(
        _matmul_kernel,
        out_shape=jax.ShapeDtypeStruct((BATCH * MEM_LEN, DEPTH * 2 * GROUP_WIDTH), BF16),
        grid=(BATCH * MEM_LEN // tm, DEPTH * 2 * GROUP_WIDTH // tn),
        in_specs=[pl.BlockSpec((tm, D_MODEL), lambda i, j: (i, 0)),
                  pl.BlockSpec((D_MODEL, tn), lambda i, j: (0, j))],
        out_specs=pl.BlockSpec((tm, tn), lambda i, j: (i, j)),
        compiler_params=_cparams(("parallel", "parallel")),
        name="mem_kv",
    )(a, w)


def _inproj_kernel(x_ref, w_ref, o_ref, xb_ref):
    @pl.when(pl.program_id(1) == 0)
    def _():
        xb_ref[...] = x_ref[...].astype(BF16)

    xb = xb_ref[...]
    for k2 in range(TN_IN // 256):
        r = _dot(xb, w_ref[:, k2 * 256:(k2 + 1) * 256])
        o_ref[2 * k2] = r[:, :LANE]
        o_ref[2 * k2 + 1] = r[:, LANE:]


def _inproj(x2d, w_p, layer):
    nb = TN_IN // LANE
    return pl.pallas_call(
        _inproj_kernel,
        out_shape=jax.ShapeDtypeStruct((NBLK, ROWS, LANE), F32),
        grid=(ROWS // TM_IN, NBLK * LANE // TN_IN),
        in_specs=[pl.BlockSpec((TM_IN, D_MODEL), lambda i, j: (i, 0)),
                  pl.BlockSpec((None, D_MODEL, TN_IN), lambda i, j: (layer, 0, j))],
        out_specs=pl.BlockSpec((nb, TM_IN, LANE), lambda i, j: (j, i, 0)),
        scratch_shapes=[pltpu.VMEM((TM_IN, D_MODEL), BF16)],
        compiler_params=_cparams(("parallel", "arbitrary")),
        name="inproj",
    )(x2d, w_p)


def _w_in_kernel(w_ref, o_ref):
    off = np.concatenate([[0], np.cumsum(IN_SPLITS)]).tolist()
    (o_rq, _, _, o_cq, o_ckv, o_kpe, o_lx, _, _, o_li, _, o_mq, o_z, _) = off

    def window(start, width):
        lo = start // LANE * LANE
        hi = min(-(-(start + width) // LANE) * LANE, off[-1])
        return w_ref[0, :, lo:hi][:, start - lo:start - lo + width]

    def put(blk, start, width):
        o_ref[0, :, blk * LANE:blk * LANE + width] = window(start, width).astype(BF16)

    put(BLK_RQ, o_rq, 3 * GROUP_WIDTH)
    put(BLK_LX, o_lx, 3 * GROUP_WIDTH)
    put(BLK_MQ, o_mq, GROUP_WIDTH)
    put(BLK_Z, o_z, MIX_WIDTH)
    put(BLK_CQ, o_cq, MLA_Q_RANK)
    put(BLK_CKV, o_ckv, MLA_KV_RANK)
    half = MLA_ROPE // 2
    kpe = w_ref[0, :, o_kpe:o_kpe + LANE]
    gts = w_ref[0, :, o_li // LANE * LANE:o_li // LANE * LANE + LANE]
    g0 = o_li % LANE
    lane = lax.broadcasted_iota(jnp.int32, kpe.shape, 1)
    blk = jnp.where(lane < half, kpe, 0.0)
    blk = blk + jnp.where((lane >= 64) & (lane < 64 + half), pltpu.roll(kpe, 64 - half, 1), 0.0)
    blk = blk + jnp.where((lane >= GATE_I_LANE) & (lane < GATE_I_LANE + 2 * HEADS),
                          pltpu.roll(gts, (GATE_I_LANE - g0) % LANE, 1), 0.0)
    o_ref[0, :, BLK_KPE * LANE:(BLK_KPE + 1) * LANE] = blk.astype(BF16)
    o_ref[0, :, (BLK_KPE + 1) * LANE:NBLK * LANE] = jnp.zeros((w_ref.shape[1], (NBLK - BLK_KPE - 1) * LANE), BF16)


def _prep_w_in(w_in):
    tr = 256
    width = w_in.shape[-1]
    return pl.pallas_call(
        _w_in_kernel,
        out_shape=jax.ShapeDtypeStruct((DEPTH, D_MODEL, NBLK * LANE), BF16),
        grid=(DEPTH, D_MODEL // tr),
        in_specs=[pl.BlockSpec((1, tr, width), lambda l, i: (l, i, 0))],
        out_specs=pl.BlockSpec((1, tr, NBLK * LANE), lambda l, i: (l, i, 0)),
        compiler_params=_cparams(("parallel", "parallel")),
        name="w_in_relayout",
    )(w_in)


def _group_spec(base, rows):
    nchunk = SEQ // rows
    return pl.BlockSpec((HEADS, rows, LANE), lambda b, c: (base // HEADS, b * nchunk + c, 0))


def _layer_spec(shape, layer):
    nd = len(shape) - 1
    return pl.BlockSpec((None,) + tuple(shape[1:]), lambda *_: (layer,) + (0,) * nd)


def _const_spec(shape):
    nd = len(shape)
    return pl.BlockSpec(tuple(shape), lambda *_: (0,) * nd)


def _head_norm_gate(h, g, z):
    mu = jnp.mean(h, axis=-1, keepdims=True)
    var = jnp.mean(jnp.square(h - mu), axis=-1, keepdims=True)
    return (h - mu) * lax.rsqrt(var + LN_EPS) * g, _silu(z)


def _ret_kernel(q_ref, k_ref, v_ref, z_ref, cos_ref, sin_ref, gq_ref, gk_ref, gc_ref, g_ref, o_ref, st_ref):
    @pl.when(pl.program_id(1) == 0)
    def _():
        st_ref[...] = jnp.zeros_like(st_ref)

    cs = cos_ref[0]
    sn = sin_ref[0]
    row = lax.broadcasted_iota(jnp.int32, (RCHUNK, RCHUNK), 0)
    col = lax.broadcasted_iota(jnp.int32, (RCHUNK, RCHUNK), 1)
    causal = col <= row
    for h in range(HEADS):
        q = q_ref[h]
        q = (q * cs + pltpu.roll(q, HEAD_DIM // 2, 1) * sn) * gq_ref[h]
        k = k_ref[h]
        k = (k * cs + pltpu.roll(k, HEAD_DIM // 2, 1) * sn) * HEAD_DIM ** -0.5 * gk_ref[h]
        qb = q.astype(BF16)
        kb = k.astype(BF16)
        vb = v_ref[h].astype(BF16)
        sc = jnp.where(causal, _dot_nt(qb, kb), 0.0)
        st = st_ref[h]
        out = _dot(sc.astype(BF16), vb) + _dot(qb, st.astype(BF16))
        st_ref[h] = gc_ref[h] * (st + _dot(k.T.astype(BF16), vb))
        hn, gate = _head_norm_gate(out, g_ref[h], z_ref[h])
        o_ref[h] = (hn * gate).astype(BF16)


def _ret_consts():
    log_g = jnp.log1p(-jnp.exp2(-5.0 - jnp.arange(HEADS, dtype=F32)))
    idx = jnp.arange(RCHUNK, dtype=F32)
    full = (HEADS, RCHUNK, HEAD_DIM)
    gq = jnp.broadcast_to(jnp.exp(log_g[:, None] * (idx + 1.0))[..., None], full)
    gk = jnp.broadcast_to(jnp.exp(-log_g[:, None] * (idx + 1.0))[..., None], full)
    gc = jnp.broadcast_to(jnp.exp(log_g * RCHUNK)[:, None, None], (HEADS, 1, HEAD_DIM))
    return gq, gk, gc


def _retention(hb, cos_r, sin_r, consts, norm_g, layer):
    gq, gk, gc = consts
    nchunk = SEQ // RCHUNK
    tab = pl.BlockSpec((1, RCHUNK, LANE), lambda b, c: (b, c, 0))
    return pl.pallas_call(
        _ret_kernel,
        out_shape=jax.ShapeDtypeStruct((HEADS, ROWS, LANE), BF16),
        grid=(BATCH, nchunk),
        in_specs=[_group_spec(BLK_RQ, RCHUNK), _group_spec(BLK_RK, RCHUNK), _group_spec(BLK_RV, RCHUNK),
                  _group_spec(BLK_Z, RCHUNK), tab, tab,
                  _const_spec(gq.shape), _const_spec(gk.shape), _const_spec(gc.shape),
                  _layer_spec(norm_g.shape, layer)],
        out_specs=pl.BlockSpec((HEADS, RCHUNK, LANE), lambda b, c: (0, b * nchunk + c, 0)),
        scratch_shapes=[pltpu.VMEM((HEADS, HEAD_DIM, HEAD_DIM), F32)],
        compiler_params=_cparams(("parallel", "arbitrary")),
        name="retention",
    )(hb, hb, hb, hb, cos_r, sin_r, gq, gk, gc, norm_g)


def _mlstm_kernel(lx_ref, lv_ref, lo_ref, gt_ref, z_ref, cw_ref, cb_ref, wq_ref, wk_ref, gb_ref,
                  skip_ref, g_ref, o_ref, xp_ref, c_ref, n_ref, m_ref, s_ref, p_ref):
    @pl.when(pl.program_id(1) == 0)
    def _():
        xp_ref[:, 0:CONV_PAD, :] = jnp.zeros((HEADS, CONV_PAD, HEAD_DIM), F32)
        c_ref[...] = jnp.zeros_like(c_ref)
        n_ref[...] = jnp.zeros_like(n_ref)
        m_ref[...] = jnp.zeros_like(m_ref)

    n = RCHUNK
    row = lax.broadcasted_iota(jnp.int32, (n, n), 0)
    col = lax.broadcasted_iota(jnp.int32, (n, n), 1)
    tril = jnp.where(col <= row, 1.0, 0.0).astype(BF16)
    r128 = lax.broadcasted_iota(jnp.int32, (LANE, LANE), 0)
    c128 = lax.broadcasted_iota(jnp.int32, (LANE, LANE), 1)
    ident = jnp.where(r128 == c128, 1.0, 0.0).astype(BF16)
    srow = lax.broadcasted_iota(jnp.int32, (STRIP, LANE), 0)
    scol = lax.broadcasted_iota(jnp.int32, (STRIP, LANE), 1)

    lane = lax.broadcasted_iota(jnp.int32, (n, LANE), 1)
    x = gt_ref[0] + gb_ref[...]
    x = jnp.where((lane >= GATE_F_LANE) & (lane < GATE_F_LANE + HEADS), _log_sigmoid(x), x) * LOG2E
    x1, x2, x3 = _split3(x)
    cum = _dot(tril, x1) + _dot(tril, x2) + _dot(tril, x3)
    rt = x - pltpu.roll(cum, LANE - (GATE_F_LANE - GATE_I_LANE), 1)
    r1, r2, r3 = _split3(rt)
    rtt = _dot_nt(ident, r1) + _dot_nt(ident, r2) + _dot_nt(ident, r3)

    for h in range(HEADS):
        xp_ref[h, CONV_PAD:, :] = lx_ref[h]
        acc = jnp.zeros((n, HEAD_DIM), F32) + cb_ref[h]
        for j in range(CONV_WIDTH):
            off = CONV_PAD - (CONV_WIDTH - 1) + j
            acc = acc + xp_ref[h, off:off + n, :] * cw_ref[h, j:j + 1, :]
        xp_ref[h, 0:CONV_PAD, :] = lx_ref[h, n - CONV_PAD:n, :]
        xc = _silu(acc)
        xcb = xc.astype(BF16)
        q = _dot(xcb, wq_ref[h])
        k = _dot(xcb, wk_ref[h]) * HEAD_DIM ** -0.5
        qb = q.astype(BF16)
        kb = k.astype(BF16)
        vb = lv_ref[h].astype(BF16)
        s_ref[h] = _dot_nt(qb, kb)

        li = GATE_I_LANE + h
        lf = GATE_F_LANE + h
        r_row = rtt[li:li + 1, :]
        r_col = rt[:, li:li + 1]
        cum_col = cum[:, lf:lf + 1]
        m_st = m_ref[h]

        base = m_st
        us, dens = [], []
        for i in range(n // STRIP):
            rows = slice(i * STRIP, (i + 1) * STRIP)
            d0 = (i * STRIP // LANE) * LANE
            w = d0 + LANE
            if d0 > 0 and (i * STRIP) % LANE == 0:
                base = jnp.maximum(base, jnp.max(r_row[:, d0 - LANE:d0], axis=-1, keepdims=True))
            mask = (scol + d0) <= (srow + i * STRIP)
            rmd = jnp.where(mask, r_row[:, d0:w], -jnp.inf)
            u = jnp.maximum(jnp.max(rmd, axis=-1, keepdims=True), base)
            sd = s_ref[h, rows, d0:w] * jnp.exp2(rmd - u)
            den = jnp.sum(sd, axis=-1, keepdims=True)
            p_ref[h, rows, d0:w] = sd.astype(BF16)
            if d0 > 0:
                sl = s_ref[h, rows, 0:d0] * jnp.exp2(r_row[:, 0:d0] - u)
                den = den + jnp.sum(sl, axis=-1, keepdims=True)
                p_ref[h, rows, 0:d0] = sl.astype(BF16)
            if w < n:
                p_ref[h, rows, w:n] = jnp.zeros((STRIP, n - w), BF16)
            us.append(u)
            dens.append(den)
        u = jnp.concatenate(us, axis=0)
        dsum = jnp.concatenate(dens, axis=0)

        w_inter = jnp.exp2(m_st - u)
        c_st = c_ref[h]
        n_st = n_ref[h]
        num = _dot(p_ref[h], vb) + w_inter * _dot(qb, c_st.astype(BF16))
        den = dsum + w_inter * jnp.sum(q * n_st, axis=-1, keepdims=True)
        hcell = num / jnp.maximum(jnp.abs(den), jnp.exp2(-(cum_col + u)))

        u_last = u[n - 1:n, :]
        decay = jnp.exp2(m_st - u_last)
        kw = k * jnp.exp2(r_col - u_last)
        c_ref[h] = decay * c_st + _dot(kw.T.astype(BF16), vb)
        n_ref[h] = decay * n_st + jnp.sum(kw, axis=0, keepdims=True)
        m_ref[h] = cum_col[n - 1:n, :] + u_last

        cell = hcell * jax.nn.sigmoid(lo_ref[h])
        hn, gate = _head_norm_gate(cell, g_ref[h], z_ref[h])
        o_ref[h] = ((hn + skip_ref[h] * xc) * gate).astype(BF16)


def _prep_mlstm(conv_w, conv_b, w_q, w_k, i_bias, f_bias, skip, norm_g):
    nl = conv_w.shape[0]
    cw = conv_w.reshape(nl, CONV_WIDTH, HEADS, HEAD_DIM).transpose(0, 2, 1, 3)
    cb = conv_b.reshape(nl, HEADS, 1, HEAD_DIM)
    zeros = lambda w: jnp.zeros((nl, w), F32)
    gb = jnp.concatenate([zeros(GATE_I_LANE), i_bias, f_bias, zeros(LANE - GATE_F_LANE - HEADS)], axis=-1)
    return (cw, cb, w_q.astype(BF16), w_k.astype(BF16), gb.reshape(nl, 1, LANE),
            skip.reshape(nl, HEADS, 1, HEAD_DIM), norm_g.reshape(nl, HEADS, 1, HEAD_DIM))


def _mlstm(hb, params, layer):
    nchunk = SEQ // RCHUNK
    return pl.pallas_call(
        _mlstm_kernel,
        out_shape=jax.ShapeDtypeStruct((HEADS, ROWS, LANE), BF16),
        grid=(BATCH, nchunk),
        in_specs=[_group_spec(BLK_LX, RCHUNK), _group_spec(BLK_LV, RCHUNK), _group_spec(BLK_LO, RCHUNK),
                  pl.BlockSpec((1, RCHUNK, LANE), lambda b, c: (BLK_KPE, b * nchunk + c, 0)),
                  _group_spec(BLK_Z + 2 * HEADS, RCHUNK)] + [_layer_spec(p.shape, layer) for p in params],
        out_specs=pl.BlockSpec((HEADS, RCHUNK, LANE), lambda b, c: (0, b * nchunk + c, 0)),
        scratch_shapes=[pltpu.VMEM((HEADS, RCHUNK + CONV_PAD, HEAD_DIM), F32),
                        pltpu.VMEM((HEADS, HEAD_DIM, HEAD_DIM), F32),
                        pltpu.VMEM((HEADS, 1, HEAD_DIM), F32),
                        pltpu.VMEM((HEADS, 1, 1), F32),
                        pltpu.VMEM((HEADS, RCHUNK, RCHUNK), F32),
                        pltpu.VMEM((HEADS, RCHUNK, RCHUNK), BF16)],
        compiler_params=_cparams(("parallel", "arbitrary")),
        name="mlstm",
    )(hb, hb, hb, hb, hb, *params)


def _mla_prep_kernel(lat_ref, kpe_ref, cm_ref, sm_ref, qg_ref, wuq_ref, kvg_ref, wukv_ref,
                     q_out, k_out, v_out):
    scale = (MLA_NOPE + MLA_ROPE) ** -0.5 * LOG2E
    cs = cm_ref[0]
    sn = sm_ref[0]
    cq = jnp.concatenate([lat_ref[0], lat_ref[1], lat_ref[2]], axis=-1)
    qn = cq * lax.rsqrt(jnp.mean(jnp.square(cq), axis=-1, keepdims=True) + RMS_EPS) * qg_ref[...]
    q = _dot(qn.astype(BF16), wuq_ref[...])
    ckv = lat_ref[3]
    kvn = ckv * lax.rsqrt(jnp.mean(jnp.square(ckv), axis=-1, keepdims=True) + RMS_EPS) * kvg_ref[...]
    kv = _dot(kvn.astype(BF16), wukv_ref[...])
    kpe = kpe_ref[0]
    krot = (kpe * cs + pltpu.roll(kpe, LANE // 2, 1) * sn).astype(BF16)
    for h in range(HEADS):
        qr = q[:, GROUP_WIDTH + h * LANE:GROUP_WIDTH + (h + 1) * LANE]
        qr = qr * cs + pltpu.roll(qr, LANE // 2, 1) * sn
        q_out[0, h, :, 0:LANE] = (q[:, h * LANE:(h + 1) * LANE] * scale).astype(BF16)
        q_out[0, h, :, LANE:2 * LANE] = (qr * scale).astype(BF16)
        k_out[0, h, :, 0:LANE] = kv[:, h * LANE:(h + 1) * LANE].astype(BF16)
        k_out[0, h, :, LANE:2 * LANE] = krot
        v_out[0, h, :, 0:LANE] = kv[:, GROUP_WIDTH + h * LANE:GROUP_WIDTH + (h + 1) * LANE].astype(BF16)
        v_out[0, h, :, LANE:2 * LANE] = jnp.ones((TS_PREP, LANE), BF16)


def _prep_mla_weights(w_uq, w_ukv):
    lead = w_uq.shape[:-1]
    wq = w_uq.reshape(lead + (HEADS, MLA_NOPE + MLA_ROPE))
    nope = wq[..., :MLA_NOPE].reshape(lead + (GROUP_WIDTH,))
    half = MLA_ROPE // 2
    zeros = jnp.zeros(lead + (HEADS, 64 - half), w_uq.dtype)
    rope = jnp.concatenate([wq[..., MLA_NOPE:MLA_NOPE + half], zeros, wq[..., MLA_NOPE + half:], zeros], axis=-1)
    wq_p = jnp.concatenate([nope, rope.reshape(lead + (HEADS * LANE,))], axis=-1).astype(BF16)
    lead = w_ukv.shape[:-1]
    wkv = w_ukv.reshape(lead + (HEADS, MLA_NOPE + HEAD_DIM))
    wkv_p = jnp.concatenate([wkv[..., :MLA_NOPE].reshape(lead + (GROUP_WIDTH,)),
                             wkv[..., MLA_NOPE:].reshape(lead + (GROUP_WIDTH,))], axis=-1).astype(BF16)
    return wq_p, wkv_p


def _mla_prep(hb, cos_m, sin_m, qg, wuq_p, kvg, wukv_p, layer):
    ns = SEQ // TS_PREP
    tab = pl.BlockSpec((1, TS_PREP, LANE), lambda b, i: (b, i, 0))
    qk_shape = jax.ShapeDtypeStruct((BATCH, HEADS, SEQ, 2 * LANE), BF16)
    return pl.pallas_call(
        _mla_prep_kernel,
        out_shape=(qk_shape, qk_shape, qk_shape),
        grid=(BATCH, ns),
        in_specs=[pl.BlockSpec((HEADS, TS_PREP, LANE), lambda b, i: (BLK_CQ // HEADS, b * ns + i, 0)),
                  pl.BlockSpec((1, TS_PREP, LANE), lambda b, i: (BLK_KPE, b * ns + i, 0)),
                  tab, tab, _layer_spec(qg.shape, layer), _layer_spec(wuq_p.shape, layer),
                  _layer_spec(kvg.shape, layer), _layer_spec(wukv_p.shape, layer)],
        out_specs=(pl.BlockSpec((1, HEADS, TS_PREP, 2 * LANE), lambda b, i: (b, 0, i, 0)),
                   pl.BlockSpec((1, HEADS, TS_PREP, 2 * LANE), lambda b, i: (b, 0, i, 0)),
                   pl.BlockSpec((1, HEADS, TS_PREP, 2 * LANE), lambda b, i: (b, 0, i, 0))),
        compiler_params=_cparams(("parallel", "parallel")),
        name="mla_prep",
    )(hb, hb, cos_m, sin_m, qg, wuq_p, kvg, wukv_p)


def _attn_tile(nfull, q_ref, k_ref, v_ref, z_ref, o_ref, s_ref, p_ref, m_ref):
    kvlen = (nfull + 1) * TQ_ATT
    d0 = nfull * TQ_ATT
    srow = lax.broadcasted_iota(jnp.int32, (STRIP, LANE), 0)
    scol = lax.broadcasted_iota(jnp.int32, (STRIP, LANE), 1)

    def strip_blocks(g, i):
        rows = slice(i * STRIP, (i + 1) * STRIP)
        wd = -(-(i + 1) * STRIP // LANE) * LANE
        ncol = (d0 + wd) // LANE
        blks = [s_ref[g, rows, c * LANE:(c + 1) * LANE] for c in range(ncol)]
        blks[-1] = jnp.where(scol + (wd - LANE) <= srow + i * STRIP, blks[-1], -jnp.inf)
        return rows, ncol, blks

    for g in range(ATT_HEADS):
        q = q_ref[0, g]
        for j in range(nfull + 1):
            s_ref[g, :, j * TQ_ATT:(j + 1) * TQ_ATT] = _dot_nt(q, k_ref[0, g, j * TQ_ATT:(j + 1) * TQ_ATT, :])
    for g in range(ATT_HEADS):
        for i in range(TQ_ATT // STRIP):
            rows, ncol, blks = strip_blocks(g, i)
            mx = blks[0]
            for blk in blks[1:]:
                mx = jnp.maximum(mx, blk)
            m_ref[g, rows, :] = jnp.broadcast_to(jnp.max(mx, axis=-1, keepdims=True), (STRIP, LANE))
        for i in range(TQ_ATT // STRIP):
            rows, ncol, blks = strip_blocks(g, i)
            m = m_ref[g, rows, :]
            for c, blk in enumerate(blks):
                p_ref[g, rows, c * LANE:(c + 1) * LANE] = jnp.exp2(blk - m).astype(BF16)
            if ncol * LANE < kvlen:
                p_ref[g, rows, ncol * LANE:kvlen] = jnp.zeros((STRIP, kvlen - ncol * LANE), BF16)
        pv = _dot(p_ref[g, :, 0:kvlen], v_ref[0, g, 0:kvlen, :])
        o_ref[g] = (pv[:, :LANE] / pv[:, LANE:] * _silu(z_ref[g])).astype(BF16)


def _mla_attn_kernel(q_ref, k_ref, v_ref, z_ref, o_ref, s_ref, p_ref, m_ref):
    qi = pl.program_id(2)
    for nfull in range(SEQ // TQ_ATT):
        @pl.when(qi == nfull)
        def _(nfull=nfull):
            _attn_tile(nfull, q_ref, k_ref, v_ref, z_ref, o_ref, s_ref, p_ref, m_ref)


def _mla_attn(q, k, v, hb):
    nq = SEQ // TQ_ATT
    return pl.pallas_call(
        _mla_attn_kernel,
        out_shape=jax.ShapeDtypeStruct((HEADS, ROWS, LANE), BF16),
        grid=(BATCH, HEADS // ATT_HEADS, nq),
        in_specs=[pl.BlockSpec((1, ATT_HEADS, TQ_ATT, 2 * LANE), lambda b, h, i: (b, h, i, 0)),
                  pl.BlockSpec((1, ATT_HEADS, SEQ, 2 * LANE), lambda b, h, i: (b, h, 0, 0)),
                  pl.BlockSpec((1, ATT_HEADS, SEQ, 2 * LANE), lambda b, h, i: (b, h, 0, 0)),
                  pl.BlockSpec((ATT_HEADS, TQ_ATT, LANE),
                               lambda b, h, i: ((BLK_Z + HEADS) // ATT_HEADS + h, b * nq + i, 0))],
        out_specs=pl.BlockSpec((ATT_HEADS, TQ_ATT, LANE), lambda b, h, i: (h, b * nq + i, 0)),
        scratch_shapes=[pltpu.VMEM((ATT_HEADS, TQ_ATT, SEQ), F32), pltpu.VMEM((ATT_HEADS, TQ_ATT, SEQ), BF16),
                        pltpu.VMEM((ATT_HEADS, TQ_ATT, LANE), F32)],
        compiler_params=_cparams(("parallel", "parallel", "parallel")),
        name="mla_attn",
    )(q, k, v, hb)


def _mem_attn_kernel(q_ref, k_ref, v_ref, z_ref, o_ref):
    ones = jnp.ones((MEM_LEN, LANE), BF16)
    for h in range(HEADS):
        cols = slice(h * LANE, (h + 1) * LANE)
        q = (q_ref[h] * (HEAD_DIM ** -0.5 * LOG2E)).astype(BF16)
        s = _dot_nt(q, k_ref[:, cols])
        p = jnp.exp2(s - jnp.max(s, axis=-1, keepdims=True)).astype(BF16)
        pv = _dot(p, jnp.concatenate([v_ref[:, cols], ones], axis=-1))
        o_ref[h] = (pv[:, :LANE] / pv[:, LANE:] * _silu(z_ref[h])).astype(BF16)


def _mem_attn(hb, kvm, layer):
    nq = SEQ // TQ_MEM
    return pl.pallas_call(
        _mem_attn_kernel,
        out_shape=jax.ShapeDtypeStruct((HEADS, ROWS, LANE), BF16),
        grid=(BATCH, nq),
        in_specs=[_group_spec(BLK_MQ, TQ_MEM),
                  pl.BlockSpec((MEM_LEN, GROUP_WIDTH), lambda b, i: (b, 2 * layer)),
                  pl.BlockSpec((MEM_LEN, GROUP_WIDTH), lambda b, i: (b, 2 * layer + 1)),
                  _group_spec(BLK_Z + 3 * HEADS, TQ_MEM)],
        out_specs=pl.BlockSpec((HEADS, TQ_MEM, LANE), lambda b, i: (0, b * nq + i, 0)),
        compiler_params=_cparams(("parallel", "parallel")),
        name="mem_attn",
    )(hb, kvm, kvm, hb)


def _out_kernel(ya_ref, yb_ref, yc_ref, yd_ref, w_ref, x_ref, g_ref, b_ref, o_ref):
    for t in range(TM_OUT // SUB_OUT):
        rows = slice(t * SUB_OUT, (t + 1) * SUB_OUT)
        parts = [ref[h, rows, :] for ref in (ya_ref, yb_ref, yc_ref, yd_ref) for h in range(HEADS)]
        y = jnp.concatenate(parts, axis=-1)
        r = DEEPNORM_ALPHA * x_ref[rows, :] + _dot(y, w_ref[...])
        mu = jnp.mean(r, axis=-1, keepdims=True)
        var = jnp.mean(jnp.square(r - mu), axis=-1, keepdims=True)
        o_ref[rows, :] = (r - mu) * lax.rsqrt(var + LN_EPS) * g_ref[...] + b_ref[...]


def _outproj(ya, yb, yc, yd, w_out, x2d, ln_g, ln_b, layer):
    yspec = pl.BlockSpec((HEADS, TM_OUT, LANE), lambda i: (0, i, 0))
    return pl.pallas_call(
        _out_kernel,
        out_shape=jax.ShapeDtypeStruct((ROWS, D_MODEL), F32),
        grid=(ROWS // TM_OUT,),
        in_specs=[yspec, yspec, yspec, yspec, _layer_spec(w_out.shape, layer),
                  pl.BlockSpec((TM_OUT, D_MODEL), lambda i: (i, 0)),
                  _layer_spec(ln_g.shape, layer), _layer_spec(ln_b.shape, layer)],
        out_specs=pl.BlockSpec((TM_OUT, D_MODEL), lambda i: (i, 0)),
        compiler_params=_cparams(("parallel",)),
        name="outproj_ln",
    )(ya, yb, yc, yd, w_out, x2d, ln_g, ln_b)


def kernel(x, mem, positions, w_in, ret_norm_g, mla_q_norm_g, mla_w_uq, mla_kv_norm_g, mla_w_ukv, ml_conv_w, ml_conv_b, ml_w_q, ml_w_k, ml_i_bias, ml_f_bias, ml_skip, ml_norm_g, w_mem_kv, w_out, ln_g, ln_b):
    assert x.shape == (BATCH, SEQ, D_MODEL) and mem.shape == (BATCH, MEM_LEN, D_MODEL)
    cos_r, sin_r, cos_m, sin_m = _rope_tables(positions)
    kvm = _mem_kv(mem, w_mem_kv)
    w_in_p = _prep_w_in(w_in)
    wuq_p, wukv_p = _prep_mla_weights(mla_w_uq, mla_w_ukv)
    w_out_b = w_out.astype(BF16)
    ret_consts = _ret_consts()
    ret_g = ret_norm_g.reshape(DEPTH, HEADS, 1, HEAD_DIM)
    mlstm_params = _prep_mlstm(ml_conv_w, ml_conv_b, ml_w_q, ml_w_k, ml_i_bias, ml_f_bias, ml_skip, ml_norm_g)
    qg = mla_q_norm_g.reshape(DEPTH, 1, MLA_Q_RANK)
    kvg = mla_kv_norm_g.reshape(DEPTH, 1, MLA_KV_RANK)
    lng = ln_g.reshape(DEPTH, 1, D_MODEL)
    lnb = ln_b.reshape(DEPTH, 1, D_MODEL)

    x2d = x.reshape(ROWS, D_MODEL)
    for l in range(DEPTH):
        hb = _inproj(x2d, w_in_p, l)
        ya = _retention(hb, cos_r, sin_r, ret_consts, ret_g, l)
        q, k, v = _mla_prep(hb, cos_m, sin_m, qg, wuq_p, kvg, wukv_p, l)
        yb = _mla_attn(q, k, v, hb)
        yc = _mlstm(hb, mlstm_params, l)
        yd = _mem_attn(hb, kvm, l)
        x2d = _outproj(ya, yb, yc, yd, w_out_b, x2d, lng, lnb, l)
    return x2d.reshape(BATCH, SEQ, D_MODEL)
```

```python
import numpy as np
import jax
import jax.numpy as jnp
from jax import lax
from jax.experimental import pallas as pl
from jax.experimental.pallas import tpu as pltpu

F32 = jnp.float32
BF16 = jnp.bfloat16

D_MODEL = 2048
BATCH = 8
SEQ = 2048
DEPTH = 4
MEM_LEN = 256
HEAD_DIM = 128
HEADS = 4
GROUP_WIDTH = HEADS * HEAD_DIM
MLA_NOPE = 128
MLA_ROPE = 64
MLA_Q_RANK = 384
MLA_KV_RANK = 128
CONV_WIDTH = 4
MIX_WIDTH = 4 * GROUP_WIDTH
ROPE_THETA = 10000.0
LN_EPS = 1e-5
RMS_EPS = 1e-6
DEEPNORM_ALPHA = (2 * DEPTH) ** 0.25
IN_SPLITS = (512, 512, 512, MLA_Q_RANK, MLA_KV_RANK, MLA_ROPE, 512, 512, 512, HEADS, HEADS, 512, MIX_WIDTH)

LANE = 128
ROWS = BATCH * SEQ

BLK_RQ, BLK_RK, BLK_RV = 0, 4, 8
BLK_LX, BLK_LV, BLK_LO = 12, 16, 20
BLK_MQ = 24
BLK_Z = 28
BLK_CQ, BLK_CKV = 44, 47
BLK_KPE = 48
NBLK = 50
GATE_I_LANE, GATE_F_LANE = 32, 36

TM_IN, TN_IN = 1024, 1280
TM_OUT = 512
SUB_OUT = 256
TS_PREP = 512
TQ_ATT = 512
ATT_HEADS = 2
TQ_MEM = 512
RCHUNK = 512
CONV_PAD = 8
STRIP = 64
LOG2E = 1.4426950408889634
VMEM_LIMIT = 56 * 1024 * 1024


def _cparams(sem):
    return pltpu.CompilerParams(dimension_semantics=sem, vmem_limit_bytes=VMEM_LIMIT)


def _silu(z):
    return z * jax.nn.sigmoid(z)


def _log_sigmoid(x):
    return jnp.minimum(x, 0.0) - jnp.log1p(jnp.exp(-jnp.abs(x)))


def _split3(x):
    x1 = x.astype(BF16)
    r1 = x - x1.astype(F32)
    x2 = r1.astype(BF16)
    x3 = (r1 - x2.astype(F32)).astype(BF16)
    return x1, x2, x3


def _dot(a, b):
    return jnp.dot(a, b, preferred_element_type=F32)


def _rowsum_mxu(x, ones):
    hi = x.astype(BF16)
    lo = (x - hi.astype(F32)).astype(BF16)
    return _dot(hi, ones) + _dot(lo, ones)


def _dot_nt(a, b):
    return lax.dot_general(a, b, (((1,), (1,)), ((), ())), preferred_element_type=F32)


def _tables_kernel(pos_ref, c_ref, cr_ref, sr_ref, cm_ref, sm_ref):
    pos = pos_ref[0].astype(F32)
    ang_r = pos * c_ref[0:1, :]
    cr_ref[0] = jnp.cos(ang_r)
    sr_ref[0] = jnp.sin(ang_r) * c_ref[1:2, :]
    ang_m = pos * c_ref[2:3, :]
    cm_ref[0] = jnp.cos(ang_m) * c_ref[3:4, :]
    sm_ref[0] = jnp.sin(ang_m) * c_ref[4:5, :]


def _rope_tables(positions):
    half_r = HEAD_DIM // 2
    fr = ROPE_THETA ** (-jnp.arange(half_r, dtype=F32) / half_r)
    half_m = MLA_ROPE // 2
    fm = ROPE_THETA ** (-jnp.arange(half_m, dtype=F32) / half_m)
    z32 = jnp.zeros((half_m,), F32)
    o32 = jnp.ones((half_m,), F32)
    rows = [
        jnp.concatenate([fr, fr]),
        jnp.concatenate([-jnp.ones((half_r,), F32), jnp.ones((half_r,), F32)]),
        jnp.concatenate([fm, z32, fm, z32]),
        jnp.concatenate([o32, z32, o32, z32]),
        jnp.concatenate([-o32, z32, o32, z32]),
    ]
    consts = jnp.concatenate([jnp.stack(rows), jnp.zeros((3, LANE), F32)], axis=0)
    ts = 512
    tab = jax.ShapeDtypeStruct((BATCH, SEQ, LANE), F32)
    spec = pl.BlockSpec((1, ts, LANE), lambda b, i: (b, i, 0))
    return pl.pallas_call(
        _tables_kernel,
        out_shape=(tab, tab, tab, tab),
        grid=(BATCH, SEQ // ts),
        in_specs=[pl.BlockSpec((1, ts, 1), lambda b, i: (b, i, 0)),
                  pl.BlockSpec((8, LANE), lambda b, i: (0, 0))],
        out_specs=(spec, spec, spec, spec),
        compiler_params=_cparams(("parallel", "parallel")),
        name="rope_tables",
    )(positions.reshape(BATCH, SEQ, 1), consts)


def _matmul_kernel(a_ref, w_ref, o_ref):
    o_ref[...] = _dot(a_ref[...].astype(BF16), w_ref[...]).astype(o_ref.dtype)


def _mem_kv(mem, w_mem_kv):
    w = jnp.transpose(w_mem_kv, (1, 0, 2)).reshape(D_MODEL, DEPTH * 2 * GROUP_WIDTH).astype(BF16)
    a = mem.reshape(BATCH * MEM_LEN, D_MODEL)
    tm, tn = 512, 1024
    return pl.pallas_call(
        _matmul_kernel,
        out_shape=jax.ShapeDtypeStruct((BATCH * MEM_LEN, DEPTH * 2 * GROUP_WIDTH), BF16),
        grid=(BATCH * MEM_LEN // tm, DEPTH * 2 * GROUP_WIDTH // tn),
        in_specs=[pl.BlockSpec((tm, D_MODEL), lambda i, j: (i, 0)),
                  pl.BlockSpec((D_MODEL, tn), lambda i, j: (0, j))],
        out_specs=pl.BlockSpec((tm, tn), lambda i, j: (i, j)),
        compiler_params=_cparams(("parallel", "parallel")),
        name="mem_kv",
    )(a, w)


def _inproj_kernel(x_ref, w_ref, o_ref, xb_ref):
    @pl.when(pl.program_id(1) == 0)
    def _():
        xb_ref[...] = x_ref[...].astype(BF16)

    xb = xb_ref[...]
    for k2 in range(TN_IN // 256):
        r = _dot_nt(xb, w_ref[k2 * 256:(k2 + 1) * 256, :])
        o_ref[2 * k2] = r[:, :LANE]
        o_ref[2 * k2 + 1] = r[:, LANE:]


def _inproj(x2d, wt_p, layer):
    nb = TN_IN // LANE
    return pl.pallas_call(
        _inproj_kernel,
        out_shape=jax.ShapeDtypeStruct((NBLK, ROWS, LANE), F32),
        grid=(ROWS // TM_IN, NBLK * LANE // TN_IN),
        in_specs=[pl.BlockSpec((TM_IN, D_MODEL), lambda i, j: (i, 0)),
                  pl.BlockSpec((None, TN_IN, D_MODEL), lambda i, j: (layer, j, 0))],
        out_specs=pl.BlockSpec((nb, TM_IN, LANE), lambda i, j: (j, i, 0)),
        scratch_shapes=[pltpu.VMEM((TM_IN, D_MODEL), BF16)],
        compiler_params=_cparams(("parallel", "arbitrary")),
        name="inproj",
    )(x2d, wt_p)


def _prep_w_in(w_in):
    wt = jnp.swapaxes(w_in, 1, 2)
    idx = np.cumsum(IN_SPLITS)[:-1].tolist()
    (r_q, r_k, r_v, a_cq, a_ckv, a_kpe, l_x, l_v, l_o, l_i, l_f, c_q, z) = jnp.split(wt, idx, axis=1)
    half = MLA_ROPE // 2
    zeros = lambda n: jnp.zeros((wt.shape[0], n, wt.shape[2]), wt.dtype)
    blk = [a_kpe[:, :half], l_i, l_f, zeros(64 - half - 2 * HEADS), a_kpe[:, half:], zeros(64 - half)]
    return jnp.concatenate([r_q, r_k, r_v, l_x, l_v, l_o, c_q, z, a_cq, a_ckv] + blk + [zeros(LANE)],
                           axis=1).astype(BF16)


def _group_spec(base, rows):
    nchunk = SEQ // rows
    return pl.BlockSpec((HEADS, rows, LANE), lambda b, c: (base // HEADS, b * nchunk + c, 0))


def _layer_spec(shape, layer):
    nd = len(shape) - 1
    return pl.BlockSpec((None,) + tuple(shape[1:]), lambda *_: (layer,) + (0,) * nd)


def _const_spec(shape):
    nd = len(shape)
    return pl.BlockSpec(tuple(shape), lambda *_: (0,) * nd)


def _head_norm_gate(h, g, z):
    mu = jnp.mean(h, axis=-1, keepdims=True)
    var = jnp.mean(jnp.square(h - mu), axis=-1, keepdims=True)
    return (h - mu) * lax.rsqrt(var + LN_EPS) * g, _silu(z)


def _ret_kernel(q_ref, k_ref, v_ref, z_ref, cos_ref, sin_ref, gq_ref, gk_ref, gc_ref, g_ref, o_ref, st_ref):
    @pl.when(pl.program_id(1) == 0)
    def _():
        st_ref[...] = jnp.zeros_like(st_ref)

    cs = cos_ref[0]
    sn = sin_ref[0]
    row = lax.broadcasted_iota(jnp.int32, (RCHUNK, RCHUNK), 0)
    col = lax.broadcasted_iota(jnp.int32, (RCHUNK, RCHUNK), 1)
    causal = col <= row
    for h in range(HEADS):
        q = q_ref[h]
        q = (q * cs + pltpu.roll(q, HEAD_DIM // 2, 1) * sn) * gq_ref[h]
        k = k_ref[h]
        k = (k * cs + pltpu.roll(k, HEAD_DIM // 2, 1) * sn) * HEAD_DIM ** -0.5 * gk_ref[h]
        qb = q.astype(BF16)
        kb = k.astype(BF16)
        vb = v_ref[h].astype(BF16)
        sc = jnp.where(causal, _dot_nt(qb, kb), 0.0)
        st = st_ref[h]
        out = _dot(sc.astype(BF16), vb) + _dot(qb, st.astype(BF16))
        st_ref[h] = gc_ref[h] * (st + _dot(k.T.astype(BF16), vb))
        hn, gate = _head_norm_gate(out, g_ref[h], z_ref[h])
        o_ref[h] = (hn * gate).astype(BF16)


def _ret_consts():
    log_g = jnp.log1p(-jnp.exp2(-5.0 - jnp.arange(HEADS, dtype=F32)))
    idx = jnp.arange(RCHUNK, dtype=F32)
    full = (HEADS, RCHUNK, HEAD_DIM)
    gq = jnp.broadcast_to(jnp.exp(log_g[:, None] * (idx + 1.0))[..., None], full)
    gk = jnp.broadcast_to(jnp.exp(-log_g[:, None] * (idx + 1.0))[..., None], full)
    gc = jnp.broadcast_to(jnp.exp(log_g * RCHUNK)[:, None, None], (HEADS, 1, HEAD_DIM))
    return gq, gk, gc


def _retention(hb, cos_r, sin_r, consts, norm_g, layer):
    gq, gk, gc = consts
    nchunk = SEQ // RCHUNK
    tab = pl.BlockSpec((1, RCHUNK, LANE), lambda b, c: (b, c, 0))
    return pl.pallas_call(
        _ret_kernel,
        out_shape=jax.ShapeDtypeStruct((HEADS, ROWS, LANE), BF16),
        grid=(BATCH, nchunk),
        in_specs=[_group_spec(BLK_RQ, RCHUNK), _group_spec(BLK_RK, RCHUNK), _group_spec(BLK_RV, RCHUNK),
                  _group_spec(BLK_Z, RCHUNK), tab, tab,
                  _const_spec(gq.shape), _const_spec(gk.shape), _const_spec(gc.shape),
                  _layer_spec(norm_g.shape, layer)],
        out_specs=pl.BlockSpec((HEADS, RCHUNK, LANE), lambda b, c: (0, b * nchunk + c, 0)),
        scratch_shapes=[pltpu.VMEM((HEADS, HEAD_DIM, HEAD_DIM), F32)],
        compiler_params=_cparams(("parallel", "arbitrary")),
        name="retention",
    )(hb, hb, hb, hb, cos_r, sin_r, gq, gk, gc, norm_g)


def _mlstm_kernel(lx_ref, lv_ref, lo_ref, gt_ref, z_ref, cw_ref, cb_ref, wq_ref, wk_ref, gb_ref,
                  skip_ref, g_ref, o_ref, xp_ref, st_ref, m_ref, s_ref, p_ref):
    @pl.when(pl.program_id(1) == 0)
    def _():
        xp_ref[:, 0:CONV_PAD, :] = jnp.zeros((HEADS, CONV_PAD, HEAD_DIM), F32)
        st_ref[...] = jnp.zeros_like(st_ref)
        m_ref[...] = jnp.zeros_like(m_ref)

    n = RCHUNK
    ones_nd = jnp.ones((n, HEAD_DIM), BF16)
    ones_dd = jnp.ones((HEAD_DIM, HEAD_DIM), BF16)
    row = lax.broadcasted_iota(jnp.int32, (n, n), 0)
    col = lax.broadcasted_iota(jnp.int32, (n, n), 1)
    tril = jnp.where(col <= row, 1.0, 0.0).astype(BF16)
    r128 = lax.broadcasted_iota(jnp.int32, (LANE, LANE), 0)
    c128 = lax.broadcasted_iota(jnp.int32, (LANE, LANE), 1)
    ident = jnp.where(r128 == c128, 1.0, 0.0).astype(BF16)
    srow = lax.broadcasted_iota(jnp.int32, (STRIP, LANE), 0)
    scol = lax.broadcasted_iota(jnp.int32, (STRIP, LANE), 1)

    lane = lax.broadcasted_iota(jnp.int32, (n, LANE), 1)
    x = gt_ref[0] + gb_ref[...]
    x = jnp.where((lane >= GATE_F_LANE) & (lane < GATE_F_LANE + HEADS), _log_sigmoid(x), x) * LOG2E
    x1, x2, x3 = _split3(x)
    cum = _dot(tril, x1) + _dot(tril, x2) + _dot(tril, x3)
    rt = x - pltpu.roll(cum, LANE - (GATE_F_LANE - GATE_I_LANE), 1)
    r1, r2, r3 = _split3(rt)
    rtt = _dot_nt(ident, r1) + _dot_nt(ident, r2) + _dot_nt(ident, r3)

    for h in range(HEADS):
        xp_ref[h, CONV_PAD:, :] = lx_ref[h]
        acc = jnp.zeros((n, HEAD_DIM), F32) + cb_ref[h]
        for j in range(CONV_WIDTH):
            off = CONV_PAD - (CONV_WIDTH - 1) + j
            acc = acc + xp_ref[h, off:off + n, :] * cw_ref[h, j:j + 1, :]
        xp_ref[h, 0:CONV_PAD, :] = lx_ref[h, n - CONV_PAD:n, :]
        xc = _silu(acc)
        xcb = xc.astype(BF16)
        q = _dot(xcb, wq_ref[h])
        k = _dot(xcb, wk_ref[h]) * HEAD_DIM ** -0.5
        qb = q.astype(BF16)
        kb = k.astype(BF16)
        vb = jnp.concatenate([lv_ref[h].astype(BF16), ones_nd], axis=-1)
        s_ref[h] = _dot_nt(qb, kb)

        li = GATE_I_LANE + h
        lf = GATE_F_LANE + h
        r_row = rtt[li:li + 1, :]
        m_st = m_ref[h]

        base = m_st
        us = []
        for i in range(n // STRIP):
            rows = slice(i * STRIP, (i + 1) * STRIP)
            d0 = (i * STRIP // LANE) * LANE
            w = d0 + LANE
            if d0 > 0 and (i * STRIP) % LANE == 0:
                base = jnp.maximum(base, jnp.max(r_row[:, d0 - LANE:d0], axis=-1, keepdims=True))
            mask = (scol + d0) <= (srow + i * STRIP)
            rmd = jnp.where(mask, r_row[:, d0:w], -jnp.inf)
            u_col = jnp.maximum(jnp.max(rmd, axis=-1, keepdims=True), base)
            u = jnp.broadcast_to(u_col, (STRIP, LANE))
            p_ref[h, rows, d0:w] = (s_ref[h, rows, d0:w] * jnp.exp2(rmd - u)).astype(BF16)
            for c in range(d0 // LANE):
                cols = slice(c * LANE, (c + 1) * LANE)
                p_ref[h, rows, cols] = (s_ref[h, rows, cols] * jnp.exp2(r_row[:, cols] - u)).astype(BF16)
            if w < n:
                p_ref[h, rows, w:n] = jnp.zeros((STRIP, n - w), BF16)
            us.append(u)
            u_last = u_col[STRIP - 1:STRIP, :]
        u = jnp.concatenate(us, axis=0)

        w_inter = jnp.exp2(m_st - u)
        st = st_ref[h]
        intra = _dot(p_ref[h], vb)
        inter = _dot(qb, st.astype(BF16))
        num = intra[:, :LANE] + w_inter * inter[:, :LANE]
        den = intra[:, LANE:] + w_inter * inter[:, LANE:]
        cum_f = jnp.broadcast_to(cum[:, lf:lf + 1], (n, LANE))
        hcell = num / jnp.maximum(jnp.abs(den), jnp.exp2(-(cum_f + u)))

        decay = jnp.exp2(m_st - u_last)
        kw = k * jnp.exp2(jnp.broadcast_to(rt[:, li:li + 1], (n, LANE)) - u_last)
        upd = _dot(kw.T.astype(BF16), vb)
        st_ref[h] = decay * st + upd
        m_ref[h] = cum[n - 1:n, lf:lf + 1] + u_last

        cell = hcell * jax.nn.sigmoid(lo_ref[h])
        mu = _rowsum_mxu(cell, ones_dd) * (1.0 / HEAD_DIM)
        dev = cell - mu
        var = _rowsum_mxu(dev * dev, ones_dd) * (1.0 / HEAD_DIM)
        hn = dev * lax.rsqrt(var + LN_EPS) * g_ref[h]
        o_ref[h] = ((hn + skip_ref[h] * xc) * _silu(z_ref[h])).astype(BF16)


def _prep_mlstm(conv_w, conv_b, w_q, w_k, i_bias, f_bias, skip, norm_g):
    nl = conv_w.shape[0]
    cw = conv_w.reshape(nl, CONV_WIDTH, HEADS, HEAD_DIM).transpose(0, 2, 1, 3)
    cb = conv_b.reshape(nl, HEADS, 1, HEAD_DIM)
    zeros = lambda w: jnp.zeros((nl, w), F32)
    gb = jnp.concatenate([zeros(GATE_I_LANE), i_bias, f_bias, zeros(LANE - GATE_F_LANE - HEADS)], axis=-1)
    return (cw, cb, w_q.astype(BF16), w_k.astype(BF16), gb.reshape(nl, 1, LANE),
            skip.reshape(nl, HEADS, 1, HEAD_DIM), norm_g.reshape(nl, HEADS, 1, HEAD_DIM))


def _mlstm(hb, params, layer):
    nchunk = SEQ // RCHUNK
    return pl.pallas_call(
        _mlstm_kernel,
        out_shape=jax.ShapeDtypeStruct((HEADS, ROWS, LANE), BF16),
        grid=(BATCH, nchunk),
        in_specs=[_group_spec(BLK_LX, RCHUNK), _group_spec(BLK_LV, RCHUNK), _group_spec(BLK_LO, RCHUNK),
                  pl.BlockSpec((1, RCHUNK, LANE), lambda b, c: (BLK_KPE, b * nchunk + c, 0)),
                  _group_spec(BLK_Z + 2 * HEADS, RCHUNK)] + [_layer_spec(p.shape, layer) for p in params],
        out_specs=pl.BlockSpec((HEADS, RCHUNK, LANE), lambda b, c: (0, b * nchunk + c, 0)),
        scratch_shapes=[pltpu.VMEM((HEADS, RCHUNK + CONV_PAD, HEAD_DIM), F32),
                        pltpu.VMEM((HEADS, HEAD_DIM, 2 * HEAD_DIM), F32),
                        pltpu.VMEM((HEADS, 1, 1), F32),
                        pltpu.VMEM((HEADS, RCHUNK, RCHUNK), F32),
                        pltpu.VMEM((HEADS, RCHUNK, RCHUNK), BF16)],
        compiler_params=_cparams(("parallel", "arbitrary")),
        name="mlstm",
    )(hb, hb, hb, hb, hb, *params)


def _mla_prep_kernel(lat_ref, kpe_ref, cm_ref, sm_ref, qg_ref, wuq_ref, kvg_ref, wukv_ref,
                     q_out, k_out, v_out):
    scale = (MLA_NOPE + MLA_ROPE) ** -0.5 * LOG2E
    cs = cm_ref[0]
    sn = sm_ref[0]
    cq = jnp.concatenate([lat_ref[0], lat_ref[1], lat_ref[2]], axis=-1)
    qn = cq * lax.rsqrt(jnp.mean(jnp.square(cq), axis=-1, keepdims=True) + RMS_EPS) * qg_ref[...]
    q = _dot(qn.astype(BF16), wuq_ref[...])
    ckv = lat_ref[3]
    kvn = ckv * lax.rsqrt(jnp.mean(jnp.square(ckv), axis=-1, keepdims=True) + RMS_EPS) * kvg_ref[...]
    kv = _dot(kvn.astype(BF16), wukv_ref[...])
    kpe = kpe_ref[0]
    krot = (kpe * cs + pltpu.roll(kpe, LANE // 2, 1) * sn).astype(BF16)
    for h in range(HEADS):
        qr = q[:, GROUP_WIDTH + h * LANE:GROUP_WIDTH + (h + 1) * LANE]
        qr = qr * cs + pltpu.roll(qr, LANE // 2, 1) * sn
        q_out[0, h, :, 0:LANE] = (q[:, h * LANE:(h + 1) * LANE] * scale).astype(BF16)
        q_out[0, h, :, LANE:2 * LANE] = (qr * scale).astype(BF16)
        k_out[0, h, :, 0:LANE] = kv[:, h * LANE:(h + 1) * LANE].astype(BF16)
        k_out[0, h, :, LANE:2 * LANE] = krot
        v_out[0, h, :, 0:LANE] = kv[:, GROUP_WIDTH + h * LANE:GROUP_WIDTH + (h + 1) * LANE].astype(BF16)
        v_out[0, h, :, LANE:2 * LANE] = jnp.ones((TS_PREP, LANE), BF16)


def _prep_mla_weights(w_uq, w_ukv):
    lead = w_uq.shape[:-1]
    wq = w_uq.reshape(lead + (HEADS, MLA_NOPE + MLA_ROPE))
    nope = wq[..., :MLA_NOPE].reshape(lead + (GROUP_WIDTH,))
    half = MLA_ROPE // 2
    zeros = jnp.zeros(lead + (HEADS, 64 - half), w_uq.dtype)
    rope = jnp.concatenate([wq[..., MLA_NOPE:MLA_NOPE + half], zeros, wq[..., MLA_NOPE + half:], zeros], axis=-1)
    wq_p = jnp.concatenate([nope, rope.reshape(lead + (HEADS * LANE,))], axis=-1).astype(BF16)
    lead = w_ukv.shape[:-1]
    wkv = w_ukv.reshape(lead + (HEADS, MLA_NOPE + HEAD_DIM))
    wkv_p = jnp.concatenate([wkv[..., :MLA_NOPE].reshape(lead + (GROUP_WIDTH,)),
                             wkv[..., MLA_NOPE:].reshape(lead + (GROUP_WIDTH,))], axis=-1).astype(BF16)
    return wq_p, wkv_p


def _mla_prep(hb, cos_m, sin_m, qg, wuq_p, kvg, wukv_p, layer):
    ns = SEQ // TS_PREP
    tab = pl.BlockSpec((1, TS_PREP, LANE), lambda b, i: (b, i, 0))
    qk_shape = jax.ShapeDtypeStruct((BATCH, HEADS, SEQ, 2 * LANE), BF16)
    return pl.pallas_call(
        _mla_prep_kernel,
        out_shape=(qk_shape, qk_shape, qk_shape),
        grid=(BATCH, ns),
        in_specs=[pl.BlockSpec((HEADS, TS_PREP, LANE), lambda b, i: (BLK_CQ // HEADS, b * ns + i, 0)),
                  pl.BlockSpec((1, TS_PREP, LANE), lambda b, i: (BLK_KPE, b * ns + i, 0)),
                  tab, tab, _layer_spec(qg.shape, layer), _layer_spec(wuq_p.shape, layer),
                  _layer_spec(kvg.shape, layer), _layer_spec(wukv_p.shape, layer)],
        out_specs=(pl.BlockSpec((1, HEADS, TS_PREP, 2 * LANE), lambda b, i: (b, 0, i, 0)),
                   pl.BlockSpec((1, HEADS, TS_PREP, 2 * LANE), lambda b, i: (b, 0, i, 0)),
                   pl.BlockSpec((1, HEADS, TS_PREP, 2 * LANE), lambda b, i: (b, 0, i, 0))),
        compiler_params=_cparams(("parallel", "parallel")),
        name="mla_prep",
    )(hb, hb, cos_m, sin_m, qg, wuq_p, kvg, wukv_p)


def _attn_tile(nfull, q_ref, k_ref, v_ref, z_ref, o_ref, s_ref, p_ref, m_ref):
    kvlen = (nfull + 1) * TQ_ATT
    d0 = nfull * TQ_ATT
    srow = lax.broadcasted_iota(jnp.int32, (STRIP, LANE), 0)
    scol = lax.broadcasted_iota(jnp.int32, (STRIP, LANE), 1)

    def strip_blocks(g, i):
        rows = slice(i * STRIP, (i + 1) * STRIP)
        wd = -(-(i + 1) * STRIP // LANE) * LANE
        ncol = (d0 + wd) // LANE
        blks = [s_ref[g, rows, c * LANE:(c + 1) * LANE] for c in range(ncol)]
        blks[-1] = jnp.where(scol + (wd - LANE) <= srow + i * STRIP, blks[-1], -jnp.inf)
        return rows, ncol, blks

    for g in range(ATT_HEADS):
        q = q_ref[0, g]
        for j in range(nfull + 1):
            s_ref[g, :, j * TQ_ATT:(j + 1) * TQ_ATT] = _dot_nt(q, k_ref[0, g, j * TQ_ATT:(j + 1) * TQ_ATT, :])
    for g in range(ATT_HEADS):
        for i in range(TQ_ATT // STRIP):
            rows, ncol, blks = strip_blocks(g, i)
            mx = blks[0]
            for blk in blks[1:]:
                mx = jnp.maximum(mx, blk)
            m_ref[g, rows, :] = jnp.broadcast_to(jnp.max(mx, axis=-1, keepdims=True), (STRIP, LANE))
        for i in range(TQ_ATT // STRIP):
            rows, ncol, blks = strip_blocks(g, i)
            m = m_ref[g, rows, :]
            for c, blk in enumerate(blks):
                p_ref[g, rows, c * LANE:(c + 1) * LANE] = jnp.exp2(blk - m).astype(BF16)
            if ncol * LANE < kvlen:
                p_ref[g, rows, ncol * LANE:kvlen] = jnp.zeros((STRIP, kvlen - ncol * LANE), BF16)
        pv = _dot(p_ref[g, :, 0:kvlen], v_ref[0, g, 0:kvlen, :])
        o_ref[g] = (pv[:, :LANE] / pv[:, LANE:] * _silu(z_ref[g])).astype(BF16)


def _mla_attn_kernel(q_ref, k_ref, v_ref, z_ref, o_ref, s_ref, p_ref, m_ref):
    qi = pl.program_id(2)
    for nfull in range(SEQ // TQ_ATT):
        @pl.when(qi == nfull)
        def _(nfull=nfull):
            _attn_tile(nfull, q_ref, k_ref, v_ref, z_ref, o_ref, s_ref, p_ref, m_ref)


def _mla_attn(q, k, v, hb):
    nq = SEQ // TQ_ATT
    return pl.pallas_call(
        _mla_attn_kernel,
        out_shape=jax.ShapeDtypeStruct((HEADS, ROWS, LANE), BF16),
        grid=(BATCH, HEADS // ATT_HEADS, nq),
        in_specs=[pl.BlockSpec((1, ATT_HEADS, TQ_ATT, 2 * LANE), lambda b, h, i: (b, h, i, 0)),
                  pl.BlockSpec((1, ATT_HEADS, SEQ, 2 * LANE), lambda b, h, i: (b, h, 0, 0)),
                  pl.BlockSpec((1, ATT_HEADS, SEQ, 2 * LANE), lambda b, h, i: (b, h, 0, 0)),
                  pl.BlockSpec((ATT_HEADS, TQ_ATT, LANE),
                               lambda b, h, i: ((BLK_Z + HEADS) // ATT_HEADS + h, b * nq + i, 0))],
        out_specs=pl.BlockSpec((ATT_HEADS, TQ_ATT, LANE), lambda b, h, i: (h, b * nq + i, 0)),
        scratch_shapes=[pltpu.VMEM((ATT_HEADS, TQ_ATT, SEQ), F32), pltpu.VMEM((ATT_HEADS, TQ_ATT, SEQ), BF16),
                        pltpu.VMEM((ATT_HEADS, TQ_ATT, LANE), F32)],
        compiler_params=_cparams(("parallel", "parallel", "parallel")),
        name="mla_attn",
    )(q, k, v, hb)


def _mem_attn_kernel(q_ref, k_ref, v_ref, z_ref, o_ref):
    ones = jnp.ones((MEM_LEN, LANE), BF16)
    for h in range(HEADS):
        cols = slice(h * LANE, (h + 1) * LANE)
        q = (q_ref[h] * (HEAD_DIM ** -0.5 * LOG2E)).astype(BF16)
        s = _dot_nt(q, k_ref[:, cols])
        p = jnp.exp2(s - jnp.max(s, axis=-1, keepdims=True)).astype(BF16)
        pv = _dot(p, jnp.concatenate([v_ref[:, cols], ones], axis=-1))
        o_ref[h] = (pv[:, :LANE] / pv[:, LANE:] * _silu(z_ref[h])).astype(BF16)


def _mem_attn(hb, kvm, layer):
    nq = SEQ // TQ_MEM
    return pl.pallas_call(
        _mem_attn_kernel,
        out_shape=jax.ShapeDtypeStruct((HEADS, ROWS, LANE), BF16),
        grid=(BATCH, nq),
        in_specs=[_group_spec(BLK_MQ, TQ_MEM),
                  pl.BlockSpec((MEM_LEN, GROUP_WIDTH), lambda b, i: (b, 2 * layer)),
                  pl.BlockSpec((MEM_LEN, GROUP_WIDTH), lambda b, i: (b, 2 * layer + 1)),
                  _group_spec(BLK_Z + 3 * HEADS, TQ_MEM)],
        out_specs=pl.BlockSpec((HEADS, TQ_MEM, LANE), lambda b, i: (0, b * nq + i, 0)),
        compiler_params=_cparams(("parallel", "parallel")),
        name="mem_attn",
    )(hb, kvm, kvm, hb)


def _out_kernel(ya_ref, yb_ref, yc_ref, yd_ref, w_ref, x_ref, g_ref, b_ref, o_ref):
    for t in range(TM_OUT // SUB_OUT):
        rows = slice(t * SUB_OUT, (t + 1) * SUB_OUT)
        parts = [ref[h, rows, :] for ref in (ya_ref, yb_ref, yc_ref, yd_ref) for h in range(HEADS)]
        y = jnp.concatenate(parts, axis=-1)
        r = DEEPNORM_ALPHA * x_ref[rows, :] + _dot(y, w_ref[...])
        mu = jnp.mean(r, axis=-1, keepdims=True)
        var = jnp.mean(jnp.square(r - mu), axis=-1, keepdims=True)
        o_ref[rows, :] = (r - mu) * lax.rsqrt(var + LN_EPS) * g_ref[...] + b_ref[...]


def _outproj(ya, yb, yc, yd, w_out, x2d, ln_g, ln_b, layer):
    yspec = pl.BlockSpec((HEADS, TM_OUT, LANE), lambda i: (0, i, 0))
    return pl.pallas_call(
        _out_kernel,
        out_shape=jax.ShapeDtypeStruct((ROWS, D_MODEL), F32),
        grid=(ROWS // TM_OUT,),
        in_specs=[yspec, yspec, yspec, yspec, _layer_spec(w_out.shape, layer),
                  pl.BlockSpec((TM_OUT, D_MODEL), lambda i: (i, 0)),
                  _layer_spec(ln_g.shape, layer), _layer_spec(ln_b.shape, layer)],
        out_specs=pl.BlockSpec((TM_OUT, D_MODEL), lambda i: (i, 0)),
        compiler_params=_cparams(("parallel",)),
        name="outproj_ln",
    )(ya, yb, yc, yd, w_out, x2d, ln_g, ln_b)


def kernel(x, mem, positions, w_in, ret_norm_g, mla_q_norm_g, mla_w_uq, mla_kv_norm_g, mla_w_ukv, ml_conv_w, ml_conv_b, ml_w_q, ml_w_k, ml_i_bias, ml_f_bias, ml_skip, ml_norm_g, w_mem_kv, w_out, ln_g, ln_b):
    assert x.shape == (BATCH, SEQ, D_MODEL) and mem.shape == (BATCH, MEM_LEN, D_MODEL)
    cos_r, sin_r, cos_m, sin_m = _rope_tables(positions)
    kvm = _mem_kv(mem, w_mem_kv)
    w_in_p = _prep_w_in(w_in)
    wuq_p, wukv_p = _prep_mla_weights(mla_w_uq, mla_w_ukv)
    w_out_b = w_out.astype(BF16)
    ret_consts = _ret_consts()
    ret_g = ret_norm_g.reshape(DEPTH, HEADS, 1, HEAD_DIM)
    mlstm_params = _prep_mlstm(ml_conv_w, ml_conv_b, ml_w_q, ml_w_k, ml_i_bias, ml_f_bias, ml_skip, ml_norm_g)
    qg = mla_q_norm_g.reshape(DEPTH, 1, MLA_Q_RANK)
    kvg = mla_kv_norm_g.reshape(DEPTH, 1, MLA_KV_RANK)
    lng = ln_g.reshape(DEPTH, 1, D_MODEL)
    lnb = ln_b.reshape(DEPTH, 1, D_MODEL)

    x2d = x.reshape(ROWS, D_MODEL)
    for l in range(DEPTH):
        hb = _inproj(x2d, w_in_p, l)
        ya = _retention(hb, cos_r, sin_r, ret_consts, ret_g, l)
        q, k, v = _mla_prep(hb, cos_m, sin_m, qg, wuq_p, kvg, wukv_p, l)
        yb = _mla_attn(q, k, v, hb)
        yc = _mlstm(hb, mlstm_params, l)
        yd = _mem_attn(hb, kvm, l)
        x2d = _outproj(ya, yb, yc, yd, w_out_b, x2d, lng, lnb, l)
    return x2d.reshape(BATCH, SEQ, D_MODEL)
```

```python
import numpy as np
import jax
import jax.numpy as jnp
from jax import lax
from jax.experimental import pallas as pl
from jax.experimental.pallas import tpu as pltpu

F32 = jnp.float32
BF16 = jnp.bfloat16

D_MODEL = 2048
BATCH = 8
SEQ = 2048
DEPTH = 4
MEM_LEN = 256
HEAD_DIM = 128
HEADS = 4
GROUP_WIDTH = HEADS * HEAD_DIM
MLA_NOPE = 128
MLA_ROPE = 64
MLA_Q_RANK = 384
MLA_KV_RANK = 128
CONV_WIDTH = 4
MIX_WIDTH = 4 * GROUP_WIDTH
ROPE_THETA = 10000.0
LN_EPS = 1e-5
RMS_EPS = 1e-6
DEEPNORM_ALPHA = (2 * DEPTH) ** 0.25
IN_SPLITS = (512, 512, 512, MLA_Q_RANK, MLA_KV_RANK, MLA_ROPE, 512, 512, 512, HEADS, HEADS, 512, MIX_WIDTH)

LANE = 128
ROWS = BATCH * SEQ

BLK_RQ, BLK_RK, BLK_RV = 0, 4, 8
BLK_LX, BLK_LV, BLK_LO = 12, 16, 20
BLK_MQ = 24
BLK_Z = 28
BLK_CQ, BLK_CKV = 44, 47
BLK_KPE = 48
NBLK = 50
GATE_I_LANE, GATE_F_LANE = 32, 36

TM_IN, TN_IN = 1024, 1280
TM_OUT = 512
SUB_OUT = 256
TS_PREP = 512
TQ_ATT = 512
ATT_HEADS = 2
TQ_MEM = 512
RCHUNK = 512
CONV_PAD = 8
STRIP = 64
LOG2E = 1.4426950408889634
VMEM_LIMIT = 56 * 1024 * 1024


def _cparams(sem):
    return pltpu.CompilerParams(dimension_semantics=sem, vmem_limit_bytes=VMEM_LIMIT)


def _silu(z):
    return z * jax.nn.sigmoid(z)


def _log_sigmoid(x):
    return jnp.minimum(x, 0.0) - jnp.log1p(jnp.exp(-jnp.abs(x)))


def _split3(x):
    x1 = x.astype(BF16)
    r1 = x - x1.astype(F32)
    x2 = r1.astype(BF16)
    x3 = (r1 - x2.astype(F32)).astype(BF16)
    return x1, x2, x3


def _dot(a, b):
    return jnp.dot(a, b, preferred_element_type=F32)


def _rowsum_mxu(x, w):
    hi = x.astype(BF16)
    lo = (x - hi.astype(F32)).astype(BF16)
    return _dot(hi, w) + _dot(lo, w)


def _dot_nt(a, b):
    return lax.dot_general(a, b, (((1,), (1,)), ((), ())), preferred_element_type=F32)


def _tables_kernel(pos_ref, c_ref, cr_ref, sr_ref, cm_ref, sm_ref):
    pos = pos_ref[0].astype(F32)
    ang = pos * c_ref[0:1, :]
    cs = jnp.cos(ang)
    sn = jnp.sin(ang)
    cs_sw = pltpu.roll(cs, LANE // 2, 1)
    sn_sw = pltpu.roll(sn, LANE // 2, 1)
    lane = lax.broadcasted_iota(jnp.int32, cs.shape, 1)
    half_m = MLA_ROPE // 2
    first = lane < LANE // 2
    m_lo = lane < half_m
    m_hi = (lane >= LANE // 2) & (lane < LANE // 2 + half_m)
    cr_ref[0] = jnp.where(first, cs, cs_sw)
    sr_ref[0] = jnp.where(first, -sn, sn_sw)
    cm_ref[0] = jnp.where(m_lo, cs_sw, jnp.where(m_hi, cs, 0.0))
    sm_ref[0] = jnp.where(m_lo, -sn_sw, jnp.where(m_hi, sn, 0.0))


def _rope_tables(positions):
    half_r = HEAD_DIM // 2
    fr = ROPE_THETA ** (-jnp.arange(half_r, dtype=F32) / half_r)
    half_m = MLA_ROPE // 2
    fm = ROPE_THETA ** (-jnp.arange(half_m, dtype=F32) / half_m)
    freqs = jnp.concatenate([fr, fm, jnp.zeros((LANE - half_r - half_m,), F32)])
    consts = jnp.concatenate([freqs[None, :], jnp.zeros((7, LANE), F32)], axis=0)
    ts = 512
    tab = jax.ShapeDtypeStruct((BATCH, SEQ, LANE), F32)
    spec = pl.BlockSpec((1, ts, LANE), lambda b, i: (b, i, 0))
    return pl.pallas_call(
        _tables_kernel,
        out_shape=(tab, tab, tab, tab),
        grid=(BATCH, SEQ // ts),
        in_specs=[pl.BlockSpec((1, ts, 1), lambda b, i: (b, i, 0)),
                  pl.BlockSpec((8, LANE), lambda b, i: (0, 0))],
        out_specs=(spec, spec, spec, spec),
        compiler_params=_cparams(("parallel", "parallel")),
        name="rope_tables",
    )(positions.reshape(BATCH, SEQ, 1), consts)


def _matmul_kernel(a_ref, w_ref, o_ref):
    o_ref[...] = _dot(a_ref[...].astype(BF16), w_ref[...]).astype(o_ref.dtype)


def _mem_kv(mem, w_mem_kv):
    w = jnp.transpose(w_mem_kv, (1, 0, 2)).reshape(D_MODEL, DEPTH * 2 * GROUP_WIDTH).astype(BF16)
    a = mem.reshape(BATCH * MEM_LEN, D_MODEL)
    tm, tn = 512, 1024
    return pl.pallas_call(
        _matmul_kernel,
        out_shape=jax.ShapeDtypeStruct((BATCH * MEM_LEN, DEPTH * 2 * GROUP_WIDTH), BF16),
        grid=(BATCH * MEM_LEN // tm, DEPTH * 2 * GROUP_WIDTH // tn),
        in_specs=[pl.BlockSpec((tm, D_MODEL), lambda i, j: (i, 0)),
                  pl.BlockSpec((D_MODEL, tn), lambda i, j: (0, j))],
        out_specs=pl.BlockSpec((tm, tn), lambda i, j: (i, j)),
        compiler_params=_cparams(("parallel", "parallel")),
        name="mem_kv",
    )(a, w)


def _inproj_kernel(x_ref, w_ref, o_ref, xb_ref):
    @pl.when(pl.program_id(1) == 0)
    def _():
        xb_ref[...] = x_ref[...].astype(BF16)

    xb = xb_ref[...]
    for k2 in range(TN_IN // 256):
        r = _dot_nt(xb, w_ref[k2 * 256:(k2 + 1) * 256, :])
        o_ref[2 * k2] = r[:, :LANE]
        o_ref[2 * k2 + 1] = r[:, LANE:]


def _inproj(x2d, wt_p, layer):
    nb = TN_IN // LANE
    return pl.pallas_call(
        _inproj_kernel,
        out_shape=jax.ShapeDtypeStruct((NBLK, ROWS, LANE), F32),
        grid=(ROWS // TM_IN, NBLK * LANE // TN_IN),
        in_specs=[pl.BlockSpec((TM_IN, D_MODEL), lambda i, j: (i, 0)),
                  pl.BlockSpec((None, TN_IN, D_MODEL), lambda i, j: (layer, j, 0))],
        out_specs=pl.BlockSpec((nb, TM_IN, LANE), lambda i, j: (j, i, 0)),
        scratch_shapes=[pltpu.VMEM((TM_IN, D_MODEL), BF16)],
        compiler_params=_cparams(("parallel", "arbitrary")),
        name="inproj",
    )(x2d, wt_p)


def _prep_w_in(w_in):
    wt = jnp.swapaxes(w_in, 1, 2)
    idx = np.cumsum(IN_SPLITS)[:-1].tolist()
    (r_q, r_k, r_v, a_cq, a_ckv, a_kpe, l_x, l_v, l_o, l_i, l_f, c_q, z) = jnp.split(wt, idx, axis=1)
    half = MLA_ROPE // 2
    zeros = lambda n: jnp.zeros((wt.shape[0], n, wt.shape[2]), wt.dtype)
    blk = [a_kpe[:, :half], l_i, l_f, zeros(64 - half - 2 * HEADS), a_kpe[:, half:], zeros(64 - half)]
    return jnp.concatenate([r_q, r_k, r_v, l_x, l_v, l_o, c_q, z, a_cq, a_ckv] + blk + [zeros(LANE)],
                           axis=1).astype(BF16)


def _group_spec(base, rows):
    nchunk = SEQ // rows
    return pl.BlockSpec((HEADS, rows, LANE), lambda b, c: (base // HEADS, b * nchunk + c, 0))


def _layer_spec(shape, layer):
    nd = len(shape) - 1
    return pl.BlockSpec((None,) + tuple(shape[1:]), lambda *_: (layer,) + (0,) * nd)


def _const_spec(shape):
    nd = len(shape)
    return pl.BlockSpec(tuple(shape), lambda *_: (0,) * nd)


def _ret_kernel(q_ref, k_ref, v_ref, z_ref, cos_ref, sin_ref, gq_ref, gk_ref, gc_ref, g_ref, o_ref, st_ref):
    @pl.when(pl.program_id(1) == 0)
    def _():
        st_ref[...] = jnp.zeros_like(st_ref)

    cs = cos_ref[0]
    sn = sin_ref[0]
    row = lax.broadcasted_iota(jnp.int32, (RCHUNK, RCHUNK), 0)
    col = lax.broadcasted_iota(jnp.int32, (RCHUNK, RCHUNK), 1)
    causal = col <= row
    for h in range(HEADS):
        q = q_ref[h]
        q = (q * cs + pltpu.roll(q, HEAD_DIM // 2, 1) * sn) * gq_ref[h]
        k = k_ref[h]
        k = (k * cs + pltpu.roll(k, HEAD_DIM // 2, 1) * sn) * gk_ref[h]
        qb = q.astype(BF16)
        kb = k.astype(BF16)
        vb = v_ref[h].astype(BF16)
        sc = jnp.where(causal, _dot_nt(qb, kb), 0.0)
        st = st_ref[h]
        out = _dot(sc.astype(BF16), vb) + _dot(qb, st.astype(BF16))
        st_ref[h] = gc_ref[h] * (st + _dot(k.T.astype(BF16), vb))
        mu = jnp.mean(out, axis=-1, keepdims=True)
        dev = out - mu
        var = jnp.mean(dev * dev, axis=-1, keepdims=True)
        hn = dev * lax.rsqrt(var + LN_EPS) * g_ref[h]
        o_ref[h] = (hn * _silu(z_ref[h])).astype(BF16)


def _ret_consts():
    log_g = jnp.log1p(-jnp.exp2(-5.0 - jnp.arange(HEADS, dtype=F32)))
    idx = jnp.arange(RCHUNK, dtype=F32)
    full = (HEADS, RCHUNK, HEAD_DIM)
    gq = jnp.broadcast_to(jnp.exp(log_g[:, None] * (idx + 1.0))[..., None], full)
    gk = jnp.broadcast_to((jnp.exp(-log_g[:, None] * (idx + 1.0)) * HEAD_DIM ** -0.5)[..., None], full)
    gc = jnp.broadcast_to(jnp.exp(log_g * RCHUNK)[:, None, None], (HEADS, 1, HEAD_DIM))
    return gq, gk, gc


def _retention(hb, cos_r, sin_r, consts, norm_g, layer):
    gq, gk, gc = consts
    nchunk = SEQ // RCHUNK
    tab = pl.BlockSpec((1, RCHUNK, LANE), lambda b, c: (b, c, 0))
    return pl.pallas_call(
        _ret_kernel,
        out_shape=jax.ShapeDtypeStruct((HEADS, ROWS, LANE), BF16),
        grid=(BATCH, nchunk),
        in_specs=[_group_spec(BLK_RQ, RCHUNK), _group_spec(BLK_RK, RCHUNK), _group_spec(BLK_RV, RCHUNK),
                  _group_spec(BLK_Z, RCHUNK), tab, tab,
                  _const_spec(gq.shape), _const_spec(gk.shape), _const_spec(gc.shape),
                  _layer_spec(norm_g.shape, layer)],
        out_specs=pl.BlockSpec((HEADS, RCHUNK, LANE), lambda b, c: (0, b * nchunk + c, 0)),
        scratch_shapes=[pltpu.VMEM((HEADS, HEAD_DIM, HEAD_DIM), F32)],
        compiler_params=_cparams(("parallel", "arbitrary")),
        name="retention",
    )(hb, hb, hb, hb, cos_r, sin_r, gq, gk, gc, norm_g)


def _mlstm_kernel(lx_ref, lv_ref, lo_ref, gt_ref, z_ref, cw_ref, cb_ref, wq_ref, wk_ref, gb_ref,
                  skip_ref, g_ref, o_ref, xp_ref, st_ref, m_ref, s_ref, p_ref):
    @pl.when(pl.program_id(1) == 0)
    def _():
        xp_ref[:, 0:CONV_PAD, :] = jnp.zeros((HEADS, CONV_PAD, HEAD_DIM), F32)
        st_ref[...] = jnp.zeros_like(st_ref)
        m_ref[...] = jnp.zeros_like(m_ref)

    n = RCHUNK
    ones_nd = jnp.ones((n, HEAD_DIM), BF16)
    mean_dd = jnp.full((HEAD_DIM, HEAD_DIM), 1.0 / HEAD_DIM, BF16)
    row = lax.broadcasted_iota(jnp.int32, (n, n), 0)
    col = lax.broadcasted_iota(jnp.int32, (n, n), 1)
    tril = jnp.where(col <= row, 1.0, 0.0).astype(BF16)
    r128 = lax.broadcasted_iota(jnp.int32, (LANE, LANE), 0)
    c128 = lax.broadcasted_iota(jnp.int32, (LANE, LANE), 1)
    ident = jnp.where(r128 == c128, 1.0, 0.0).astype(BF16)
    srow = lax.broadcasted_iota(jnp.int32, (STRIP, LANE), 0)
    scol = lax.broadcasted_iota(jnp.int32, (STRIP, LANE), 1)

    lane = lax.broadcasted_iota(jnp.int32, (n, LANE), 1)
    x = gt_ref[0] + gb_ref[...]
    x = jnp.where((lane >= GATE_F_LANE) & (lane < GATE_F_LANE + HEADS), _log_sigmoid(x), x) * LOG2E
    x1, x2, x3 = _split3(x)
    cum = _dot(tril, x1) + _dot(tril, x2) + _dot(tril, x3)
    rt = x - pltpu.roll(cum, LANE - (GATE_F_LANE - GATE_I_LANE), 1)
    r1, r2, r3 = _split3(rt)
    rtt = _dot_nt(ident, r1) + _dot_nt(ident, r2) + _dot_nt(ident, r3)

    for h in range(HEADS):
        xp_ref[h, CONV_PAD:, :] = lx_ref[h]
        acc = jnp.zeros((n, HEAD_DIM), F32) + cb_ref[h]
        for j in range(CONV_WIDTH):
            off = CONV_PAD - (CONV_WIDTH - 1) + j
            acc = acc + xp_ref[h, off:off + n, :] * cw_ref[h, j:j + 1, :]
        xp_ref[h, 0:CONV_PAD, :] = lx_ref[h, n - CONV_PAD:n, :]
        xc = _silu(acc)
        xcb = xc.astype(BF16)
        q = _dot(xcb, wq_ref[h])
        k = _dot(xcb, wk_ref[h]) * HEAD_DIM ** -0.5
        qb = q.astype(BF16)
        kb = k.astype(BF16)
        vb = jnp.concatenate([lv_ref[h].astype(BF16), ones_nd], axis=-1)
        s_ref[h] = _dot_nt(qb, kb)

        li = GATE_I_LANE + h
        lf = GATE_F_LANE + h
        r_row = rtt[li:li + 1, :]
        m_st = m_ref[h]

        base = m_st
        us = []
        for i in range(n // STRIP):
            rows = slice(i * STRIP, (i + 1) * STRIP)
            d0 = (i * STRIP // LANE) * LANE
            w = d0 + LANE
            if d0 > 0 and (i * STRIP) % LANE == 0:
                base = jnp.maximum(base, jnp.max(r_row[:, d0 - LANE:d0], axis=-1, keepdims=True))
            mask = (scol + d0) <= (srow + i * STRIP)
            rmd = jnp.where(mask, r_row[:, d0:w], -jnp.inf)
            u_col = jnp.maximum(jnp.max(rmd, axis=-1, keepdims=True), base)
            u = jnp.broadcast_to(u_col, (STRIP, LANE))
            p_ref[h, rows, d0:w] = (s_ref[h, rows, d0:w] * jnp.exp2(rmd - u)).astype(BF16)
            for c in range(d0 // LANE):
                cols = slice(c * LANE, (c + 1) * LANE)
                p_ref[h, rows, cols] = (s_ref[h, rows, cols] * jnp.exp2(r_row[:, cols] - u)).astype(BF16)
            if w < n:
                p_ref[h, rows, w:n] = jnp.zeros((STRIP, n - w), BF16)
            us.append(u)
            u_last = u_col[STRIP - 1:STRIP, :]
        u = jnp.concatenate(us, axis=0)

        w_inter = jnp.exp2(m_st - u)
        st = st_ref[h]
        intra = _dot(p_ref[h], vb)
        inter = _dot(qb, st.astype(BF16))
        num = intra[:, :LANE] + w_inter * inter[:, :LANE]
        den = intra[:, LANE:] + w_inter * inter[:, LANE:]
        cum_f = jnp.broadcast_to(cum[:, lf:lf + 1], (n, LANE))
        hcell = num / jnp.maximum(jnp.abs(den), jnp.exp2(-(cum_f + u)))

        decay = jnp.exp2(m_st - u_last)
        kw = k * jnp.exp2(jnp.broadcast_to(rt[:, li:li + 1], (n, LANE)) - u_last)
        upd = _dot(kw.T.astype(BF16), vb)
        st_ref[h] = decay * st + upd
        m_ref[h] = cum[n - 1:n, lf:lf + 1] + u_last

        cell = hcell * jax.nn.sigmoid(lo_ref[h])
        mu = _rowsum_mxu(cell, mean_dd)
        dev = cell - mu
        var = _rowsum_mxu(dev * dev, mean_dd)
        hn = dev * lax.rsqrt(var + LN_EPS) * g_ref[h]
        o_ref[h] = ((hn + skip_ref[h] * xc) * _silu(z_ref[h])).astype(BF16)


def _prep_mlstm(conv_w, conv_b, w_q, w_k, i_bias, f_bias, skip, norm_g):
    nl = conv_w.shape[0]
    cw = conv_w.reshape(nl, CONV_WIDTH, HEADS, HEAD_DIM).transpose(0, 2, 1, 3)
    cb = conv_b.reshape(nl, HEADS, 1, HEAD_DIM)
    zeros = lambda w: jnp.zeros((nl, w), F32)
    gb = jnp.concatenate([zeros(GATE_I_LANE), i_bias, f_bias, zeros(LANE - GATE_F_LANE - HEADS)], axis=-1)
    return (cw, cb, w_q.astype(BF16), w_k.astype(BF16), gb.reshape(nl, 1, LANE),
            skip.reshape(nl, HEADS, 1, HEAD_DIM), norm_g.reshape(nl, HEADS, 1, HEAD_DIM))


def _mlstm(hb, params, layer):
    nchunk = SEQ // RCHUNK
    return pl.pallas_call(
        _mlstm_kernel,
        out_shape=jax.ShapeDtypeStruct((HEADS, ROWS, LANE), BF16),
        grid=(BATCH, nchunk),
        in_specs=[_group_spec(BLK_LX, RCHUNK), _group_spec(BLK_LV, RCHUNK), _group_spec(BLK_LO, RCHUNK),
                  pl.BlockSpec((1, RCHUNK, LANE), lambda b, c: (BLK_KPE, b * nchunk + c, 0)),
                  _group_spec(BLK_Z + 2 * HEADS, RCHUNK)] + [_layer_spec(p.shape, layer) for p in params],
        out_specs=pl.BlockSpec((HEADS, RCHUNK, LANE), lambda b, c: (0, b * nchunk + c, 0)),
        scratch_shapes=[pltpu.VMEM((HEADS, RCHUNK + CONV_PAD, HEAD_DIM), F32),
                        pltpu.VMEM((HEADS, HEAD_DIM, 2 * HEAD_DIM), F32),
                        pltpu.VMEM((HEADS, 1, 1), F32),
                        pltpu.VMEM((HEADS, RCHUNK, RCHUNK), F32),
                        pltpu.VMEM((HEADS, RCHUNK, RCHUNK), BF16)],
        compiler_params=_cparams(("parallel", "arbitrary")),
        name="mlstm",
    )(hb, hb, hb, hb, hb, *params)


def _mla_prep_kernel(lat_ref, kpe_ref, cm_ref, sm_ref, qg_ref, wuq_ref, kvg_ref, wukv_ref,
                     q_out, k_out, v_out):
    scale = (MLA_NOPE + MLA_ROPE) ** -0.5 * LOG2E
    cs = cm_ref[0]
    sn = sm_ref[0]
    cq = jnp.concatenate([lat_ref[0], lat_ref[1], lat_ref[2]], axis=-1)
    qn = cq * lax.rsqrt(jnp.mean(jnp.square(cq), axis=-1, keepdims=True) + RMS_EPS) * qg_ref[...]
    q = _dot(qn.astype(BF16), wuq_ref[...])
    ckv = lat_ref[3]
    kvn = ckv * lax.rsqrt(jnp.mean(jnp.square(ckv), axis=-1, keepdims=True) + RMS_EPS) * kvg_ref[...]
    kv = _dot(kvn.astype(BF16), wukv_ref[...])
    kpe = kpe_ref[0]
    krot = (kpe * cs + pltpu.roll(kpe, LANE // 2, 1) * sn).astype(BF16)
    for h in range(HEADS):
        qr = q[:, GROUP_WIDTH + h * LANE:GROUP_WIDTH + (h + 1) * LANE]
        qr = qr * cs + pltpu.roll(qr, LANE // 2, 1) * sn
        q_out[0, h, :, 0:LANE] = (q[:, h * LANE:(h + 1) * LANE] * scale).astype(BF16)
        q_out[0, h, :, LANE:2 * LANE] = (qr * scale).astype(BF16)
        k_out[0, h, :, 0:LANE] = kv[:, h * LANE:(h + 1) * LANE].astype(BF16)
        k_out[0, h, :, LANE:2 * LANE] = krot
        v_out[0, h] = kv[:, GROUP_WIDTH + h * LANE:GROUP_WIDTH + (h + 1) * LANE].astype(BF16)


def _prep_mla_weights(w_uq, w_ukv):
    lead = w_uq.shape[:-1]
    wq = w_uq.reshape(lead + (HEADS, MLA_NOPE + MLA_ROPE))
    nope = wq[..., :MLA_NOPE].reshape(lead + (GROUP_WIDTH,))
    half = MLA_ROPE // 2
    zeros = jnp.zeros(lead + (HEADS, 64 - half), w_uq.dtype)
    rope = jnp.concatenate([wq[..., MLA_NOPE:MLA_NOPE + half], zeros, wq[..., MLA_NOPE + half:], zeros], axis=-1)
    wq_p = jnp.concatenate([nope, rope.reshape(lead + (HEADS * LANE,))], axis=-1).astype(BF16)
    lead = w_ukv.shape[:-1]
    wkv = w_ukv.reshape(lead + (HEADS, MLA_NOPE + HEAD_DIM))
    wkv_p = jnp.concatenate([wkv[..., :MLA_NOPE].reshape(lead + (GROUP_WIDTH,)),
                             wkv[..., MLA_NOPE:].reshape(lead + (GROUP_WIDTH,))], axis=-1).astype(BF16)
    return wq_p, wkv_p


def _mla_prep(hb, cos_m, sin_m, qg, wuq_p, kvg, wukv_p, layer):
    ns = SEQ // TS_PREP
    tab = pl.BlockSpec((1, TS_PREP, LANE), lambda b, i: (b, i, 0))
    qk_shape = jax.ShapeDtypeStruct((BATCH, HEADS, SEQ, 2 * LANE), BF16)
    v_shape = jax.ShapeDtypeStruct((BATCH, HEADS, SEQ, LANE), BF16)
    return pl.pallas_call(
        _mla_prep_kernel,
        out_shape=(qk_shape, qk_shape, v_shape),
        grid=(BATCH, ns),
        in_specs=[pl.BlockSpec((HEADS, TS_PREP, LANE), lambda b, i: (BLK_CQ // HEADS, b * ns + i, 0)),
                  pl.BlockSpec((1, TS_PREP, LANE), lambda b, i: (BLK_KPE, b * ns + i, 0)),
                  tab, tab, _layer_spec(qg.shape, layer), _layer_spec(wuq_p.shape, layer),
                  _layer_spec(kvg.shape, layer), _layer_spec(wukv_p.shape, layer)],
        out_specs=(pl.BlockSpec((1, HEADS, TS_PREP, 2 * LANE), lambda b, i: (b, 0, i, 0)),
                   pl.BlockSpec((1, HEADS, TS_PREP, 2 * LANE), lambda b, i: (b, 0, i, 0)),
                   pl.BlockSpec((1, HEADS, TS_PREP, LANE), lambda b, i: (b, 0, i, 0))),
        compiler_params=_cparams(("parallel", "parallel")),
        name="mla_prep",
    )(hb, hb, cos_m, sin_m, qg, wuq_p, kvg, wukv_p)


def _attn_tile(nfull, q_ref, k_ref, v_ref, z_ref, o_ref, s_ref, p_ref, m_ref):
    kvlen = (nfull + 1) * TQ_ATT
    d0 = nfull * TQ_ATT
    hq = TQ_ATT // 2
    srow = lax.broadcasted_iota(jnp.int32, (STRIP, LANE), 0)
    scol = lax.broadcasted_iota(jnp.int32, (STRIP, LANE), 1)

    def strip_blocks(g, i):
        rows = slice(i * STRIP, (i + 1) * STRIP)
        wd = -(-(i + 1) * STRIP // LANE) * LANE
        ncol = (d0 + wd) // LANE
        blks = [s_ref[g, rows, c * LANE:(c + 1) * LANE] for c in range(ncol)]
        blks[-1] = jnp.where(scol + (wd - LANE) <= srow + i * STRIP, blks[-1], -jnp.inf)
        return rows, ncol, blks

    for g in range(ATT_HEADS):
        q = q_ref[0, g]
        for j in range(nfull):
            s_ref[g, :, j * TQ_ATT:(j + 1) * TQ_ATT] = _dot_nt(q, k_ref[0, g, j * TQ_ATT:(j + 1) * TQ_ATT, :])
        s_ref[g, 0:hq, d0:d0 + hq] = _dot_nt(q[0:hq], k_ref[0, g, d0:d0 + hq, :])
        s_ref[g, hq:TQ_ATT, d0:kvlen] = _dot_nt(q[hq:TQ_ATT], k_ref[0, g, d0:kvlen, :])
    for g in range(ATT_HEADS):
        for i in range(TQ_ATT // STRIP):
            rows, ncol, blks = strip_blocks(g, i)
            mx = blks[0]
            for blk in blks[1:]:
                mx = jnp.maximum(mx, blk)
            m_ref[g, rows, :] = jnp.broadcast_to(jnp.max(mx, axis=-1, keepdims=True), (STRIP, LANE))
        for i in range(TQ_ATT // STRIP):
            rows, ncol, blks = strip_blocks(g, i)
            m = m_ref[g, rows, :]
            for c, blk in enumerate(blks):
                p_ref[g, rows, c * LANE:(c + 1) * LANE] = jnp.exp2(blk - m).astype(BF16)
            kl = d0 + hq if (i + 1) * STRIP <= hq else kvlen
            if ncol * LANE < kl:
                p_ref[g, rows, ncol * LANE:kl] = jnp.zeros((STRIP, kl - ncol * LANE), BF16)
        v1 = jnp.concatenate([v_ref[0, g, 0:kvlen, :], jnp.ones((kvlen, LANE), BF16)], axis=-1)
        for r0, r1, kl in ((0, hq, d0 + hq), (hq, TQ_ATT, kvlen)):
            pv = _dot(p_ref[g, r0:r1, 0:kl], v1[0:kl])
            o_ref[g, r0:r1, :] = (pv[:, :LANE] / pv[:, LANE:] * _silu(z_ref[g, r0:r1, :])).astype(BF16)


def _mla_attn_kernel(q_ref, k_ref, v_ref, z_ref, o_ref, s_ref, p_ref, m_ref):
    qi = pl.program_id(2)
    for nfull in range(SEQ // TQ_ATT):
        @pl.when(qi == nfull)
        def _(nfull=nfull):
            _attn_tile(nfull, q_ref, k_ref, v_ref, z_ref, o_ref, s_ref, p_ref, m_ref)


def _mla_attn(q, k, v, hb):
    nq = SEQ // TQ_ATT
    return pl.pallas_call(
        _mla_attn_kernel,
        out_shape=jax.ShapeDtypeStruct((HEADS, ROWS, LANE), BF16),
        grid=(BATCH, HEADS // ATT_HEADS, nq),
        in_specs=[pl.BlockSpec((1, ATT_HEADS, TQ_ATT, 2 * LANE), lambda b, h, i: (b, h, i, 0)),
                  pl.BlockSpec((1, ATT_HEADS, SEQ, 2 * LANE), lambda b, h, i: (b, h, 0, 0)),
                  pl.BlockSpec((1, ATT_HEADS, SEQ, LANE), lambda b, h, i: (b, h, 0, 0)),
                  pl.BlockSpec((ATT_HEADS, TQ_ATT, LANE),
                               lambda b, h, i: ((BLK_Z + HEADS) // ATT_HEADS + h, b * nq + i, 0))],
        out_specs=pl.BlockSpec((ATT_HEADS, TQ_ATT, LANE), lambda b, h, i: (h, b * nq + i, 0)),
        scratch_shapes=[pltpu.VMEM((ATT_HEADS, TQ_ATT, SEQ), F32), pltpu.VMEM((ATT_HEADS, TQ_ATT, SEQ), BF16),
                        pltpu.VMEM((ATT_HEADS, TQ_ATT, LANE), F32)],
        compiler_params=_cparams(("parallel", "parallel", "parallel")),
        name="mla_attn",
    )(q, k, v, hb)


def _mem_attn_kernel(q_ref, k_ref, v_ref, z_ref, o_ref):
    ones = jnp.ones((MEM_LEN, LANE), BF16)
    for h in range(HEADS):
        cols = slice(h * LANE, (h + 1) * LANE)
        q = (q_ref[h] * (HEAD_DIM ** -0.5 * LOG2E)).astype(BF16)
        s = _dot_nt(q, k_ref[:, cols])
        p = jnp.exp2(s - jnp.max(s, axis=-1, keepdims=True)).astype(BF16)
        pv = _dot(p, jnp.concatenate([v_ref[:, cols], ones], axis=-1))
        o_ref[h] = (pv[:, :LANE] / pv[:, LANE:] * _silu(z_ref[h])).astype(BF16)


def _mem_attn(hb, kvm, layer):
    nq = SEQ // TQ_MEM
    return pl.pallas_call(
        _mem_attn_kernel,
        out_shape=jax.ShapeDtypeStruct((HEADS, ROWS, LANE), BF16),
        grid=(BATCH, nq),
        in_specs=[_group_spec(BLK_MQ, TQ_MEM),
                  pl.BlockSpec((MEM_LEN, GROUP_WIDTH), lambda b, i: (b, 2 * layer)),
                  pl.BlockSpec((MEM_LEN, GROUP_WIDTH), lambda b, i: (b, 2 * layer + 1)),
                  _group_spec(BLK_Z + 3 * HEADS, TQ_MEM)],
        out_specs=pl.BlockSpec((HEADS, TQ_MEM, LANE), lambda b, i: (0, b * nq + i, 0)),
        compiler_params=_cparams(("parallel", "parallel")),
        name="mem_attn",
    )(hb, kvm, kvm, hb)


def _out_kernel(ya_ref, yb_ref, yc_ref, yd_ref, w_ref, x_ref, g_ref, b_ref, o_ref):
    for t in range(TM_OUT // SUB_OUT):
        rows = slice(t * SUB_OUT, (t + 1) * SUB_OUT)
        parts = [ref[h, rows, :] for ref in (ya_ref, yb_ref, yc_ref, yd_ref) for h in range(HEADS)]
        y = jnp.concatenate(parts, axis=-1)
        r = DEEPNORM_ALPHA * x_ref[rows, :] + _dot(y, w_ref[...])
        mu = jnp.mean(r, axis=-1, keepdims=True)
        var = jnp.mean(jnp.square(r - mu), axis=-1, keepdims=True)
        o_ref[rows, :] = (r - mu) * lax.rsqrt(var + LN_EPS) * g_ref[...] + b_ref[...]


def _outproj(ya, yb, yc, yd, w_out, x2d, ln_g, ln_b, layer):
    yspec = pl.BlockSpec((HEADS, TM_OUT, LANE), lambda i: (0, i, 0))
    return pl.pallas_call(
        _out_kernel,
        out_shape=jax.ShapeDtypeStruct((ROWS, D_MODEL), F32),
        grid=(ROWS // TM_OUT,),
        in_specs=[yspec, yspec, yspec, yspec,
                  pl.BlockSpec((None,) + tuple(w_out.shape[1:]), lambda i: (layer, 0, 0),
                               pipeline_mode=pl.Buffered(1)),
                  pl.BlockSpec((TM_OUT, D_MODEL), lambda i: (i, 0)),
                  _layer_spec(ln_g.shape, layer), _layer_spec(ln_b.shape, layer)],
        out_specs=pl.BlockSpec((TM_OUT, D_MODEL), lambda i: (i, 0)),
        compiler_params=_cparams(("parallel",)),
        name="outproj_ln",
    )(ya, yb, yc, yd, w_out, x2d, ln_g, ln_b)


def kernel(x, mem, positions, w_in, ret_norm_g, mla_q_norm_g, mla_w_uq, mla_kv_norm_g, mla_w_ukv, ml_conv_w, ml_conv_b, ml_w_q, ml_w_k, ml_i_bias, ml_f_bias, ml_skip, ml_norm_g, w_mem_kv, w_out, ln_g, ln_b):
    assert x.shape == (BATCH, SEQ, D_MODEL) and mem.shape == (BATCH, MEM_LEN, D_MODEL)
    cos_r, sin_r, cos_m, sin_m = _rope_tables(positions)
    kvm = _mem_kv(mem, w_mem_kv)
    w_in_p = _prep_w_in(w_in)
    wuq_p, wukv_p = _prep_mla_weights(mla_w_uq, mla_w_ukv)
    w_out_b = w_out.astype(BF16)
    ret_consts = _ret_consts()
    ret_g = ret_norm_g.reshape(DEPTH, HEADS, 1, HEAD_DIM)
    mlstm_params = _prep_mlstm(ml_conv_w, ml_conv_b, ml_w_q, ml_w_k, ml_i_bias, ml_f_bias, ml_skip, ml_norm_g)
    qg = mla_q_norm_g.reshape(DEPTH, 1, MLA_Q_RANK)
    kvg = mla_kv_norm_g.reshape(DEPTH, 1, MLA_KV_RANK)
    lng = ln_g.reshape(DEPTH, 1, D_MODEL)
    lnb = ln_b.reshape(DEPTH, 1, D_MODEL)

    x2d = x.reshape(ROWS, D_MODEL)
    for l in range(DEPTH):
        hb = _inproj(x2d, w_in_p, l)
        ya = _retention(hb, cos_r, sin_r, ret_consts, ret_g, l)
        q, k, v = _mla_prep(hb, cos_m, sin_m, qg, wuq_p, kvg, wukv_p, l)
        yb = _mla_attn(q, k, v, hb)
        yc = _mlstm(hb, mlstm_params, l)
        yd = _mem_attn(hb, kvm, l)
        x2d = _outproj(ya, yb, yc, yd, w_out_b, x2d, lng, lnb, l)
    return x2d.reshape(BATCH, SEQ, D_MODEL)
```

```python
import numpy as np
import jax
import jax.numpy as jnp
from jax import lax
from jax.experimental import pallas as pl
from jax.experimental.pallas import tpu as pltpu

F32 = jnp.float32
BF16 = jnp.bfloat16

D_MODEL = 2048
BATCH = 8
SEQ = 2048
DEPTH = 4
MEM_LEN = 256
HEAD_DIM = 128
HEADS = 4
GROUP_WIDTH = HEADS * HEAD_DIM
MLA_NOPE = 128
MLA_ROPE = 64
MLA_Q_RANK = 384
MLA_KV_RANK = 128
CONV_WIDTH = 4
MIX_WIDTH = 4 * GROUP_WIDTH
ROPE_THETA = 10000.0
LN_EPS = 1e-5
RMS_EPS = 1e-6
DEEPNORM_ALPHA = (2 * DEPTH) ** 0.25
IN_SPLITS = (512, 512, 512, MLA_Q_RANK, MLA_KV_RANK, MLA_ROPE, 512, 512, 512, HEADS, HEADS, 512, MIX_WIDTH)

LANE = 128
ROWS = BATCH * SEQ

BLK_RQ, BLK_RK, BLK_RV = 0, 4, 8
BLK_LX, BLK_LV, BLK_LO = 12, 16, 20
BLK_MQ = 24
BLK_Z = 28
BLK_CQ, BLK_CKV = 44, 47
BLK_KPE = 48
NBLK = 50
GATE_I_LANE, GATE_F_LANE = 32, 36

TM_IN, TN_IN = 1024, 1280
TM_OUT = 512
SUB_OUT = 256
TQ_ATT = 512
ATT_HEADS = 2
RCHUNK = 512
CONV_PAD = 8
STRIP = 64
LOG2E = 1.4426950408889634
VMEM_LIMIT = 56 * 1024 * 1024


def _cparams(sem):
    return pltpu.CompilerParams(dimension_semantics=sem, vmem_limit_bytes=VMEM_LIMIT)


def _silu(z):
    return z * jax.nn.sigmoid(z)


def _log_sigmoid(x):
    return jnp.minimum(x, 0.0) - jnp.log1p(jnp.exp(-jnp.abs(x)))


def _split3(x):
    x1 = x.astype(BF16)
    r1 = x - x1.astype(F32)
    x2 = r1.astype(BF16)
    x3 = (r1 - x2.astype(F32)).astype(BF16)
    return x1, x2, x3


def _dot(a, b):
    return jnp.dot(a, b, preferred_element_type=F32)


def _rowsum_mxu(x, w):
    hi = x.astype(BF16)
    lo = (x - hi.astype(F32)).astype(BF16)
    return _dot(hi, w) + _dot(lo, w)


def _dot_nt(a, b):
    return lax.dot_general(a, b, (((1,), (1,)), ((), ())), preferred_element_type=F32)


def _tables_kernel(pos_ref, c_ref, cr_ref, sr_ref, cm_ref, sm_ref):
    pos = pos_ref[0].astype(F32)
    ang = pos * c_ref[0:1, :]
    cs = jnp.cos(ang)
    sn = jnp.sin(ang)
    cs_sw = pltpu.roll(cs, LANE // 2, 1)
    sn_sw = pltpu.roll(sn, LANE // 2, 1)
    lane = lax.broadcasted_iota(jnp.int32, cs.shape, 1)
    half_m = MLA_ROPE // 2
    first = lane < LANE // 2
    m_lo = lane < half_m
    m_hi = (lane >= LANE // 2) & (lane < LANE // 2 + half_m)
    cr_ref[0] = jnp.where(first, cs, cs_sw)
    sr_ref[0] = jnp.where(first, -sn, sn_sw)
    cm_ref[0] = jnp.where(m_lo, cs_sw, jnp.where(m_hi, cs, 0.0))
    sm_ref[0] = jnp.where(m_lo, -sn_sw, jnp.where(m_hi, sn, 0.0))


def _rope_tables(positions):
    half_r = HEAD_DIM // 2
    fr = ROPE_THETA ** (-jnp.arange(half_r, dtype=F32) / half_r)
    half_m = MLA_ROPE // 2
    fm = ROPE_THETA ** (-jnp.arange(half_m, dtype=F32) / half_m)
    freqs = jnp.concatenate([fr, fm, jnp.zeros((LANE - half_r - half_m,), F32)])
    consts = jnp.concatenate([freqs[None, :], jnp.zeros((7, LANE), F32)], axis=0)
    ts = 512
    tab = jax.ShapeDtypeStruct((BATCH, SEQ, LANE), F32)
    spec = pl.BlockSpec((1, ts, LANE), lambda b, i: (b, i, 0))
    return pl.pallas_call(
        _tables_kernel,
        out_shape=(tab, tab, tab, tab),
        grid=(BATCH, SEQ // ts),
        in_specs=[pl.BlockSpec((1, ts, 1), lambda b, i: (b, i, 0)),
                  pl.BlockSpec((8, LANE), lambda b, i: (0, 0))],
        out_specs=(spec, spec, spec, spec),
        compiler_params=_cparams(("parallel", "parallel")),
        name="rope_tables",
    )(positions.reshape(BATCH, SEQ, 1), consts)


def _matmul_kernel(a_ref, w_ref, o_ref):
    o_ref[...] = _dot(a_ref[...].astype(BF16), w_ref[...]).astype(o_ref.dtype)


def _mem_kv(mem, w_mem_kv):
    w = jnp.transpose(w_mem_kv, (1, 0, 2)).reshape(D_MODEL, DEPTH * 2 * GROUP_WIDTH).astype(BF16)
    a = mem.reshape(BATCH * MEM_LEN, D_MODEL)
    tm, tn = 512, 1024
    return pl.pallas_call(
        _matmul_kernel,
        out_shape=jax.ShapeDtypeStruct((BATCH * MEM_LEN, DEPTH * 2 * GROUP_WIDTH), BF16),
        grid=(BATCH * MEM_LEN // tm, DEPTH * 2 * GROUP_WIDTH // tn),
        in_specs=[pl.BlockSpec((tm, D_MODEL), lambda i, j: (i, 0)),
                  pl.BlockSpec((D_MODEL, tn), lambda i, j: (0, j))],
        out_specs=pl.BlockSpec((tm, tn), lambda i, j: (i, j)),
        compiler_params=_cparams(("parallel", "parallel")),
        name="mem_kv",
    )(a, w)


def _inproj_kernel(x_ref, w_ref, o_ref, xb_ref):
    @pl.when(pl.program_id(1) == 0)
    def _():
        xb_ref[...] = x_ref[...].astype(BF16)

    xb = xb_ref[...]
    for k2 in range(TN_IN // 256):
        r = _dot_nt(xb, w_ref[k2 * 256:(k2 + 1) * 256, :])
        o_ref[2 * k2] = r[:, :LANE]
        o_ref[2 * k2 + 1] = r[:, LANE:]


def _inproj(x2d, wt_p, layer):
    nb = TN_IN // LANE
    return pl.pallas_call(
        _inproj_kernel,
        out_shape=jax.ShapeDtypeStruct((NBLK, ROWS, LANE), F32),
        grid=(ROWS // TM_IN, NBLK * LANE // TN_IN),
        in_specs=[pl.BlockSpec((TM_IN, D_MODEL), lambda i, j: (i, 0)),
                  pl.BlockSpec((None, TN_IN, D_MODEL), lambda i, j: (layer, j, 0))],
        out_specs=pl.BlockSpec((nb, TM_IN, LANE), lambda i, j: (j, i, 0)),
        scratch_shapes=[pltpu.VMEM((TM_IN, D_MODEL), BF16)],
        compiler_params=_cparams(("parallel", "arbitrary")),
        name="inproj",
    )(x2d, wt_p)


def _prep_w_in(w_in):
    wt = jnp.swapaxes(w_in, 1, 2)
    idx = np.cumsum(IN_SPLITS)[:-1].tolist()
    (r_q, r_k, r_v, a_cq, a_ckv, a_kpe, l_x, l_v, l_o, l_i, l_f, c_q, z) = jnp.split(wt, idx, axis=1)
    half = MLA_ROPE // 2
    zeros = lambda n: jnp.zeros((wt.shape[0], n, wt.shape[2]), wt.dtype)
    blk = [a_kpe[:, :half], l_i, l_f, zeros(64 - half - 2 * HEADS), a_kpe[:, half:], zeros(64 - half)]
    return jnp.concatenate([r_q, r_k, r_v, l_x, l_v, l_o, c_q, z, a_cq, a_ckv] + blk + [zeros(LANE)],
                           axis=1).astype(BF16)


def _group_spec(base, rows):
    nchunk = SEQ // rows
    return pl.BlockSpec((HEADS, rows, LANE), lambda b, c: (base // HEADS, b * nchunk + c, 0))


def _layer_spec(shape, layer):
    nd = len(shape) - 1
    return pl.BlockSpec((None,) + tuple(shape[1:]), lambda *_: (layer,) + (0,) * nd)


def _const_spec(shape):
    nd = len(shape)
    return pl.BlockSpec(tuple(shape), lambda *_: (0,) * nd)


def _ret_kernel(q_ref, k_ref, v_ref, z_ref, cos_ref, sin_ref, gq_ref, gk_ref, gc_ref, g_ref, o_ref, st_ref):
    @pl.when(pl.program_id(1) == 0)
    def _():
        st_ref[...] = jnp.zeros_like(st_ref)

    cs = cos_ref[0]
    sn = sin_ref[0]
    row = lax.broadcasted_iota(jnp.int32, (RCHUNK, RCHUNK), 0)
    col = lax.broadcasted_iota(jnp.int32, (RCHUNK, RCHUNK), 1)
    causal = col <= row
    for h in range(HEADS):
        q = q_ref[h]
        q = (q * cs + pltpu.roll(q, HEAD_DIM // 2, 1) * sn) * gq_ref[h]
        k = k_ref[h]
        k = (k * cs + pltpu.roll(k, HEAD_DIM // 2, 1) * sn) * gk_ref[h]
        qb = q.astype(BF16)
        kb = k.astype(BF16)
        vb = v_ref[h].astype(BF16)
        sc = jnp.where(causal, _dot_nt(qb, kb), 0.0)
        st = st_ref[h]
        out = _dot(sc.astype(BF16), vb) + _dot(qb, st.astype(BF16))
        st_ref[h] = gc_ref[h] * (st + _dot(k.T.astype(BF16), vb))
        mu = jnp.mean(out, axis=-1, keepdims=True)
        dev = out - mu
        var = jnp.mean(dev * dev, axis=-1, keepdims=True)
        hn = dev * lax.rsqrt(var + LN_EPS) * g_ref[h]
        o_ref[h] = (hn * _silu(z_ref[h])).astype(BF16)


def _ret_consts():
    log_g = jnp.log1p(-jnp.exp2(-5.0 - jnp.arange(HEADS, dtype=F32)))
    idx = jnp.arange(RCHUNK, dtype=F32)
    full = (HEADS, RCHUNK, HEAD_DIM)
    gq = jnp.broadcast_to(jnp.exp(log_g[:, None] * (idx + 1.0))[..., None], full)
    gk = jnp.broadcast_to((jnp.exp(-log_g[:, None] * (idx + 1.0)) * HEAD_DIM ** -0.5)[..., None], full)
    gc = jnp.broadcast_to(jnp.exp(log_g * RCHUNK)[:, None, None], (HEADS, 1, HEAD_DIM))
    return gq, gk, gc


def _mlstm_kernel(lx_ref, lv_ref, lo_ref, gt_ref, z_ref, cw_ref, cb_ref, wq_ref, wk_ref, gb_ref,
                  skip_ref, g_ref, o_ref, xp_ref, st_ref, m_ref, s_ref, p_ref):
    @pl.when(pl.program_id(1) == 0)
    def _():
        xp_ref[:, 0:CONV_PAD, :] = jnp.zeros((HEADS, CONV_PAD, HEAD_DIM), F32)
        st_ref[...] = jnp.zeros_like(st_ref)
        m_ref[...] = jnp.zeros_like(m_ref)

    n = RCHUNK
    ones_nd = jnp.ones((n, HEAD_DIM), BF16)
    mean_dd = jnp.full((HEAD_DIM, HEAD_DIM), 1.0 / HEAD_DIM, BF16)
    row = lax.broadcasted_iota(jnp.int32, (n, n), 0)
    col = lax.broadcasted_iota(jnp.int32, (n, n), 1)
    tril = jnp.where(col <= row, 1.0, 0.0).astype(BF16)
    r128 = lax.broadcasted_iota(jnp.int32, (LANE, LANE), 0)
    c128 = lax.broadcasted_iota(jnp.int32, (LANE, LANE), 1)
    ident = jnp.where(r128 == c128, 1.0, 0.0).astype(BF16)
    srow = lax.broadcasted_iota(jnp.int32, (STRIP, LANE), 0)
    scol = lax.broadcasted_iota(jnp.int32, (STRIP, LANE), 1)

    lane = lax.broadcasted_iota(jnp.int32, (n, LANE), 1)
    x = gt_ref[0] + gb_ref[...]
    x = jnp.where((lane >= GATE_F_LANE) & (lane < GATE_F_LANE + HEADS), _log_sigmoid(x), x) * LOG2E
    x1, x2, x3 = _split3(x)
    cum = _dot(tril, x1) + _dot(tril, x2) + _dot(tril, x3)
    rt = x - pltpu.roll(cum, LANE - (GATE_F_LANE - GATE_I_LANE), 1)
    r1, r2, r3 = _split3(rt)
    rtt = _dot_nt(ident, r1) + _dot_nt(ident, r2) + _dot_nt(ident, r3)

    for h in range(HEADS):
        xp_ref[h, CONV_PAD:, :] = lx_ref[h]
        acc = jnp.zeros((n, HEAD_DIM), F32) + cb_ref[h]
        for j in range(CONV_WIDTH):
            off = CONV_PAD - (CONV_WIDTH - 1) + j
            acc = acc + xp_ref[h, off:off + n, :] * cw_ref[h, j:j + 1, :]
        xp_ref[h, 0:CONV_PAD, :] = lx_ref[h, n - CONV_PAD:n, :]
        xc = _silu(acc)
        xcb = xc.astype(BF16)
        q = _dot(xcb, wq_ref[h])
        k = _dot(xcb, wk_ref[h]) * HEAD_DIM ** -0.5
        qb = q.astype(BF16)
        kb = k.astype(BF16)
        vb = jnp.concatenate([lv_ref[h].astype(BF16), ones_nd], axis=-1)
        s_ref[h] = _dot_nt(qb, kb)

        li = GATE_I_LANE + h
        lf = GATE_F_LANE + h
        r_row = rtt[li:li + 1, :]
        m_st = m_ref[h]

        base = m_st
        us = []
        for i in range(n // STRIP):
            rows = slice(i * STRIP, (i + 1) * STRIP)
            d0 = (i * STRIP // LANE) * LANE
            w = d0 + LANE
            if d0 > 0 and (i * STRIP) % LANE == 0:
                base = jnp.maximum(base, jnp.max(r_row[:, d0 - LANE:d0], axis=-1, keepdims=True))
            mask = (scol + d0) <= (srow + i * STRIP)
            rmd = jnp.where(mask, r_row[:, d0:w], -jnp.inf)
            u_col = jnp.maximum(jnp.max(rmd, axis=-1, keepdims=True), base)
            u = jnp.broadcast_to(u_col, (STRIP, LANE))
            p_ref[h, rows, d0:w] = (s_ref[h, rows, d0:w] * jnp.exp2(rmd - u)).astype(BF16)
            for c in range(d0 // LANE):
                cols = slice(c * LANE, (c + 1) * LANE)
                p_ref[h, rows, cols] = (s_ref[h, rows, cols] * jnp.exp2(r_row[:, cols] - u)).astype(BF16)
            if w < n:
                p_ref[h, rows, w:n] = jnp.zeros((STRIP, n - w), BF16)
            us.append(u)
            u_last = u_col[STRIP - 1:STRIP, :]
        u = jnp.concatenate(us, axis=0)

        w_inter = jnp.exp2(m_st - u)
        st = st_ref[h]
        intra = _dot(p_ref[h], vb)
        inter = _dot(qb, st.astype(BF16))
        num = intra[:, :LANE] + w_inter * inter[:, :LANE]
        den = intra[:, LANE:] + w_inter * inter[:, LANE:]
        cum_f = jnp.broadcast_to(cum[:, lf:lf + 1], (n, LANE))
        hcell = num / jnp.maximum(jnp.abs(den), jnp.exp2(-(cum_f + u)))

        decay = jnp.exp2(m_st - u_last)
        kw = k * jnp.exp2(jnp.broadcast_to(rt[:, li:li + 1], (n, LANE)) - u_last)
        upd = _dot(kw.T.astype(BF16), vb)
        st_ref[h] = decay * st + upd
        m_ref[h] = cum[n - 1:n, lf:lf + 1] + u_last

        cell = hcell * jax.nn.sigmoid(lo_ref[h])
        mu = _rowsum_mxu(cell, mean_dd)
        dev = cell - mu
        var = _rowsum_mxu(dev * dev, mean_dd)
        hn = dev * lax.rsqrt(var + LN_EPS) * g_ref[h]
        o_ref[h] = ((hn + skip_ref[h] * xc) * _silu(z_ref[h])).astype(BF16)


def _prep_mlstm(conv_w, conv_b, w_q, w_k, i_bias, f_bias, skip, norm_g):
    nl = conv_w.shape[0]
    cw = conv_w.reshape(nl, CONV_WIDTH, HEADS, HEAD_DIM).transpose(0, 2, 1, 3)
    cb = conv_b.reshape(nl, HEADS, 1, HEAD_DIM)
    zeros = lambda w: jnp.zeros((nl, w), F32)
    gb = jnp.concatenate([zeros(GATE_I_LANE), i_bias, f_bias, zeros(LANE - GATE_F_LANE - HEADS)], axis=-1)
    return (cw, cb, w_q.astype(BF16), w_k.astype(BF16), gb.reshape(nl, 1, LANE),
            skip.reshape(nl, HEADS, 1, HEAD_DIM), norm_g.reshape(nl, HEADS, 1, HEAD_DIM))


def _mlstm(hb, params, layer):
    nchunk = SEQ // RCHUNK
    return pl.pallas_call(
        _mlstm_kernel,
        out_shape=jax.ShapeDtypeStruct((HEADS, ROWS, LANE), BF16),
        grid=(BATCH, nchunk),
        in_specs=[_group_spec(BLK_LX, RCHUNK), _group_spec(BLK_LV, RCHUNK), _group_spec(BLK_LO, RCHUNK),
                  pl.BlockSpec((1, RCHUNK, LANE), lambda b, c: (BLK_KPE, b * nchunk + c, 0)),
                  _group_spec(BLK_Z + 2 * HEADS, RCHUNK)] + [_layer_spec(p.shape, layer) for p in params],
        out_specs=pl.BlockSpec((HEADS, RCHUNK, LANE), lambda b, c: (0, b * nchunk + c, 0)),
        scratch_shapes=[pltpu.VMEM((HEADS, RCHUNK + CONV_PAD, HEAD_DIM), F32),
                        pltpu.VMEM((HEADS, HEAD_DIM, 2 * HEAD_DIM), F32),
                        pltpu.VMEM((HEADS, 1, 1), F32),
                        pltpu.VMEM((HEADS, RCHUNK, RCHUNK), F32),
                        pltpu.VMEM((HEADS, RCHUNK, RCHUNK), BF16)],
        compiler_params=_cparams(("parallel", "arbitrary")),
        name="mlstm",
    )(hb, hb, hb, hb, hb, *params)


def _mla_prep_kernel(lat_ref, kpe_ref, cm_ref, sm_ref, qg_ref, wuq_ref, kvg_ref, wukv_ref,
                     q_out, k_out, v_out):
    scale = (MLA_NOPE + MLA_ROPE) ** -0.5 * LOG2E
    cs = cm_ref[0]
    sn = sm_ref[0]
    cq = jnp.concatenate([lat_ref[0], lat_ref[1], lat_ref[2]], axis=-1)
    qn = cq * lax.rsqrt(jnp.mean(jnp.square(cq), axis=-1, keepdims=True) + RMS_EPS) * qg_ref[...]
    q = _dot(qn.astype(BF16), wuq_ref[...])
    ckv = lat_ref[3]
    kvn = ckv * lax.rsqrt(jnp.mean(jnp.square(ckv), axis=-1, keepdims=True) + RMS_EPS) * kvg_ref[...]
    kv = _dot(kvn.astype(BF16), wukv_ref[...])
    kpe = kpe_ref[0]
    krot = (kpe * cs + pltpu.roll(kpe, LANE // 2, 1) * sn).astype(BF16)
    for h in range(HEADS):
        qr = q[:, GROUP_WIDTH + h * LANE:GROUP_WIDTH + (h + 1) * LANE]
        qr = qr * cs + pltpu.roll(qr, LANE // 2, 1) * sn
        q_out[0, h, :, 0:LANE] = (q[:, h * LANE:(h + 1) * LANE] * scale).astype(BF16)
        q_out[0, h, :, LANE:2 * LANE] = (qr * scale).astype(BF16)
        k_out[0, h, :, 0:LANE] = kv[:, h * LANE:(h + 1) * LANE].astype(BF16)
        k_out[0, h, :, LANE:2 * LANE] = krot
        v_out[0, h] = kv[:, GROUP_WIDTH + h * LANE:GROUP_WIDTH + (h + 1) * LANE].astype(BF16)


def _prep_mla_weights(w_uq, w_ukv):
    lead = w_uq.shape[:-1]
    wq = w_uq.reshape(lead + (HEADS, MLA_NOPE + MLA_ROPE))
    nope = wq[..., :MLA_NOPE].reshape(lead + (GROUP_WIDTH,))
    half = MLA_ROPE // 2
    zeros = jnp.zeros(lead + (HEADS, 64 - half), w_uq.dtype)
    rope = jnp.concatenate([wq[..., MLA_NOPE:MLA_NOPE + half], zeros, wq[..., MLA_NOPE + half:], zeros], axis=-1)
    wq_p = jnp.concatenate([nope, rope.reshape(lead + (HEADS * LANE,))], axis=-1).astype(BF16)
    lead = w_ukv.shape[:-1]
    wkv = w_ukv.reshape(lead + (HEADS, MLA_NOPE + HEAD_DIM))
    wkv_p = jnp.concatenate([wkv[..., :MLA_NOPE].reshape(lead + (GROUP_WIDTH,)),
                             wkv[..., MLA_NOPE:].reshape(lead + (GROUP_WIDTH,))], axis=-1).astype(BF16)
    return wq_p, wkv_p


def _attn_tile(nfull, q_ref, k_ref, v_ref, z_ref, o_ref, s_ref, p_ref, m_ref):
    kvlen = (nfull + 1) * TQ_ATT
    d0 = nfull * TQ_ATT
    hq = TQ_ATT // 2
    srow = lax.broadcasted_iota(jnp.int32, (STRIP, LANE), 0)
    scol = lax.broadcasted_iota(jnp.int32, (STRIP, LANE), 1)

    def strip_blocks(g, i):
        rows = slice(i * STRIP, (i + 1) * STRIP)
        wd = -(-(i + 1) * STRIP // LANE) * LANE
        ncol = (d0 + wd) // LANE
        blks = [s_ref[g, rows, c * LANE:(c + 1) * LANE] for c in range(ncol)]
        blks[-1] = jnp.where(scol + (wd - LANE) <= srow + i * STRIP, blks[-1], -jnp.inf)
        return rows, ncol, blks

    qrows = slice(d0, d0 + TQ_ATT)
    for g in range(ATT_HEADS):
        q = q_ref[0, g, qrows, :]
        for j in range(nfull):
            s_ref[g, :, j * TQ_ATT:(j + 1) * TQ_ATT] = _dot_nt(q, k_ref[0, g, j * TQ_ATT:(j + 1) * TQ_ATT, :])
        s_ref[g, 0:hq, d0:d0 + hq] = _dot_nt(q[0:hq], k_ref[0, g, d0:d0 + hq, :])
        s_ref[g, hq:TQ_ATT, d0:kvlen] = _dot_nt(q[hq:TQ_ATT], k_ref[0, g, d0:kvlen, :])
    for g in range(ATT_HEADS):
        for i in range(TQ_ATT // STRIP):
            rows, ncol, blks = strip_blocks(g, i)
            mx = blks[0]
            for blk in blks[1:]:
                mx = jnp.maximum(mx, blk)
            m_ref[g, rows, :] = jnp.broadcast_to(jnp.max(mx, axis=-1, keepdims=True), (STRIP, LANE))
        for i in range(TQ_ATT // STRIP):
            rows, ncol, blks = strip_blocks(g, i)
            m = m_ref[g, rows, :]
            for c, blk in enumerate(blks):
                p_ref[g, rows, c * LANE:(c + 1) * LANE] = jnp.exp2(blk - m).astype(BF16)
            kl = d0 + hq if (i + 1) * STRIP <= hq else kvlen
            if ncol * LANE < kl:
                p_ref[g, rows, ncol * LANE:kl] = jnp.zeros((STRIP, kl - ncol * LANE), BF16)
        v1 = jnp.concatenate([v_ref[0, g, 0:kvlen, :], jnp.ones((kvlen, LANE), BF16)], axis=-1)
        for r0, r1, kl in ((0, hq, d0 + hq), (hq, TQ_ATT, kvlen)):
            pv = _dot(p_ref[g, r0:r1, 0:kl], v1[0:kl])
            gate = _silu(z_ref[g, d0 + r0:d0 + r1, :])
            o_ref[g, d0 + r0:d0 + r1, :] = (pv[:, :LANE] / pv[:, LANE:] * gate).astype(BF16)


def _mla_attn_kernel(q_ref, k_ref, v_ref, z_ref, o_ref, s_ref, p_ref, m_ref):
    for nfull in range(SEQ // TQ_ATT):
        _attn_tile(nfull, q_ref, k_ref, v_ref, z_ref, o_ref, s_ref, p_ref, m_ref)


def _mla_attn(q, k, v, hb):
    return pl.pallas_call(
        _mla_attn_kernel,
        out_shape=jax.ShapeDtypeStruct((HEADS, ROWS, LANE), BF16),
        grid=(BATCH, HEADS // ATT_HEADS),
        in_specs=[pl.BlockSpec((1, ATT_HEADS, SEQ, 2 * LANE), lambda b, h: (b, h, 0, 0)),
                  pl.BlockSpec((1, ATT_HEADS, SEQ, 2 * LANE), lambda b, h: (b, h, 0, 0)),
                  pl.BlockSpec((1, ATT_HEADS, SEQ, LANE), lambda b, h: (b, h, 0, 0)),
                  pl.BlockSpec((ATT_HEADS, SEQ, LANE), lambda b, h: ((BLK_Z + HEADS) // ATT_HEADS + h, b, 0))],
        out_specs=pl.BlockSpec((ATT_HEADS, SEQ, LANE), lambda b, h: (h, b, 0)),
        scratch_shapes=[pltpu.VMEM((ATT_HEADS, TQ_ATT, SEQ), F32), pltpu.VMEM((ATT_HEADS, TQ_ATT, SEQ), BF16),
                        pltpu.VMEM((ATT_HEADS, TQ_ATT, LANE), F32)],
        compiler_params=_cparams(("parallel", "parallel")),
        name="mla_attn",
    )(q, k, v, hb)


def _mem_attn_kernel(q_ref, k_ref, v_ref, z_ref, o_ref):
    ones = jnp.ones((MEM_LEN, LANE), BF16)
    for h in range(HEADS):
        cols = slice(h * LANE, (h + 1) * LANE)
        q = (q_ref[h] * (HEAD_DIM ** -0.5 * LOG2E)).astype(BF16)
        s = _dot_nt(q, k_ref[:, cols])
        p = jnp.exp2(s - jnp.max(s, axis=-1, keepdims=True)).astype(BF16)
        pv = _dot(p, jnp.concatenate([v_ref[:, cols], ones], axis=-1))
        o_ref[h] = (pv[:, :LANE] / pv[:, LANE:] * _silu(z_ref[h])).astype(BF16)


def _light_kernel(rq, rk, rv, rz, cr, sr, gq, gk, gc, rg, mq, mk, mv, mz, lat, kpe, cm, sm, qg, wuq, kvg, wukv,
                  ya, yd, q_out, k_out, v_out, st_ref):
    _ret_kernel(rq, rk, rv, rz, cr, sr, gq, gk, gc, rg, ya, st_ref)
    _mem_attn_kernel(mq, mk, mv, mz, yd)
    _mla_prep_kernel(lat, kpe, cm, sm, qg, wuq, kvg, wukv, q_out, k_out, v_out)


def _light_mixers(hb, tables, ret_consts, ret_g, kvm, qg, wuq_p, kvg, wukv_p, layer):
    cos_r, sin_r, cos_m, sin_m = tables
    gq, gk, gc = ret_consts
    nchunk = SEQ // RCHUNK
    tab = pl.BlockSpec((1, RCHUNK, LANE), lambda b, c: (b, c, 0))
    row_blk = lambda blk: pl.BlockSpec((1, RCHUNK, LANE), lambda b, c: (blk, b * nchunk + c, 0))
    y_shape = jax.ShapeDtypeStruct((HEADS, ROWS, LANE), BF16)
    y_spec = pl.BlockSpec((HEADS, RCHUNK, LANE), lambda b, c: (0, b * nchunk + c, 0))
    qk_shape = jax.ShapeDtypeStruct((BATCH, HEADS, SEQ, 2 * LANE), BF16)
    v_shape = jax.ShapeDtypeStruct((BATCH, HEADS, SEQ, LANE), BF16)
    qk_spec = pl.BlockSpec((1, HEADS, RCHUNK, 2 * LANE), lambda b, c: (b, 0, c, 0))
    v_spec = pl.BlockSpec((1, HEADS, RCHUNK, LANE), lambda b, c: (b, 0, c, 0))
    return pl.pallas_call(
        _light_kernel,
        out_shape=(y_shape, y_shape, qk_shape, qk_shape, v_shape),
        grid=(BATCH, nchunk),
        in_specs=[_group_spec(BLK_RQ, RCHUNK), _group_spec(BLK_RK, RCHUNK), _group_spec(BLK_RV, RCHUNK),
                  _group_spec(BLK_Z, RCHUNK), tab, tab,
                  _const_spec(gq.shape), _const_spec(gk.shape), _const_spec(gc.shape),
                  _layer_spec(ret_g.shape, layer),
                  _group_spec(BLK_MQ, RCHUNK),
                  pl.BlockSpec((MEM_LEN, GROUP_WIDTH), lambda b, c: (b, 2 * layer)),
                  pl.BlockSpec((MEM_LEN, GROUP_WIDTH), lambda b, c: (b, 2 * layer + 1)),
                  _group_spec(BLK_Z + 3 * HEADS, RCHUNK),
                  _group_spec(BLK_CQ, RCHUNK), row_blk(BLK_KPE), tab, tab,
                  _layer_spec(qg.shape, layer), _layer_spec(wuq_p.shape, layer),
                  _layer_spec(kvg.shape, layer), _layer_spec(wukv_p.shape, layer)],
        out_specs=(y_spec, y_spec, qk_spec, qk_spec, v_spec),
        scratch_shapes=[pltpu.VMEM((HEADS, HEAD_DIM, HEAD_DIM), F32)],
        compiler_params=_cparams(("parallel", "arbitrary")),
        name="light_mixers",
    )(hb, hb, hb, hb, cos_r, sin_r, gq, gk, gc, ret_g, hb, kvm, kvm, hb, hb, hb, cos_m, sin_m,
      qg, wuq_p, kvg, wukv_p)


def _out_kernel(ya_ref, yb_ref, yc_ref, yd_ref, w_ref, x_ref, g_ref, b_ref, o_ref):
    for t in range(TM_OUT // SUB_OUT):
        rows = slice(t * SUB_OUT, (t + 1) * SUB_OUT)
        parts = [ref[h, rows, :] for ref in (ya_ref, yb_ref, yc_ref, yd_ref) for h in range(HEADS)]
        y = jnp.concatenate(parts, axis=-1)
        r = DEEPNORM_ALPHA * x_ref[rows, :] + _dot(y, w_ref[...])
        mu = jnp.mean(r, axis=-1, keepdims=True)
        var = jnp.mean(jnp.square(r - mu), axis=-1, keepdims=True)
        o_ref[rows, :] = (r - mu) * lax.rsqrt(var + LN_EPS) * g_ref[...] + b_ref[...]


def _outproj(ya, yb, yc, yd, w_out, x2d, ln_g, ln_b, layer):
    yspec = pl.BlockSpec((HEADS, TM_OUT, LANE), lambda i: (0, i, 0))
    return pl.pallas_call(
        _out_kernel,
        out_shape=jax.ShapeDtypeStruct((ROWS, D_MODEL), F32),
        grid=(ROWS // TM_OUT,),
        in_specs=[yspec, yspec, yspec, yspec,
                  pl.BlockSpec((None,) + tuple(w_out.shape[1:]), lambda i: (layer, 0, 0),
                               pipeline_mode=pl.Buffered(1)),
                  pl.BlockSpec((TM_OUT, D_MODEL), lambda i: (i, 0)),
                  _layer_spec(ln_g.shape, layer), _layer_spec(ln_b.shape, layer)],
        out_specs=pl.BlockSpec((TM_OUT, D_MODEL), lambda i: (i, 0)),
        compiler_params=_cparams(("parallel",)),
        name="outproj_ln",
    )(ya, yb, yc, yd, w_out, x2d, ln_g, ln_b)


def kernel(x, mem, positions, w_in, ret_norm_g, mla_q_norm_g, mla_w_uq, mla_kv_norm_g, mla_w_ukv, ml_conv_w, ml_conv_b, ml_w_q, ml_w_k, ml_i_bias, ml_f_bias, ml_skip, ml_norm_g, w_mem_kv, w_out, ln_g, ln_b):
    assert x.shape == (BATCH, SEQ, D_MODEL) and mem.shape == (BATCH, MEM_LEN, D_MODEL)
    tables = _rope_tables(positions)
    kvm = _mem_kv(mem, w_mem_kv)
    w_in_p = _prep_w_in(w_in)
    wuq_p, wukv_p = _prep_mla_weights(mla_w_uq, mla_w_ukv)
    w_out_b = w_out.astype(BF16)
    ret_consts = _ret_consts()
    ret_g = ret_norm_g.reshape(DEPTH, HEADS, 1, HEAD_DIM)
    mlstm_params = _prep_mlstm(ml_conv_w, ml_conv_b, ml_w_q, ml_w_k, ml_i_bias, ml_f_bias, ml_skip, ml_norm_g)
    qg = mla_q_norm_g.reshape(DEPTH, 1, MLA_Q_RANK)
    kvg = mla_kv_norm_g.reshape(DEPTH, 1, MLA_KV_RANK)
    lng = ln_g.reshape(DEPTH, 1, D_MODEL)
    lnb = ln_b.reshape(DEPTH, 1, D_MODEL)

    x2d = x.reshape(ROWS, D_MODEL)
    for l in range(DEPTH):
        hb = _inproj(x2d, w_in_p, l)
        ya, yd, q, k, v = _light_mixers(hb, tables, ret_consts, ret_g, kvm, qg, wuq_p, kvg, wukv_p, l)
        yb = _mla_attn(q, k, v, hb)
        yc = _mlstm(hb, mlstm_params, l)
        x2d = _outproj(ya, yb, yc, yd, w_out_b, x2d, lng, lnb, l)
    return x2d.reshape(BATCH, SEQ, D_MODEL)
```

```python
import numpy as np
import jax
import jax.numpy as jnp
from jax import lax
from jax.experimental import pallas as pl
from jax.experimental.pallas import tpu as pltpu

F32 = jnp.float32
BF16 = jnp.bfloat16

D_MODEL = 2048
BATCH = 8
SEQ = 2048
DEPTH = 4
MEM_LEN = 256
HEAD_DIM = 128
HEADS = 4
GROUP_WIDTH = HEADS * HEAD_DIM
MLA_NOPE = 128
MLA_ROPE = 64
MLA_Q_RANK = 384
MLA_KV_RANK = 128
CONV_WIDTH = 4
MIX_WIDTH = 4 * GROUP_WIDTH
ROPE_THETA = 10000.0
LN_EPS = 1e-5
RMS_EPS = 1e-6
DEEPNORM_ALPHA = (2 * DEPTH) ** 0.25
IN_SPLITS = (512, 512, 512, MLA_Q_RANK, MLA_KV_RANK, MLA_ROPE, 512, 512, 512, HEADS, HEADS, 512, MIX_WIDTH)

LANE = 128
ROWS = BATCH * SEQ

BLK_RQ, BLK_RK, BLK_RV = 0, 4, 8
BLK_LX, BLK_LV, BLK_LO = 12, 16, 20
BLK_MQ = 24
BLK_Z = 28
BLK_CQ, BLK_CKV = 44, 47
BLK_KPE = 48
NBLK = 50
GATE_I_LANE, GATE_F_LANE = 32, 36

TM_IN, TN_IN = 1024, 1280
TM_OUT = 512
SUB_OUT = 256
TQ_ATT = 512
ATT_HEADS = 2
RCHUNK = 512
CONV_PAD = 8
STRIP = 64
LOG2E = 1.4426950408889634
VMEM_LIMIT = 56 * 1024 * 1024


def _cparams(sem):
    return pltpu.CompilerParams(dimension_semantics=sem, vmem_limit_bytes=VMEM_LIMIT)


def _silu(z):
    return z * jax.nn.sigmoid(z)


def _log_sigmoid(x):
    return jnp.minimum(x, 0.0) - jnp.log1p(jnp.exp(-jnp.abs(x)))


def _split3(x):
    x1 = x.astype(BF16)
    r1 = x - x1.astype(F32)
    x2 = r1.astype(BF16)
    x3 = (r1 - x2.astype(F32)).astype(BF16)
    return x1, x2, x3


def _dot(a, b):
    return jnp.dot(a, b, preferred_element_type=F32)


def _rowsum_mxu(x, w):
    hi = x.astype(BF16)
    lo = (x - hi.astype(F32)).astype(BF16)
    return _dot(hi, w) + _dot(lo, w)


def _dot_nt(a, b):
    return lax.dot_general(a, b, (((1,), (1,)), ((), ())), preferred_element_type=F32)


def _tables_kernel(pos_ref, c_ref, cr_ref, sr_ref, cm_ref, sm_ref):
    pos = pos_ref[0].astype(F32)
    ang = pos * c_ref[0:1, :]
    cs = jnp.cos(ang)
    sn = jnp.sin(ang)
    cs_sw = pltpu.roll(cs, LANE // 2, 1)
    sn_sw = pltpu.roll(sn, LANE // 2, 1)
    lane = lax.broadcasted_iota(jnp.int32, cs.shape, 1)
    half_m = MLA_ROPE // 2
    first = lane < LANE // 2
    m_lo = lane < half_m
    m_hi = (lane >= LANE // 2) & (lane < LANE // 2 + half_m)
    cr_ref[0] = jnp.where(first, cs, cs_sw)
    sr_ref[0] = jnp.where(first, -sn, sn_sw)
    cm_ref[0] = jnp.where(m_lo, cs_sw, jnp.where(m_hi, cs, 0.0))
    sm_ref[0] = jnp.where(m_lo, -sn_sw, jnp.where(m_hi, sn, 0.0))


def _rope_tables(positions):
    half_r = HEAD_DIM // 2
    fr = ROPE_THETA ** (-jnp.arange(half_r, dtype=F32) / half_r)
    half_m = MLA_ROPE // 2
    fm = ROPE_THETA ** (-jnp.arange(half_m, dtype=F32) / half_m)
    freqs = jnp.concatenate([fr, fm, jnp.zeros((LANE - half_r - half_m,), F32)])
    consts = jnp.concatenate([freqs[None, :], jnp.zeros((7, LANE), F32)], axis=0)
    ts = 512
    tab = jax.ShapeDtypeStruct((BATCH, SEQ, LANE), F32)
    spec = pl.BlockSpec((1, ts, LANE), lambda b, i: (b, i, 0))
    return pl.pallas_call(
        _tables_kernel,
        out_shape=(tab, tab, tab, tab),
        grid=(BATCH, SEQ // ts),
        in_specs=[pl.BlockSpec((1, ts, 1), lambda b, i: (b, i, 0)),
                  pl.BlockSpec((8, LANE), lambda b, i: (0, 0))],
        out_specs=(spec, spec, spec, spec),
        compiler_params=_cparams(("parallel", "parallel")),
        name="rope_tables",
    )(positions.reshape(BATCH, SEQ, 1), consts)


def _matmul_kernel(a_ref, w_ref, o_ref):
    o_ref[...] = _dot(a_ref[...].astype(BF16), w_ref[...]).astype(o_ref.dtype)


def _mem_kv(mem, w_mem_kv):
    w = jnp.transpose(w_mem_kv, (1, 0, 2)).reshape(D_MODEL, DEPTH * 2 * GROUP_WIDTH).astype(BF16)
    a = mem.reshape(BATCH * MEM_LEN, D_MODEL)
    tm, tn = 512, 1024
    return pl.pallas_call(
        _matmul_kernel,
        out_shape=jax.ShapeDtypeStruct((BATCH * MEM_LEN, DEPTH * 2 * GROUP_WIDTH), BF16),
        grid=(BATCH * MEM_LEN // tm, DEPTH * 2 * GROUP_WIDTH // tn),
        in_specs=[pl.BlockSpec((tm, D_MODEL), lambda i, j: (i, 0)),
                  pl.BlockSpec((D_MODEL, tn), lambda i, j: (0, j))],
        out_specs=pl.BlockSpec((tm, tn), lambda i, j: (i, j)),
        compiler_params=_cparams(("parallel", "parallel")),
        name="mem_kv",
    )(a, w)


def _inproj_kernel(x_ref, w_ref, o_ref, xb_ref):
    @pl.when(pl.program_id(1) == 0)
    def _():
        xb_ref[...] = x_ref[...].astype(BF16)

    xb = xb_ref[...]
    for k2 in range(TN_IN // 256):
        r = _dot_nt(xb, w_ref[k2 * 256:(k2 + 1) * 256, :])
        o_ref[2 * k2] = r[:, :LANE]
        o_ref[2 * k2 + 1] = r[:, LANE:]


def _inproj(x2d, wt_p, layer):
    nb = TN_IN // LANE
    return pl.pallas_call(
        _inproj_kernel,
        out_shape=jax.ShapeDtypeStruct((NBLK, ROWS, LANE), F32),
        grid=(ROWS // TM_IN, NBLK * LANE // TN_IN),
        in_specs=[pl.BlockSpec((TM_IN, D_MODEL), lambda i, j: (i, 0)),
                  pl.BlockSpec((None, TN_IN, D_MODEL), lambda i, j: (layer, j, 0))],
        out_specs=pl.BlockSpec((nb, TM_IN, LANE), lambda i, j: (j, i, 0)),
        scratch_shapes=[pltpu.VMEM((TM_IN, D_MODEL), BF16)],
        compiler_params=_cparams(("parallel", "arbitrary")),
        name="inproj",
    )(x2d, wt_p)


def _prep_w_in(w_in):
    wt = jnp.swapaxes(w_in, 1, 2)
    idx = np.cumsum(IN_SPLITS)[:-1].tolist()
    (r_q, r_k, r_v, a_cq, a_ckv, a_kpe, l_x, l_v, l_o, l_i, l_f, c_q, z) = jnp.split(wt, idx, axis=1)
    half = MLA_ROPE // 2
    zeros = lambda n: jnp.zeros((wt.shape[0], n, wt.shape[2]), wt.dtype)
    blk = [a_kpe[:, :half], l_i, l_f, zeros(64 - half - 2 * HEADS), a_kpe[:, half:], zeros(64 - half)]
    return jnp.concatenate([r_q, r_k, r_v, l_x, l_v, l_o, c_q, z, a_cq, a_ckv] + blk + [zeros(LANE)],
                           axis=1).astype(BF16)


def _group_spec(base, rows):
    nchunk = SEQ // rows
    return pl.BlockSpec((HEADS, rows, LANE), lambda b, c: (base // HEADS, b * nchunk + c, 0))


def _layer_spec(shape, layer):
    nd = len(shape) - 1
    return pl.BlockSpec((None,) + tuple(shape[1:]), lambda *_: (layer,) + (0,) * nd)


def _const_spec(shape):
    nd = len(shape)
    return pl.BlockSpec(tuple(shape), lambda *_: (0,) * nd)


def _ret_kernel(q_ref, k_ref, v_ref, z_ref, cos_ref, sin_ref, gq_ref, gk_ref, gc_ref, g_ref, o_ref, st_ref):
    @pl.when(pl.program_id(1) == 0)
    def _():
        st_ref[...] = jnp.zeros_like(st_ref)

    cs = cos_ref[0]
    sn = sin_ref[0]
    row = lax.broadcasted_iota(jnp.int32, (RCHUNK, RCHUNK), 0)
    col = lax.broadcasted_iota(jnp.int32, (RCHUNK, RCHUNK), 1)
    causal = col <= row
    qbs, kts, vbs, scs = [], [], [], []
    for h in range(HEADS):
        q = q_ref[h]
        q = (q * cs + pltpu.roll(q, HEAD_DIM // 2, 1) * sn) * gq_ref[h]
        k = k_ref[h]
        k = (k * cs + pltpu.roll(k, HEAD_DIM // 2, 1) * sn) * gk_ref[h]
        qb = q.astype(BF16)
        scs.append(_dot_nt(qb, k.astype(BF16)))
        qbs.append(qb)
        kts.append(k.T.astype(BF16))
        vbs.append(v_ref[h].astype(BF16))
    outs = []
    for h in range(HEADS):
        st = st_ref[h]
        sc = jnp.where(causal, scs[h], 0.0).astype(BF16)
        outs.append(_dot(sc, vbs[h]) + _dot(qbs[h], st.astype(BF16)))
        st_ref[h] = gc_ref[h] * (st + _dot(kts[h], vbs[h]))
    for h in range(HEADS):
        out = outs[h]
        mu = jnp.mean(out, axis=-1, keepdims=True)
        dev = out - mu
        var = jnp.mean(dev * dev, axis=-1, keepdims=True)
        hn = dev * lax.rsqrt(var + LN_EPS) * g_ref[h]
        o_ref[h] = (hn * _silu(z_ref[h])).astype(BF16)


def _ret_consts():
    log_g = jnp.log1p(-jnp.exp2(-5.0 - jnp.arange(HEADS, dtype=F32)))
    idx = jnp.arange(RCHUNK, dtype=F32)
    full = (HEADS, RCHUNK, HEAD_DIM)
    gq = jnp.broadcast_to(jnp.exp(log_g[:, None] * (idx + 1.0))[..., None], full)
    gk = jnp.broadcast_to((jnp.exp(-log_g[:, None] * (idx + 1.0)) * HEAD_DIM ** -0.5)[..., None], full)
    gc = jnp.broadcast_to(jnp.exp(log_g * RCHUNK)[:, None, None], (HEADS, 1, HEAD_DIM))
    return gq, gk, gc


def _mlstm_kernel(lx_ref, lv_ref, lo_ref, gt_ref, z_ref, cw_ref, cb_ref, wq_ref, wk_ref, gb_ref,
                  skip_ref, g_ref, o_ref, xp_ref, st_ref, m_ref, s_ref, p_ref):
    @pl.when(pl.program_id(1) == 0)
    def _():
        xp_ref[:, 0:CONV_PAD, :] = jnp.zeros((HEADS, CONV_PAD, HEAD_DIM), F32)
        st_ref[...] = jnp.zeros_like(st_ref)
        m_ref[...] = jnp.zeros_like(m_ref)

    n = RCHUNK
    ones_nd = jnp.ones((n, HEAD_DIM), BF16)
    mean_dd = jnp.full((HEAD_DIM, HEAD_DIM), 1.0 / HEAD_DIM, BF16)
    row = lax.broadcasted_iota(jnp.int32, (n, n), 0)
    col = lax.broadcasted_iota(jnp.int32, (n, n), 1)
    tril = jnp.where(col <= row, 1.0, 0.0).astype(BF16)
    r128 = lax.broadcasted_iota(jnp.int32, (LANE, LANE), 0)
    c128 = lax.broadcasted_iota(jnp.int32, (LANE, LANE), 1)
    ident = jnp.where(r128 == c128, 1.0, 0.0).astype(BF16)
    srow = lax.broadcasted_iota(jnp.int32, (STRIP, LANE), 0)
    scol = lax.broadcasted_iota(jnp.int32, (STRIP, LANE), 1)

    lane = lax.broadcasted_iota(jnp.int32, (n, LANE), 1)
    x = gt_ref[0] + gb_ref[...]
    x = jnp.where((lane >= GATE_F_LANE) & (lane < GATE_F_LANE + HEADS), _log_sigmoid(x), x) * LOG2E
    x1, x2, x3 = _split3(x)
    cum = _dot(tril, x1) + _dot(tril, x2) + _dot(tril, x3)
    rt = x - pltpu.roll(cum, LANE - (GATE_F_LANE - GATE_I_LANE), 1)
    r1, r2, r3 = _split3(rt)
    rtt = _dot_nt(ident, r1) + _dot_nt(ident, r2) + _dot_nt(ident, r3)

    xcs, ks, qbs, vbs = [], [], [], []
    for h in range(HEADS):
        xp_ref[h, CONV_PAD:, :] = lx_ref[h]
        acc = jnp.zeros((n, HEAD_DIM), F32) + cb_ref[h]
        for j in range(CONV_WIDTH):
            off = CONV_PAD - (CONV_WIDTH - 1) + j
            acc = acc + xp_ref[h, off:off + n, :] * cw_ref[h, j:j + 1, :]
        xp_ref[h, 0:CONV_PAD, :] = lx_ref[h, n - CONV_PAD:n, :]
        xc = _silu(acc)
        xcb = xc.astype(BF16)
        k = _dot(xcb, wk_ref[h]) * HEAD_DIM ** -0.5
        qb = _dot(xcb, wq_ref[h]).astype(BF16)
        s_ref[h] = _dot_nt(qb, k.astype(BF16))
        xcs.append(xc)
        ks.append(k)
        qbs.append(qb)
        vbs.append(jnp.concatenate([lv_ref[h].astype(BF16), ones_nd], axis=-1))

    r_rows = [rtt[GATE_I_LANE + h:GATE_I_LANE + h + 1, :] for h in range(HEADS)]
    bases = [m_ref[h] for h in range(HEADS)]
    strips = [[] for _ in range(HEADS)]
    u_lasts = [None] * HEADS
    for i in range(n // STRIP):
        rows = slice(i * STRIP, (i + 1) * STRIP)
        d0 = (i * STRIP // LANE) * LANE
        w = d0 + LANE
        mask = (scol + d0) <= (srow + i * STRIP)
        for h in range(HEADS):
            r_row = r_rows[h]
            if d0 > 0 and (i * STRIP) % LANE == 0:
                bases[h] = jnp.maximum(bases[h], jnp.max(r_row[:, d0 - LANE:d0], axis=-1, keepdims=True))
            rmd = jnp.where(mask, r_row[:, d0:w], -jnp.inf)
            u_col = jnp.maximum(jnp.max(rmd, axis=-1, keepdims=True), bases[h])
            u = jnp.broadcast_to(u_col, (STRIP, LANE))
            p_ref[h, rows, d0:w] = (s_ref[h, rows, d0:w] * jnp.exp2(rmd - u)).astype(BF16)
            for c in range(d0 // LANE):
                cols = slice(c * LANE, (c + 1) * LANE)
                p_ref[h, rows, cols] = (s_ref[h, rows, cols] * jnp.exp2(r_row[:, cols] - u)).astype(BF16)
            if w < n:
                p_ref[h, rows, w:n] = jnp.zeros((STRIP, n - w), BF16)
            strips[h].append(u)
            u_lasts[h] = u_col[STRIP - 1:STRIP, :]
    us = [jnp.concatenate(st_h, axis=0) for st_h in strips]

    cells = []
    for h in range(HEADS):
        li = GATE_I_LANE + h
        lf = GATE_F_LANE + h
        m_st = m_ref[h]
        u, u_last, vb = us[h], u_lasts[h], vbs[h]
        w_inter = jnp.exp2(m_st - u)
        st = st_ref[h]
        intra = _dot(p_ref[h], vb)
        inter = _dot(qbs[h], st.astype(BF16))
        num = intra[:, :LANE] + w_inter * inter[:, :LANE]
        den = intra[:, LANE:] + w_inter * inter[:, LANE:]
        cum_f = jnp.broadcast_to(cum[:, lf:lf + 1], (n, LANE))
        cells.append(num / jnp.maximum(jnp.abs(den), jnp.exp2(-(cum_f + u))))

        decay = jnp.exp2(m_st - u_last)
        kw = ks[h] * jnp.exp2(jnp.broadcast_to(rt[:, li:li + 1], (n, LANE)) - u_last)
        upd = _dot(kw.T.astype(BF16), vb)
        st_ref[h] = decay * st + upd
        m_ref[h] = cum[n - 1:n, lf:lf + 1] + u_last

    for h in range(HEADS):
        cell = cells[h] * jax.nn.sigmoid(lo_ref[h])
        mu = _rowsum_mxu(cell, mean_dd)
        dev = cell - mu
        var = _rowsum_mxu(dev * dev, mean_dd)
        hn = dev * lax.rsqrt(var + LN_EPS) * g_ref[h]
        o_ref[h] = ((hn + skip_ref[h] * xcs[h]) * _silu(z_ref[h])).astype(BF16)


def _prep_mlstm(conv_w, conv_b, w_q, w_k, i_bias, f_bias, skip, norm_g):
    nl = conv_w.shape[0]
    cw = conv_w.reshape(nl, CONV_WIDTH, HEADS, HEAD_DIM).transpose(0, 2, 1, 3)
    cb = conv_b.reshape(nl, HEADS, 1, HEAD_DIM)
    zeros = lambda w: jnp.zeros((nl, w), F32)
    gb = jnp.concatenate([zeros(GATE_I_LANE), i_bias, f_bias, zeros(LANE - GATE_F_LANE - HEADS)], axis=-1)
    return (cw, cb, w_q.astype(BF16), w_k.astype(BF16), gb.reshape(nl, 1, LANE),
            skip.reshape(nl, HEADS, 1, HEAD_DIM), norm_g.reshape(nl, HEADS, 1, HEAD_DIM))


def _mlstm(hb, params, layer):
    nchunk = SEQ // RCHUNK
    return pl.pallas_call(
        _mlstm_kernel,
        out_shape=jax.ShapeDtypeStruct((HEADS, ROWS, LANE), BF16),
        grid=(BATCH, nchunk),
        in_specs=[_group_spec(BLK_LX, RCHUNK), _group_spec(BLK_LV, RCHUNK), _group_spec(BLK_LO, RCHUNK),
                  pl.BlockSpec((1, RCHUNK, LANE), lambda b, c: (BLK_KPE, b * nchunk + c, 0)),
                  _group_spec(BLK_Z + 2 * HEADS, RCHUNK)] + [_layer_spec(p.shape, layer) for p in params],
        out_specs=pl.BlockSpec((HEADS, RCHUNK, LANE), lambda b, c: (0, b * nchunk + c, 0)),
        scratch_shapes=[pltpu.VMEM((HEADS, RCHUNK + CONV_PAD, HEAD_DIM), F32),
                        pltpu.VMEM((HEADS, HEAD_DIM, 2 * HEAD_DIM), F32),
                        pltpu.VMEM((HEADS, 1, 1), F32),
                        pltpu.VMEM((HEADS, RCHUNK, RCHUNK), F32),
                        pltpu.VMEM((HEADS, RCHUNK, RCHUNK), BF16)],
        compiler_params=_cparams(("parallel", "arbitrary")),
        name="mlstm",
    )(hb, hb, hb, hb, hb, *params)


def _mla_prep_kernel(lat_ref, kpe_ref, cm_ref, sm_ref, qg_ref, wuq_ref, kvg_ref, wukv_ref,
                     q_out, k_out, v_out):
    scale = (MLA_NOPE + MLA_ROPE) ** -0.5 * LOG2E
    cs = cm_ref[0]
    sn = sm_ref[0]
    cq = jnp.concatenate([lat_ref[0], lat_ref[1], lat_ref[2]], axis=-1)
    qn = cq * lax.rsqrt(jnp.mean(jnp.square(cq), axis=-1, keepdims=True) + RMS_EPS) * qg_ref[...]
    q = _dot(qn.astype(BF16), wuq_ref[...])
    ckv = lat_ref[3]
    kvn = ckv * lax.rsqrt(jnp.mean(jnp.square(ckv), axis=-1, keepdims=True) + RMS_EPS) * kvg_ref[...]
    kv = _dot(kvn.astype(BF16), wukv_ref[...])
    kpe = kpe_ref[0]
    krot = (kpe * cs + pltpu.roll(kpe, LANE // 2, 1) * sn).astype(BF16)
    for h in range(HEADS):
        qr = q[:, GROUP_WIDTH + h * LANE:GROUP_WIDTH + (h + 1) * LANE]
        qr = qr * cs + pltpu.roll(qr, LANE // 2, 1) * sn
        q_out[0, h, :, 0:LANE] = (q[:, h * LANE:(h + 1) * LANE] * scale).astype(BF16)
        q_out[0, h, :, LANE:2 * LANE] = (qr * scale).astype(BF16)
        k_out[0, h, :, 0:LANE] = kv[:, h * LANE:(h + 1) * LANE].astype(BF16)
        k_out[0, h, :, LANE:2 * LANE] = krot
        v_out[0, h] = kv[:, GROUP_WIDTH + h * LANE:GROUP_WIDTH + (h + 1) * LANE].astype(BF16)


def _prep_mla_weights(w_uq, w_ukv):
    lead = w_uq.shape[:-1]
    wq = w_uq.reshape(lead + (HEADS, MLA_NOPE + MLA_ROPE))
    nope = wq[..., :MLA_NOPE].reshape(lead + (GROUP_WIDTH,))
    half = MLA_ROPE // 2
    zeros = jnp.zeros(lead + (HEADS, 64 - half), w_uq.dtype)
    rope = jnp.concatenate([wq[..., MLA_NOPE:MLA_NOPE + half], zeros, wq[..., MLA_NOPE + half:], zeros], axis=-1)
    wq_p = jnp.concatenate([nope, rope.reshape(lead + (HEADS * LANE,))], axis=-1).astype(BF16)
    lead = w_ukv.shape[:-1]
    wkv = w_ukv.reshape(lead + (HEADS, MLA_NOPE + HEAD_DIM))
    wkv_p = jnp.concatenate([wkv[..., :MLA_NOPE].reshape(lead + (GROUP_WIDTH,)),
                             wkv[..., MLA_NOPE:].reshape(lead + (GROUP_WIDTH,))], axis=-1).astype(BF16)
    return wq_p, wkv_p


def _attn_tile(nfull, q_ref, k_ref, v_ref, z_ref, o_ref, s_ref, p_ref, m_ref):
    kvlen = (nfull + 1) * TQ_ATT
    d0 = nfull * TQ_ATT
    hq = TQ_ATT // 2
    srow = lax.broadcasted_iota(jnp.int32, (STRIP, LANE), 0)
    scol = lax.broadcasted_iota(jnp.int32, (STRIP, LANE), 1)

    def strip_blocks(g, i):
        rows = slice(i * STRIP, (i + 1) * STRIP)
        wd = -(-(i + 1) * STRIP // LANE) * LANE
        ncol = (d0 + wd) // LANE
        blks = [s_ref[g, rows, c * LANE:(c + 1) * LANE] for c in range(ncol)]
        blks[-1] = jnp.where(scol + (wd - LANE) <= srow + i * STRIP, blks[-1], -jnp.inf)
        return rows, ncol, blks

    qrows = slice(d0, d0 + TQ_ATT)
    for g in range(ATT_HEADS):
        q = q_ref[0, g, qrows, :]
        for j in range(nfull):
            s_ref[g, :, j * TQ_ATT:(j + 1) * TQ_ATT] = _dot_nt(q, k_ref[0, g, j * TQ_ATT:(j + 1) * TQ_ATT, :])
        s_ref[g, 0:hq, d0:d0 + hq] = _dot_nt(q[0:hq], k_ref[0, g, d0:d0 + hq, :])
        s_ref[g, hq:TQ_ATT, d0:kvlen] = _dot_nt(q[hq:TQ_ATT], k_ref[0, g, d0:kvlen, :])
    for g in range(ATT_HEADS):
        for i in range(TQ_ATT // STRIP):
            rows, ncol, blks = strip_blocks(g, i)
            mx = blks[0]
            for blk in blks[1:]:
                mx = jnp.maximum(mx, blk)
            m_ref[g, rows, :] = jnp.broadcast_to(jnp.max(mx, axis=-1, keepdims=True), (STRIP, LANE))
        for i in range(TQ_ATT // STRIP):
            rows, ncol, blks = strip_blocks(g, i)
            m = m_ref[g, rows, :]
            for c, blk in enumerate(blks):
                p_ref[g, rows, c * LANE:(c + 1) * LANE] = jnp.exp2(blk - m).astype(BF16)
            kl = d0 + hq if (i + 1) * STRIP <= hq else kvlen
            if ncol * LANE < kl:
                p_ref[g, rows, ncol * LANE:kl] = jnp.zeros((STRIP, kl - ncol * LANE), BF16)
        v1 = jnp.concatenate([v_ref[0, g, 0:kvlen, :], jnp.ones((kvlen, LANE), BF16)], axis=-1)
        for r0, r1, kl in ((0, hq, d0 + hq), (hq, TQ_ATT, kvlen)):
            pv = _dot(p_ref[g, r0:r1, 0:kl], v1[0:kl])
            gate = _silu(z_ref[g, d0 + r0:d0 + r1, :])
            o_ref[g, d0 + r0:d0 + r1, :] = (pv[:, :LANE] / pv[:, LANE:] * gate).astype(BF16)


def _mla_attn_kernel(q_ref, k_ref, v_ref, z_ref, o_ref, s_ref, p_ref, m_ref):
    for nfull in range(SEQ // TQ_ATT):
        _attn_tile(nfull, q_ref, k_ref, v_ref, z_ref, o_ref, s_ref, p_ref, m_ref)


def _mla_attn(q, k, v, hb):
    return pl.pallas_call(
        _mla_attn_kernel,
        out_shape=jax.ShapeDtypeStruct((HEADS, ROWS, LANE), BF16),
        grid=(BATCH, HEADS // ATT_HEADS),
        in_specs=[pl.BlockSpec((1, ATT_HEADS, SEQ, 2 * LANE), lambda b, h: (b, h, 0, 0)),
                  pl.BlockSpec((1, ATT_HEADS, SEQ, 2 * LANE), lambda b, h: (b, h, 0, 0)),
                  pl.BlockSpec((1, ATT_HEADS, SEQ, LANE), lambda b, h: (b, h, 0, 0)),
                  pl.BlockSpec((ATT_HEADS, SEQ, LANE), lambda b, h: ((BLK_Z + HEADS) // ATT_HEADS + h, b, 0))],
        out_specs=pl.BlockSpec((ATT_HEADS, SEQ, LANE), lambda b, h: (h, b, 0)),
        scratch_shapes=[pltpu.VMEM((ATT_HEADS, TQ_ATT, SEQ), F32), pltpu.VMEM((ATT_HEADS, TQ_ATT, SEQ), BF16),
                        pltpu.VMEM((ATT_HEADS, TQ_ATT, LANE), F32)],
        compiler_params=_cparams(("parallel", "parallel")),
        name="mla_attn",
    )(q, k, v, hb)


def _mem_attn_kernel(q_ref, k_ref, v_ref, z_ref, o_ref):
    ones = jnp.ones((MEM_LEN, LANE), BF16)
    scores = []
    for h in range(HEADS):
        q = (q_ref[h] * (HEAD_DIM ** -0.5 * LOG2E)).astype(BF16)
        scores.append(_dot_nt(q, k_ref[:, h * LANE:(h + 1) * LANE]))
    for h in range(HEADS):
        s = scores[h]
        p = jnp.exp2(s - jnp.max(s, axis=-1, keepdims=True)).astype(BF16)
        pv = _dot(p, jnp.concatenate([v_ref[:, h * LANE:(h + 1) * LANE], ones], axis=-1))
        o_ref[h] = (pv[:, :LANE] / pv[:, LANE:] * _silu(z_ref[h])).astype(BF16)


def _light_kernel(rq, rk, rv, rz, cr, sr, gq, gk, gc, rg, mq, mk, mv, mz, lat, kpe, cm, sm, qg, wuq, kvg, wukv,
                  ya, yd, q_out, k_out, v_out, st_ref):
    _ret_kernel(rq, rk, rv, rz, cr, sr, gq, gk, gc, rg, ya, st_ref)
    _mem_attn_kernel(mq, mk, mv, mz, yd)
    _mla_prep_kernel(lat, kpe, cm, sm, qg, wuq, kvg, wukv, q_out, k_out, v_out)


def _light_mixers(hb, tables, ret_consts, ret_g, kvm, qg, wuq_p, kvg, wukv_p, layer):
    cos_r, sin_r, cos_m, sin_m = tables
    gq, gk, gc = ret_consts
    nchunk = SEQ // RCHUNK
    tab = pl.BlockSpec((1, RCHUNK, LANE), lambda b, c: (b, c, 0))
    row_blk = lambda blk: pl.BlockSpec((1, RCHUNK, LANE), lambda b, c: (blk, b * nchunk + c, 0))
    y_shape = jax.ShapeDtypeStruct((HEADS, ROWS, LANE), BF16)
    y_spec = pl.BlockSpec((HEADS, RCHUNK, LANE), lambda b, c: (0, b * nchunk + c, 0))
    qk_shape = jax.ShapeDtypeStruct((BATCH, HEADS, SEQ, 2 * LANE), BF16)
    v_shape = jax.ShapeDtypeStruct((BATCH, HEADS, SEQ, LANE), BF16)
    qk_spec = pl.BlockSpec((1, HEADS, RCHUNK, 2 * LANE), lambda b, c: (b, 0, c, 0))
    v_spec = pl.BlockSpec((1, HEADS, RCHUNK, LANE), lambda b, c: (b, 0, c, 0))
    return pl.pallas_call(
        _light_kernel,
        out_shape=(y_shape, y_shape, qk_shape, qk_shape, v_shape),
        grid=(BATCH, nchunk),
        in_specs=[_group_spec(BLK_RQ, RCHUNK), _group_spec(BLK_RK, RCHUNK), _group_spec(BLK_RV, RCHUNK),
                  _group_spec(BLK_Z, RCHUNK), tab, tab,
                  _const_spec(gq.shape), _const_spec(gk.shape), _const_spec(gc.shape),
                  _layer_spec(ret_g.shape, layer),
                  _group_spec(BLK_MQ, RCHUNK),
                  pl.BlockSpec((MEM_LEN, GROUP_WIDTH), lambda b, c: (b, 2 * layer)),
                  pl.BlockSpec((MEM_LEN, GROUP_WIDTH), lambda b, c: (b, 2 * layer + 1)),
                  _group_spec(BLK_Z + 3 * HEADS, RCHUNK),
                  _group_spec(BLK_CQ, RCHUNK), row_blk(BLK_KPE), tab, tab,
                  _layer_spec(qg.shape, layer), _layer_spec(wuq_p.shape, layer),
                  _layer_spec(kvg.shape, layer), _layer_spec(wukv_p.shape, layer)],
        out_specs=(y_spec, y_spec, qk_spec, qk_spec, v_spec),
        scratch_shapes=[pltpu.VMEM((HEADS, HEAD_DIM, HEAD_DIM), F32)],
        compiler_params=_cparams(("parallel", "arbitrary")),
        name="light_mixers",
    )(hb, hb, hb, hb, cos_r, sin_r, gq, gk, gc, ret_g, hb, kvm, kvm, hb, hb, hb, cos_m, sin_m,
      qg, wuq_p, kvg, wukv_p)


def _out_kernel(ya_ref, yb_ref, yc_ref, yd_ref, w_ref, x_ref, g_ref, b_ref, o_ref):
    for t in range(TM_OUT // SUB_OUT):
        rows = slice(t * SUB_OUT, (t + 1) * SUB_OUT)
        parts = [ref[h, rows, :] for ref in (ya_ref, yb_ref, yc_ref, yd_ref) for h in range(HEADS)]
        y = jnp.concatenate(parts, axis=-1)
        r = DEEPNORM_ALPHA * x_ref[rows, :] + _dot(y, w_ref[...])
        mu = jnp.mean(r, axis=-1, keepdims=True)
        var = jnp.mean(jnp.square(r - mu), axis=-1, keepdims=True)
        o_ref[rows, :] = (r - mu) * lax.rsqrt(var + LN_EPS) * g_ref[...] + b_ref[...]


def _outproj(ya, yb, yc, yd, w_out, x2d, ln_g, ln_b, layer):
    yspec = pl.BlockSpec((HEADS, TM_OUT, LANE), lambda i: (0, i, 0))
    return pl.pallas_call(
        _out_kernel,
        out_shape=jax.ShapeDtypeStruct((ROWS, D_MODEL), F32),
        grid=(ROWS // TM_OUT,),
        in_specs=[yspec, yspec, yspec, yspec,
                  pl.BlockSpec((None,) + tuple(w_out.shape[1:]), lambda i: (layer, 0, 0),
                               pipeline_mode=pl.Buffered(1)),
                  pl.BlockSpec((TM_OUT, D_MODEL), lambda i: (i, 0)),
                  _layer_spec(ln_g.shape, layer), _layer_spec(ln_b.shape, layer)],
        out_specs=pl.BlockSpec((TM_OUT, D_MODEL), lambda i: (i, 0)),
        compiler_params=_cparams(("parallel",)),
        name="outproj_ln",
    )(ya, yb, yc, yd, w_out, x2d, ln_g, ln_b)


def kernel(x, mem, positions, w_in, ret_norm_g, mla_q_norm_g, mla_w_uq, mla_kv_norm_g, mla_w_ukv, ml_conv_w, ml_conv_b, ml_w_q, ml_w_k, ml_i_bias, ml_f_bias, ml_skip, ml_norm_g, w_mem_kv, w_out, ln_g, ln_b):
    assert x.shape == (BATCH, SEQ, D_MODEL) and mem.shape == (BATCH, MEM_LEN, D_MODEL)
    tables = _rope_tables(positions)
    kvm = _mem_kv(mem, w_mem_kv)
    w_in_p = _prep_w_in(w_in)
    wuq_p, wukv_p = _prep_mla_weights(mla_w_uq, mla_w_ukv)
    w_out_b = w_out.astype(BF16)
    ret_consts = _ret_consts()
    ret_g = ret_norm_g.reshape(DEPTH, HEADS, 1, HEAD_DIM)
    mlstm_params = _prep_mlstm(ml_conv_w, ml_conv_b, ml_w_q, ml_w_k, ml_i_bias, ml_f_bias, ml_skip, ml_norm_g)
    qg = mla_q_norm_g.reshape(DEPTH, 1, MLA_Q_RANK)
    kvg = mla_kv_norm_g.reshape(DEPTH, 1, MLA_KV_RANK)
    lng = ln_g.reshape(DEPTH, 1, D_MODEL)
    lnb = ln_b.reshape(DEPTH, 1, D_MODEL)

    x2d = x.reshape(ROWS, D_MODEL)
    for l in range(DEPTH):
        hb = _inproj(x2d, w_in_p, l)
        ya, yd, q, k, v = _light_mixers(hb, tables, ret_consts, ret_g, kvm, qg, wuq_p, kvg, wukv_p, l)
        yb = _mla_attn(q, k, v, hb)
        yc = _mlstm(hb, mlstm_params, l)
        x2d = _outproj(ya, yb, yc, yd, w_out_b, x2d, lng, lnb, l)
    return x2d.reshape(BATCH, SEQ, D_MODEL)
```

```python
import numpy as np
import jax
import jax.numpy as jnp
from jax import lax
from jax.experimental import pallas as pl
from jax.experimental.pallas import tpu as pltpu

F32 = jnp.float32
BF16 = jnp.bfloat16

D_MODEL = 2048
BATCH = 8
SEQ = 2048
DEPTH = 4
MEM_LEN = 256
HEAD_DIM = 128
HEADS = 4
GROUP_WIDTH = HEADS * HEAD_DIM
MLA_NOPE = 128
MLA_ROPE = 64
MLA_Q_RANK = 384
MLA_KV_RANK = 128
CONV_WIDTH = 4
MIX_WIDTH = 4 * GROUP_WIDTH
ROPE_THETA = 10000.0
LN_EPS = 1e-5
RMS_EPS = 1e-6
DEEPNORM_ALPHA = (2 * DEPTH) ** 0.25
IN_SPLITS = (512, 512, 512, MLA_Q_RANK, MLA_KV_RANK, MLA_ROPE, 512, 512, 512, HEADS, HEADS, 512, MIX_WIDTH)

LANE = 128
ROWS = BATCH * SEQ

BLK_RQ, BLK_RK, BLK_RV = 0, 4, 8
BLK_LV, BLK_LO = 12, 16
BLK_MQ = 20
BLK_ZA, BLK_ZB, BLK_ZD = 24, 28, 32
BLK_CQ, BLK_CKV = 36, 39
NBLK16 = 40
BLK_LX, BLK_ZC = 0, 4
BLK_KPE = 8
NBLK32 = 10
NBLK = NBLK16 + NBLK32
GATE_I_LANE, GATE_F_LANE = 32, 36

TM_IN, TN_IN = 1024, 1280
TM_OUT = 512
SUB_OUT = 256
TQ_ATT = 512
ATT_HEADS = 2
RCHUNK = 512
CONV_PAD = 8
STRIP = 64
LOG2E = 1.4426950408889634
VMEM_LIMIT = 56 * 1024 * 1024


def _cparams(sem):
    return pltpu.CompilerParams(dimension_semantics=sem, vmem_limit_bytes=VMEM_LIMIT)


def _silu(z):
    return z * jax.nn.sigmoid(z)


def _log_sigmoid(x):
    return jnp.minimum(x, 0.0) - jnp.log1p(jnp.exp(-jnp.abs(x)))


def _split3(x):
    x1 = x.astype(BF16)
    r1 = x - x1.astype(F32)
    x2 = r1.astype(BF16)
    x3 = (r1 - x2.astype(F32)).astype(BF16)
    return x1, x2, x3


def _dot(a, b):
    return jnp.dot(a, b, preferred_element_type=F32)


def _rowsum_mxu(x, w):
    hi = x.astype(BF16)
    lo = (x - hi.astype(F32)).astype(BF16)
    return _dot(hi, w) + _dot(lo, w)


def _dot_nt(a, b):
    return lax.dot_general(a, b, (((1,), (1,)), ((), ())), preferred_element_type=F32)


def _tables_kernel(pos_ref, c_ref, cr_ref, sr_ref, cm_ref, sm_ref):
    pos = pos_ref[0].astype(F32)
    ang = pos * c_ref[0:1, :]
    cs = jnp.cos(ang)
    sn = jnp.sin(ang)
    cs_sw = pltpu.roll(cs, LANE // 2, 1)
    sn_sw = pltpu.roll(sn, LANE // 2, 1)
    lane = lax.broadcasted_iota(jnp.int32, cs.shape, 1)
    half_m = MLA_ROPE // 2
    first = lane < LANE // 2
    m_lo = lane < half_m
    m_hi = (lane >= LANE // 2) & (lane < LANE // 2 + half_m)
    cr_ref[0] = jnp.where(first, cs, cs_sw)
    sr_ref[0] = jnp.where(first, -sn, sn_sw)
    cm_ref[0] = jnp.where(m_lo, cs_sw, jnp.where(m_hi, cs, 0.0))
    sm_ref[0] = jnp.where(m_lo, -sn_sw, jnp.where(m_hi, sn, 0.0))


def _rope_tables(positions):
    half_r = HEAD_DIM // 2
    fr = ROPE_THETA ** (-jnp.arange(half_r, dtype=F32) / half_r)
    half_m = MLA_ROPE // 2
    fm = ROPE_THETA ** (-jnp.arange(half_m, dtype=F32) / half_m)
    freqs = jnp.concatenate([fr, fm, jnp.zeros((LANE - half_r - half_m,), F32)])
    consts = jnp.concatenate([freqs[None, :], jnp.zeros((7, LANE), F32)], axis=0)
    ts = 512
    tab = jax.ShapeDtypeStruct((BATCH, SEQ, LANE), F32)
    spec = pl.BlockSpec((1, ts, LANE), lambda b, i: (b, i, 0))
    return pl.pallas_call(
        _tables_kernel,
        out_shape=(tab, tab, tab, tab),
        grid=(BATCH, SEQ // ts),
        in_specs=[pl.BlockSpec((1, ts, 1), lambda b, i: (b, i, 0)),
                  pl.BlockSpec((8, LANE), lambda b, i: (0, 0))],
        out_specs=(spec, spec, spec, spec),
        compiler_params=_cparams(("parallel", "parallel")),
        name="rope_tables",
    )(positions.reshape(BATCH, SEQ, 1), consts)


def _matmul_kernel(a_ref, w_ref, o_ref):
    o_ref[...] = _dot(a_ref[...].astype(BF16), w_ref[...]).astype(o_ref.dtype)


def _mem_kv(mem, w_mem_kv):
    w = jnp.transpose(w_mem_kv, (1, 0, 2)).reshape(D_MODEL, DEPTH * 2 * GROUP_WIDTH).astype(BF16)
    a = mem.reshape(BATCH * MEM_LEN, D_MODEL)
    tm, tn = 512, 1024
    return pl.pallas_call(
        _matmul_kernel,
        out_shape=jax.ShapeDtypeStruct((BATCH * MEM_LEN, DEPTH * 2 * GROUP_WIDTH), BF16),
        grid=(BATCH * MEM_LEN // tm, DEPTH * 2 * GROUP_WIDTH // tn),
        in_specs=[pl.BlockSpec((tm, D_MODEL), lambda i, j: (i, 0)),
                  pl.BlockSpec((D_MODEL, tn), lambda i, j: (0, j))],
        out_specs=pl.BlockSpec((tm, tn), lambda i, j: (i, j)),
        compiler_params=_cparams(("parallel", "parallel")),
        name="mem_kv",
    )(a, w)


def _inproj_kernel(x_ref, w_ref, o16_ref, o32_ref, xb_ref):
    j = pl.program_id(1)

    @pl.when(j == 0)
    def _():
        xb_ref[...] = x_ref[...].astype(BF16)

    def tile(o_ref):
        xb = xb_ref[...]
        for k2 in range(TN_IN // 256):
            r = _dot_nt(xb, w_ref[k2 * 256:(k2 + 1) * 256, :])
            o_ref[2 * k2] = r[:, :LANE].astype(o_ref.dtype)
            o_ref[2 * k2 + 1] = r[:, LANE:].astype(o_ref.dtype)

    @pl.when(j < NBLK16 * LANE // TN_IN)
    def _():
        tile(o16_ref)

    @pl.when(j == NBLK16 * LANE // TN_IN)
    def _():
        tile(o32_ref)


def _inproj(x2d, wt_p, layer):
    nb = TN_IN // LANE
    assert NBLK32 == nb and NBLK16 % nb == 0
    last16 = NBLK16 // nb - 1
    return pl.pallas_call(
        _inproj_kernel,
        out_shape=(jax.ShapeDtypeStruct((NBLK16, ROWS, LANE), BF16),
                   jax.ShapeDtypeStruct((NBLK32, ROWS, LANE), F32)),
        grid=(ROWS // TM_IN, NBLK * LANE // TN_IN),
        in_specs=[pl.BlockSpec((TM_IN, D_MODEL), lambda i, j: (i, 0)),
                  pl.BlockSpec((None, TN_IN, D_MODEL), lambda i, j: (layer, j, 0))],
        out_specs=(pl.BlockSpec((nb, TM_IN, LANE), lambda i, j: (jnp.minimum(j, last16), i, 0)),
                   pl.BlockSpec((nb, TM_IN, LANE), lambda i, j: (0, i, 0))),
        scratch_shapes=[pltpu.VMEM((TM_IN, D_MODEL), BF16)],
        compiler_params=_cparams(("parallel", "arbitrary")),
        name="inproj",
    )(x2d, wt_p)


def _prep_w_in(w_in):
    wt = jnp.swapaxes(w_in, 1, 2)
    idx = np.cumsum(IN_SPLITS)[:-1].tolist()
    (r_q, r_k, r_v, a_cq, a_ckv, a_kpe, l_x, l_v, l_o, l_i, l_f, c_q, z) = jnp.split(wt, idx, axis=1)
    half = MLA_ROPE // 2
    zeros = lambda n: jnp.zeros((wt.shape[0], n, wt.shape[2]), wt.dtype)
    blk = [a_kpe[:, :half], l_i, l_f, zeros(64 - half - 2 * HEADS), a_kpe[:, half:], zeros(64 - half)]
    z_a, z_b, z_c, z_d = jnp.split(z, 4, axis=1)
    return jnp.concatenate([r_q, r_k, r_v, l_v, l_o, c_q, z_a, z_b, z_d, a_cq, a_ckv, l_x, z_c] + blk
                           + [zeros(LANE)], axis=1).astype(BF16)


def _group_spec(base, rows):
    nchunk = SEQ // rows
    return pl.BlockSpec((HEADS, rows, LANE), lambda b, c: (base // HEADS, b * nchunk + c, 0))


def _layer_spec(shape, layer):
    nd = len(shape) - 1
    return pl.BlockSpec((None,) + tuple(shape[1:]), lambda *_: (layer,) + (0,) * nd)


def _const_spec(shape):
    nd = len(shape)
    return pl.BlockSpec(tuple(shape), lambda *_: (0,) * nd)


def _ret_kernel(q_ref, k_ref, v_ref, z_ref, cos_ref, sin_ref, gq_ref, gk_ref, gc_ref, g_ref, o_ref, st_ref):
    @pl.when(pl.program_id(1) == 0)
    def _():
        st_ref[...] = jnp.zeros_like(st_ref)

    cs = cos_ref[0]
    sn = sin_ref[0]
    row = lax.broadcasted_iota(jnp.int32, (RCHUNK, RCHUNK), 0)
    col = lax.broadcasted_iota(jnp.int32, (RCHUNK, RCHUNK), 1)
    causal = col <= row
    qbs, kts, vbs, scs = [], [], [], []
    for h in range(HEADS):
        q = q_ref[h].astype(F32)
        q = (q * cs + pltpu.roll(q, HEAD_DIM // 2, 1) * sn) * gq_ref[h]
        k = k_ref[h].astype(F32)
        k = (k * cs + pltpu.roll(k, HEAD_DIM // 2, 1) * sn) * gk_ref[h]
        qb = q.astype(BF16)
        scs.append(_dot_nt(qb, k.astype(BF16)))
        qbs.append(qb)
        kts.append(k.T.astype(BF16))
        vbs.append(v_ref[h])
    outs = []
    for h in range(HEADS):
        st = st_ref[h]
        sc = jnp.where(causal, scs[h], 0.0).astype(BF16)
        outs.append(_dot(sc, vbs[h]) + _dot(qbs[h], st.astype(BF16)))
        st_ref[h] = gc_ref[h] * (st + _dot(kts[h], vbs[h]))
    for h in range(HEADS):
        out = outs[h]
        mu = jnp.mean(out, axis=-1, keepdims=True)
        dev = out - mu
        var = jnp.mean(dev * dev, axis=-1, keepdims=True)
        hn = dev * lax.rsqrt(var + LN_EPS) * g_ref[h]
        o_ref[h] = (hn * _silu(z_ref[h].astype(F32))).astype(BF16)


def _ret_consts():
    log_g = jnp.log1p(-jnp.exp2(-5.0 - jnp.arange(HEADS, dtype=F32)))
    idx = jnp.arange(RCHUNK, dtype=F32)
    full = (HEADS, RCHUNK, HEAD_DIM)
    gq = jnp.broadcast_to(jnp.exp(log_g[:, None] * (idx + 1.0))[..., None], full)
    gk = jnp.broadcast_to((jnp.exp(-log_g[:, None] * (idx + 1.0)) * HEAD_DIM ** -0.5)[..., None], full)
    gc = jnp.broadcast_to(jnp.exp(log_g * RCHUNK)[:, None, None], (HEADS, 1, HEAD_DIM))
    return gq, gk, gc


def _mlstm_kernel(lx_ref, lv_ref, lo_ref, gt_ref, z_ref, cw_ref, cb_ref, wq_ref, wk_ref, gb_ref,
                  skip_ref, g_ref, o_ref, xp_ref, st_ref, m_ref, s_ref, p_ref):
    @pl.when(pl.program_id(1) == 0)
    def _():
        xp_ref[:, 0:CONV_PAD, :] = jnp.zeros((HEADS, CONV_PAD, HEAD_DIM), F32)
        st_ref[...] = jnp.zeros_like(st_ref)
        m_ref[...] = jnp.zeros_like(m_ref)

    n = RCHUNK
    ones_nd = jnp.ones((n, HEAD_DIM), BF16)
    mean_dd = jnp.full((HEAD_DIM, HEAD_DIM), 1.0 / HEAD_DIM, BF16)
    row = lax.broadcasted_iota(jnp.int32, (n, n), 0)
    col = lax.broadcasted_iota(jnp.int32, (n, n), 1)
    tril = jnp.where(col <= row, 1.0, 0.0).astype(BF16)
    r128 = lax.broadcasted_iota(jnp.int32, (LANE, LANE), 0)
    c128 = lax.broadcasted_iota(jnp.int32, (LANE, LANE), 1)
    ident = jnp.where(r128 == c128, 1.0, 0.0).astype(BF16)
    srow = lax.broadcasted_iota(jnp.int32, (STRIP, LANE), 0)
    scol = lax.broadcasted_iota(jnp.int32, (STRIP, LANE), 1)

    lane = lax.broadcasted_iota(jnp.int32, (n, LANE), 1)
    x = gt_ref[0] + gb_ref[...]
    x = jnp.where((lane >= GATE_F_LANE) & (lane < GATE_F_LANE + HEADS), _log_sigmoid(x), x) * LOG2E
    x1, x2, x3 = _split3(x)
    cum = _dot(tril, x1) + _dot(tril, x2) + _dot(tril, x3)
    rt = x - pltpu.roll(cum, LANE - (GATE_F_LANE - GATE_I_LANE), 1)
    r1, r2, r3 = _split3(rt)
    rtt = _dot_nt(ident, r1) + _dot_nt(ident, r2) + _dot_nt(ident, r3)

    xcs, ks, qbs, vbs = [], [], [], []
    for h in range(HEADS):
        xp_ref[h, CONV_PAD:, :] = lx_ref[h]
        acc = jnp.zeros((n, HEAD_DIM), F32) + cb_ref[h]
        for j in range(CONV_WIDTH):
            off = CONV_PAD - (CONV_WIDTH - 1) + j
            acc = acc + xp_ref[h, off:off + n, :] * cw_ref[h, j:j + 1, :]
        xp_ref[h, 0:CONV_PAD, :] = lx_ref[h, n - CONV_PAD:n, :]
        xc = _silu(acc)
        xcb = xc.astype(BF16)
        k = _dot(xcb, wk_ref[h]) * HEAD_DIM ** -0.5
        qb = _dot(xcb, wq_ref[h]).astype(BF16)
        s_ref[h] = _dot_nt(qb, k.astype(BF16))
        xcs.append(xc)
        ks.append(k)
        qbs.append(qb)
        vbs.append(jnp.concatenate([lv_ref[h], ones_nd], axis=-1))

    r_rows = [rtt[GATE_I_LANE + h:GATE_I_LANE + h + 1, :] for h in range(HEADS)]
    bases = [m_ref[h] for h in range(HEADS)]
    strips = [[] for _ in range(HEADS)]
    u_lasts = [None] * HEADS
    for i in range(n // STRIP):
        rows = slice(i * STRIP, (i + 1) * STRIP)
        d0 = (i * STRIP // LANE) * LANE
        w = d0 + LANE
        mask = (scol + d0) <= (srow + i * STRIP)
        for h in range(HEADS):
            r_row = r_rows[h]
            if d0 > 0 and (i * STRIP) % LANE == 0:
                bases[h] = jnp.maximum(bases[h], jnp.max(r_row[:, d0 - LANE:d0], axis=-1, keepdims=True))
            rmd = jnp.where(mask, r_row[:, d0:w], -jnp.inf)
            u_col = jnp.maximum(jnp.max(rmd, axis=-1, keepdims=True), bases[h])
            u = jnp.broadcast_to(u_col, (STRIP, LANE))
            p_ref[h, rows, d0:w] = (s_ref[h, rows, d0:w] * jnp.exp2(rmd - u)).astype(BF16)
            for c in range(d0 // LANE):
                cols = slice(c * LANE, (c + 1) * LANE)
                p_ref[h, rows, cols] = (s_ref[h, rows, cols] * jnp.exp2(r_row[:, cols] - u)).astype(BF16)
            if w < n:
                p_ref[h, rows, w:n] = jnp.zeros((STRIP, n - w), BF16)
            strips[h].append(u)
            u_lasts[h] = u_col[STRIP - 1:STRIP, :]
    us = [jnp.concatenate(st_h, axis=0) for st_h in strips]

    cells = []
    for h in range(HEADS):
        li = GATE_I_LANE + h
        lf = GATE_F_LANE + h
        m_st = m_ref[h]
        u, u_last, vb = us[h], u_lasts[h], vbs[h]
        w_inter = jnp.exp2(m_st - u)
        st = st_ref[h]
        intra = _dot(p_ref[h], vb)
        inter = _dot(qbs[h], st.astype(BF16))
        num = intra[:, :LANE] + w_inter * inter[:, :LANE]
        den = intra[:, LANE:] + w_inter * inter[:, LANE:]
        cum_f = jnp.broadcast_to(cum[:, lf:lf + 1], (n, LANE))
        cells.append(num / jnp.maximum(jnp.abs(den), jnp.exp2(-(cum_f + u))))

        decay = jnp.exp2(m_st - u_last)
        kw = ks[h] * jnp.exp2(jnp.broadcast_to(rt[:, li:li + 1], (n, LANE)) - u_last)
        upd = _dot(kw.T.astype(BF16), vb)
        st_ref[h] = decay * st + upd
        m_ref[h] = cum[n - 1:n, lf:lf + 1] + u_last

    for h in range(HEADS):
        cell = cells[h] * jax.nn.sigmoid(lo_ref[h].astype(F32))
        mu = _rowsum_mxu(cell, mean_dd)
        dev = cell - mu
        var = _rowsum_mxu(dev * dev, mean_dd)
        hn = dev * lax.rsqrt(var + LN_EPS) * g_ref[h]
        o_ref[h] = ((hn + skip_ref[h] * xcs[h]) * _silu(z_ref[h])).astype(BF16)


def _prep_mlstm(conv_w, conv_b, w_q, w_k, i_bias, f_bias, skip, norm_g):
    nl = conv_w.shape[0]
    cw = conv_w.reshape(nl, CONV_WIDTH, HEADS, HEAD_DIM).transpose(0, 2, 1, 3)
    cb = conv_b.reshape(nl, HEADS, 1, HEAD_DIM)
    zeros = lambda w: jnp.zeros((nl, w), F32)
    gb = jnp.concatenate([zeros(GATE_I_LANE), i_bias, f_bias, zeros(LANE - GATE_F_LANE - HEADS)], axis=-1)
    return (cw, cb, w_q.astype(BF16), w_k.astype(BF16), gb.reshape(nl, 1, LANE),
            skip.reshape(nl, HEADS, 1, HEAD_DIM), norm_g.reshape(nl, HEADS, 1, HEAD_DIM))


def _mlstm(hb16, hb32, params, layer):
    nchunk = SEQ // RCHUNK
    return pl.pallas_call(
        _mlstm_kernel,
        out_shape=jax.ShapeDtypeStruct((HEADS, ROWS, LANE), BF16),
        grid=(BATCH, nchunk),
        in_specs=[_group_spec(BLK_LX, RCHUNK), _group_spec(BLK_LV, RCHUNK), _group_spec(BLK_LO, RCHUNK),
                  pl.BlockSpec((1, RCHUNK, LANE), lambda b, c: (BLK_KPE, b * nchunk + c, 0)),
                  _group_spec(BLK_ZC, RCHUNK)] + [_layer_spec(p.shape, layer) for p in params],
        out_specs=pl.BlockSpec((HEADS, RCHUNK, LANE), lambda b, c: (0, b * nchunk + c, 0)),
        scratch_shapes=[pltpu.VMEM((HEADS, RCHUNK + CONV_PAD, HEAD_DIM), F32),
                        pltpu.VMEM((HEADS, HEAD_DIM, 2 * HEAD_DIM), F32),
                        pltpu.VMEM((HEADS, 1, 1), F32),
                        pltpu.VMEM((HEADS, RCHUNK, RCHUNK), F32),
                        pltpu.VMEM((HEADS, RCHUNK, RCHUNK), BF16)],
        compiler_params=_cparams(("parallel", "arbitrary")),
        name="mlstm",
    )(hb32, hb16, hb16, hb32, hb32, *params)


def _mla_prep_kernel(lat_ref, kpe_ref, cm_ref, sm_ref, qg_ref, wuq_ref, kvg_ref, wukv_ref,
                     q_out, k_out, v_out):
    scale = (MLA_NOPE + MLA_ROPE) ** -0.5 * LOG2E
    cs = cm_ref[0]
    sn = sm_ref[0]
    cq = jnp.concatenate([lat_ref[0], lat_ref[1], lat_ref[2]], axis=-1).astype(F32)
    qn = cq * lax.rsqrt(jnp.mean(jnp.square(cq), axis=-1, keepdims=True) + RMS_EPS) * qg_ref[...]
    q = _dot(qn.astype(BF16), wuq_ref[...])
    ckv = lat_ref[3].astype(F32)
    kvn = ckv * lax.rsqrt(jnp.mean(jnp.square(ckv), axis=-1, keepdims=True) + RMS_EPS) * kvg_ref[...]
    kv = _dot(kvn.astype(BF16), wukv_ref[...])
    kpe = kpe_ref[0]
    krot = (kpe * cs + pltpu.roll(kpe, LANE // 2, 1) * sn).astype(BF16)
    for h in range(HEADS):
        qr = q[:, GROUP_WIDTH + h * LANE:GROUP_WIDTH + (h + 1) * LANE]
        qr = qr * cs + pltpu.roll(qr, LANE // 2, 1) * sn
        q_out[0, h, :, 0:LANE] = (q[:, h * LANE:(h + 1) * LANE] * scale).astype(BF16)
        q_out[0, h, :, LANE:2 * LANE] = (qr * scale).astype(BF16)
        k_out[0, h, :, 0:LANE] = kv[:, h * LANE:(h + 1) * LANE].astype(BF16)
        k_out[0, h, :, LANE:2 * LANE] = krot
        v_out[0, h] = kv[:, GROUP_WIDTH + h * LANE:GROUP_WIDTH + (h + 1) * LANE].astype(BF16)


def _prep_mla_weights(w_uq, w_ukv):
    lead = w_uq.shape[:-1]
    wq = w_uq.reshape(lead + (HEADS, MLA_NOPE + MLA_ROPE))
    nope = wq[..., :MLA_NOPE].reshape(lead + (GROUP_WIDTH,))
    half = MLA_ROPE // 2
    zeros = jnp.zeros(lead + (HEADS, 64 - half), w_uq.dtype)
    rope = jnp.concatenate([wq[..., MLA_NOPE:MLA_NOPE + half], zeros, wq[..., MLA_NOPE + half:], zeros], axis=-1)
    wq_p = jnp.concatenate([nope, rope.reshape(lead + (HEADS * LANE,))], axis=-1).astype(BF16)
    lead = w_ukv.shape[:-1]
    wkv = w_ukv.reshape(lead + (HEADS, MLA_NOPE + HEAD_DIM))
    wkv_p = jnp.concatenate([wkv[..., :MLA_NOPE].reshape(lead + (GROUP_WIDTH,)),
                             wkv[..., MLA_NOPE:].reshape(lead + (GROUP_WIDTH,))], axis=-1).astype(BF16)
    return wq_p, wkv_p


def _attn_tile(nfull, q_ref, k_ref, v_ref, z_ref, o_ref, s_ref, p_ref, m_ref):
    kvlen = (nfull + 1) * TQ_ATT
    d0 = nfull * TQ_ATT
    hq = TQ_ATT // 2
    srow = lax.broadcasted_iota(jnp.int32, (STRIP, LANE), 0)
    scol = lax.broadcasted_iota(jnp.int32, (STRIP, LANE), 1)

    def strip_blocks(g, i):
        rows = slice(i * STRIP, (i + 1) * STRIP)
        wd = -(-(i + 1) * STRIP // LANE) * LANE
        ncol = (d0 + wd) // LANE
        blks = [s_ref[g, rows, c * LANE:(c + 1) * LANE] for c in range(ncol)]
        blks[-1] = jnp.where(scol + (wd - LANE) <= srow + i * STRIP, blks[-1], -jnp.inf)
        return rows, ncol, blks

    qrows = slice(d0, d0 + TQ_ATT)
    for g in range(ATT_HEADS):
        q = q_ref[0, g, qrows, :]
        for j in range(nfull):
            s_ref[g, :, j * TQ_ATT:(j + 1) * TQ_ATT] = _dot_nt(q, k_ref[0, g, j * TQ_ATT:(j + 1) * TQ_ATT, :])
        s_ref[g, 0:hq, d0:d0 + hq] = _dot_nt(q[0:hq], k_ref[0, g, d0:d0 + hq, :])
        s_ref[g, hq:TQ_ATT, d0:kvlen] = _dot_nt(q[hq:TQ_ATT], k_ref[0, g, d0:kvlen, :])
    for g in range(ATT_HEADS):
        for i in range(TQ_ATT // STRIP):
            rows, ncol, blks = strip_blocks(g, i)
            mx = blks[0]
            for blk in blks[1:]:
                mx = jnp.maximum(mx, blk)
            m_ref[g, rows, :] = jnp.broadcast_to(jnp.max(mx, axis=-1, keepdims=True), (STRIP, LANE))
        for i in range(TQ_ATT // STRIP):
            rows, ncol, blks = strip_blocks(g, i)
            m = m_ref[g, rows, :]
            for c, blk in enumerate(blks):
                p_ref[g, rows, c * LANE:(c + 1) * LANE] = jnp.exp2(blk - m).astype(BF16)
            kl = d0 + hq if (i + 1) * STRIP <= hq else kvlen
            if ncol * LANE < kl:
                p_ref[g, rows, ncol * LANE:kl] = jnp.zeros((STRIP, kl - ncol * LANE), BF16)
        v1 = jnp.concatenate([v_ref[0, g, 0:kvlen, :], jnp.ones((kvlen, LANE), BF16)], axis=-1)
        for r0, r1, kl in ((0, hq, d0 + hq), (hq, TQ_ATT, kvlen)):
            pv = _dot(p_ref[g, r0:r1, 0:kl], v1[0:kl])
            gate = _silu(z_ref[g, d0 + r0:d0 + r1, :].astype(F32))
            o_ref[g, d0 + r0:d0 + r1, :] = (pv[:, :LANE] / pv[:, LANE:] * gate).astype(BF16)


def _mla_attn_kernel(q_ref, k_ref, v_ref, z_ref, o_ref, s_ref, p_ref, m_ref):
    for nfull in range(SEQ // TQ_ATT):
        _attn_tile(nfull, q_ref, k_ref, v_ref, z_ref, o_ref, s_ref, p_ref, m_ref)


def _mla_attn(q, k, v, hb16):
    return pl.pallas_call(
        _mla_attn_kernel,
        out_shape=jax.ShapeDtypeStruct((HEADS, ROWS, LANE), BF16),
        grid=(BATCH, HEADS // ATT_HEADS),
        in_specs=[pl.BlockSpec((1, ATT_HEADS, SEQ, 2 * LANE), lambda b, h: (b, h, 0, 0)),
                  pl.BlockSpec((1, ATT_HEADS, SEQ, 2 * LANE), lambda b, h: (b, h, 0, 0)),
                  pl.BlockSpec((1, ATT_HEADS, SEQ, LANE), lambda b, h: (b, h, 0, 0)),
                  pl.BlockSpec((ATT_HEADS, SEQ, LANE), lambda b, h: (BLK_ZB // ATT_HEADS + h, b, 0))],
        out_specs=pl.BlockSpec((ATT_HEADS, SEQ, LANE), lambda b, h: (h, b, 0)),
        scratch_shapes=[pltpu.VMEM((ATT_HEADS, TQ_ATT, SEQ), F32), pltpu.VMEM((ATT_HEADS, TQ_ATT, SEQ), BF16),
                        pltpu.VMEM((ATT_HEADS, TQ_ATT, LANE), F32)],
        compiler_params=_cparams(("parallel", "parallel")),
        name="mla_attn",
    )(q, k, v, hb16)


def _mem_attn_kernel(q_ref, k_ref, v_ref, z_ref, o_ref):
    ones = jnp.ones((MEM_LEN, LANE), BF16)
    scores = []
    for h in range(HEADS):
        q = (q_ref[h].astype(F32) * (HEAD_DIM ** -0.5 * LOG2E)).astype(BF16)
        scores.append(_dot_nt(q, k_ref[:, h * LANE:(h + 1) * LANE]))
    for h in range(HEADS):
        s = scores[h]
        p = jnp.exp2(s - jnp.max(s, axis=-1, keepdims=True)).astype(BF16)
        pv = _dot(p, jnp.concatenate([v_ref[:, h * LANE:(h + 1) * LANE], ones], axis=-1))
        o_ref[h] = (pv[:, :LANE] / pv[:, LANE:] * _silu(z_ref[h].astype(F32))).astype(BF16)


def _light_kernel(rq, rk, rv, rz, cr, sr, gq, gk, gc, rg, mq, mk, mv, mz, lat, kpe, cm, sm, qg, wuq, kvg, wukv,
                  ya, yd, q_out, k_out, v_out, st_ref):
    _ret_kernel(rq, rk, rv, rz, cr, sr, gq, gk, gc, rg, ya, st_ref)
    _mem_attn_kernel(mq, mk, mv, mz, yd)
    _mla_prep_kernel(lat, kpe, cm, sm, qg, wuq, kvg, wukv, q_out, k_out, v_out)


def _light_mixers(hb16, hb32, tables, ret_consts, ret_g, kvm, qg, wuq_p, kvg, wukv_p, layer):
    cos_r, sin_r, cos_m, sin_m = tables
    gq, gk, gc = ret_consts
    nchunk = SEQ // RCHUNK
    tab = pl.BlockSpec((1, RCHUNK, LANE), lambda b, c: (b, c, 0))
    row_blk = lambda blk: pl.BlockSpec((1, RCHUNK, LANE), lambda b, c: (blk, b * nchunk + c, 0))
    y_shape = jax.ShapeDtypeStruct((HEADS, ROWS, LANE), BF16)
    y_spec = pl.BlockSpec((HEADS, RCHUNK, LANE), lambda b, c: (0, b * nchunk + c, 0))
    qk_shape = jax.ShapeDtypeStruct((BATCH, HEADS, SEQ, 2 * LANE), BF16)
    v_shape = jax.ShapeDtypeStruct((BATCH, HEADS, SEQ, LANE), BF16)
    qk_spec = pl.BlockSpec((1, HEADS, RCHUNK, 2 * LANE), lambda b, c: (b, 0, c, 0))
    v_spec = pl.BlockSpec((1, HEADS, RCHUNK, LANE), lambda b, c: (b, 0, c, 0))
    return pl.pallas_call(
        _light_kernel,
        out_shape=(y_shape, y_shape, qk_shape, qk_shape, v_shape),
        grid=(BATCH, nchunk),
        in_specs=[_group_spec(BLK_RQ, RCHUNK), _group_spec(BLK_RK, RCHUNK), _group_spec(BLK_RV, RCHUNK),
                  _group_spec(BLK_ZA, RCHUNK), tab, tab,
                  _const_spec(gq.shape), _const_spec(gk.shape), _const_spec(gc.shape),
                  _layer_spec(ret_g.shape, layer),
                  _group_spec(BLK_MQ, RCHUNK),
                  pl.BlockSpec((MEM_LEN, GROUP_WIDTH), lambda b, c: (b, 2 * layer)),
                  pl.BlockSpec((MEM_LEN, GROUP_WIDTH), lambda b, c: (b, 2 * layer + 1)),
                  _group_spec(BLK_ZD, RCHUNK),
                  _group_spec(BLK_CQ, RCHUNK), row_blk(BLK_KPE), tab, tab,
                  _layer_spec(qg.shape, layer), _layer_spec(wuq_p.shape, layer),
                  _layer_spec(kvg.shape, layer), _layer_spec(wukv_p.shape, layer)],
        out_specs=(y_spec, y_spec, qk_spec, qk_spec, v_spec),
        scratch_shapes=[pltpu.VMEM((HEADS, HEAD_DIM, HEAD_DIM), F32)],
        compiler_params=_cparams(("parallel", "arbitrary")),
        name="light_mixers",
    )(hb16, hb16, hb16, hb16, cos_r, sin_r, gq, gk, gc, ret_g, hb16, kvm, kvm, hb16, hb16, hb32, cos_m, sin_m,
      qg, wuq_p, kvg, wukv_p)


def _out_kernel(ya_ref, yb_ref, yc_ref, yd_ref, w_ref, x_ref, g_ref, b_ref, o_ref):
    for t in range(TM_OUT // SUB_OUT):
        rows = slice(t * SUB_OUT, (t + 1) * SUB_OUT)
        parts = [ref[h, rows, :] for ref in (ya_ref, yb_ref, yc_ref, yd_ref) for h in range(HEADS)]
        y = jnp.concatenate(parts, axis=-1)
        r = DEEPNORM_ALPHA * x_ref[rows, :] + _dot(y, w_ref[...])
        mu = jnp.mean(r, axis=-1, keepdims=True)
        var = jnp.mean(jnp.square(r - mu), axis=-1, keepdims=True)
        o_ref[rows, :] = (r - mu) * lax.rsqrt(var + LN_EPS) * g_ref[...] + b_ref[...]


def _outproj(ya, yb, yc, yd, w_out, x2d, ln_g, ln_b, layer):
    yspec = pl.BlockSpec((HEADS, TM_OUT, LANE), lambda i: (0, i, 0))
    return pl.pallas_call(
        _out_kernel,
        out_shape=jax.ShapeDtypeStruct((ROWS, D_MODEL), F32),
        grid=(ROWS // TM_OUT,),
        in_specs=[yspec, yspec, yspec, yspec,
                  pl.BlockSpec((None,) + tuple(w_out.shape[1:]), lambda i: (layer, 0, 0),
                               pipeline_mode=pl.Buffered(1)),
                  pl.BlockSpec((TM_OUT, D_MODEL), lambda i: (i, 0)),
                  _layer_spec(ln_g.shape, layer), _layer_spec(ln_b.shape, layer)],
        out_specs=pl.BlockSpec((TM_OUT, D_MODEL), lambda i: (i, 0)),
        compiler_params=_cparams(("parallel",)),
        name="outproj_ln",
    )(ya, yb, yc, yd, w_out, x2d, ln_g, ln_b)


def kernel(x, mem, positions, w_in, ret_norm_g, mla_q_norm_g, mla_w_uq, mla_kv_norm_g, mla_w_ukv, ml_conv_w, ml_conv_b, ml_w_q, ml_w_k, ml_i_bias, ml_f_bias, ml_skip, ml_norm_g, w_mem_kv, w_out, ln_g, ln_b):
    assert x.shape == (BATCH, SEQ, D_MODEL) and mem.shape == (BATCH, MEM_LEN, D_MODEL)
    tables = _rope_tables(positions)
    kvm = _mem_kv(mem, w_mem_kv)
    w_in_p = _prep_w_in(w_in)
    wuq_p, wukv_p = _prep_mla_weights(mla_w_uq, mla_w_ukv)
    w_out_b = w_out.astype(BF16)
    ret_consts = _ret_consts()
    ret_g = ret_norm_g.reshape(DEPTH, HEADS, 1, HEAD_DIM)
    mlstm_params = _prep_mlstm(ml_conv_w, ml_conv_b, ml_w_q, ml_w_k, ml_i_bias, ml_f_bias, ml_skip, ml_norm_g)
    qg = mla_q_norm_g.reshape(DEPTH, 1, MLA_Q_RANK)
    kvg = mla_kv_norm_g.reshape(DEPTH, 1, MLA_KV_RANK)
    lng = ln_g.reshape(DEPTH, 1, D_MODEL)
    lnb = ln_b.reshape(DEPTH, 1, D_MODEL)

    x2d = x.reshape(ROWS, D_MODEL)
    for l in range(DEPTH):
        hb16, hb32 = _inproj(x2d, w_in_p, l)
        ya, yd, q, k, v = _light_mixers(hb16, hb32, tables, ret_consts, ret_g, kvm, qg, wuq_p, kvg, wukv_p, l)
        yb = _mla_attn(q, k, v, hb16)
        yc = _mlstm(hb16, hb32, mlstm_params, l)
        x2d = _outproj(ya, yb, yc, yd, w_out_b, x2d, lng, lnb, l)
    return x2d.reshape(BATCH, SEQ, D_MODEL)
```

```python
import numpy as np
import jax
import jax.numpy as jnp
from jax import lax
from jax.experimental import pallas as pl
from jax.experimental.pallas import tpu as pltpu

F32 = jnp.float32
BF16 = jnp.bfloat16

D_MODEL = 2048
BATCH = 8
SEQ = 2048
DEPTH = 4
MEM_LEN = 256
HEAD_DIM = 128
HEADS = 4
GROUP_WIDTH = HEADS * HEAD_DIM
MLA_NOPE = 128
MLA_ROPE = 64
MLA_Q_RANK = 384
MLA_KV_RANK = 128
CONV_WIDTH = 4
MIX_WIDTH = 4 * GROUP_WIDTH
ROPE_THETA = 10000.0
LN_EPS = 1e-5
RMS_EPS = 1e-6
DEEPNORM_ALPHA = (2 * DEPTH) ** 0.25
IN_SPLITS = (512, 512, 512, MLA_Q_RANK, MLA_KV_RANK, MLA_ROPE, 512, 512, 512, HEADS, HEADS, 512, MIX_WIDTH)

LANE = 128
ROWS = BATCH * SEQ

BLK_RQ, BLK_RK, BLK_RV = 0, 4, 8
BLK_LV, BLK_LO = 12, 16
BLK_MQ = 20
BLK_ZA, BLK_ZB, BLK_ZD = 24, 28, 32
BLK_CQ, BLK_CKV = 36, 39
NBLK16 = 40
BLK_LX, BLK_ZC = 0, 4
BLK_KPE = 8
NBLK32 = 10
NBLK = NBLK16 + NBLK32
GATE_I_LANE, GATE_F_LANE = 32, 36

TM_IN, TN_IN = 1024, 1280
TM_OUT = 512
SUB_OUT = 256
TQ_ATT = 512
ATT_HEADS = 2
RCHUNK = 512
CONV_PAD = 8
STRIP = 64
LOG2E = 1.4426950408889634
VMEM_LIMIT = 56 * 1024 * 1024


def _cparams(sem):
    return pltpu.CompilerParams(dimension_semantics=sem, vmem_limit_bytes=VMEM_LIMIT)


def _silu(z):
    return z * jax.nn.sigmoid(z)


def _log_sigmoid(x):
    return jnp.minimum(x, 0.0) - jnp.log1p(jnp.exp(-jnp.abs(x)))


def _split3(x):
    x1 = x.astype(BF16)
    r1 = x - x1.astype(F32)
    x2 = r1.astype(BF16)
    x3 = (r1 - x2.astype(F32)).astype(BF16)
    return x1, x2, x3


def _dot(a, b):
    return jnp.dot(a, b, preferred_element_type=F32)


def _rowsum_mxu(x, w):
    hi = x.astype(BF16)
    lo = (x - hi.astype(F32)).astype(BF16)
    return _dot(hi, w) + _dot(lo, w)


def _dot_nt(a, b):
    return lax.dot_general(a, b, (((1,), (1,)), ((), ())), preferred_element_type=F32)


def _tables_kernel(pos_ref, c_ref, cr_ref, sr_ref, cm_ref, sm_ref):
    pos = pos_ref[0].astype(F32)
    ang = pos * c_ref[0:1, :]
    cs = jnp.cos(ang)
    sn = jnp.sin(ang)
    cs_sw = pltpu.roll(cs, LANE // 2, 1)
    sn_sw = pltpu.roll(sn, LANE // 2, 1)
    lane = lax.broadcasted_iota(jnp.int32, cs.shape, 1)
    half_m = MLA_ROPE // 2
    first = lane < LANE // 2
    m_lo = lane < half_m
    m_hi = (lane >= LANE // 2) & (lane < LANE // 2 + half_m)
    cr_ref[0] = jnp.where(first, cs, cs_sw)
    sr_ref[0] = jnp.where(first, -sn, sn_sw)
    cm_ref[0] = jnp.where(m_lo, cs_sw, jnp.where(m_hi, cs, 0.0))
    sm_ref[0] = jnp.where(m_lo, -sn_sw, jnp.where(m_hi, sn, 0.0))


def _rope_tables(positions):
    half_r = HEAD_DIM // 2
    fr = ROPE_THETA ** (-jnp.arange(half_r, dtype=F32) / half_r)
    half_m = MLA_ROPE // 2
    fm = ROPE_THETA ** (-jnp.arange(half_m, dtype=F32) / half_m)
    freqs = jnp.concatenate([fr, fm, jnp.zeros((LANE - half_r - half_m,), F32)])
    consts = jnp.concatenate([freqs[None, :], jnp.zeros((7, LANE), F32)], axis=0)
    ts = 512
    tab = jax.ShapeDtypeStruct((BATCH, SEQ, LANE), F32)
    spec = pl.BlockSpec((1, ts, LANE), lambda b, i: (b, i, 0))
    return pl.pallas_call(
        _tables_kernel,
        out_shape=(tab, tab, tab, tab),
        grid=(BATCH, SEQ // ts),
        in_specs=[pl.BlockSpec((1, ts, 1), lambda b, i: (b, i, 0)),
                  pl.BlockSpec((8, LANE), lambda b, i: (0, 0))],
        out_specs=(spec, spec, spec, spec),
        compiler_params=_cparams(("parallel", "parallel")),
        name="rope_tables",
    )(positions.reshape(BATCH, SEQ, 1), consts)


def _matmul_kernel(a_ref, w_ref, o_ref):
    o_ref[...] = _dot(a_ref[...].astype(BF16), w_ref[...]).astype(o_ref.dtype)


def _mem_kv(mem, w_mem_kv):
    w = jnp.transpose(w_mem_kv, (1, 0, 2)).reshape(D_MODEL, DEPTH * 2 * GROUP_WIDTH).astype(BF16)
    a = mem.reshape(BATCH * MEM_LEN, D_MODEL)
    tm, tn = 512, 1024
    return pl.pallas_call(
        _matmul_kernel,
        out_shape=jax.ShapeDtypeStruct((BATCH * MEM_LEN, DEPTH * 2 * GROUP_WIDTH), BF16),
        grid=(BATCH * MEM_LEN // tm, DEPTH * 2 * GROUP_WIDTH // tn),
        in_specs=[pl.BlockSpec((tm, D_MODEL), lambda i, j: (i, 0)),
                  pl.BlockSpec((D_MODEL, tn), lambda i, j: (0, j))],
        out_specs=pl.BlockSpec((tm, tn), lambda i, j: (i, j)),
        compiler_params=_cparams(("parallel", "parallel")),
        name="mem_kv",
    )(a, w)


def _inproj_kernel(x_ref, w_ref, o16_ref, o32_ref, xb_ref):
    j = pl.program_id(1)

    @pl.when(j == 0)
    def _():
        xb_ref[...] = x_ref[...].astype(BF16)

    def tile(o_ref):
        xb = xb_ref[...]
        for k2 in range(TN_IN // 256):
            r = _dot_nt(xb, w_ref[k2 * 256:(k2 + 1) * 256, :])
            o_ref[2 * k2] = r[:, :LANE].astype(o_ref.dtype)
            o_ref[2 * k2 + 1] = r[:, LANE:].astype(o_ref.dtype)

    @pl.when(j < NBLK16 * LANE // TN_IN)
    def _():
        tile(o16_ref)

    @pl.when(j == NBLK16 * LANE // TN_IN)
    def _():
        tile(o32_ref)


def _inproj(x2d, wt_p, layer):
    nb = TN_IN // LANE
    assert NBLK32 == nb and NBLK16 % nb == 0
    last16 = NBLK16 // nb - 1
    return pl.pallas_call(
        _inproj_kernel,
        out_shape=(jax.ShapeDtypeStruct((NBLK16, ROWS, LANE), BF16),
                   jax.ShapeDtypeStruct((NBLK32, ROWS, LANE), F32)),
        grid=(ROWS // TM_IN, NBLK * LANE // TN_IN),
        in_specs=[pl.BlockSpec((TM_IN, D_MODEL), lambda i, j: (i, 0)),
                  pl.BlockSpec((None, TN_IN, D_MODEL), lambda i, j: (layer, j, 0))],
        out_specs=(pl.BlockSpec((nb, TM_IN, LANE), lambda i, j: (jnp.minimum(j, last16), i, 0)),
                   pl.BlockSpec((nb, TM_IN, LANE), lambda i, j: (0, i, 0))),
        scratch_shapes=[pltpu.VMEM((TM_IN, D_MODEL), BF16)],
        compiler_params=_cparams(("parallel", "arbitrary")),
        name="inproj",
    )(x2d, wt_p)


def _w_in_groups():
    off = np.concatenate([[0], np.cumsum(IN_SPLITS)]).tolist()
    (o_rq, o_rk, o_rv, o_cq, _, o_kpe, o_lx, o_lv, o_lo, o_li, _, o_mq, o_z, _) = off
    groups = [o_rq, o_rk, o_rv, o_lv, o_lo, o_mq, o_z, o_z + GROUP_WIDTH, o_z + 3 * GROUP_WIDTH,
              o_cq, o_lx, o_z + 2 * GROUP_WIDTH]
    return groups, o_kpe, o_li


def _w_in_kernel(tbl_ref, w_ref, kpe_ref, gate_ref, o_ref):
    del tbl_ref
    ngroup = NBLK // HEADS

    @pl.when(pl.program_id(1) < ngroup)
    def _():
        o_ref[0] = w_ref[0].astype(BF16)

    @pl.when(pl.program_id(1) == ngroup)
    def _():
        half = MLA_ROPE // 2
        zeros = lambda n: jnp.zeros((n, D_MODEL), F32)
        kpe = kpe_ref[0]
        blk = jnp.concatenate([kpe[0:half], gate_ref[0], zeros(64 - half - 2 * HEADS),
                               kpe[half:2 * half], zeros(64 - half)], axis=0)
        o_ref[0, 0:LANE, :] = blk.astype(BF16)
        o_ref[0, LANE:, :] = jnp.zeros((GROUP_WIDTH - LANE, D_MODEL), BF16)


def _prep_w_in(w_in):
    wt = jnp.swapaxes(w_in, 1, 2)
    groups, o_kpe, o_li = _w_in_groups()
    assert len(groups) * HEADS + 2 == NBLK and o_kpe % LANE == 0 and o_li % (2 * HEADS) == 0
    sub = 8
    assert all(g % sub == 0 for g in groups)
    table = jnp.asarray([g // sub for g in groups] + [0], jnp.int32)
    grid_spec = pltpu.PrefetchScalarGridSpec(
        num_scalar_prefetch=1,
        grid=(DEPTH, len(groups) + 1),
        in_specs=[pl.BlockSpec((pl.Element(1), pl.Element(GROUP_WIDTH), pl.Element(D_MODEL)),
                               lambda l, j, tbl: (l, pl.multiple_of(tbl[j] * sub, sub), 0)),
                  pl.BlockSpec((1, LANE, D_MODEL), lambda l, j, tbl: (l, o_kpe // LANE, 0)),
                  pl.BlockSpec((1, 2 * HEADS, D_MODEL), lambda l, j, tbl: (l, o_li // (2 * HEADS), 0))],
        out_specs=pl.BlockSpec((1, GROUP_WIDTH, D_MODEL), lambda l, j, tbl: (l, j, 0)),
    )
    return pl.pallas_call(
        _w_in_kernel,
        out_shape=jax.ShapeDtypeStruct((DEPTH, NBLK * LANE, D_MODEL), BF16),
        grid_spec=grid_spec,
        compiler_params=_cparams(("parallel", "arbitrary")),
        name="w_in_relayout",
    )(table, wt, wt, wt)


def _group_spec(base, rows):
    nchunk = SEQ // rows
    return pl.BlockSpec((HEADS, rows, LANE), lambda b, c: (base // HEADS, b * nchunk + c, 0))


def _layer_spec(shape, layer):
    nd = len(shape) - 1
    return pl.BlockSpec((None,) + tuple(shape[1:]), lambda *_: (layer,) + (0,) * nd)


def _const_spec(shape):
    nd = len(shape)
    return pl.BlockSpec(tuple(shape), lambda *_: (0,) * nd)


def _ret_kernel(q_ref, k_ref, v_ref, z_ref, cos_ref, sin_ref, gq_ref, gk_ref, gc_ref, g_ref, o_ref, st_ref):
    @pl.when(pl.program_id(1) == 0)
    def _():
        st_ref[...] = jnp.zeros_like(st_ref)

    cs = cos_ref[0]
    sn = sin_ref[0]
    row = lax.broadcasted_iota(jnp.int32, (RCHUNK, RCHUNK), 0)
    col = lax.broadcasted_iota(jnp.int32, (RCHUNK, RCHUNK), 1)
    causal = col <= row
    qbs, kts, vbs, scs = [], [], [], []
    for h in range(HEADS):
        q = q_ref[h].astype(F32)
        q = (q * cs + pltpu.roll(q, HEAD_DIM // 2, 1) * sn) * gq_ref[h]
        k = k_ref[h].astype(F32)
        k = (k * cs + pltpu.roll(k, HEAD_DIM // 2, 1) * sn) * gk_ref[h]
        qb = q.astype(BF16)
        scs.append(_dot_nt(qb, k.astype(BF16)))
        qbs.append(qb)
        kts.append(k.T.astype(BF16))
        vbs.append(v_ref[h])
    outs = []
    for h in range(HEADS):
        st = st_ref[h]
        sc = jnp.where(causal, scs[h], 0.0).astype(BF16)
        outs.append(_dot(sc, vbs[h]) + _dot(qbs[h], st.astype(BF16)))
        st_ref[h] = gc_ref[h] * (st + _dot(kts[h], vbs[h]))
    for h in range(HEADS):
        out = outs[h]
        mu = jnp.mean(out, axis=-1, keepdims=True)
        dev = out - mu
        var = jnp.mean(dev * dev, axis=-1, keepdims=True)
        hn = dev * lax.rsqrt(var + LN_EPS) * g_ref[h]
        o_ref[h] = (hn * _silu(z_ref[h].astype(F32))).astype(BF16)


def _ret_consts():
    log_g = jnp.log1p(-jnp.exp2(-5.0 - jnp.arange(HEADS, dtype=F32)))
    idx = jnp.arange(RCHUNK, dtype=F32)
    full = (HEADS, RCHUNK, HEAD_DIM)
    gq = jnp.broadcast_to(jnp.exp(log_g[:, None] * (idx + 1.0))[..., None], full)
    gk = jnp.broadcast_to((jnp.exp(-log_g[:, None] * (idx + 1.0)) * HEAD_DIM ** -0.5)[..., None], full)
    gc = jnp.broadcast_to(jnp.exp(log_g * RCHUNK)[:, None, None], (HEADS, 1, HEAD_DIM))
    return gq, gk, gc


def _mlstm_kernel(lx_ref, lv_ref, lo_ref, gt_ref, z_ref, cw_ref, cb_ref, wq_ref, wk_ref, gb_ref,
                  skip_ref, g_ref, o_ref, xp_ref, st_ref, m_ref, s_ref, p_ref):
    @pl.when(pl.program_id(1) == 0)
    def _():
        xp_ref[:, 0:CONV_PAD, :] = jnp.zeros((HEADS, CONV_PAD, HEAD_DIM), F32)
        st_ref[...] = jnp.zeros_like(st_ref)
        m_ref[...] = jnp.zeros_like(m_ref)

    n = RCHUNK
    ones_nd = jnp.ones((n, HEAD_DIM), BF16)
    mean_dd = jnp.full((HEAD_DIM, HEAD_DIM), 1.0 / HEAD_DIM, BF16)
    row = lax.broadcasted_iota(jnp.int32, (n, n), 0)
    col = lax.broadcasted_iota(jnp.int32, (n, n), 1)
    tril = jnp.where(col <= row, 1.0, 0.0).astype(BF16)
    r128 = lax.broadcasted_iota(jnp.int32, (LANE, LANE), 0)
    c128 = lax.broadcasted_iota(jnp.int32, (LANE, LANE), 1)
    ident = jnp.where(r128 == c128, 1.0, 0.0).astype(BF16)
    srow = lax.broadcasted_iota(jnp.int32, (STRIP, LANE), 0)
    scol = lax.broadcasted_iota(jnp.int32, (STRIP, LANE), 1)

    lane = lax.broadcasted_iota(jnp.int32, (n, LANE), 1)
    x = gt_ref[0] + gb_ref[...]
    x = jnp.where((lane >= GATE_F_LANE) & (lane < GATE_F_LANE + HEADS), _log_sigmoid(x), x) * LOG2E
    x1, x2, x3 = _split3(x)
    cum = _dot(tril, x1) + _dot(tril, x2) + _dot(tril, x3)
    rt = x - pltpu.roll(cum, LANE - (GATE_F_LANE - GATE_I_LANE), 1)
    r1, r2, r3 = _split3(rt)
    rtt = _dot_nt(ident, r1) + _dot_nt(ident, r2) + _dot_nt(ident, r3)

    xcs, ks, qbs, vbs = [], [], [], []
    for h in range(HEADS):
        xp_ref[h, CONV_PAD:, :] = lx_ref[h]
        acc = jnp.zeros((n, HEAD_DIM), F32) + cb_ref[h]
        for j in range(CONV_WIDTH):
            off = CONV_PAD - (CONV_WIDTH - 1) + j
            acc = acc + xp_ref[h, off:off + n, :] * cw_ref[h, j:j + 1, :]
        xp_ref[h, 0:CONV_PAD, :] = lx_ref[h, n - CONV_PAD:n, :]
        xc = _silu(acc)
        xcb = xc.astype(BF16)
        k = _dot(xcb, wk_ref[h]) * HEAD_DIM ** -0.5
        qb = _dot(xcb, wq_ref[h]).astype(BF16)
        s_ref[h] = _dot_nt(qb, k.astype(BF16))
        xcs.append(xc)
        ks.append(k)
        qbs.append(qb)
        vbs.append(jnp.concatenate([lv_ref[h], ones_nd], axis=-1))

    r_rows = [rtt[GATE_I_LANE + h:GATE_I_LANE + h + 1, :] for h in range(HEADS)]
    bases = [m_ref[h] for h in range(HEADS)]
    strips = [[] for _ in range(HEADS)]
    u_lasts = [None] * HEADS
    for i in range(n // STRIP):
        rows = slice(i * STRIP, (i + 1) * STRIP)
        d0 = (i * STRIP // LANE) * LANE
        w = d0 + LANE
        mask = (scol + d0) <= (srow + i * STRIP)
        for h in range(HEADS):
            r_row = r_rows[h]
            if d0 > 0 and (i * STRIP) % LANE == 0:
                bases[h] = jnp.maximum(bases[h], jnp.max(r_row[:, d0 - LANE:d0], axis=-1, keepdims=True))
            rmd = jnp.where(mask, r_row[:, d0:w], -jnp.inf)
            u_col = jnp.maximum(jnp.max(rmd, axis=-1, keepdims=True), bases[h])
            u = jnp.broadcast_to(u_col, (STRIP, LANE))
            p_ref[h, rows, d0:w] = (s_ref[h, rows, d0:w] * jnp.exp2(rmd - u)).astype(BF16)
            for c in range(d0 // LANE):
                cols = slice(c * LANE, (c + 1) * LANE)
                p_ref[h, rows, cols] = (s_ref[h, rows, cols] * jnp.exp2(r_row[:, cols] - u)).astype(BF16)
            if w < n:
                p_ref[h, rows, w:n] = jnp.zeros((STRIP, n - w), BF16)
            strips[h].append(u)
            u_lasts[h] = u_col[STRIP - 1:STRIP, :]
    us = [jnp.concatenate(st_h, axis=0) for st_h in strips]

    cells = []
    for h in range(HEADS):
        li = GATE_I_LANE + h
        lf = GATE_F_LANE + h
        m_st = m_ref[h]
        u, u_last, vb = us[h], u_lasts[h], vbs[h]
        w_inter = jnp.exp2(m_st - u)
        st = st_ref[h]
        intra = _dot(p_ref[h], vb)
        inter = _dot(qbs[h], st.astype(BF16))
        num = intra[:, :LANE] + w_inter * inter[:, :LANE]
        den = intra[:, LANE:] + w_inter * inter[:, LANE:]
        cum_f = jnp.broadcast_to(cum[:, lf:lf + 1], (n, LANE))
        cells.append(num / jnp.maximum(jnp.abs(den), jnp.exp2(-(cum_f + u))))

        decay = jnp.exp2(m_st - u_last)
        kw = ks[h] * jnp.exp2(jnp.broadcast_to(rt[:, li:li + 1], (n, LANE)) - u_last)
        upd = _dot(kw.T.astype(BF16), vb)
        st_ref[h] = decay * st + upd
        m_ref[h] = cum[n - 1:n, lf:lf + 1] + u_last

    for h in range(HEADS):
        cell = cells[h] * jax.nn.sigmoid(lo_ref[h].astype(F32))
        mu = _rowsum_mxu(cell, mean_dd)
        dev = cell - mu
        var = _rowsum_mxu(dev * dev, mean_dd)
        hn = dev * lax.rsqrt(var + LN_EPS) * g_ref[h]
        o_ref[h] = ((hn + skip_ref[h] * xcs[h]) * _silu(z_ref[h])).astype(BF16)


def _prep_mlstm(conv_w, conv_b, w_q, w_k, i_bias, f_bias, skip, norm_g):
    nl = conv_w.shape[0]
    cw = conv_w.reshape(nl, CONV_WIDTH, HEADS, HEAD_DIM).transpose(0, 2, 1, 3)
    cb = conv_b.reshape(nl, HEADS, 1, HEAD_DIM)
    zeros = lambda w: jnp.zeros((nl, w), F32)
    gb = jnp.concatenate([zeros(GATE_I_LANE), i_bias, f_bias, zeros(LANE - GATE_F_LANE - HEADS)], axis=-1)
    return (cw, cb, w_q.astype(BF16), w_k.astype(BF16), gb.reshape(nl, 1, LANE),
            skip.reshape(nl, HEADS, 1, HEAD_DIM), norm_g.reshape(nl, HEADS, 1, HEAD_DIM))


def _mlstm(hb16, hb32, params, layer):
    nchunk = SEQ // RCHUNK
    return pl.pallas_call(
        _mlstm_kernel,
        out_shape=jax.ShapeDtypeStruct((HEADS, ROWS, LANE), BF16),
        grid=(BATCH, nchunk),
        in_specs=[_group_spec(BLK_LX, RCHUNK), _group_spec(BLK_LV, RCHUNK), _group_spec(BLK_LO, RCHUNK),
                  pl.BlockSpec((1, RCHUNK, LANE), lambda b, c: (BLK_KPE, b * nchunk + c, 0)),
                  _group_spec(BLK_ZC, RCHUNK)] + [_layer_spec(p.shape, layer) for p in params],
        out_specs=pl.BlockSpec((HEADS, RCHUNK, LANE), lambda b, c: (0, b * nchunk + c, 0)),
        scratch_shapes=[pltpu.VMEM((HEADS, RCHUNK + CONV_PAD, HEAD_DIM), F32),
                        pltpu.VMEM((HEADS, HEAD_DIM, 2 * HEAD_DIM), F32),
                        pltpu.VMEM((HEADS, 1, 1), F32),
                        pltpu.VMEM((HEADS, RCHUNK, RCHUNK), F32),
                        pltpu.VMEM((HEADS, RCHUNK, RCHUNK), BF16)],
        compiler_params=_cparams(("parallel", "arbitrary")),
        name="mlstm",
    )(hb32, hb16, hb16, hb32, hb32, *params)


def _mla_prep_kernel(lat_ref, kpe_ref, cm_ref, sm_ref, qg_ref, wuq_ref, kvg_ref, wukv_ref,
                     q_out, k_out, v_out):
    scale = (MLA_NOPE + MLA_ROPE) ** -0.5 * LOG2E
    cs = cm_ref[0]
    sn = sm_ref[0]
    cq = jnp.concatenate([lat_ref[0], lat_ref[1], lat_ref[2]], axis=-1).astype(F32)
    qn = cq * lax.rsqrt(jnp.mean(jnp.square(cq), axis=-1, keepdims=True) + RMS_EPS) * qg_ref[...]
    q = _dot(qn.astype(BF16), wuq_ref[...])
    ckv = lat_ref[3].astype(F32)
    kvn = ckv * lax.rsqrt(jnp.mean(jnp.square(ckv), axis=-1, keepdims=True) + RMS_EPS) * kvg_ref[...]
    kv = _dot(kvn.astype(BF16), wukv_ref[...])
    kpe = kpe_ref[0]
    krot = (kpe * cs + pltpu.roll(kpe, LANE // 2, 1) * sn).astype(BF16)
    for h in range(HEADS):
        qr = q[:, GROUP_WIDTH + h * LANE:GROUP_WIDTH + (h + 1) * LANE]
        qr = qr * cs + pltpu.roll(qr, LANE // 2, 1) * sn
        q_out[0, h, :, 0:LANE] = (q[:, h * LANE:(h + 1) * LANE] * scale).astype(BF16)
        q_out[0, h, :, LANE:2 * LANE] = (qr * scale).astype(BF16)
        k_out[0, h, :, 0:LANE] = kv[:, h * LANE:(h + 1) * LANE].astype(BF16)
        k_out[0, h, :, LANE:2 * LANE] = krot
        v_out[0, h] = kv[:, GROUP_WIDTH + h * LANE:GROUP_WIDTH + (h + 1) * LANE].astype(BF16)


def _prep_mla_weights(w_uq, w_ukv):
    lead = w_uq.shape[:-1]
    wq = w_uq.reshape(lead + (HEADS, MLA_NOPE + MLA_ROPE))
    nope = wq[..., :MLA_NOPE].reshape(lead + (GROUP_WIDTH,))
    half = MLA_ROPE // 2
    zeros = jnp.zeros(lead + (HEADS, 64 - half), w_uq.dtype)
    rope = jnp.concatenate([wq[..., MLA_NOPE:MLA_NOPE + half], zeros, wq[..., MLA_NOPE + half:], zeros], axis=-1)
    wq_p = jnp.concatenate([nope, rope.reshape(lead + (HEADS * LANE,))], axis=-1).astype(BF16)
    lead = w_ukv.shape[:-1]
    wkv = w_ukv.reshape(lead + (HEADS, MLA_NOPE + HEAD_DIM))
    wkv_p = jnp.concatenate([wkv[..., :MLA_NOPE].reshape(lead + (GROUP_WIDTH,)),
                             wkv[..., MLA_NOPE:].reshape(lead + (GROUP_WIDTH,))], axis=-1).astype(BF16)
    return wq_p, wkv_p


def _attn_tile(nfull, q_ref, k_ref, v_ref, z_ref, o_ref, s_ref, p_ref, m_ref):
    kvlen = (nfull + 1) * TQ_ATT
    d0 = nfull * TQ_ATT
    hq = TQ_ATT // 2
    srow = lax.broadcasted_iota(jnp.int32, (STRIP, LANE), 0)
    scol = lax.broadcasted_iota(jnp.int32, (STRIP, LANE), 1)

    def strip_blocks(g, i):
        rows = slice(i * STRIP, (i + 1) * STRIP)
        wd = -(-(i + 1) * STRIP // LANE) * LANE
        ncol = (d0 + wd) // LANE
        blks = [s_ref[g, rows, c * LANE:(c + 1) * LANE] for c in range(ncol)]
        blks[-1] = jnp.where(scol + (wd - LANE) <= srow + i * STRIP, blks[-1], -jnp.inf)
        return rows, ncol, blks

    qrows = slice(d0, d0 + TQ_ATT)
    for g in range(ATT_HEADS):
        q = q_ref[0, g, qrows, :]
        for j in range(nfull):
            s_ref[g, :, j * TQ_ATT:(j + 1) * TQ_ATT] = _dot_nt(q, k_ref[0, g, j * TQ_ATT:(j + 1) * TQ_ATT, :])
        s_ref[g, 0:hq, d0:d0 + hq] = _dot_nt(q[0:hq], k_ref[0, g, d0:d0 + hq, :])
        s_ref[g, hq:TQ_ATT, d0:kvlen] = _dot_nt(q[hq:TQ_ATT], k_ref[0, g, d0:kvlen, :])
    for g in range(ATT_HEADS):
        for i in range(TQ_ATT // STRIP):
            rows, ncol, blks = strip_blocks(g, i)
            mx = blks[0]
            for blk in blks[1:]:
                mx = jnp.maximum(mx, blk)
            m_ref[g, rows, :] = jnp.broadcast_to(jnp.max(mx, axis=-1, keepdims=True), (STRIP, LANE))
        for i in range(TQ_ATT // STRIP):
            rows, ncol, blks = strip_blocks(g, i)
            m = m_ref[g, rows, :]
            for c, blk in enumerate(blks):
                p_ref[g, rows, c * LANE:(c + 1) * LANE] = jnp.exp2(blk - m).astype(BF16)
            kl = d0 + hq if (i + 1) * STRIP <= hq else kvlen
            if ncol * LANE < kl:
                p_ref[g, rows, ncol * LANE:kl] = jnp.zeros((STRIP, kl - ncol * LANE), BF16)
        v1 = jnp.concatenate([v_ref[0, g, 0:kvlen, :], jnp.ones((kvlen, LANE), BF16)], axis=-1)
        for r0, r1, kl in ((0, hq, d0 + hq), (hq, TQ_ATT, kvlen)):
            pv = _dot(p_ref[g, r0:r1, 0:kl], v1[0:kl])
            gate = _silu(z_ref[g, d0 + r0:d0 + r1, :].astype(F32))
            o_ref[g, d0 + r0:d0 + r1, :] = (pv[:, :LANE] / pv[:, LANE:] * gate).astype(BF16)


def _mla_attn_kernel(q_ref, k_ref, v_ref, z_ref, o_ref, s_ref, p_ref, m_ref):
    for nfull in range(SEQ // TQ_ATT):
        _attn_tile(nfull, q_ref, k_ref, v_ref, z_ref, o_ref, s_ref, p_ref, m_ref)


def _mla_attn(q, k, v, hb16):
    return pl.pallas_call(
        _mla_attn_kernel,
        out_shape=jax.ShapeDtypeStruct((HEADS, ROWS, LANE), BF16),
        grid=(BATCH, HEADS // ATT_HEADS),
        in_specs=[pl.BlockSpec((1, ATT_HEADS, SEQ, 2 * LANE), lambda b, h: (b, h, 0, 0)),
                  pl.BlockSpec((1, ATT_HEADS, SEQ, 2 * LANE), lambda b, h: (b, h, 0, 0)),
                  pl.BlockSpec((1, ATT_HEADS, SEQ, LANE), lambda b, h: (b, h, 0, 0)),
                  pl.BlockSpec((ATT_HEADS, SEQ, LANE), lambda b, h: (BLK_ZB // ATT_HEADS + h, b, 0))],
        out_specs=pl.BlockSpec((ATT_HEADS, SEQ, LANE), lambda b, h: (h, b, 0)),
        scratch_shapes=[pltpu.VMEM((ATT_HEADS, TQ_ATT, SEQ), F32), pltpu.VMEM((ATT_HEADS, TQ_ATT, SEQ), BF16),
                        pltpu.VMEM((ATT_HEADS, TQ_ATT, LANE), F32)],
        compiler_params=_cparams(("parallel", "parallel")),
        name="mla_attn",
    )(q, k, v, hb16)


def _mem_attn_kernel(q_ref, k_ref, v_ref, z_ref, o_ref):
    ones = jnp.ones((MEM_LEN, LANE), BF16)
    scores = []
    for h in range(HEADS):
        q = (q_ref[h].astype(F32) * (HEAD_DIM ** -0.5 * LOG2E)).astype(BF16)
        scores.append(_dot_nt(q, k_ref[:, h * LANE:(h + 1) * LANE]))
    for h in range(HEADS):
        s = scores[h]
        p = jnp.exp2(s - jnp.max(s, axis=-1, keepdims=True)).astype(BF16)
        pv = _dot(p, jnp.concatenate([v_ref[:, h * LANE:(h + 1) * LANE], ones], axis=-1))
        o_ref[h] = (pv[:, :LANE] / pv[:, LANE:] * _silu(z_ref[h].astype(F32))).astype(BF16)


def _light_kernel(rq, rk, rv, rz, cr, sr, gq, gk, gc, rg, mq, mk, mv, mz, lat, kpe, cm, sm, qg, wuq, kvg, wukv,
                  ya, yd, q_out, k_out, v_out, st_ref):
    _ret_kernel(rq, rk, rv, rz, cr, sr, gq, gk, gc, rg, ya, st_ref)
    _mem_attn_kernel(mq, mk, mv, mz, yd)
    _mla_prep_kernel(lat, kpe, cm, sm, qg, wuq, kvg, wukv, q_out, k_out, v_out)


def _light_mixers(hb16, hb32, tables, ret_consts, ret_g, kvm, qg, wuq_p, kvg, wukv_p, layer):
    cos_r, sin_r, cos_m, sin_m = tables
    gq, gk, gc = ret_consts
    nchunk = SEQ // RCHUNK
    tab = pl.BlockSpec((1, RCHUNK, LANE), lambda b, c: (b, c, 0))
    row_blk = lambda blk: pl.BlockSpec((1, RCHUNK, LANE), lambda b, c: (blk, b * nchunk + c, 0))
    y_shape = jax.ShapeDtypeStruct((HEADS, ROWS, LANE), BF16)
    y_spec = pl.BlockSpec((HEADS, RCHUNK, LANE), lambda b, c: (0, b * nchunk + c, 0))
    qk_shape = jax.ShapeDtypeStruct((BATCH, HEADS, SEQ, 2 * LANE), BF16)
    v_shape = jax.ShapeDtypeStruct((BATCH, HEADS, SEQ, LANE), BF16)
    qk_spec = pl.BlockSpec((1, HEADS, RCHUNK, 2 * LANE), lambda b, c: (b, 0, c, 0))
    v_spec = pl.BlockSpec((1, HEADS, RCHUNK, LANE), lambda b, c: (b, 0, c, 0))
    return pl.pallas_call(
        _light_kernel,
        out_shape=(y_shape, y_shape, qk_shape, qk_shape, v_shape),
        grid=(BATCH, nchunk),
        in_specs=[_group_spec(BLK_RQ, RCHUNK), _group_spec(BLK_RK, RCHUNK), _group_spec(BLK_RV, RCHUNK),
                  _group_spec(BLK_ZA, RCHUNK), tab, tab,
                  _const_spec(gq.shape), _const_spec(gk.shape), _const_spec(gc.shape),
                  _layer_spec(ret_g.shape, layer),
                  _group_spec(BLK_MQ, RCHUNK),
                  pl.BlockSpec((MEM_LEN, GROUP_WIDTH), lambda b, c: (b, 2 * layer)),
                  pl.BlockSpec((MEM_LEN, GROUP_WIDTH), lambda b, c: (b, 2 * layer + 1)),
                  _group_spec(BLK_ZD, RCHUNK),
                  _group_spec(BLK_CQ, RCHUNK), row_blk(BLK_KPE), tab, tab,
                  _layer_spec(qg.shape, layer), _layer_spec(wuq_p.shape, layer),
                  _layer_spec(kvg.shape, layer), _layer_spec(wukv_p.shape, layer)],
        out_specs=(y_spec, y_spec, qk_spec, qk_spec, v_spec),
        scratch_shapes=[pltpu.VMEM((HEADS, HEAD_DIM, HEAD_DIM), F32)],
        compiler_params=_cparams(("parallel", "arbitrary")),
        name="light_mixers",
    )(hb16, hb16, hb16, hb16, cos_r, sin_r, gq, gk, gc, ret_g, hb16, kvm, kvm, hb16, hb16, hb32, cos_m, sin_m,
      qg, wuq_p, kvg, wukv_p)


def _out_kernel(ya_ref, yb_ref, yc_ref, yd_ref, w_ref, x_ref, g_ref, b_ref, o_ref):
    for t in range(TM_OUT // SUB_OUT):
        rows = slice(t * SUB_OUT, (t + 1) * SUB_OUT)
        parts = [ref[h, rows, :] for ref in (ya_ref, yb_ref, yc_ref, yd_ref) for h in range(HEADS)]
        y = jnp.concatenate(parts, axis=-1)
        r = DEEPNORM_ALPHA * x_ref[rows, :] + _dot(y, w_ref[...])
        mu = jnp.mean(r, axis=-1, keepdims=True)
        var = jnp.mean(jnp.square(r - mu), axis=-1, keepdims=True)
        o_ref[rows, :] = (r - mu) * lax.rsqrt(var + LN_EPS) * g_ref[...] + b_ref[...]


def _outproj(ya, yb, yc, yd, w_out, x2d, ln_g, ln_b, layer):
    yspec = pl.BlockSpec((HEADS, TM_OUT, LANE), lambda i: (0, i, 0))
    return pl.pallas_call(
        _out_kernel,
        out_shape=jax.ShapeDtypeStruct((ROWS, D_MODEL), F32),
        grid=(ROWS // TM_OUT,),
        in_specs=[yspec, yspec, yspec, yspec,
                  pl.BlockSpec((None,) + tuple(w_out.shape[1:]), lambda i: (layer, 0, 0),
                               pipeline_mode=pl.Buffered(1)),
                  pl.BlockSpec((TM_OUT, D_MODEL), lambda i: (i, 0)),
                  _layer_spec(ln_g.shape, layer), _layer_spec(ln_b.shape, layer)],
        out_specs=pl.BlockSpec((TM_OUT, D_MODEL), lambda i: (i, 0)),
        compiler_params=_cparams(("parallel",)),
        name="outproj_ln",
    )(ya, yb, yc, yd, w_out, x2d, ln_g, ln_b)


def kernel(x, mem, positions, w_in, ret_norm_g, mla_q_norm_g, mla_w_uq, mla_kv_norm_g, mla_w_ukv, ml_conv_w, ml_conv_b, ml_w_q, ml_w_k, ml_i_bias, ml_f_bias, ml_skip, ml_norm_g, w_mem_kv, w_out, ln_g, ln_b):
    assert x.shape == (BATCH, SEQ, D_MODEL) and mem.shape == (BATCH, MEM_LEN, D_MODEL)
    tables = _rope_tables(positions)
    kvm = _mem_kv(mem, w_mem_kv)
    w_in_p = _prep_w_in(w_in)
    wuq_p, wukv_p = _prep_mla_weights(mla_w_uq, mla_w_ukv)
    w_out_b = w_out.astype(BF16)
    ret_consts = _ret_consts()
    ret_g = ret_norm_g.reshape(DEPTH, HEADS, 1, HEAD_DIM)
    mlstm_params = _prep_mlstm(ml_conv_w, ml_conv_b, ml_w_q, ml_w_k, ml_i_bias, ml_f_bias, ml_skip, ml_norm_g)
    qg = mla_q_norm_g.reshape(DEPTH, 1, MLA_Q_RANK)
    kvg = mla_kv_norm_g.reshape(DEPTH, 1, MLA_KV_RANK)
    lng = ln_g.reshape(DEPTH, 1, D_MODEL)
    lnb = ln_b.reshape(DEPTH, 1, D_MODEL)

    x2d = x.reshape(ROWS, D_MODEL)
    for l in range(DEPTH):
        hb16, hb32 = _inproj(x2d, w_in_p, l)
        ya, yd, q, k, v = _light_mixers(hb16, hb32, tables, ret_consts, ret_g, kvm, qg, wuq_p, kvg, wukv_p, l)
        yb = _mla_attn(q, k, v, hb16)
        yc = _mlstm(hb16, hb32, mlstm_params, l)
        x2d = _outproj(ya, yb, yc, yd, w_out_b, x2d, lng, lnb, l)
    return x2d.reshape(BATCH, SEQ, D_MODEL)
```

```python
import numpy as np
import jax
import jax.numpy as jnp
from jax import lax
from jax.experimental import pallas as pl
from jax.experimental.pallas import tpu as pltpu

F32 = jnp.float32
BF16 = jnp.bfloat16

D_MODEL = 2048
BATCH = 8
SEQ = 2048
DEPTH = 4
MEM_LEN = 256
HEAD_DIM = 128
HEADS = 4
GROUP_WIDTH = HEADS * HEAD_DIM
MLA_NOPE = 128
MLA_ROPE = 64
MLA_Q_RANK = 384
MLA_KV_RANK = 128
CONV_WIDTH = 4
MIX_WIDTH = 4 * GROUP_WIDTH
ROPE_THETA = 10000.0
LN_EPS = 1e-5
RMS_EPS = 1e-6
DEEPNORM_ALPHA = (2 * DEPTH) ** 0.25
IN_SPLITS = (512, 512, 512, MLA_Q_RANK, MLA_KV_RANK, MLA_ROPE, 512, 512, 512, HEADS, HEADS, 512, MIX_WIDTH)

LANE = 128
ROWS = BATCH * SEQ

BLK_RQ, BLK_RK, BLK_RV = 0, 4, 8
BLK_LV, BLK_LO = 12, 16
BLK_MQ = 20
BLK_ZA, BLK_ZB, BLK_ZD = 24, 28, 32
BLK_CQ, BLK_CKV = 36, 39
NBLK16 = 40
BLK_LX, BLK_ZC = 0, 4
BLK_KPE = 8
NBLK32 = 10
NBLK = NBLK16 + NBLK32
GATE_I_LANE, GATE_F_LANE = 32, 36

TM_IN, TN_IN = 1024, 1280
TM_OUT = 512
SUB_OUT = 256
TQ_ATT = 512
ATT_HEADS = 2
RCHUNK = 512
CONV_PAD = 8
STRIP = 64
LOG2E = 1.4426950408889634
VMEM_LIMIT = 56 * 1024 * 1024


def _cparams(sem):
    return pltpu.CompilerParams(dimension_semantics=sem, vmem_limit_bytes=VMEM_LIMIT)


def _silu(z):
    return z * jax.nn.sigmoid(z)


def _log_sigmoid(x):
    return jnp.minimum(x, 0.0) - jnp.log1p(jnp.exp(-jnp.abs(x)))


def _split3(x):
    x1 = x.astype(BF16)
    r1 = x - x1.astype(F32)
    x2 = r1.astype(BF16)
    x3 = (r1 - x2.astype(F32)).astype(BF16)
    return x1, x2, x3


def _dot(a, b):
    return jnp.dot(a, b, preferred_element_type=F32)


def _rowsum_mxu(x, w):
    hi = x.astype(BF16)
    lo = (x - hi.astype(F32)).astype(BF16)
    return _dot(hi, w) + _dot(lo, w)


def _dot_nt(a, b):
    return lax.dot_general(a, b, (((1,), (1,)), ((), ())), preferred_element_type=F32)


def _tables_kernel(pos_ref, c_ref, cr_ref, sr_ref, cm_ref, sm_ref):
    pos = pos_ref[0].astype(F32)
    ang = pos * c_ref[0:1, :]
    cs = jnp.cos(ang)
    sn = jnp.sin(ang)
    cs_sw = pltpu.roll(cs, LANE // 2, 1)
    sn_sw = pltpu.roll(sn, LANE // 2, 1)
    lane = lax.broadcasted_iota(jnp.int32, cs.shape, 1)
    half_m = MLA_ROPE // 2
    first = lane < LANE // 2
    m_lo = lane < half_m
    m_hi = (lane >= LANE // 2) & (lane < LANE // 2 + half_m)
    cr_ref[0] = jnp.where(first, cs, cs_sw)
    sr_ref[0] = jnp.where(first, -sn, sn_sw)
    cm_ref[0] = jnp.where(m_lo, cs_sw, jnp.where(m_hi, cs, 0.0))
    sm_ref[0] = jnp.where(m_lo, -sn_sw, jnp.where(m_hi, sn, 0.0))


def _rope_tables(positions):
    half_r = HEAD_DIM // 2
    fr = ROPE_THETA ** (-jnp.arange(half_r, dtype=F32) / half_r)
    half_m = MLA_ROPE // 2
    fm = ROPE_THETA ** (-jnp.arange(half_m, dtype=F32) / half_m)
    freqs = jnp.concatenate([fr, fm, jnp.zeros((LANE - half_r - half_m,), F32)])
    consts = jnp.concatenate([freqs[None, :], jnp.zeros((7, LANE), F32)], axis=0)
    ts = 512
    tab = jax.ShapeDtypeStruct((BATCH, SEQ, LANE), F32)
    spec = pl.BlockSpec((1, ts, LANE), lambda b, i: (b, i, 0))
    return pl.pallas_call(
        _tables_kernel,
        out_shape=(tab, tab, tab, tab),
        grid=(BATCH, SEQ // ts),
        in_specs=[pl.BlockSpec((1, ts, 1), lambda b, i: (b, i, 0)),
                  pl.BlockSpec((8, LANE), lambda b, i: (0, 0))],
        out_specs=(spec, spec, spec, spec),
        compiler_params=_cparams(("parallel", "parallel")),
        name="rope_tables",
    )(positions.reshape(BATCH, SEQ, 1), consts)


def _mem_kv_kernel(a_ref, w_ref, o_ref, wb_ref):
    @pl.when(pl.program_id(1) == 0)
    def _():
        wb_ref[...] = w_ref[...].astype(BF16)

    o_ref[...] = _dot(a_ref[...].astype(BF16), wb_ref[...]).astype(BF16)


def _mem_kv(mem, w_mem_kv):
    a = mem.reshape(BATCH * MEM_LEN, D_MODEL)
    tm, tn = 1024, 2 * GROUP_WIDTH
    return pl.pallas_call(
        _mem_kv_kernel,
        out_shape=jax.ShapeDtypeStruct((BATCH * MEM_LEN, DEPTH * tn), BF16),
        grid=(DEPTH, BATCH * MEM_LEN // tm),
        in_specs=[pl.BlockSpec((tm, D_MODEL), lambda l, i: (i, 0)),
                  pl.BlockSpec((None, D_MODEL, tn), lambda l, i: (l, 0, 0))],
        out_specs=pl.BlockSpec((tm, tn), lambda l, i: (i, l)),
        scratch_shapes=[pltpu.VMEM((D_MODEL, tn), BF16)],
        compiler_params=_cparams(("parallel", "arbitrary")),
        name="mem_kv",
    )(a, w_mem_kv)


def _inproj_kernel(x_ref, w_ref, o16_ref, o32_ref, *scratch):
    j = pl.program_id(1)
    if scratch:
        xb_ref, = scratch

        @pl.when(j == 0)
        def _():
            xb_ref[...] = x_ref[...].astype(BF16)
    else:
        xb_ref = x_ref

    def tile(o_ref):
        xb = xb_ref[...]
        for k2 in range(TN_IN // 256):
            r = _dot_nt(xb, w_ref[k2 * 256:(k2 + 1) * 256, :])
            o_ref[2 * k2] = r[:, :LANE].astype(o_ref.dtype)
            o_ref[2 * k2 + 1] = r[:, LANE:].astype(o_ref.dtype)

    @pl.when(j < NBLK16 * LANE // TN_IN)
    def _():
        tile(o16_ref)

    @pl.when(j == NBLK16 * LANE // TN_IN)
    def _():
        tile(o32_ref)


def _inproj(x2d, wt_p, layer):
    nb = TN_IN // LANE
    assert NBLK32 == nb and NBLK16 % nb == 0
    last16 = NBLK16 // nb - 1
    return pl.pallas_call(
        _inproj_kernel,
        out_shape=(jax.ShapeDtypeStruct((NBLK16, ROWS, LANE), BF16),
                   jax.ShapeDtypeStruct((NBLK32, ROWS, LANE), F32)),
        grid=(ROWS // TM_IN, NBLK * LANE // TN_IN),
        in_specs=[pl.BlockSpec((TM_IN, D_MODEL), lambda i, j: (i, 0)),
                  pl.BlockSpec((None, TN_IN, D_MODEL), lambda i, j: (layer, j, 0))],
        out_specs=(pl.BlockSpec((nb, TM_IN, LANE), lambda i, j: (jnp.minimum(j, last16), i, 0)),
                   pl.BlockSpec((nb, TM_IN, LANE), lambda i, j: (0, i, 0))),
        scratch_shapes=[] if x2d.dtype == BF16 else [pltpu.VMEM((TM_IN, D_MODEL), BF16)],
        compiler_params=_cparams(("parallel", "arbitrary")),
        name="inproj",
    )(x2d, wt_p)


def _w_in_groups():
    off = np.concatenate([[0], np.cumsum(IN_SPLITS)]).tolist()
    (o_rq, o_rk, o_rv, o_cq, _, o_kpe, o_lx, o_lv, o_lo, o_li, _, o_mq, o_z, _) = off
    groups = [o_rq, o_rk, o_rv, o_lv, o_lo, o_mq, o_z, o_z + GROUP_WIDTH, o_z + 3 * GROUP_WIDTH,
              o_cq, o_lx, o_z + 2 * GROUP_WIDTH]
    return groups, o_kpe, o_li


def _w_in_kernel(tbl_ref, w_ref, kpe_ref, gate_ref, o_ref):
    del tbl_ref
    ngroup = NBLK // HEADS

    @pl.when(pl.program_id(1) < ngroup)
    def _():
        o_ref[0] = w_ref[0].astype(BF16)

    @pl.when(pl.program_id(1) == ngroup)
    def _():
        half = MLA_ROPE // 2
        zeros = lambda n: jnp.zeros((n, D_MODEL), F32)
        kpe = kpe_ref[0]
        blk = jnp.concatenate([kpe[0:half], gate_ref[0], zeros(64 - half - 2 * HEADS),
                               kpe[half:2 * half], zeros(64 - half)], axis=0)
        o_ref[0, 0:LANE, :] = blk.astype(BF16)
        o_ref[0, LANE:, :] = jnp.zeros((GROUP_WIDTH - LANE, D_MODEL), BF16)


def _prep_w_in(w_in):
    wt = jnp.swapaxes(w_in, 1, 2)
    groups, o_kpe, o_li = _w_in_groups()
    assert len(groups) * HEADS + 2 == NBLK and o_kpe % LANE == 0 and o_li % (2 * HEADS) == 0
    sub = 8
    assert all(g % sub == 0 for g in groups)
    table = jnp.asarray([g // sub for g in groups] + [0], jnp.int32)
    grid_spec = pltpu.PrefetchScalarGridSpec(
        num_scalar_prefetch=1,
        grid=(DEPTH, len(groups) + 1),
        in_specs=[pl.BlockSpec((pl.Element(1), pl.Element(GROUP_WIDTH), pl.Element(D_MODEL)),
                               lambda l, j, tbl: (l, pl.multiple_of(tbl[j] * sub, sub), 0)),
                  pl.BlockSpec((1, LANE, D_MODEL), lambda l, j, tbl: (l, o_kpe // LANE, 0)),
                  pl.BlockSpec((1, 2 * HEADS, D_MODEL), lambda l, j, tbl: (l, o_li // (2 * HEADS), 0))],
        out_specs=pl.BlockSpec((1, GROUP_WIDTH, D_MODEL), lambda l, j, tbl: (l, j, 0)),
    )
    return pl.pallas_call(
        _w_in_kernel,
        out_shape=jax.ShapeDtypeStruct((DEPTH, NBLK * LANE, D_MODEL), BF16),
        grid_spec=grid_spec,
        compiler_params=_cparams(("parallel", "arbitrary")),
        name="w_in_relayout",
    )(table, wt, wt, wt)


def _group_spec(base, rows):
    nchunk = SEQ // rows
    return pl.BlockSpec((HEADS, rows, LANE), lambda b, c: (base // HEADS, b * nchunk + c, 0))


def _layer_spec(shape, layer):
    nd = len(shape) - 1
    return pl.BlockSpec((None,) + tuple(shape[1:]), lambda *_: (layer,) + (0,) * nd)


def _const_spec(shape):
    nd = len(shape)
    return pl.BlockSpec(tuple(shape), lambda *_: (0,) * nd)


def _ret_kernel(q_ref, k_ref, v_ref, z_ref, cos_ref, sin_ref, gq_ref, gk_ref, gc_ref, g_ref, o_ref, st_ref):
    @pl.when(pl.program_id(1) == 0)
    def _():
        st_ref[...] = jnp.zeros_like(st_ref)

    cs = cos_ref[0]
    sn = sin_ref[0]
    row = lax.broadcasted_iota(jnp.int32, (RCHUNK, RCHUNK), 0)
    col = lax.broadcasted_iota(jnp.int32, (RCHUNK, RCHUNK), 1)
    causal = col <= row
    qbs, kts, vbs, scs = [], [], [], []
    for h in range(HEADS):
        q = q_ref[h].astype(F32)
        q = (q * cs + pltpu.roll(q, HEAD_DIM // 2, 1) * sn) * gq_ref[h]
        k = k_ref[h].astype(F32)
        k = (k * cs + pltpu.roll(k, HEAD_DIM // 2, 1) * sn) * gk_ref[h]
        qb = q.astype(BF16)
        scs.append(_dot_nt(qb, k.astype(BF16)))
        qbs.append(qb)
        kts.append(k.T.astype(BF16))
        vbs.append(v_ref[h])
    outs = []
    for h in range(HEADS):
        st = st_ref[h]
        sc = jnp.where(causal, scs[h], 0.0).astype(BF16)
        outs.append(_dot(sc, vbs[h]) + _dot(qbs[h], st.astype(BF16)))
        st_ref[h] = gc_ref[h] * (st + _dot(kts[h], vbs[h]))
    for h in range(HEADS):
        out = outs[h]
        mu = jnp.mean(out, axis=-1, keepdims=True)
        dev = out - mu
        var = jnp.mean(dev * dev, axis=-1, keepdims=True)
        hn = dev * lax.rsqrt(var + LN_EPS) * g_ref[h]
        o_ref[h] = (hn * _silu(z_ref[h].astype(F32))).astype(BF16)


def _ret_consts():
    log_g = jnp.log1p(-jnp.exp2(-5.0 - jnp.arange(HEADS, dtype=F32)))
    idx = jnp.arange(RCHUNK, dtype=F32)
    full = (HEADS, RCHUNK, HEAD_DIM)
    gq = jnp.broadcast_to(jnp.exp(log_g[:, None] * (idx + 1.0))[..., None], full)
    gk = jnp.broadcast_to((jnp.exp(-log_g[:, None] * (idx + 1.0)) * HEAD_DIM ** -0.5)[..., None], full)
    gc = jnp.broadcast_to(jnp.exp(log_g * RCHUNK)[:, None, None], (HEADS, 1, HEAD_DIM))
    return gq, gk, gc


def _mlstm_kernel(lx_ref, lv_ref, lo_ref, gt_ref, z_ref, cw_ref, cb_ref, wq_ref, wk_ref, gb_ref,
                  skip_ref, g_ref, o_ref, xp_ref, st_ref, m_ref, s_ref, p_ref):
    @pl.when(pl.program_id(1) == 0)
    def _():
        xp_ref[:, 0:CONV_PAD, :] = jnp.zeros((HEADS, CONV_PAD, HEAD_DIM), F32)
        st_ref[...] = jnp.zeros_like(st_ref)
        m_ref[...] = jnp.zeros_like(m_ref)

    n = RCHUNK
    ones_nd = jnp.ones((n, HEAD_DIM), BF16)
    mean_dd = jnp.full((HEAD_DIM, HEAD_DIM), 1.0 / HEAD_DIM, BF16)
    row = lax.broadcasted_iota(jnp.int32, (n, n), 0)
    col = lax.broadcasted_iota(jnp.int32, (n, n), 1)
    tril = jnp.where(col <= row, 1.0, 0.0).astype(BF16)
    r128 = lax.broadcasted_iota(jnp.int32, (LANE, LANE), 0)
    c128 = lax.broadcasted_iota(jnp.int32, (LANE, LANE), 1)
    ident = jnp.where(r128 == c128, 1.0, 0.0).astype(BF16)
    srow = lax.broadcasted_iota(jnp.int32, (STRIP, LANE), 0)
    scol = lax.broadcasted_iota(jnp.int32, (STRIP, LANE), 1)

    lane = lax.broadcasted_iota(jnp.int32, (n, LANE), 1)
    x = gt_ref[0] + gb_ref[...]
    x = jnp.where((lane >= GATE_F_LANE) & (lane < GATE_F_LANE + HEADS), _log_sigmoid(x), x) * LOG2E
    x1, x2, x3 = _split3(x)
    cum = _dot(tril, x1) + _dot(tril, x2) + _dot(tril, x3)
    rt = x - pltpu.roll(cum, LANE - (GATE_F_LANE - GATE_I_LANE), 1)
    r1, r2, r3 = _split3(rt)
    rtt = _dot_nt(ident, r1) + _dot_nt(ident, r2) + _dot_nt(ident, r3)

    xcs, ks, qbs, vbs = [], [], [], []
    for h in range(HEADS):
        xp_ref[h, CONV_PAD:, :] = lx_ref[h]
        acc = jnp.zeros((n, HEAD_DIM), F32) + cb_ref[h]
        for j in range(CONV_WIDTH):
            off = CONV_PAD - (CONV_WIDTH - 1) + j
            acc = acc + xp_ref[h, off:off + n, :] * cw_ref[h, j:j + 1, :]
        xp_ref[h, 0:CONV_PAD, :] = lx_ref[h, n - CONV_PAD:n, :]
        xc = _silu(acc)
        xcb = xc.astype(BF16)
        k = _dot(xcb, wk_ref[h]) * HEAD_DIM ** -0.5
        qb = _dot(xcb, wq_ref[h]).astype(BF16)
        s_ref[h] = _dot_nt(qb, k.astype(BF16))
        xcs.append(xc)
        ks.append(k)
        qbs.append(qb)
        vbs.append(jnp.concatenate([lv_ref[h], ones_nd], axis=-1))

    r_rows = [rtt[GATE_I_LANE + h:GATE_I_LANE + h + 1, :] for h in range(HEADS)]
    bases = [m_ref[h] for h in range(HEADS)]
    strips = [[] for _ in range(HEADS)]
    u_lasts = [None] * HEADS
    for i in range(n // STRIP):
        rows = slice(i * STRIP, (i + 1) * STRIP)
        d0 = (i * STRIP // LANE) * LANE
        w = d0 + LANE
        mask = (scol + d0) <= (srow + i * STRIP)
        for h in range(HEADS):
            r_row = r_rows[h]
            if d0 > 0 and (i * STRIP) % LANE == 0:
                bases[h] = jnp.maximum(bases[h], jnp.max(r_row[:, d0 - LANE:d0], axis=-1, keepdims=True))
            rmd = jnp.where(mask, r_row[:, d0:w], -jnp.inf)
            u_col = jnp.maximum(jnp.max(rmd, axis=-1, keepdims=True), bases[h])
            u = jnp.broadcast_to(u_col, (STRIP, LANE))
            p_ref[h, rows, d0:w] = (s_ref[h, rows, d0:w] * jnp.exp2(rmd - u)).astype(BF16)
            for c in range(d0 // LANE):
                cols = slice(c * LANE, (c + 1) * LANE)
                p_ref[h, rows, cols] = (s_ref[h, rows, cols] * jnp.exp2(r_row[:, cols] - u)).astype(BF16)
            if w < n:
                p_ref[h, rows, w:n] = jnp.zeros((STRIP, n - w), BF16)
            strips[h].append(u)
            u_lasts[h] = u_col[STRIP - 1:STRIP, :]
    us = [jnp.concatenate(st_h, axis=0) for st_h in strips]

    cells = []
    for h in range(HEADS):
        li = GATE_I_LANE + h
        lf = GATE_F_LANE + h
        m_st = m_ref[h]
        u, u_last, vb = us[h], u_lasts[h], vbs[h]
        w_inter = jnp.exp2(m_st - u)
        st = st_ref[h]
        intra = _dot(p_ref[h], vb)
        inter = _dot(qbs[h], st.astype(BF16))
        num = intra[:, :LANE] + w_inter * inter[:, :LANE]
        den = intra[:, LANE:] + w_inter * inter[:, LANE:]
        cum_f = jnp.broadcast_to(cum[:, lf:lf + 1], (n, LANE))
        cells.append(num / jnp.maximum(jnp.abs(den), jnp.exp2(-(cum_f + u))))

        decay = jnp.exp2(m_st - u_last)
        kw = ks[h] * jnp.exp2(jnp.broadcast_to(rt[:, li:li + 1], (n, LANE)) - u_last)
        upd = _dot(kw.T.astype(BF16), vb)
        st_ref[h] = decay * st + upd
        m_ref[h] = cum[n - 1:n, lf:lf + 1] + u_last

    for h in range(HEADS):
        cell = cells[h] * jax.nn.sigmoid(lo_ref[h].astype(F32))
        mu = _rowsum_mxu(cell, mean_dd)
        dev = cell - mu
        var = _rowsum_mxu(dev * dev, mean_dd)
        hn = dev * lax.rsqrt(var + LN_EPS) * g_ref[h]
        o_ref[h] = ((hn + skip_ref[h] * xcs[h]) * _silu(z_ref[h])).astype(BF16)


def _prep_mlstm(conv_w, conv_b, w_q, w_k, i_bias, f_bias, skip, norm_g):
    nl = conv_w.shape[0]
    cw = conv_w.reshape(nl, CONV_WIDTH, HEADS, HEAD_DIM).transpose(0, 2, 1, 3)
    cb = conv_b.reshape(nl, HEADS, 1, HEAD_DIM)
    zeros = lambda w: jnp.zeros((nl, w), F32)
    gb = jnp.concatenate([zeros(GATE_I_LANE), i_bias, f_bias, zeros(LANE - GATE_F_LANE - HEADS)], axis=-1)
    return (cw, cb, w_q.astype(BF16), w_k.astype(BF16), gb.reshape(nl, 1, LANE),
            skip.reshape(nl, HEADS, 1, HEAD_DIM), norm_g.reshape(nl, HEADS, 1, HEAD_DIM))


def _mlstm(hb16, hb32, params, layer):
    nchunk = SEQ // RCHUNK
    return pl.pallas_call(
        _mlstm_kernel,
        out_shape=jax.ShapeDtypeStruct((HEADS, ROWS, LANE), BF16),
        grid=(BATCH, nchunk),
        in_specs=[_group_spec(BLK_LX, RCHUNK), _group_spec(BLK_LV, RCHUNK), _group_spec(BLK_LO, RCHUNK),
                  pl.BlockSpec((1, RCHUNK, LANE), lambda b, c: (BLK_KPE, b * nchunk + c, 0)),
                  _group_spec(BLK_ZC, RCHUNK)] + [_layer_spec(p.shape, layer) for p in params],
        out_specs=pl.BlockSpec((HEADS, RCHUNK, LANE), lambda b, c: (0, b * nchunk + c, 0)),
        scratch_shapes=[pltpu.VMEM((HEADS, RCHUNK + CONV_PAD, HEAD_DIM), F32),
                        pltpu.VMEM((HEADS, HEAD_DIM, 2 * HEAD_DIM), F32),
                        pltpu.VMEM((HEADS, 1, 1), F32),
                        pltpu.VMEM((HEADS, RCHUNK, RCHUNK), F32),
                        pltpu.VMEM((HEADS, RCHUNK, RCHUNK), BF16)],
        compiler_params=_cparams(("parallel", "arbitrary")),
        name="mlstm",
    )(hb32, hb16, hb16, hb32, hb32, *params)


def _mla_prep_kernel(lat_ref, kpe_ref, cm_ref, sm_ref, qg_ref, wuq_ref, kvg_ref, wukv_ref,
                     q_out, k_out, v_out):
    scale = (MLA_NOPE + MLA_ROPE) ** -0.5 * LOG2E
    cs = cm_ref[0]
    sn = sm_ref[0]
    cq = jnp.concatenate([lat_ref[0], lat_ref[1], lat_ref[2]], axis=-1).astype(F32)
    qn = cq * lax.rsqrt(jnp.mean(jnp.square(cq), axis=-1, keepdims=True) + RMS_EPS) * qg_ref[...]
    q = _dot(qn.astype(BF16), wuq_ref[...])
    ckv = lat_ref[3].astype(F32)
    kvn = ckv * lax.rsqrt(jnp.mean(jnp.square(ckv), axis=-1, keepdims=True) + RMS_EPS) * kvg_ref[...]
    kv = _dot(kvn.astype(BF16), wukv_ref[...])
    kpe = kpe_ref[0]
    krot = (kpe * cs + pltpu.roll(kpe, LANE // 2, 1) * sn).astype(BF16)
    for h in range(HEADS):
        qr = q[:, GROUP_WIDTH + h * LANE:GROUP_WIDTH + (h + 1) * LANE]
        qr = qr * cs + pltpu.roll(qr, LANE // 2, 1) * sn
        q_out[0, h, :, 0:LANE] = (q[:, h * LANE:(h + 1) * LANE] * scale).astype(BF16)
        q_out[0, h, :, LANE:2 * LANE] = (qr * scale).astype(BF16)
        k_out[0, h, :, 0:LANE] = kv[:, h * LANE:(h + 1) * LANE].astype(BF16)
        k_out[0, h, :, LANE:2 * LANE] = krot
        v_out[0, h] = kv[:, GROUP_WIDTH + h * LANE:GROUP_WIDTH + (h + 1) * LANE].astype(BF16)


def _prep_mla_weights(w_uq, w_ukv):
    lead = w_uq.shape[:-1]
    wq = w_uq.reshape(lead + (HEADS, MLA_NOPE + MLA_ROPE))
    nope = wq[..., :MLA_NOPE].reshape(lead + (GROUP_WIDTH,))
    half = MLA_ROPE // 2
    zeros = jnp.zeros(lead + (HEADS, 64 - half), w_uq.dtype)
    rope = jnp.concatenate([wq[..., MLA_NOPE:MLA_NOPE + half], zeros, wq[..., MLA_NOPE + half:], zeros], axis=-1)
    wq_p = jnp.concatenate([nope, rope.reshape(lead + (HEADS * LANE,))], axis=-1).astype(BF16)
    lead = w_ukv.shape[:-1]
    wkv = w_ukv.reshape(lead + (HEADS, MLA_NOPE + HEAD_DIM))
    wkv_p = jnp.concatenate([wkv[..., :MLA_NOPE].reshape(lead + (GROUP_WIDTH,)),
                             wkv[..., MLA_NOPE:].reshape(lead + (GROUP_WIDTH,))], axis=-1).astype(BF16)
    return wq_p, wkv_p


def _attn_tile(nfull, q_ref, k_ref, v_ref, z_ref, o_ref, s_ref, p_ref, m_ref):
    kvlen = (nfull + 1) * TQ_ATT
    d0 = nfull * TQ_ATT
    hq = TQ_ATT // 2
    srow = lax.broadcasted_iota(jnp.int32, (STRIP, LANE), 0)
    scol = lax.broadcasted_iota(jnp.int32, (STRIP, LANE), 1)

    def strip_blocks(g, i):
        rows = slice(i * STRIP, (i + 1) * STRIP)
        wd = -(-(i + 1) * STRIP // LANE) * LANE
        ncol = (d0 + wd) // LANE
        blks = [s_ref[g, rows, c * LANE:(c + 1) * LANE] for c in range(ncol)]
        blks[-1] = jnp.where(scol + (wd - LANE) <= srow + i * STRIP, blks[-1], -jnp.inf)
        return rows, ncol, blks

    qrows = slice(d0, d0 + TQ_ATT)
    for g in range(ATT_HEADS):
        q = q_ref[0, g, qrows, :]
        for j in range(nfull):
            s_ref[g, :, j * TQ_ATT:(j + 1) * TQ_ATT] = _dot_nt(q, k_ref[0, g, j * TQ_ATT:(j + 1) * TQ_ATT, :])
        s_ref[g, 0:hq, d0:d0 + hq] = _dot_nt(q[0:hq], k_ref[0, g, d0:d0 + hq, :])
        s_ref[g, hq:TQ_ATT, d0:kvlen] = _dot_nt(q[hq:TQ_ATT], k_ref[0, g, d0:kvlen, :])
    for g in range(ATT_HEADS):
        for i in range(TQ_ATT // STRIP):
            rows, ncol, blks = strip_blocks(g, i)
            mx = blks[0]
            for blk in blks[1:]:
                mx = jnp.maximum(mx, blk)
            m_ref[g, rows, :] = jnp.broadcast_to(jnp.max(mx, axis=-1, keepdims=True), (STRIP, LANE))
        for i in range(TQ_ATT // STRIP):
            rows, ncol, blks = strip_blocks(g, i)
            m = m_ref[g, rows, :]
            for c, blk in enumerate(blks):
                p_ref[g, rows, c * LANE:(c + 1) * LANE] = jnp.exp2(blk - m).astype(BF16)
            kl = d0 + hq if (i + 1) * STRIP <= hq else kvlen
            if ncol * LANE < kl:
                p_ref[g, rows, ncol * LANE:kl] = jnp.zeros((STRIP, kl - ncol * LANE), BF16)
        v1 = jnp.concatenate([v_ref[0, g, 0:kvlen, :], jnp.ones((kvlen, LANE), BF16)], axis=-1)
        for r0, r1, kl in ((0, hq, d0 + hq), (hq, TQ_ATT, kvlen)):
            pv = _dot(p_ref[g, r0:r1, 0:kl], v1[0:kl])
            gate = _silu(z_ref[g, d0 + r0:d0 + r1, :].astype(F32))
            o_ref[g, d0 + r0:d0 + r1, :] = (pv[:, :LANE] / pv[:, LANE:] * gate).astype(BF16)


def _mla_attn_kernel(q_ref, k_ref, v_ref, z_ref, o_ref, s_ref, p_ref, m_ref):
    for nfull in range(SEQ // TQ_ATT):
        _attn_tile(nfull, q_ref, k_ref, v_ref, z_ref, o_ref, s_ref, p_ref, m_ref)


def _mla_attn(q, k, v, hb16):
    return pl.pallas_call(
        _mla_attn_kernel,
        out_shape=jax.ShapeDtypeStruct((HEADS, ROWS, LANE), BF16),
        grid=(BATCH, HEADS // ATT_HEADS),
        in_specs=[pl.BlockSpec((1, ATT_HEADS, SEQ, 2 * LANE), lambda b, h: (b, h, 0, 0)),
                  pl.BlockSpec((1, ATT_HEADS, SEQ, 2 * LANE), lambda b, h: (b, h, 0, 0)),
                  pl.BlockSpec((1, ATT_HEADS, SEQ, LANE), lambda b, h: (b, h, 0, 0)),
                  pl.BlockSpec((ATT_HEADS, SEQ, LANE), lambda b, h: (BLK_ZB // ATT_HEADS + h, b, 0))],
        out_specs=pl.BlockSpec((ATT_HEADS, SEQ, LANE), lambda b, h: (h, b, 0)),
        scratch_shapes=[pltpu.VMEM((ATT_HEADS, TQ_ATT, SEQ), F32), pltpu.VMEM((ATT_HEADS, TQ_ATT, SEQ), BF16),
                        pltpu.VMEM((ATT_HEADS, TQ_ATT, LANE), F32)],
        compiler_params=_cparams(("parallel", "parallel")),
        name="mla_attn",
    )(q, k, v, hb16)


def _mem_attn_kernel(q_ref, k_ref, v_ref, z_ref, o_ref):
    ones = jnp.ones((MEM_LEN, LANE), BF16)
    scores = []
    for h in range(HEADS):
        q = (q_ref[h].astype(F32) * (HEAD_DIM ** -0.5 * LOG2E)).astype(BF16)
        scores.append(_dot_nt(q, k_ref[:, h * LANE:(h + 1) * LANE]))
    for h in range(HEADS):
        s = scores[h]
        p = jnp.exp2(s - jnp.max(s, axis=-1, keepdims=True)).astype(BF16)
        pv = _dot(p, jnp.concatenate([v_ref[:, h * LANE:(h + 1) * LANE], ones], axis=-1))
        o_ref[h] = (pv[:, :LANE] / pv[:, LANE:] * _silu(z_ref[h].astype(F32))).astype(BF16)


def _light_kernel(rq, rk, rv, rz, cr, sr, gq, gk, gc, rg, mq, mk, mv, mz, lat, kpe, cm, sm, qg, wuq, kvg, wukv,
                  ya, yd, q_out, k_out, v_out, st_ref):
    _ret_kernel(rq, rk, rv, rz, cr, sr, gq, gk, gc, rg, ya, st_ref)
    _mem_attn_kernel(mq, mk, mv, mz, yd)
    _mla_prep_kernel(lat, kpe, cm, sm, qg, wuq, kvg, wukv, q_out, k_out, v_out)


def _light_mixers(hb16, hb32, tables, ret_consts, ret_g, kvm, qg, wuq_p, kvg, wukv_p, layer):
    cos_r, sin_r, cos_m, sin_m = tables
    gq, gk, gc = ret_consts
    nchunk = SEQ // RCHUNK
    tab = pl.BlockSpec((1, RCHUNK, LANE), lambda b, c: (b, c, 0))
    row_blk = lambda blk: pl.BlockSpec((1, RCHUNK, LANE), lambda b, c: (blk, b * nchunk + c, 0))
    y_shape = jax.ShapeDtypeStruct((HEADS, ROWS, LANE), BF16)
    y_spec = pl.BlockSpec((HEADS, RCHUNK, LANE), lambda b, c: (0, b * nchunk + c, 0))
    qk_shape = jax.ShapeDtypeStruct((BATCH, HEADS, SEQ, 2 * LANE), BF16)
    v_shape = jax.ShapeDtypeStruct((BATCH, HEADS, SEQ, LANE), BF16)
    qk_spec = pl.BlockSpec((1, HEADS, RCHUNK, 2 * LANE), lambda b, c: (b, 0, c, 0))
    v_spec = pl.BlockSpec((1, HEADS, RCHUNK, LANE), lambda b, c: (b, 0, c, 0))
    return pl.pallas_call(
        _light_kernel,
        out_shape=(y_shape, y_shape, qk_shape, qk_shape, v_shape),
        grid=(BATCH, nchunk),
        in_specs=[_group_spec(BLK_RQ, RCHUNK), _group_spec(BLK_RK, RCHUNK), _group_spec(BLK_RV, RCHUNK),
                  _group_spec(BLK_ZA, RCHUNK), tab, tab,
                  _const_spec(gq.shape), _const_spec(gk.shape), _const_spec(gc.shape),
                  _layer_spec(ret_g.shape, layer),
                  _group_spec(BLK_MQ, RCHUNK),
                  pl.BlockSpec((MEM_LEN, GROUP_WIDTH), lambda b, c: (b, 2 * layer)),
                  pl.BlockSpec((MEM_LEN, GROUP_WIDTH), lambda b, c: (b, 2 * layer + 1)),
                  _group_spec(BLK_ZD, RCHUNK),
                  _group_spec(BLK_CQ, RCHUNK), row_blk(BLK_KPE), tab, tab,
                  _layer_spec(qg.shape, layer), _layer_spec(wuq_p.shape, layer),
                  _layer_spec(kvg.shape, layer), _layer_spec(wukv_p.shape, layer)],
        out_specs=(y_spec, y_spec, qk_spec, qk_spec, v_spec),
        scratch_shapes=[pltpu.VMEM((HEADS, HEAD_DIM, HEAD_DIM), F32)],
        compiler_params=_cparams(("parallel", "arbitrary")),
        name="light_mixers",
    )(hb16, hb16, hb16, hb16, cos_r, sin_r, gq, gk, gc, ret_g, hb16, kvm, kvm, hb16, hb16, hb32, cos_m, sin_m,
      qg, wuq_p, kvg, wukv_p)


def _out_kernel(ya_ref, yb_ref, yc_ref, yd_ref, w_ref, x_ref, g_ref, b_ref, o_ref, *bf16_out):
    for t in range(TM_OUT // SUB_OUT):
        rows = slice(t * SUB_OUT, (t + 1) * SUB_OUT)
        parts = [ref[h, rows, :] for ref in (ya_ref, yb_ref, yc_ref, yd_ref) for h in range(HEADS)]
        y = jnp.concatenate(parts, axis=-1)
        r = DEEPNORM_ALPHA * x_ref[rows, :] + _dot(y, w_ref[...])
        mu = jnp.mean(r, axis=-1, keepdims=True)
        var = jnp.mean(jnp.square(r - mu), axis=-1, keepdims=True)
        out = (r - mu) * lax.rsqrt(var + LN_EPS) * g_ref[...] + b_ref[...]
        o_ref[rows, :] = out
        for ob_ref in bf16_out:
            ob_ref[rows, :] = out.astype(BF16)


def _outproj(ya, yb, yc, yd, w_out, x2d, ln_g, ln_b, layer, with_bf16):
    yspec = pl.BlockSpec((HEADS, TM_OUT, LANE), lambda i: (0, i, 0))
    xspec = pl.BlockSpec((TM_OUT, D_MODEL), lambda i: (i, 0))
    dtypes = (F32, BF16) if with_bf16 else (F32,)
    return pl.pallas_call(
        _out_kernel,
        out_shape=tuple(jax.ShapeDtypeStruct((ROWS, D_MODEL), dt) for dt in dtypes),
        grid=(ROWS // TM_OUT,),
        in_specs=[yspec, yspec, yspec, yspec,
                  pl.BlockSpec((None,) + tuple(w_out.shape[1:]), lambda i: (layer, 0, 0),
                               pipeline_mode=pl.Buffered(1)),
                  xspec, _layer_spec(ln_g.shape, layer), _layer_spec(ln_b.shape, layer)],
        out_specs=tuple(xspec for _ in dtypes),
        compiler_params=_cparams(("parallel",)),
        name="outproj_ln",
    )(ya, yb, yc, yd, w_out, x2d, ln_g, ln_b)


def kernel(x, mem, positions, w_in, ret_norm_g, mla_q_norm_g, mla_w_uq, mla_kv_norm_g, mla_w_ukv, ml_conv_w, ml_conv_b, ml_w_q, ml_w_k, ml_i_bias, ml_f_bias, ml_skip, ml_norm_g, w_mem_kv, w_out, ln_g, ln_b):
    assert x.shape == (BATCH, SEQ, D_MODEL) and mem.shape == (BATCH, MEM_LEN, D_MODEL)
    tables = _rope_tables(positions)
    kvm = _mem_kv(mem, w_mem_kv)
    w_in_p = _prep_w_in(w_in)
    wuq_p, wukv_p = _prep_mla_weights(mla_w_uq, mla_w_ukv)
    w_out_b = w_out.astype(BF16)
    ret_consts = _ret_consts()
    ret_g = ret_norm_g.reshape(DEPTH, HEADS, 1, HEAD_DIM)
    mlstm_params = _prep_mlstm(ml_conv_w, ml_conv_b, ml_w_q, ml_w_k, ml_i_bias, ml_f_bias, ml_skip, ml_norm_g)
    qg = mla_q_norm_g.reshape(DEPTH, 1, MLA_Q_RANK)
    kvg = mla_kv_norm_g.reshape(DEPTH, 1, MLA_KV_RANK)
    lng = ln_g.reshape(DEPTH, 1, D_MODEL)
    lnb = ln_b.reshape(DEPTH, 1, D_MODEL)

    x2d = x.reshape(ROWS, D_MODEL)
    xb = []
    for l in range(DEPTH):
        hb16, hb32 = _inproj(xb[0] if xb else x2d, w_in_p, l)
        ya, yd, q, k, v = _light_mixers(hb16, hb32, tables, ret_consts, ret_g, kvm, qg, wuq_p, kvg, wukv_p, l)
        yb = _mla_attn(q, k, v, hb16)
        yc = _mlstm(hb16, hb32, mlstm_params, l)
        x2d, *xb = _outproj(ya, yb, yc, yd, w_out_b, x2d, lng, lnb, l, with_bf16=l + 1 < DEPTH)
    return x2d.reshape(BATCH, SEQ, D_MODEL)
```

```python
import numpy as np
import jax
import jax.numpy as jnp
from jax import lax
from jax.experimental import pallas as pl
from jax.experimental.pallas import tpu as pltpu

F32 = jnp.float32
BF16 = jnp.bfloat16

D_MODEL = 2048
BATCH = 8
SEQ = 2048
DEPTH = 4
MEM_LEN = 256
HEAD_DIM = 128
HEADS = 4
GROUP_WIDTH = HEADS * HEAD_DIM
MLA_NOPE = 128
MLA_ROPE = 64
MLA_Q_RANK = 384
MLA_KV_RANK = 128
CONV_WIDTH = 4
MIX_WIDTH = 4 * GROUP_WIDTH
ROPE_THETA = 10000.0
LN_EPS = 1e-5
RMS_EPS = 1e-6
DEEPNORM_ALPHA = (2 * DEPTH) ** 0.25
IN_SPLITS = (512, 512, 512, MLA_Q_RANK, MLA_KV_RANK, MLA_ROPE, 512, 512, 512, HEADS, HEADS, 512, MIX_WIDTH)

LANE = 128
ROWS = BATCH * SEQ

BLK_RQ, BLK_RK, BLK_RV = 0, 4, 8
BLK_LV, BLK_LO = 12, 16
BLK_MQ = 20
BLK_ZA, BLK_ZB, BLK_ZD = 24, 28, 32
BLK_CQ, BLK_CKV = 36, 39
NBLK16 = 40
BLK_LX, BLK_ZC = 0, 4
BLK_KPE = 8
NBLK32 = 10
NBLK = NBLK16 + NBLK32
GATE_I_LANE, GATE_F_LANE = 32, 36

TM_IN = 512
TM_OUT = 512
SUB_OUT = 256
TQ_ATT = 512
ATT_HEADS = 2
RCHUNK = 512
CONV_PAD = 8
STRIP = 64
LOG2E = 1.4426950408889634
VMEM_LIMIT = 56 * 1024 * 1024


def _cparams(sem):
    return pltpu.CompilerParams(dimension_semantics=sem, vmem_limit_bytes=VMEM_LIMIT)


def _silu(z):
    return z * jax.nn.sigmoid(z)


def _log_sigmoid(x):
    return jnp.minimum(x, 0.0) - jnp.log1p(jnp.exp(-jnp.abs(x)))


def _split3(x):
    x1 = x.astype(BF16)
    r1 = x - x1.astype(F32)
    x2 = r1.astype(BF16)
    x3 = (r1 - x2.astype(F32)).astype(BF16)
    return x1, x2, x3


def _dot(a, b):
    return jnp.dot(a, b, preferred_element_type=F32)


def _rowsum_mxu(x, w):
    hi = x.astype(BF16)
    lo = (x - hi.astype(F32)).astype(BF16)
    return _dot(hi, w) + _dot(lo, w)


def _dot_nt(a, b):
    return lax.dot_general(a, b, (((1,), (1,)), ((), ())), preferred_element_type=F32)


def _tables_kernel(pos_ref, c_ref, cr_ref, sr_ref, cm_ref, sm_ref):
    pos = pos_ref[0].astype(F32)
    ang = pos * c_ref[0:1, :]
    cs = jnp.cos(ang)
    sn = jnp.sin(ang)
    cs_sw = pltpu.roll(cs, LANE // 2, 1)
    sn_sw = pltpu.roll(sn, LANE // 2, 1)
    lane = lax.broadcasted_iota(jnp.int32, cs.shape, 1)
    half_m = MLA_ROPE // 2
    first = lane < LANE // 2
    m_lo = lane < half_m
    m_hi = (lane >= LANE // 2) & (lane < LANE // 2 + half_m)
    cr_ref[0] = jnp.where(first, cs, cs_sw)
    sr_ref[0] = jnp.where(first, -sn, sn_sw)
    cm_ref[0] = jnp.where(m_lo, cs_sw, jnp.where(m_hi, cs, 0.0))
    sm_ref[0] = jnp.where(m_lo, -sn_sw, jnp.where(m_hi, sn, 0.0))


def _rope_tables(positions):
    half_r = HEAD_DIM // 2
    fr = ROPE_THETA ** (-jnp.arange(half_r, dtype=F32) / half_r)
    half_m = MLA_ROPE // 2
    fm = ROPE_THETA ** (-jnp.arange(half_m, dtype=F32) / half_m)
    freqs = jnp.concatenate([fr, fm, jnp.zeros((LANE - half_r - half_m,), F32)])
    consts = jnp.concatenate([freqs[None, :], jnp.zeros((7, LANE), F32)], axis=0)
    ts = 512
    tab = jax.ShapeDtypeStruct((BATCH, SEQ, LANE), F32)
    spec = pl.BlockSpec((1, ts, LANE), lambda b, i: (b, i, 0))
    return pl.pallas_call(
        _tables_kernel,
        out_shape=(tab, tab, tab, tab),
        grid=(BATCH, SEQ // ts),
        in_specs=[pl.BlockSpec((1, ts, 1), lambda b, i: (b, i, 0)),
                  pl.BlockSpec((8, LANE), lambda b, i: (0, 0))],
        out_specs=(spec, spec, spec, spec),
        compiler_params=_cparams(("parallel", "parallel")),
        name="rope_tables",
    )(positions.reshape(BATCH, SEQ, 1), consts)


def _mem_kv_kernel(a_ref, w_ref, o_ref, wb_ref):
    @pl.when(pl.program_id(1) == 0)
    def _():
        wb_ref[...] = w_ref[...].astype(BF16)

    o_ref[...] = _dot(a_ref[...].astype(BF16), wb_ref[...]).astype(BF16)


def _mem_kv(mem, w_mem_kv):
    a = mem.reshape(BATCH * MEM_LEN, D_MODEL)
    tm, tn = 1024, 2 * GROUP_WIDTH
    return pl.pallas_call(
        _mem_kv_kernel,
        out_shape=jax.ShapeDtypeStruct((BATCH * MEM_LEN, DEPTH * tn), BF16),
        grid=(DEPTH, BATCH * MEM_LEN // tm),
        in_specs=[pl.BlockSpec((tm, D_MODEL), lambda l, i: (i, 0)),
                  pl.BlockSpec((None, D_MODEL, tn), lambda l, i: (l, 0, 0))],
        out_specs=pl.BlockSpec((tm, tn), lambda l, i: (i, l)),
        scratch_shapes=[pltpu.VMEM((D_MODEL, tn), BF16)],
        compiler_params=_cparams(("parallel", "arbitrary")),
        name="mem_kv",
    )(a, w_mem_kv)


def _inproj_kernel(x_ref, w_ref, o16_ref, o32_ref):
    xb = x_ref[...].astype(BF16)
    for t in range(NBLK * LANE // 256):
        r = _dot_nt(xb, w_ref[t * 256:(t + 1) * 256, :])
        for half, blk in enumerate((2 * t, 2 * t + 1)):
            o_ref, k = (o16_ref, blk) if blk < NBLK16 else (o32_ref, blk - NBLK16)
            o_ref[k] = r[:, half * LANE:(half + 1) * LANE].astype(o_ref.dtype)


def _inproj(x2d, wt_p, layer):
    return pl.pallas_call(
        _inproj_kernel,
        out_shape=(jax.ShapeDtypeStruct((NBLK16, ROWS, LANE), BF16),
                   jax.ShapeDtypeStruct((NBLK32, ROWS, LANE), F32)),
        grid=(ROWS // TM_IN,),
        in_specs=[pl.BlockSpec((TM_IN, D_MODEL), lambda i: (i, 0)),
                  pl.BlockSpec((None, NBLK * LANE, D_MODEL), lambda i: (layer, 0, 0),
                               pipeline_mode=pl.Buffered(1))],
        out_specs=(pl.BlockSpec((NBLK16, TM_IN, LANE), lambda i: (0, i, 0)),
                   pl.BlockSpec((NBLK32, TM_IN, LANE), lambda i: (0, i, 0))),
        compiler_params=_cparams(("parallel",)),
        name="inproj",
    )(x2d, wt_p)


def _w_in_groups():
    off = np.concatenate([[0], np.cumsum(IN_SPLITS)]).tolist()
    (o_rq, o_rk, o_rv, o_cq, _, o_kpe, o_lx, o_lv, o_lo, o_li, _, o_mq, o_z, _) = off
    groups = [o_rq, o_rk, o_rv, o_lv, o_lo, o_mq, o_z, o_z + GROUP_WIDTH, o_z + 3 * GROUP_WIDTH,
              o_cq, o_lx, o_z + 2 * GROUP_WIDTH]
    return groups, o_kpe, o_li


def _w_in_kernel(tbl_ref, w_ref, kpe_ref, gate_ref, o_ref):
    del tbl_ref
    ngroup = NBLK // HEADS

    @pl.when(pl.program_id(1) < ngroup)
    def _():
        o_ref[0] = w_ref[0].astype(BF16)

    @pl.when(pl.program_id(1) == ngroup)
    def _():
        half = MLA_ROPE // 2
        zeros = lambda n: jnp.zeros((n, D_MODEL), F32)
        kpe = kpe_ref[0]
        blk = jnp.concatenate([kpe[0:half], gate_ref[0], zeros(64 - half - 2 * HEADS),
                               kpe[half:2 * half], zeros(64 - half)], axis=0)
        o_ref[0, 0:LANE, :] = blk.astype(BF16)
        o_ref[0, LANE:, :] = jnp.zeros((GROUP_WIDTH - LANE, D_MODEL), BF16)


def _prep_w_in(w_in):
    wt = jnp.swapaxes(w_in, 1, 2)
    groups, o_kpe, o_li = _w_in_groups()
    assert len(groups) * HEADS + 2 == NBLK and o_kpe % LANE == 0 and o_li % (2 * HEADS) == 0
    sub = 8
    assert all(g % sub == 0 for g in groups)
    table = jnp.asarray([g // sub for g in groups] + [0], jnp.int32)
    grid_spec = pltpu.PrefetchScalarGridSpec(
        num_scalar_prefetch=1,
        grid=(DEPTH, len(groups) + 1),
        in_specs=[pl.BlockSpec((pl.Element(1), pl.Element(GROUP_WIDTH), pl.Element(D_MODEL)),
                               lambda l, j, tbl: (l, pl.multiple_of(tbl[j] * sub, sub), 0)),
                  pl.BlockSpec((1, LANE, D_MODEL), lambda l, j, tbl: (l, o_kpe // LANE, 0)),
                  pl.BlockSpec((1, 2 * HEADS, D_MODEL), lambda l, j, tbl: (l, o_li // (2 * HEADS), 0))],
        out_specs=pl.BlockSpec((1, GROUP_WIDTH, D_MODEL), lambda l, j, tbl: (l, j, 0)),
    )
    return pl.pallas_call(
        _w_in_kernel,
        out_shape=jax.ShapeDtypeStruct((DEPTH, NBLK * LANE, D_MODEL), BF16),
        grid_spec=grid_spec,
        compiler_params=_cparams(("parallel", "arbitrary")),
        name="w_in_relayout",
    )(table, wt, wt, wt)


def _group_spec(base, rows):
    nchunk = SEQ // rows
    return pl.BlockSpec((HEADS, rows, LANE), lambda b, c: (base // HEADS, b * nchunk + c, 0))


def _layer_spec(shape, layer):
    nd = len(shape) - 1
    return pl.BlockSpec((None,) + tuple(shape[1:]), lambda *_: (layer,) + (0,) * nd)


def _const_spec(shape):
    nd = len(shape)
    return pl.BlockSpec(tuple(shape), lambda *_: (0,) * nd)


def _ret_kernel(q_ref, k_ref, v_ref, z_ref, cos_ref, sin_ref, gq_ref, gk_ref, gc_ref, g_ref, o_ref, st_ref):
    @pl.when(pl.program_id(1) == 0)
    def _():
        st_ref[...] = jnp.zeros_like(st_ref)

    cs = cos_ref[0]
    sn = sin_ref[0]
    row = lax.broadcasted_iota(jnp.int32, (RCHUNK, RCHUNK), 0)
    col = lax.broadcasted_iota(jnp.int32, (RCHUNK, RCHUNK), 1)
    causal = col <= row
    qbs, kts, vbs, scs = [], [], [], []
    for h in range(HEADS):
        q = q_ref[h].astype(F32)
        q = (q * cs + pltpu.roll(q, HEAD_DIM // 2, 1) * sn) * gq_ref[h]
        k = k_ref[h].astype(F32)
        k = (k * cs + pltpu.roll(k, HEAD_DIM // 2, 1) * sn) * gk_ref[h]
        qb = q.astype(BF16)
        scs.append(_dot_nt(qb, k.astype(BF16)))
        qbs.append(qb)
        kts.append(k.T.astype(BF16))
        vbs.append(v_ref[h])
    outs = []
    for h in range(HEADS):
        st = st_ref[h]
        sc = jnp.where(causal, scs[h], 0.0).astype(BF16)
        outs.append(_dot(sc, vbs[h]) + _dot(qbs[h], st.astype(BF16)))
        st_ref[h] = gc_ref[h] * (st + _dot(kts[h], vbs[h]))
    for h in range(HEADS):
        out = outs[h]
        mu = jnp.mean(out, axis=-1, keepdims=True)
        dev = out - mu
        var = jnp.mean(dev * dev, axis=-1, keepdims=True)
        hn = dev * lax.rsqrt(var + LN_EPS) * g_ref[h]
        o_ref[h] = (hn * _silu(z_ref[h].astype(F32))).astype(BF16)


def _ret_consts():
    log_g = jnp.log1p(-jnp.exp2(-5.0 - jnp.arange(HEADS, dtype=F32)))
    idx = jnp.arange(RCHUNK, dtype=F32)
    full = (HEADS, RCHUNK, HEAD_DIM)
    gq = jnp.broadcast_to(jnp.exp(log_g[:, None] * (idx + 1.0))[..., None], full)
    gk = jnp.broadcast_to((jnp.exp(-log_g[:, None] * (idx + 1.0)) * HEAD_DIM ** -0.5)[..., None], full)
    gc = jnp.broadcast_to(jnp.exp(log_g * RCHUNK)[:, None, None], (HEADS, 1, HEAD_DIM))
    return gq, gk, gc


def _mlstm_kernel(lx_ref, lv_ref, lo_ref, gt_ref, z_ref, cw_ref, cb_ref, wq_ref, wk_ref, gb_ref,
                  skip_ref, g_ref, o_ref, xp_ref, st_ref, m_ref, s_ref, p_ref):
    @pl.when(pl.program_id(1) == 0)
    def _():
        xp_ref[:, 0:CONV_PAD, :] = jnp.zeros((HEADS, CONV_PAD, HEAD_DIM), F32)
        st_ref[...] = jnp.zeros_like(st_ref)
        m_ref[...] = jnp.zeros_like(m_ref)

    n = RCHUNK
    ones_nd = jnp.ones((n, HEAD_DIM), BF16)
    mean_dd = jnp.full((HEAD_DIM, HEAD_DIM), 1.0 / HEAD_DIM, BF16)
    row = lax.broadcasted_iota(jnp.int32, (n, n), 0)
    col = lax.broadcasted_iota(jnp.int32, (n, n), 1)
    tril = jnp.where(col <= row, 1.0, 0.0).astype(BF16)
    r128 = lax.broadcasted_iota(jnp.int32, (LANE, LANE), 0)
    c128 = lax.broadcasted_iota(jnp.int32, (LANE, LANE), 1)
    ident = jnp.where(r128 == c128, 1.0, 0.0).astype(BF16)
    srow = lax.broadcasted_iota(jnp.int32, (STRIP, LANE), 0)
    scol = lax.broadcasted_iota(jnp.int32, (STRIP, LANE), 1)

    lane = lax.broadcasted_iota(jnp.int32, (n, LANE), 1)
    x = gt_ref[0] + gb_ref[...]
    x = jnp.where((lane >= GATE_F_LANE) & (lane < GATE_F_LANE + HEADS), _log_sigmoid(x), x) * LOG2E
    x1, x2, x3 = _split3(x)
    cum = _dot(tril, x1) + _dot(tril, x2) + _dot(tril, x3)
    rt = x - pltpu.roll(cum, LANE - (GATE_F_LANE - GATE_I_LANE), 1)
    r1, r2, r3 = _split3(rt)
    rtt = _dot_nt(ident, r1) + _dot_nt(ident, r2) + _dot_nt(ident, r3)

    xcs, ks, qbs, vbs = [], [], [], []
    for h in range(HEADS):
        xp_ref[h, CONV_PAD:, :] = lx_ref[h]
        acc = jnp.zeros((n, HEAD_DIM), F32) + cb_ref[h]
        for j in range(CONV_WIDTH):
            off = CONV_PAD - (CONV_WIDTH - 1) + j
            acc = acc + xp_ref[h, off:off + n, :] * cw_ref[h, j:j + 1, :]
        xp_ref[h, 0:CONV_PAD, :] = lx_ref[h, n - CONV_PAD:n, :]
        xc = _silu(acc)
        xcb = xc.astype(BF16)
        k = _dot(xcb, wk_ref[h]) * HEAD_DIM ** -0.5
        qb = _dot(xcb, wq_ref[h]).astype(BF16)
        s_ref[h] = _dot_nt(qb, k.astype(BF16))
        xcs.append(xc)
        ks.append(k)
        qbs.append(qb)
        vbs.append(jnp.concatenate([lv_ref[h], ones_nd], axis=-1))

    r_rows = [rtt[GATE_I_LANE + h:GATE_I_LANE + h + 1, :] for h in range(HEADS)]
    bases = [m_ref[h] for h in range(HEADS)]
    strips = [[] for _ in range(HEADS)]
    u_lasts = [None] * HEADS
    for i in range(n // STRIP):
        rows = slice(i * STRIP, (i + 1) * STRIP)
        d0 = (i * STRIP // LANE) * LANE
        w = d0 + LANE
        mask = (scol + d0) <= (srow + i * STRIP)
        for h in range(HEADS):
            r_row = r_rows[h]
            if d0 > 0 and (i * STRIP) % LANE == 0:
                bases[h] = jnp.maximum(bases[h], jnp.max(r_row[:, d0 - LANE:d0], axis=-1, keepdims=True))
            rmd = jnp.where(mask, r_row[:, d0:w], -jnp.inf)
            u_col = jnp.maximum(jnp.max(rmd, axis=-1, keepdims=True), bases[h])
            u = jnp.broadcast_to(u_col, (STRIP, LANE))
            p_ref[h, rows, d0:w] = (s_ref[h, rows, d0:w] * jnp.exp2(rmd - u)).astype(BF16)
            for c in range(d0 // LANE):
                cols = slice(c * LANE, (c + 1) * LANE)
                p_ref[h, rows, cols] = (s_ref[h, rows, cols] * jnp.exp2(r_row[:, cols] - u)).astype(BF16)
            if w < n:
                p_ref[h, rows, w:n] = jnp.zeros((STRIP, n - w), BF16)
            strips[h].append(u)
            u_lasts[h] = u_col[STRIP - 1:STRIP, :]
    us = [jnp.concatenate(st_h, axis=0) for st_h in strips]

    cells = []
    for h in range(HEADS):
        li = GATE_I_LANE + h
        lf = GATE_F_LANE + h
        m_st = m_ref[h]
        u, u_last, vb = us[h], u_lasts[h], vbs[h]
        w_inter = jnp.exp2(m_st - u)
        st = st_ref[h]
        intra = _dot(p_ref[h], vb)
        inter = _dot(qbs[h], st.astype(BF16))
        num = intra[:, :LANE] + w_inter * inter[:, :LANE]
        den = intra[:, LANE:] + w_inter * inter[:, LANE:]
        cum_f = jnp.broadcast_to(cum[:, lf:lf + 1], (n, LANE))
        cells.append(num / jnp.maximum(jnp.abs(den), jnp.exp2(-(cum_f + u))))

        decay = jnp.exp2(m_st - u_last)
        kw = ks[h] * jnp.exp2(jnp.broadcast_to(rt[:, li:li + 1], (n, LANE)) - u_last)
        upd = _dot(kw.T.astype(BF16), vb)
        st_ref[h] = decay * st + upd
        m_ref[h] = cum[n - 1:n, lf:lf + 1] + u_last

    for h in range(HEADS):
        cell = cells[h] * jax.nn.sigmoid(lo_ref[h].astype(F32))
        mu = _rowsum_mxu(cell, mean_dd)
        dev = cell - mu
        var = _rowsum_mxu(dev * dev, mean_dd)
        hn = dev * lax.rsqrt(var + LN_EPS) * g_ref[h]
        o_ref[h] = ((hn + skip_ref[h] * xcs[h]) * _silu(z_ref[h])).astype(BF16)


def _prep_mlstm(conv_w, conv_b, w_q, w_k, i_bias, f_bias, skip, norm_g):
    nl = conv_w.shape[0]
    cw = conv_w.reshape(nl, CONV_WIDTH, HEADS, HEAD_DIM).transpose(0, 2, 1, 3)
    cb = conv_b.reshape(nl, HEADS, 1, HEAD_DIM)
    zeros = lambda w: jnp.zeros((nl, w), F32)
    gb = jnp.concatenate([zeros(GATE_I_LANE), i_bias, f_bias, zeros(LANE - GATE_F_LANE - HEADS)], axis=-1)
    return (cw, cb, w_q.astype(BF16), w_k.astype(BF16), gb.reshape(nl, 1, LANE),
            skip.reshape(nl, HEADS, 1, HEAD_DIM), norm_g.reshape(nl, HEADS, 1, HEAD_DIM))


def _mlstm(hb16, hb32, params, layer):
    nchunk = SEQ // RCHUNK
    return pl.pallas_call(
        _mlstm_kernel,
        out_shape=jax.ShapeDtypeStruct((HEADS, ROWS, LANE), BF16),
        grid=(BATCH, nchunk),
        in_specs=[_group_spec(BLK_LX, RCHUNK), _group_spec(BLK_LV, RCHUNK), _group_spec(BLK_LO, RCHUNK),
                  pl.BlockSpec((1, RCHUNK, LANE), lambda b, c: (BLK_KPE, b * nchunk + c, 0)),
                  _group_spec(BLK_ZC, RCHUNK)] + [_layer_spec(p.shape, layer) for p in params],
        out_specs=pl.BlockSpec((HEADS, RCHUNK, LANE), lambda b, c: (0, b * nchunk + c, 0)),
        scratch_shapes=[pltpu.VMEM((HEADS, RCHUNK + CONV_PAD, HEAD_DIM), F32),
                        pltpu.VMEM((HEADS, HEAD_DIM, 2 * HEAD_DIM), F32),
                        pltpu.VMEM((HEADS, 1, 1), F32),
                        pltpu.VMEM((HEADS, RCHUNK, RCHUNK), F32),
                        pltpu.VMEM((HEADS, RCHUNK, RCHUNK), BF16)],
        compiler_params=_cparams(("parallel", "arbitrary")),
        name="mlstm",
    )(hb32, hb16, hb16, hb32, hb32, *params)


def _mla_prep_kernel(lat_ref, kpe_ref, cm_ref, sm_ref, qg_ref, wuq_ref, kvg_ref, wukv_ref,
                     q_out, k_out, v_out):
    scale = (MLA_NOPE + MLA_ROPE) ** -0.5 * LOG2E
    cs = cm_ref[0]
    sn = sm_ref[0]
    cq = jnp.concatenate([lat_ref[0], lat_ref[1], lat_ref[2]], axis=-1).astype(F32)
    qn = cq * lax.rsqrt(jnp.mean(jnp.square(cq), axis=-1, keepdims=True) + RMS_EPS) * qg_ref[...]
    q = _dot(qn.astype(BF16), wuq_ref[...])
    ckv = lat_ref[3].astype(F32)
    kvn = ckv * lax.rsqrt(jnp.mean(jnp.square(ckv), axis=-1, keepdims=True) + RMS_EPS) * kvg_ref[...]
    kv = _dot(kvn.astype(BF16), wukv_ref[...])
    kpe = kpe_ref[0]
    krot = (kpe * cs + pltpu.roll(kpe, LANE // 2, 1) * sn).astype(BF16)
    for h in range(HEADS):
        qr = q[:, GROUP_WIDTH + h * LANE:GROUP_WIDTH + (h + 1) * LANE]
        qr = qr * cs + pltpu.roll(qr, LANE // 2, 1) * sn
        q_out[0, h, :, 0:LANE] = (q[:, h * LANE:(h + 1) * LANE] * scale).astype(BF16)
        q_out[0, h, :, LANE:2 * LANE] = (qr * scale).astype(BF16)
        k_out[0, h, :, 0:LANE] = kv[:, h * LANE:(h + 1) * LANE].astype(BF16)
        k_out[0, h, :, LANE:2 * LANE] = krot
        v_out[0, h] = kv[:, GROUP_WIDTH + h * LANE:GROUP_WIDTH + (h + 1) * LANE].astype(BF16)


def _prep_mla_weights(w_uq, w_ukv):
    lead = w_uq.shape[:-1]
    wq = w_uq.reshape(lead + (HEADS, MLA_NOPE + MLA_ROPE))
    nope = wq[..., :MLA_NOPE].reshape(lead + (GROUP_WIDTH,))
    half = MLA_ROPE // 2
    zeros = jnp.zeros(lead + (HEADS, 64 - half), w_uq.dtype)
    rope = jnp.concatenate([wq[..., MLA_NOPE:MLA_NOPE + half], zeros, wq[..., MLA_NOPE + half:], zeros], axis=-1)
    wq_p = jnp.concatenate([nope, rope.reshape(lead + (HEADS * LANE,))], axis=-1).astype(BF16)
    lead = w_ukv.shape[:-1]
    wkv = w_ukv.reshape(lead + (HEADS, MLA_NOPE + HEAD_DIM))
    wkv_p = jnp.concatenate([wkv[..., :MLA_NOPE].reshape(lead + (GROUP_WIDTH,)),
                             wkv[..., MLA_NOPE:].reshape(lead + (GROUP_WIDTH,))], axis=-1).astype(BF16)
    return wq_p, wkv_p


def _attn_tile(nfull, q_ref, k_ref, v_ref, z_ref, o_ref, s_ref, p_ref, m_ref):
    kvlen = (nfull + 1) * TQ_ATT
    d0 = nfull * TQ_ATT
    hq = TQ_ATT // 2
    srow = lax.broadcasted_iota(jnp.int32, (STRIP, LANE), 0)
    scol = lax.broadcasted_iota(jnp.int32, (STRIP, LANE), 1)

    def strip_blocks(g, i):
        rows = slice(i * STRIP, (i + 1) * STRIP)
        wd = -(-(i + 1) * STRIP // LANE) * LANE
        ncol = (d0 + wd) // LANE
        blks = [s_ref[g, rows, c * LANE:(c + 1) * LANE] for c in range(ncol)]
        blks[-1] = jnp.where(scol + (wd - LANE) <= srow + i * STRIP, blks[-1], -jnp.inf)
        return rows, ncol, blks

    qrows = slice(d0, d0 + TQ_ATT)
    for g in range(ATT_HEADS):
        q = q_ref[0, g, qrows, :]
        for j in range(nfull):
            s_ref[g, :, j * TQ_ATT:(j + 1) * TQ_ATT] = _dot_nt(q, k_ref[0, g, j * TQ_ATT:(j + 1) * TQ_ATT, :])
        s_ref[g, 0:hq, d0:d0 + hq] = _dot_nt(q[0:hq], k_ref[0, g, d0:d0 + hq, :])
        s_ref[g, hq:TQ_ATT, d0:kvlen] = _dot_nt(q[hq:TQ_ATT], k_ref[0, g, d0:kvlen, :])
    for g in range(ATT_HEADS):
        for i in range(TQ_ATT // STRIP):
            rows, ncol, blks = strip_blocks(g, i)
            mx = blks[0]
            for blk in blks[1:]:
                mx = jnp.maximum(mx, blk)
            m_ref[g, rows, :] = jnp.broadcast_to(jnp.max(mx, axis=-1, keepdims=True), (STRIP, LANE))
        for i in range(TQ_ATT // STRIP):
            rows, ncol, blks = strip_blocks(g, i)
            m = m_ref[g, rows, :]
            for c, blk in enumerate(blks):
                p_ref[g, rows, c * LANE:(c + 1) * LANE] = jnp.exp2(blk - m).astype(BF16)
            kl = d0 + hq if (i + 1) * STRIP <= hq else kvlen
            if ncol * LANE < kl:
                p_ref[g, rows, ncol * LANE:kl] = jnp.zeros((STRIP, kl - ncol * LANE), BF16)
        v1 = jnp.concatenate([v_ref[0, g, 0:kvlen, :], jnp.ones((kvlen, LANE), BF16)], axis=-1)
        for r0, r1, kl in ((0, hq, d0 + hq), (hq, TQ_ATT, kvlen)):
            pv = _dot(p_ref[g, r0:r1, 0:kl], v1[0:kl])
            gate = _silu(z_ref[g, d0 + r0:d0 + r1, :].astype(F32))
            o_ref[g, d0 + r0:d0 + r1, :] = (pv[:, :LANE] / pv[:, LANE:] * gate).astype(BF16)


def _mla_attn_kernel(q_ref, k_ref, v_ref, z_ref, o_ref, s_ref, p_ref, m_ref):
    for nfull in range(SEQ // TQ_ATT):
        _attn_tile(nfull, q_ref, k_ref, v_ref, z_ref, o_ref, s_ref, p_ref, m_ref)


def _mla_attn(q, k, v, hb16):
    return pl.pallas_call(
        _mla_attn_kernel,
        out_shape=jax.ShapeDtypeStruct((HEADS, ROWS, LANE), BF16),
        grid=(BATCH, HEADS // ATT_HEADS),
        in_specs=[pl.BlockSpec((1, ATT_HEADS, SEQ, 2 * LANE), lambda b, h: (b, h, 0, 0)),
                  pl.BlockSpec((1, ATT_HEADS, SEQ, 2 * LANE), lambda b, h: (b, h, 0, 0)),
                  pl.BlockSpec((1, ATT_HEADS, SEQ, LANE), lambda b, h: (b, h, 0, 0)),
                  pl.BlockSpec((ATT_HEADS, SEQ, LANE), lambda b, h: (BLK_ZB // ATT_HEADS + h, b, 0))],
        out_specs=pl.BlockSpec((ATT_HEADS, SEQ, LANE), lambda b, h: (h, b, 0)),
        scratch_shapes=[pltpu.VMEM((ATT_HEADS, TQ_ATT, SEQ), F32), pltpu.VMEM((ATT_HEADS, TQ_ATT, SEQ), BF16),
                        pltpu.VMEM((ATT_HEADS, TQ_ATT, LANE), F32)],
        compiler_params=_cparams(("parallel", "parallel")),
        name="mla_attn",
    )(q, k, v, hb16)


def _mem_attn_kernel(q_ref, k_ref, v_ref, z_ref, o_ref):
    ones = jnp.ones((MEM_LEN, LANE), BF16)
    scores = []
    for h in range(HEADS):
        q = (q_ref[h].astype(F32) * (HEAD_DIM ** -0.5 * LOG2E)).astype(BF16)
        scores.append(_dot_nt(q, k_ref[:, h * LANE:(h + 1) * LANE]))
    for h in range(HEADS):
        s = scores[h]
        p = jnp.exp2(s - jnp.max(s, axis=-1, keepdims=True)).astype(BF16)
        pv = _dot(p, jnp.concatenate([v_ref[:, h * LANE:(h + 1) * LANE], ones], axis=-1))
        o_ref[h] = (pv[:, :LANE] / pv[:, LANE:] * _silu(z_ref[h].astype(F32))).astype(BF16)


def _light_kernel(rq, rk, rv, rz, cr, sr, gq, gk, gc, rg, mq, mk, mv, mz, lat, kpe, cm, sm, qg, wuq, kvg, wukv,
                  ya, yd, q_out, k_out, v_out, st_ref):
    _ret_kernel(rq, rk, rv, rz, cr, sr, gq, gk, gc, rg, ya, st_ref)
    _mem_attn_kernel(mq, mk, mv, mz, yd)
    _mla_prep_kernel(lat, kpe, cm, sm, qg, wuq, kvg, wukv, q_out, k_out, v_out)


def _light_mixers(hb16, hb32, tables, ret_consts, ret_g, kvm, qg, wuq_p, kvg, wukv_p, layer):
    cos_r, sin_r, cos_m, sin_m = tables
    gq, gk, gc = ret_consts
    nchunk = SEQ // RCHUNK
    tab = pl.BlockSpec((1, RCHUNK, LANE), lambda b, c: (b, c, 0))
    row_blk = lambda blk: pl.BlockSpec((1, RCHUNK, LANE), lambda b, c: (blk, b * nchunk + c, 0))
    y_shape = jax.ShapeDtypeStruct((HEADS, ROWS, LANE), BF16)
    y_spec = pl.BlockSpec((HEADS, RCHUNK, LANE), lambda b, c: (0, b * nchunk + c, 0))
    qk_shape = jax.ShapeDtypeStruct((BATCH, HEADS, SEQ, 2 * LANE), BF16)
    v_shape = jax.ShapeDtypeStruct((BATCH, HEADS, SEQ, LANE), BF16)
    qk_spec = pl.BlockSpec((1, HEADS, RCHUNK, 2 * LANE), lambda b, c: (b, 0, c, 0))
    v_spec = pl.BlockSpec((1, HEADS, RCHUNK, LANE), lambda b, c: (b, 0, c, 0))
    return pl.pallas_call(
        _light_kernel,
        out_shape=(y_shape, y_shape, qk_shape, qk_shape, v_shape),
        grid=(BATCH, nchunk),
        in_specs=[_group_spec(BLK_RQ, RCHUNK), _group_spec(BLK_RK, RCHUNK), _group_spec(BLK_RV, RCHUNK),
                  _group_spec(BLK_ZA, RCHUNK), tab, tab,
                  _const_spec(gq.shape), _const_spec(gk.shape), _const_spec(gc.shape),
                  _layer_spec(ret_g.shape, layer),
                  _group_spec(BLK_MQ, RCHUNK),
                  pl.BlockSpec((MEM_LEN, GROUP_WIDTH), lambda b, c: (b, 2 * layer)),
                  pl.BlockSpec((MEM_LEN, GROUP_WIDTH), lambda b, c: (b, 2 * layer + 1)),
                  _group_spec(BLK_ZD, RCHUNK),
                  _group_spec(BLK_CQ, RCHUNK), row_blk(BLK_KPE), tab, tab,
                  _layer_spec(qg.shape, layer), _layer_spec(wuq_p.shape, layer),
                  _layer_spec(kvg.shape, layer), _layer_spec(wukv_p.shape, layer)],
        out_specs=(y_spec, y_spec, qk_spec, qk_spec, v_spec),
        scratch_shapes=[pltpu.VMEM((HEADS, HEAD_DIM, HEAD_DIM), F32)],
        compiler_params=_cparams(("parallel", "arbitrary")),
        name="light_mixers",
    )(hb16, hb16, hb16, hb16, cos_r, sin_r, gq, gk, gc, ret_g, hb16, kvm, kvm, hb16, hb16, hb32, cos_m, sin_m,
      qg, wuq_p, kvg, wukv_p)


def _out_kernel(ya_ref, yb_ref, yc_ref, yd_ref, w_ref, x_ref, g_ref, b_ref, o_ref, *bf16_out):
    for t in range(TM_OUT // SUB_OUT):
        rows = slice(t * SUB_OUT, (t + 1) * SUB_OUT)
        parts = [ref[h, rows, :] for ref in (ya_ref, yb_ref, yc_ref, yd_ref) for h in range(HEADS)]
        y = jnp.concatenate(parts, axis=-1)
        r = DEEPNORM_ALPHA * x_ref[rows, :] + _dot(y, w_ref[...])
        mu = jnp.mean(r, axis=-1, keepdims=True)
        var = jnp.mean(jnp.square(r - mu), axis=-1, keepdims=True)
        out = (r - mu) * lax.rsqrt(var + LN_EPS) * g_ref[...] + b_ref[...]
        o_ref[rows, :] = out
        for ob_ref in bf16_out:
            ob_ref[rows, :] = out.astype(BF16)


def _outproj(ya, yb, yc, yd, w_out, x2d, ln_g, ln_b, layer, with_bf16):
    yspec = pl.BlockSpec((HEADS, TM_OUT, LANE), lambda i: (0, i, 0))
    xspec = pl.BlockSpec((TM_OUT, D_MODEL), lambda i: (i, 0))
    dtypes = (F32, BF16) if with_bf16 else (F32,)
    return pl.pallas_call(
        _out_kernel,
        out_shape=tuple(jax.ShapeDtypeStruct((ROWS, D_MODEL), dt) for dt in dtypes),
        grid=(ROWS // TM_OUT,),
        in_specs=[yspec, yspec, yspec, yspec,
                  pl.BlockSpec((None,) + tuple(w_out.shape[1:]), lambda i: (layer, 0, 0),
                               pipeline_mode=pl.Buffered(1)),
                  xspec, _layer_spec(ln_g.shape, layer), _layer_spec(ln_b.shape, layer)],
        out_specs=tuple(xspec for _ in dtypes),
        compiler_params=_cparams(("parallel",)),
        name="outproj_ln",
    )(ya, yb, yc, yd, w_out, x2d, ln_g, ln_b)


def kernel(x, mem, positions, w_in, ret_norm_g, mla_q_norm_g, mla_w_uq, mla_kv_norm_g, mla_w_ukv, ml_conv_w, ml_conv_b, ml_w_q, ml_w_k, ml_i_bias, ml_f_bias, ml_skip, ml_norm_g, w_mem_kv, w_out, ln_g, ln_b):
    assert x.shape == (BATCH, SEQ, D_MODEL) and mem.shape == (BATCH, MEM_LEN, D_MODEL)
    tables = _rope_tables(positions)
    kvm = _mem_kv(mem, w_mem_kv)
    w_in_p = _prep_w_in(w_in)
    wuq_p, wukv_p = _prep_mla_weights(mla_w_uq, mla_w_ukv)
    w_out_b = w_out.astype(BF16)
    ret_consts = _ret_consts()
    ret_g = ret_norm_g.reshape(DEPTH, HEADS, 1, HEAD_DIM)
    mlstm_params = _prep_mlstm(ml_conv_w, ml_conv_b, ml_w_q, ml_w_k, ml_i_bias, ml_f_bias, ml_skip, ml_norm_g)
    qg = mla_q_norm_g.reshape(DEPTH, 1, MLA_Q_RANK)
    kvg = mla_kv_norm_g.reshape(DEPTH, 1, MLA_KV_RANK)
    lng = ln_g.reshape(DEPTH, 1, D_MODEL)
    lnb = ln_b.reshape(DEPTH, 1, D_MODEL)

    x2d = x.reshape(ROWS, D_MODEL)
    xb = []
    for l in range(DEPTH):
        hb16, hb32 = _inproj(xb[0] if xb else x2d, w_in_p, l)
        ya, yd, q, k, v = _light_mixers(hb16, hb32, tables, ret_consts, ret_g, kvm, qg, wuq_p, kvg, wukv_p, l)
        yb = _mla_attn(q, k, v, hb16)
        yc = _mlstm(hb16, hb32, mlstm_params, l)
        x2d, *xb = _outproj(ya, yb, yc, yd, w_out_b, x2d, lng, lnb, l, with_bf16=l + 1 < DEPTH)
    return x2d.reshape(BATCH, SEQ, D_MODEL)
```

```python
import numpy as np
import jax
import jax.numpy as jnp
from jax import lax
from jax.experimental import pallas as pl
from jax.experimental.pallas import tpu as pltpu

F32 = jnp.float32
BF16 = jnp.bfloat16

D_MODEL = 2048
BATCH = 8
SEQ = 2048
DEPTH = 4
MEM_LEN = 256
HEAD_DIM = 128
HEADS = 4
GROUP_WIDTH = HEADS * HEAD_DIM
MLA_NOPE = 128
MLA_ROPE = 64
MLA_Q_RANK = 384
MLA_KV_RANK = 128
CONV_WIDTH = 4
MIX_WIDTH = 4 * GROUP_WIDTH
ROPE_THETA = 10000.0
LN_EPS = 1e-5
RMS_EPS = 1e-6
DEEPNORM_ALPHA = (2 * DEPTH) ** 0.25
IN_SPLITS = (512, 512, 512, MLA_Q_RANK, MLA_KV_RANK, MLA_ROPE, 512, 512, 512, HEADS, HEADS, 512, MIX_WIDTH)

LANE = 128
ROWS = BATCH * SEQ

BLK_RQ, BLK_RK, BLK_RV = 0, 4, 8
BLK_LV, BLK_LO = 12, 16
BLK_MQ = 20
BLK_ZA, BLK_ZB, BLK_ZD = 24, 28, 32
BLK_CQ, BLK_CKV = 36, 39
NBLK16 = 40
BLK_LX, BLK_ZC = 0, 4
BLK_KPE = 8
NBLK32 = 10
NBLK = NBLK16 + NBLK32
GATE_I_LANE, GATE_F_LANE = 32, 36

TM_IN = 512
TM_OUT = 512
SUB_OUT = 256
TQ_ATT = 512
ATT_HEADS = 2
RCHUNK = 512
CONV_PAD = 8
STRIP = 64
LOG2E = 1.4426950408889634
VMEM_LIMIT = 56 * 1024 * 1024


def _cparams(sem):
    return pltpu.CompilerParams(dimension_semantics=sem, vmem_limit_bytes=VMEM_LIMIT)


def _silu(z):
    return z * jax.nn.sigmoid(z)


def _log_sigmoid(x):
    return jnp.minimum(x, 0.0) - jnp.log1p(jnp.exp(-jnp.abs(x)))


def _split3(x):
    x1 = x.astype(BF16)
    r1 = x - x1.astype(F32)
    x2 = r1.astype(BF16)
    x3 = (r1 - x2.astype(F32)).astype(BF16)
    return x1, x2, x3


def _dot(a, b):
    return jnp.dot(a, b, preferred_element_type=F32)


def _rowsum_mxu(x, w):
    hi = x.astype(BF16)
    lo = (x - hi.astype(F32)).astype(BF16)
    return _dot(hi, w) + _dot(lo, w)


def _dot_nt(a, b):
    return lax.dot_general(a, b, (((1,), (1,)), ((), ())), preferred_element_type=F32)


def _tables_kernel(pos_ref, c_ref, cr_ref, sr_ref, cm_ref, sm_ref):
    pos = pos_ref[0].astype(F32)
    ang = pos * c_ref[0:1, :]
    cs = jnp.cos(ang)
    sn = jnp.sin(ang)
    cs_sw = pltpu.roll(cs, LANE // 2, 1)
    sn_sw = pltpu.roll(sn, LANE // 2, 1)
    lane = lax.broadcasted_iota(jnp.int32, cs.shape, 1)
    half_m = MLA_ROPE // 2
    first = lane < LANE // 2
    m_lo = lane < half_m
    m_hi = (lane >= LANE // 2) & (lane < LANE // 2 + half_m)
    cr_ref[0] = jnp.where(first, cs, cs_sw)
    sr_ref[0] = jnp.where(first, -sn, sn_sw)
    cm_ref[0] = jnp.where(m_lo, cs_sw, jnp.where(m_hi, cs, 0.0))
    sm_ref[0] = jnp.where(m_lo, -sn_sw, jnp.where(m_hi, sn, 0.0))


def _rope_tables(positions):
    half_r = HEAD_DIM // 2
    fr = ROPE_THETA ** (-jnp.arange(half_r, dtype=F32) / half_r)
    half_m = MLA_ROPE // 2
    fm = ROPE_THETA ** (-jnp.arange(half_m, dtype=F32) / half_m)
    freqs = jnp.concatenate([fr, fm, jnp.zeros((LANE - half_r - half_m,), F32)])
    consts = jnp.concatenate([freqs[None, :], jnp.zeros((7, LANE), F32)], axis=0)
    ts = 512
    tab = jax.ShapeDtypeStruct((BATCH, SEQ, LANE), F32)
    spec = pl.BlockSpec((1, ts, LANE), lambda b, i: (b, i, 0))
    return pl.pallas_call(
        _tables_kernel,
        out_shape=(tab, tab, tab, tab),
        grid=(BATCH, SEQ // ts),
        in_specs=[pl.BlockSpec((1, ts, 1), lambda b, i: (b, i, 0)),
                  pl.BlockSpec((8, LANE), lambda b, i: (0, 0))],
        out_specs=(spec, spec, spec, spec),
        compiler_params=_cparams(("parallel", "parallel")),
        name="rope_tables",
    )(positions.reshape(BATCH, SEQ, 1), consts)


def _mem_kv_kernel(a_ref, w_ref, o_ref, wb_ref):
    @pl.when(pl.program_id(1) == 0)
    def _():
        wb_ref[...] = w_ref[...].astype(BF16)

    o_ref[...] = _dot(a_ref[...].astype(BF16), wb_ref[...]).astype(BF16)


def _mem_kv(mem, w_mem_kv):
    a = mem.reshape(BATCH * MEM_LEN, D_MODEL)
    tm, tn = 1024, 2 * GROUP_WIDTH
    return pl.pallas_call(
        _mem_kv_kernel,
        out_shape=jax.ShapeDtypeStruct((BATCH * MEM_LEN, DEPTH * tn), BF16),
        grid=(DEPTH, BATCH * MEM_LEN // tm),
        in_specs=[pl.BlockSpec((tm, D_MODEL), lambda l, i: (i, 0)),
                  pl.BlockSpec((None, D_MODEL, tn), lambda l, i: (l, 0, 0))],
        out_specs=pl.BlockSpec((tm, tn), lambda l, i: (i, l)),
        scratch_shapes=[pltpu.VMEM((D_MODEL, tn), BF16)],
        compiler_params=_cparams(("parallel", "arbitrary")),
        name="mem_kv",
    )(a, w_mem_kv)


def _inproj_kernel(x_ref, w_ref, o16_ref, o32_ref):
    xb = x_ref[...].astype(BF16)
    for t in range(NBLK * LANE // 256):
        r = _dot_nt(xb, w_ref[t * 256:(t + 1) * 256, :])
        for half, blk in enumerate((2 * t, 2 * t + 1)):
            o_ref, k = (o16_ref, blk) if blk < NBLK16 else (o32_ref, blk - NBLK16)
            o_ref[k] = r[:, half * LANE:(half + 1) * LANE].astype(o_ref.dtype)


def _inproj(x2d, wt_p, layer):
    return pl.pallas_call(
        _inproj_kernel,
        out_shape=(jax.ShapeDtypeStruct((NBLK16, ROWS, LANE), BF16),
                   jax.ShapeDtypeStruct((NBLK32, ROWS, LANE), F32)),
        grid=(ROWS // TM_IN,),
        in_specs=[pl.BlockSpec((TM_IN, D_MODEL), lambda i: (i, 0)),
                  pl.BlockSpec((None, NBLK * LANE, D_MODEL), lambda i: (layer, 0, 0),
                               pipeline_mode=pl.Buffered(1))],
        out_specs=(pl.BlockSpec((NBLK16, TM_IN, LANE), lambda i: (0, i, 0)),
                   pl.BlockSpec((NBLK32, TM_IN, LANE), lambda i: (0, i, 0))),
        compiler_params=_cparams(("parallel",)),
        name="inproj",
    )(x2d, wt_p)


def _w_in_groups():
    off = np.concatenate([[0], np.cumsum(IN_SPLITS)]).tolist()
    (o_rq, o_rk, o_rv, o_cq, _, o_kpe, o_lx, o_lv, o_lo, o_li, _, o_mq, o_z, _) = off
    groups = [o_rq, o_rk, o_rv, o_lv, o_lo, o_mq, o_z, o_z + GROUP_WIDTH, o_z + 3 * GROUP_WIDTH,
              o_cq, o_lx, o_z + 2 * GROUP_WIDTH]
    return groups, o_kpe, o_li


def _w_in_kernel(tbl_ref, w_ref, kpe_ref, gate_ref, o_ref):
    del tbl_ref
    ngroup = NBLK // HEADS

    @pl.when(pl.program_id(1) < ngroup)
    def _():
        o_ref[0] = w_ref[0].astype(BF16)

    @pl.when(pl.program_id(1) == ngroup)
    def _():
        half = MLA_ROPE // 2
        zeros = lambda n: jnp.zeros((n, D_MODEL), F32)
        kpe = kpe_ref[0]
        blk = jnp.concatenate([kpe[0:half], gate_ref[0], zeros(64 - half - 2 * HEADS),
                               kpe[half:2 * half], zeros(64 - half)], axis=0)
        o_ref[0, 0:LANE, :] = blk.astype(BF16)
        o_ref[0, LANE:, :] = jnp.zeros((GROUP_WIDTH - LANE, D_MODEL), BF16)


def _prep_w_in(w_in):
    wt = jnp.swapaxes(w_in, 1, 2)
    groups, o_kpe, o_li = _w_in_groups()
    assert len(groups) * HEADS + 2 == NBLK and o_kpe % LANE == 0 and o_li % (2 * HEADS) == 0
    sub = 8
    assert all(g % sub == 0 for g in groups)
    table = jnp.asarray([g // sub for g in groups] + [0], jnp.int32)
    grid_spec = pltpu.PrefetchScalarGridSpec(
        num_scalar_prefetch=1,
        grid=(DEPTH, len(groups) + 1),
        in_specs=[pl.BlockSpec((pl.Element(1), pl.Element(GROUP_WIDTH), pl.Element(D_MODEL)),
                               lambda l, j, tbl: (l, pl.multiple_of(tbl[j] * sub, sub), 0)),
                  pl.BlockSpec((1, LANE, D_MODEL), lambda l, j, tbl: (l, o_kpe // LANE, 0)),
                  pl.BlockSpec((1, 2 * HEADS, D_MODEL), lambda l, j, tbl: (l, o_li // (2 * HEADS), 0))],
        out_specs=pl.BlockSpec((1, GROUP_WIDTH, D_MODEL), lambda l, j, tbl: (l, j, 0)),
    )
    return pl.pallas_call(
        _w_in_kernel,
        out_shape=jax.ShapeDtypeStruct((DEPTH, NBLK * LANE, D_MODEL), BF16),
        grid_spec=grid_spec,
        compiler_params=_cparams(("parallel", "arbitrary")),
        name="w_in_relayout",
    )(table, wt, wt, wt)


def _group_spec(base, rows):
    nchunk = SEQ // rows
    return pl.BlockSpec((HEADS, rows, LANE), lambda b, c: (base // HEADS, b * nchunk + c, 0))


def _layer_spec(shape, layer):
    nd = len(shape) - 1
    return pl.BlockSpec((None,) + tuple(shape[1:]), lambda *_: (layer,) + (0,) * nd)


def _const_spec(shape):
    nd = len(shape)
    return pl.BlockSpec(tuple(shape), lambda *_: (0,) * nd)


def _ret_kernel(q_ref, k_ref, v_ref, z_ref, cos_ref, sin_ref, gq_ref, gk_ref, gc_ref, g_ref, o_ref, st_ref):
    @pl.when(pl.program_id(1) == 0)
    def _():
        st_ref[...] = jnp.zeros_like(st_ref)

    cs = cos_ref[0]
    sn = sin_ref[0]
    row = lax.broadcasted_iota(jnp.int32, (RCHUNK, RCHUNK), 0)
    col = lax.broadcasted_iota(jnp.int32, (RCHUNK, RCHUNK), 1)
    causal = col <= row
    qbs, kts, vbs, scs = [], [], [], []
    for h in range(HEADS):
        q = q_ref[h].astype(F32)
        q = (q * cs + pltpu.roll(q, HEAD_DIM // 2, 1) * sn) * gq_ref[h]
        k = k_ref[h].astype(F32)
        k = (k * cs + pltpu.roll(k, HEAD_DIM // 2, 1) * sn) * gk_ref[h]
        qb = q.astype(BF16)
        scs.append(_dot_nt(qb, k.astype(BF16)))
        qbs.append(qb)
        kts.append(k.T.astype(BF16))
        vbs.append(v_ref[h])
    outs = []
    for h in range(HEADS):
        st = st_ref[h]
        sc = jnp.where(causal, scs[h], 0.0).astype(BF16)
        outs.append(_dot(sc, vbs[h]) + _dot(qbs[h], st.astype(BF16)))
        st_ref[h] = gc_ref[h] * (st + _dot(kts[h], vbs[h]))
    for h in range(HEADS):
        out = outs[h]
        mu = jnp.mean(out, axis=-1, keepdims=True)
        dev = out - mu
        var = jnp.mean(dev * dev, axis=-1, keepdims=True)
        hn = dev * lax.rsqrt(var + LN_EPS) * g_ref[h]
        o_ref[h] = (hn * _silu(z_ref[h].astype(F32))).astype(BF16)


def _ret_consts():
    log_g = jnp.log1p(-jnp.exp2(-5.0 - jnp.arange(HEADS, dtype=F32)))
    idx = jnp.arange(RCHUNK, dtype=F32)
    full = (HEADS, RCHUNK, HEAD_DIM)
    gq = jnp.broadcast_to(jnp.exp(log_g[:, None] * (idx + 1.0))[..., None], full)
    gk = jnp.broadcast_to((jnp.exp(-log_g[:, None] * (idx + 1.0)) * HEAD_DIM ** -0.5)[..., None], full)
    gc = jnp.broadcast_to(jnp.exp(log_g * RCHUNK)[:, None, None], (HEADS, 1, HEAD_DIM))
    return gq, gk, gc


def _mlstm_kernel(lx_ref, lv_ref, lo_ref, gt_ref, z_ref, cw_ref, cb_ref, wq_ref, wk_ref, gb_ref,
                  skip_ref, g_ref, o_ref, xp_ref, st_ref, m_ref, s_ref, p_ref):
    @pl.when(pl.program_id(1) == 0)
    def _():
        xp_ref[:, 0:CONV_PAD, :] = jnp.zeros((HEADS, CONV_PAD, HEAD_DIM), F32)
        st_ref[...] = jnp.zeros_like(st_ref)
        m_ref[...] = jnp.zeros_like(m_ref)

    n = RCHUNK
    ones_nd = jnp.ones((n, HEAD_DIM), BF16)
    mean_dd = jnp.full((HEAD_DIM, HEAD_DIM), 1.0 / HEAD_DIM, BF16)
    row = lax.broadcasted_iota(jnp.int32, (n, n), 0)
    col = lax.broadcasted_iota(jnp.int32, (n, n), 1)
    tril = jnp.where(col <= row, 1.0, 0.0).astype(BF16)
    r128 = lax.broadcasted_iota(jnp.int32, (LANE, LANE), 0)
    c128 = lax.broadcasted_iota(jnp.int32, (LANE, LANE), 1)
    ident = jnp.where(r128 == c128, 1.0, 0.0).astype(BF16)
    srow = lax.broadcasted_iota(jnp.int32, (STRIP, LANE), 0)
    scol = lax.broadcasted_iota(jnp.int32, (STRIP, LANE), 1)

    lane = lax.broadcasted_iota(jnp.int32, (n, LANE), 1)
    x = gt_ref[0] + gb_ref[...]
    x = jnp.where((lane >= GATE_F_LANE) & (lane < GATE_F_LANE + HEADS), _log_sigmoid(x), x) * LOG2E
    x1, x2, x3 = _split3(x)
    cum = _dot(tril, x1) + _dot(tril, x2) + _dot(tril, x3)
    rt = x - pltpu.roll(cum, LANE - (GATE_F_LANE - GATE_I_LANE), 1)
    r1, r2, r3 = _split3(rt)
    rtt = _dot_nt(ident, r1) + _dot_nt(ident, r2) + _dot_nt(ident, r3)

    xcs, ks, qbs, vbs = [], [], [], []
    for h in range(HEADS):
        xp_ref[h, CONV_PAD:, :] = lx_ref[h]
        acc = jnp.zeros((n, HEAD_DIM), F32) + cb_ref[h]
        for j in range(CONV_WIDTH):
            off = CONV_PAD - (CONV_WIDTH - 1) + j
            acc = acc + xp_ref[h, off:off + n, :] * cw_ref[h, j:j + 1, :]
        xp_ref[h, 0:CONV_PAD, :] = lx_ref[h, n - CONV_PAD:n, :]
        xc = _silu(acc)
        xcb = xc.astype(BF16)
        k = _dot(xcb, wk_ref[h]) * HEAD_DIM ** -0.5
        qb = _dot(xcb, wq_ref[h]).astype(BF16)
        s_ref[h] = _dot_nt(qb, k.astype(BF16))
        xcs.append(xc)
        ks.append(k)
        qbs.append(qb)
        vbs.append(jnp.concatenate([lv_ref[h], ones_nd], axis=-1))

    r_rows = [rtt[GATE_I_LANE + h:GATE_I_LANE + h + 1, :] for h in range(HEADS)]
    bases = [m_ref[h] for h in range(HEADS)]
    strips = [[] for _ in range(HEADS)]
    u_lasts = [None] * HEADS
    for i in range(n // STRIP):
        rows = slice(i * STRIP, (i + 1) * STRIP)
        d0 = (i * STRIP // LANE) * LANE
        w = d0 + LANE
        mask = (scol + d0) <= (srow + i * STRIP)
        for h in range(HEADS):
            r_row = r_rows[h]
            if d0 > 0 and (i * STRIP) % LANE == 0:
                bases[h] = jnp.maximum(bases[h], jnp.max(r_row[:, d0 - LANE:d0], axis=-1, keepdims=True))
            rmd = jnp.where(mask, r_row[:, d0:w], -jnp.inf)
            u_col = jnp.maximum(jnp.max(rmd, axis=-1, keepdims=True), bases[h])
            u = jnp.broadcast_to(u_col, (STRIP, LANE))
            p_ref[h, rows, d0:w] = (s_ref[h, rows, d0:w] * jnp.exp2(rmd - u)).astype(BF16)
            for c in range(d0 // LANE):
                cols = slice(c * LANE, (c + 1) * LANE)
                p_ref[h, rows, cols] = (s_ref[h, rows, cols] * jnp.exp2(r_row[:, cols] - u)).astype(BF16)
            if w < n:
                p_ref[h, rows, w:n] = jnp.zeros((STRIP, n - w), BF16)
            strips[h].append(u)
            u_lasts[h] = u_col[STRIP - 1:STRIP, :]
    us = [jnp.concatenate(st_h, axis=0) for st_h in strips]

    cells = []
    for h in range(HEADS):
        li = GATE_I_LANE + h
        lf = GATE_F_LANE + h
        m_st = m_ref[h]
        u, u_last, vb = us[h], u_lasts[h], vbs[h]
        w_inter = jnp.exp2(m_st - u)
        st = st_ref[h]
        intra = _dot(p_ref[h], vb)
        inter = _dot(qbs[h], st.astype(BF16))
        num = intra[:, :LANE] + w_inter * inter[:, :LANE]
        den = intra[:, LANE:] + w_inter * inter[:, LANE:]
        cum_f = jnp.broadcast_to(cum[:, lf:lf + 1], (n, LANE))
        cells.append(num / jnp.maximum(jnp.abs(den), jnp.exp2(-(cum_f + u))))

        decay = jnp.exp2(m_st - u_last)
        kw = ks[h] * jnp.exp2(jnp.broadcast_to(rt[:, li:li + 1], (n, LANE)) - u_last)
        upd = _dot(kw.T.astype(BF16), vb)
        st_ref[h] = decay * st + upd
        m_ref[h] = cum[n - 1:n, lf:lf + 1] + u_last

    for h in range(HEADS):
        cell = cells[h] * jax.nn.sigmoid(lo_ref[h].astype(F32))
        mu = _rowsum_mxu(cell, mean_dd)
        dev = cell - mu
        var = _rowsum_mxu(dev * dev, mean_dd)
        hn = dev * lax.rsqrt(var + LN_EPS) * g_ref[h]
        o_ref[h] = ((hn + skip_ref[h] * xcs[h]) * _silu(z_ref[h])).astype(BF16)


def _prep_mlstm(conv_w, conv_b, w_q, w_k, i_bias, f_bias, skip, norm_g):
    nl = conv_w.shape[0]
    cw = conv_w.reshape(nl, CONV_WIDTH, HEADS, HEAD_DIM).transpose(0, 2, 1, 3)
    cb = conv_b.reshape(nl, HEADS, 1, HEAD_DIM)
    zeros = lambda w: jnp.zeros((nl, w), F32)
    gb = jnp.concatenate([zeros(GATE_I_LANE), i_bias, f_bias, zeros(LANE - GATE_F_LANE - HEADS)], axis=-1)
    return (cw, cb, w_q.astype(BF16), w_k.astype(BF16), gb.reshape(nl, 1, LANE),
            skip.reshape(nl, HEADS, 1, HEAD_DIM), norm_g.reshape(nl, HEADS, 1, HEAD_DIM))


def _mlstm(hb16, hb32, params, layer):
    nchunk = SEQ // RCHUNK
    return pl.pallas_call(
        _mlstm_kernel,
        out_shape=jax.ShapeDtypeStruct((HEADS, ROWS, LANE), BF16),
        grid=(BATCH, nchunk),
        in_specs=[_group_spec(BLK_LX, RCHUNK), _group_spec(BLK_LV, RCHUNK), _group_spec(BLK_LO, RCHUNK),
                  pl.BlockSpec((1, RCHUNK, LANE), lambda b, c: (BLK_KPE, b * nchunk + c, 0)),
                  _group_spec(BLK_ZC, RCHUNK)] + [_layer_spec(p.shape, layer) for p in params],
        out_specs=pl.BlockSpec((HEADS, RCHUNK, LANE), lambda b, c: (0, b * nchunk + c, 0)),
        scratch_shapes=[pltpu.VMEM((HEADS, RCHUNK + CONV_PAD, HEAD_DIM), F32),
                        pltpu.VMEM((HEADS, HEAD_DIM, 2 * HEAD_DIM), F32),
                        pltpu.VMEM((HEADS, 1, 1), F32),
                        pltpu.VMEM((HEADS, RCHUNK, RCHUNK), F32),
                        pltpu.VMEM((HEADS, RCHUNK, RCHUNK), BF16)],
        compiler_params=_cparams(("parallel", "arbitrary")),
        name="mlstm",
    )(hb32, hb16, hb16, hb32, hb32, *params)


def _mla_prep_kernel(lat_ref, kpe_ref, cm_ref, sm_ref, qg_ref, wuq_ref, kvg_ref, wukv_ref,
                     q_out, k_out, v_out):
    scale = (MLA_NOPE + MLA_ROPE) ** -0.5 * LOG2E
    cs = cm_ref[0]
    sn = sm_ref[0]
    cq = jnp.concatenate([lat_ref[0], lat_ref[1], lat_ref[2]], axis=-1).astype(F32)
    qn = cq * lax.rsqrt(jnp.mean(jnp.square(cq), axis=-1, keepdims=True) + RMS_EPS) * qg_ref[...]
    q = _dot(qn.astype(BF16), wuq_ref[...])
    ckv = lat_ref[3].astype(F32)
    kvn = ckv * lax.rsqrt(jnp.mean(jnp.square(ckv), axis=-1, keepdims=True) + RMS_EPS) * kvg_ref[...]
    kv = _dot(kvn.astype(BF16), wukv_ref[...])
    kpe = kpe_ref[0]
    krot = (kpe * cs + pltpu.roll(kpe, LANE // 2, 1) * sn).astype(BF16)
    for h in range(HEADS):
        qr = q[:, GROUP_WIDTH + h * LANE:GROUP_WIDTH + (h + 1) * LANE]
        qr = qr * cs + pltpu.roll(qr, LANE // 2, 1) * sn
        q_out[0, h, :, 0:LANE] = (q[:, h * LANE:(h + 1) * LANE] * scale).astype(BF16)
        q_out[0, h, :, LANE:2 * LANE] = (qr * scale).astype(BF16)
        k_out[0, h, :, 0:LANE] = kv[:, h * LANE:(h + 1) * LANE].astype(BF16)
        k_out[0, h, :, LANE:2 * LANE] = krot
        v_out[0, h] = kv[:, GROUP_WIDTH + h * LANE:GROUP_WIDTH + (h + 1) * LANE].astype(BF16)


def _prep_mla_weights(w_uq, w_ukv):
    lead = w_uq.shape[:-1]
    wq = w_uq.reshape(lead + (HEADS, MLA_NOPE + MLA_ROPE))
    nope = wq[..., :MLA_NOPE].reshape(lead + (GROUP_WIDTH,))
    half = MLA_ROPE // 2
    zeros = jnp.zeros(lead + (HEADS, 64 - half), w_uq.dtype)
    rope = jnp.concatenate([wq[..., MLA_NOPE:MLA_NOPE + half], zeros, wq[..., MLA_NOPE + half:], zeros], axis=-1)
    wq_p = jnp.concatenate([nope, rope.reshape(lead + (HEADS * LANE,))], axis=-1).astype(BF16)
    lead = w_ukv.shape[:-1]
    wkv = w_ukv.reshape(lead + (HEADS, MLA_NOPE + HEAD_DIM))
    wkv_p = jnp.concatenate([wkv[..., :MLA_NOPE].reshape(lead + (GROUP_WIDTH,)),
                             wkv[..., MLA_NOPE:].reshape(lead + (GROUP_WIDTH,))], axis=-1).astype(BF16)
    return wq_p, wkv_p


def _attn_tile(nfull, q_ref, k_ref, v_ref, z_ref, o_ref, s_ref, p_ref, m_ref):
    kvlen = (nfull + 1) * TQ_ATT
    d0 = nfull * TQ_ATT
    hq = TQ_ATT // 2
    srow = lax.broadcasted_iota(jnp.int32, (STRIP, LANE), 0)
    scol = lax.broadcasted_iota(jnp.int32, (STRIP, LANE), 1)

    def strip_blocks(g, i):
        rows = slice(i * STRIP, (i + 1) * STRIP)
        wd = -(-(i + 1) * STRIP // LANE) * LANE
        ncol = (d0 + wd) // LANE
        blks = [s_ref[g, rows, c * LANE:(c + 1) * LANE] for c in range(ncol)]
        blks[-1] = jnp.where(scol + (wd - LANE) <= srow + i * STRIP, blks[-1], -jnp.inf)
        return rows, ncol, blks

    qrows = slice(d0, d0 + TQ_ATT)
    for g in range(ATT_HEADS):
        q = q_ref[0, g, qrows, :]
        for j in range(nfull):
            s_ref[g, :, j * TQ_ATT:(j + 1) * TQ_ATT] = _dot_nt(q, k_ref[0, g, j * TQ_ATT:(j + 1) * TQ_ATT, :])
        s_ref[g, 0:hq, d0:d0 + hq] = _dot_nt(q[0:hq], k_ref[0, g, d0:d0 + hq, :])
        s_ref[g, hq:TQ_ATT, d0:kvlen] = _dot_nt(q[hq:TQ_ATT], k_ref[0, g, d0:kvlen, :])
    for g in range(ATT_HEADS):
        for i in range(TQ_ATT // STRIP):
            rows, ncol, blks = strip_blocks(g, i)
            mx = blks[0]
            for blk in blks[1:]:
                mx = jnp.maximum(mx, blk)
            m_ref[g, rows, :] = jnp.broadcast_to(jnp.max(mx, axis=-1, keepdims=True), (STRIP, LANE))
        for i in range(TQ_ATT // STRIP):
            rows, ncol, blks = strip_blocks(g, i)
            m = m_ref[g, rows, :]
            for c, blk in enumerate(blks):
                p_ref[g, rows, c * LANE:(c + 1) * LANE] = jnp.exp2(blk - m).astype(BF16)
            kl = d0 + hq if (i + 1) * STRIP <= hq else kvlen
            if ncol * LANE < kl:
                p_ref[g, rows, ncol * LANE:kl] = jnp.zeros((STRIP, kl - ncol * LANE), BF16)
        v1 = jnp.concatenate([v_ref[0, g, 0:kvlen, :], jnp.ones((kvlen, LANE), BF16)], axis=-1)
        for r0, r1, kl in ((0, hq, d0 + hq), (hq, TQ_ATT, kvlen)):
            pv = _dot(p_ref[g, r0:r1, 0:kl], v1[0:kl])
            gate = _silu(z_ref[g, d0 + r0:d0 + r1, :].astype(F32))
            o_ref[g, d0 + r0:d0 + r1, :] = (pv[:, :LANE] / pv[:, LANE:] * gate).astype(BF16)


def _mla_attn_kernel(q_ref, k_ref, v_ref, z_ref, o_ref, s_ref, p_ref, m_ref):
    for nfull in range(SEQ // TQ_ATT):
        _attn_tile(nfull, q_ref, k_ref, v_ref, z_ref, o_ref, s_ref, p_ref, m_ref)


def _mla_attn(q, k, v, hb16):
    return pl.pallas_call(
        _mla_attn_kernel,
        out_shape=jax.ShapeDtypeStruct((HEADS, ROWS, LANE), BF16),
        grid=(BATCH, HEADS // ATT_HEADS),
        in_specs=[pl.BlockSpec((1, ATT_HEADS, SEQ, 2 * LANE), lambda b, h: (b, h, 0, 0)),
                  pl.BlockSpec((1, ATT_HEADS, SEQ, 2 * LANE), lambda b, h: (b, h, 0, 0)),
                  pl.BlockSpec((1, ATT_HEADS, SEQ, LANE), lambda b, h: (b, h, 0, 0)),
                  pl.BlockSpec((ATT_HEADS, SEQ, LANE), lambda b, h: (BLK_ZB // ATT_HEADS + h, b, 0))],
        out_specs=pl.BlockSpec((ATT_HEADS, SEQ, LANE), lambda b, h: (h, b, 0)),
        scratch_shapes=[pltpu.VMEM((ATT_HEADS, TQ_ATT, SEQ), F32), pltpu.VMEM((ATT_HEADS, TQ_ATT, SEQ), BF16),
                        pltpu.VMEM((ATT_HEADS, TQ_ATT, LANE), F32)],
        compiler_params=_cparams(("parallel", "parallel")),
        name="mla_attn",
    )(q, k, v, hb16)


def _mem_attn_kernel(q_ref, k_ref, v_ref, z_ref, o_ref):
    ones = jnp.ones((MEM_LEN, LANE), BF16)
    scores = []
    for h in range(HEADS):
        q = (q_ref[h].astype(F32) * (HEAD_DIM ** -0.5 * LOG2E)).astype(BF16)
        scores.append(_dot_nt(q, k_ref[:, h * LANE:(h + 1) * LANE]))
    for h in range(HEADS):
        s = scores[h]
        p = jnp.exp2(s - jnp.max(s, axis=-1, keepdims=True)).astype(BF16)
        pv = _dot(p, jnp.concatenate([v_ref[:, h * LANE:(h + 1) * LANE], ones], axis=-1))
        o_ref[h] = (pv[:, :LANE] / pv[:, LANE:] * _silu(z_ref[h].astype(F32))).astype(BF16)


def _light_kernel(rq, rk, rv, rz, cr, sr, gq, gk, gc, rg, mq, mk, mv, mz, lat, kpe, cm, sm, qg, wuq, kvg, wukv,
                  ya, yd, q_out, k_out, v_out, st_ref):
    _ret_kernel(rq, rk, rv, rz, cr, sr, gq, gk, gc, rg, ya, st_ref)
    _mem_attn_kernel(mq, mk, mv, mz, yd)
    _mla_prep_kernel(lat, kpe, cm, sm, qg, wuq, kvg, wukv, q_out, k_out, v_out)


def _light_mixers(hb16, hb32, tables, ret_consts, ret_g, kvm, qg, wuq_p, kvg, wukv_p, layer):
    cos_r, sin_r, cos_m, sin_m = tables
    gq, gk, gc = ret_consts
    nchunk = SEQ // RCHUNK
    tab = pl.BlockSpec((1, RCHUNK, LANE), lambda b, c: (b, c, 0))
    row_blk = lambda blk: pl.BlockSpec((1, RCHUNK, LANE), lambda b, c: (blk, b * nchunk + c, 0))
    y_shape = jax.ShapeDtypeStruct((HEADS, ROWS, LANE), BF16)
    y_spec = pl.BlockSpec((HEADS, RCHUNK, LANE), lambda b, c: (0, b * nchunk + c, 0))
    qk_shape = jax.ShapeDtypeStruct((BATCH, HEADS, SEQ, 2 * LANE), BF16)
    v_shape = jax.ShapeDtypeStruct((BATCH, HEADS, SEQ, LANE), BF16)
    qk_spec = pl.BlockSpec((1, HEADS, RCHUNK, 2 * LANE), lambda b, c: (b, 0, c, 0))
    v_spec = pl.BlockSpec((1, HEADS, RCHUNK, LANE), lambda b, c: (b, 0, c, 0))
    return pl.pallas_call(
        _light_kernel,
        out_shape=(y_shape, y_shape, qk_shape, qk_shape, v_shape),
        grid=(BATCH, nchunk),
        in_specs=[_group_spec(BLK_RQ, RCHUNK), _group_spec(BLK_RK, RCHUNK), _group_spec(BLK_RV, RCHUNK),
                  _group_spec(BLK_ZA, RCHUNK), tab, tab,
                  _const_spec(gq.shape), _const_spec(gk.shape), _const_spec(gc.shape),
                  _layer_spec(ret_g.shape, layer),
                  _group_spec(BLK_MQ, RCHUNK),
                  pl.BlockSpec((MEM_LEN, GROUP_WIDTH), lambda b, c: (b, 2 * layer)),
                  pl.BlockSpec((MEM_LEN, GROUP_WIDTH), lambda b, c: (b, 2 * layer + 1)),
                  _group_spec(BLK_ZD, RCHUNK),
                  _group_spec(BLK_CQ, RCHUNK), row_blk(BLK_KPE), tab, tab,
                  _layer_spec(qg.shape, layer), _layer_spec(wuq_p.shape, layer),
                  _layer_spec(kvg.shape, layer), _layer_spec(wukv_p.shape, layer)],
        out_specs=(y_spec, y_spec, qk_spec, qk_spec, v_spec),
        scratch_shapes=[pltpu.VMEM((HEADS, HEAD_DIM, HEAD_DIM), F32)],
        compiler_params=_cparams(("parallel", "arbitrary")),
        name="light_mixers",
    )(hb16, hb16, hb16, hb16, cos_r, sin_r, gq, gk, gc, ret_g, hb16, kvm, kvm, hb16, hb16, hb32, cos_m, sin_m,
      qg, wuq_p, kvg, wukv_p)


def _out_kernel(ya_ref, yb_ref, yc_ref, yd_ref, w_ref, x_ref, g_ref, b_ref, o_ref, *rest):
    *bf16_out, wb_ref = rest

    @pl.when(pl.program_id(0) == 0)
    def _():
        wb_ref[...] = w_ref[...].astype(BF16)

    for t in range(TM_OUT // SUB_OUT):
        rows = slice(t * SUB_OUT, (t + 1) * SUB_OUT)
        parts = [ref[h, rows, :] for ref in (ya_ref, yb_ref, yc_ref, yd_ref) for h in range(HEADS)]
        y = jnp.concatenate(parts, axis=-1)
        for s in range(D_MODEL // 256):
            cols = slice(s * 256, (s + 1) * 256)
            o_ref[rows, cols] = DEEPNORM_ALPHA * x_ref[rows, cols] + _dot(y, wb_ref[:, cols])
        r = o_ref[rows, :]
        mu = jnp.mean(r, axis=-1, keepdims=True)
        var = jnp.mean(jnp.square(r - mu), axis=-1, keepdims=True)
        out = (r - mu) * lax.rsqrt(var + LN_EPS) * g_ref[...] + b_ref[...]
        o_ref[rows, :] = out
        for ob_ref in bf16_out:
            ob_ref[rows, :] = out.astype(BF16)


def _outproj(ya, yb, yc, yd, w_out, x2d, ln_g, ln_b, layer, with_bf16):
    yspec = pl.BlockSpec((HEADS, TM_OUT, LANE), lambda i: (0, i, 0))
    xspec = pl.BlockSpec((TM_OUT, D_MODEL), lambda i: (i, 0))
    dtypes = (F32, BF16) if with_bf16 else (F32,)
    return pl.pallas_call(
        _out_kernel,
        out_shape=tuple(jax.ShapeDtypeStruct((ROWS, D_MODEL), dt) for dt in dtypes),
        grid=(ROWS // TM_OUT,),
        in_specs=[yspec, yspec, yspec, yspec,
                  pl.BlockSpec((None,) + tuple(w_out.shape[1:]), lambda i: (layer, 0, 0),
                               pipeline_mode=pl.Buffered(1)),
                  xspec, _layer_spec(ln_g.shape, layer), _layer_spec(ln_b.shape, layer)],
        out_specs=tuple(xspec for _ in dtypes),
        scratch_shapes=[pltpu.VMEM(tuple(w_out.shape[1:]), BF16)],
        compiler_params=_cparams(("arbitrary",)),
        name="outproj_ln",
    )(ya, yb, yc, yd, w_out, x2d, ln_g, ln_b)


def kernel(x, mem, positions, w_in, ret_norm_g, mla_q_norm_g, mla_w_uq, mla_kv_norm_g, mla_w_ukv, ml_conv_w, ml_conv_b, ml_w_q, ml_w_k, ml_i_bias, ml_f_bias, ml_skip, ml_norm_g, w_mem_kv, w_out, ln_g, ln_b):
    assert x.shape == (BATCH, SEQ, D_MODEL) and mem.shape == (BATCH, MEM_LEN, D_MODEL)
    tables = _rope_tables(positions)
    kvm = _mem_kv(mem, w_mem_kv)
    w_in_p = _prep_w_in(w_in)
    wuq_p, wukv_p = _prep_mla_weights(mla_w_uq, mla_w_ukv)
    ret_consts = _ret_consts()
    ret_g = ret_norm_g.reshape(DEPTH, HEADS, 1, HEAD_DIM)
    mlstm_params = _prep_mlstm(ml_conv_w, ml_conv_b, ml_w_q, ml_w_k, ml_i_bias, ml_f_bias, ml_skip, ml_norm_g)
    qg = mla_q_norm_g.reshape(DEPTH, 1, MLA_Q_RANK)
    kvg = mla_kv_norm_g.reshape(DEPTH, 1, MLA_KV_RANK)
    lng = ln_g.reshape(DEPTH, 1, D_MODEL)
    lnb = ln_b.reshape(DEPTH, 1, D_MODEL)

    x2d = x.reshape(ROWS, D_MODEL)
    xb = []
    for l in range(DEPTH):
        hb16, hb32 = _inproj(xb[0] if xb else x2d, w_in_p, l)
        ya, yd, q, k, v = _light_mixers(hb16, hb32, tables, ret_consts, ret_g, kvm, qg, wuq_p, kvg, wukv_p, l)
        yb = _mla_attn(q, k, v, hb16)
        yc = _mlstm(hb16, hb32, mlstm_params, l)
        x2d, *xb = _outproj(ya, yb, yc, yd, w_out, x2d, lng, lnb, l, with_bf16=l + 1 < DEPTH)
    return x2d.reshape(BATCH, SEQ, D_MODEL)
```

```python
import numpy as np
import jax
import jax.numpy as jnp
from jax import lax
from jax.experimental import pallas as pl
from jax.experimental.pallas import tpu as pltpu

F32 = jnp.float32
BF16 = jnp.bfloat16

D_MODEL = 2048
BATCH = 8
SEQ = 2048
DEPTH = 4
MEM_LEN = 256
HEAD_DIM = 128
HEADS = 4
GROUP_WIDTH = HEADS * HEAD_DIM
MLA_NOPE = 128
MLA_ROPE = 64
MLA_Q_RANK = 384
MLA_KV_RANK = 128
CONV_WIDTH = 4
MIX_WIDTH = 4 * GROUP_WIDTH
ROPE_THETA = 10000.0
LN_EPS = 1e-5
RMS_EPS = 1e-6
DEEPNORM_ALPHA = (2 * DEPTH) ** 0.25
IN_SPLITS = (512, 512, 512, MLA_Q_RANK, MLA_KV_RANK, MLA_ROPE, 512, 512, 512, HEADS, HEADS, 512, MIX_WIDTH)

LANE = 128
MXU_WIDTH = 256
ROWS = BATCH * SEQ

BLK_RQ, BLK_RK, BLK_RV = 0, 4, 8
BLK_LV, BLK_LO = 12, 16
BLK_MQ = 20
BLK_ZA, BLK_ZB, BLK_ZD = 24, 28, 32
BLK_CQ, BLK_CKV = 36, 39
NBLK16 = 40
BLK_LX, BLK_ZC = 0, 4
BLK_KPE = 8
NBLK32 = 10
NBLK = NBLK16 + NBLK32
GATE_I_LANE, GATE_F_LANE = 32, 36

TM_IN = 512
TM_OUT = 512
SUB_OUT = 256
TQ_ATT = 512
ATT_HEADS = 2
RCHUNK = 512
CONV_PAD = 8
STRIP = 64
LOG2E = 1.4426950408889634
VMEM_LIMIT = 56 * 1024 * 1024


def _cparams(sem):
    return pltpu.CompilerParams(dimension_semantics=sem, vmem_limit_bytes=VMEM_LIMIT)


def _silu(z):
    return z * jax.nn.sigmoid(z)


def _log_sigmoid(x):
    return jnp.minimum(x, 0.0) - jnp.log1p(jnp.exp(-jnp.abs(x)))


def _split3(x):
    x1 = x.astype(BF16)
    r1 = x - x1.astype(F32)
    x2 = r1.astype(BF16)
    x3 = (r1 - x2.astype(F32)).astype(BF16)
    return x1, x2, x3


def _dot(a, b):
    return jnp.dot(a, b, preferred_element_type=F32)


def _rowsum_mxu(x, w):
    hi = x.astype(BF16)
    lo = (x - hi.astype(F32)).astype(BF16)
    return _dot(hi, w) + _dot(lo, w)


def _dot_nt(a, b):
    return lax.dot_general(a, b, (((1,), (1,)), ((), ())), preferred_element_type=F32)


def _tables_kernel(pos_ref, c_ref, cr_ref, sr_ref, cm_ref, sm_ref):
    pos = pos_ref[0].astype(F32)
    ang = pos * c_ref[0:1, :]
    cs = jnp.cos(ang)
    sn = jnp.sin(ang)
    cs_sw = pltpu.roll(cs, LANE // 2, 1)
    sn_sw = pltpu.roll(sn, LANE // 2, 1)
    lane = lax.broadcasted_iota(jnp.int32, cs.shape, 1)
    half_m = MLA_ROPE // 2
    first = lane < LANE // 2
    m_lo = lane < half_m
    m_hi = (lane >= LANE // 2) & (lane < LANE // 2 + half_m)
    cr_ref[0] = jnp.where(first, cs, cs_sw)
    sr_ref[0] = jnp.where(first, -sn, sn_sw)
    cm_ref[0] = jnp.where(m_lo, cs_sw, jnp.where(m_hi, cs, 0.0))
    sm_ref[0] = jnp.where(m_lo, -sn_sw, jnp.where(m_hi, sn, 0.0))


def _rope_tables(positions):
    half_r = HEAD_DIM // 2
    fr = ROPE_THETA ** (-jnp.arange(half_r, dtype=F32) / half_r)
    half_m = MLA_ROPE // 2
    fm = ROPE_THETA ** (-jnp.arange(half_m, dtype=F32) / half_m)
    freqs = jnp.concatenate([fr, fm, jnp.zeros((LANE - half_r - half_m,), F32)])
    consts = jnp.concatenate([freqs[None, :], jnp.zeros((7, LANE), F32)], axis=0)
    ts = RCHUNK
    tab = jax.ShapeDtypeStruct((BATCH, SEQ, LANE), F32)
    spec = pl.BlockSpec((1, ts, LANE), lambda b, i: (b, i, 0))
    return pl.pallas_call(
        _tables_kernel,
        out_shape=(tab, tab, tab, tab),
        grid=(BATCH, SEQ // ts),
        in_specs=[pl.BlockSpec((1, ts, 1), lambda b, i: (b, i, 0)),
                  pl.BlockSpec((8, LANE), lambda b, i: (0, 0))],
        out_specs=(spec, spec, spec, spec),
        compiler_params=_cparams(("parallel", "parallel")),
        name="rope_tables",
    )(positions.reshape(BATCH, SEQ, 1), consts)


def _mem_kv_kernel(a_ref, w_ref, o_ref, wb_ref):
    @pl.when(pl.program_id(1) == 0)
    def _():
        wb_ref[...] = w_ref[...].astype(BF16)

    o_ref[...] = _dot(a_ref[...].astype(BF16), wb_ref[...]).astype(BF16)


def _mem_kv(mem, w_mem_kv):
    a = mem.reshape(BATCH * MEM_LEN, D_MODEL)
    tm, tn = 1024, 2 * GROUP_WIDTH
    return pl.pallas_call(
        _mem_kv_kernel,
        out_shape=jax.ShapeDtypeStruct((BATCH * MEM_LEN, DEPTH * tn), BF16),
        grid=(DEPTH, BATCH * MEM_LEN // tm),
        in_specs=[pl.BlockSpec((tm, D_MODEL), lambda l, i: (i, 0)),
                  pl.BlockSpec((None, D_MODEL, tn), lambda l, i: (l, 0, 0))],
        out_specs=pl.BlockSpec((tm, tn), lambda l, i: (i, l)),
        scratch_shapes=[pltpu.VMEM((D_MODEL, tn), BF16)],
        compiler_params=_cparams(("parallel", "arbitrary")),
        name="mem_kv",
    )(a, w_mem_kv)


def _inproj_kernel(x_ref, w_ref, o16_ref, o32_ref):
    xb = x_ref[...].astype(BF16)
    for t in range(NBLK * LANE // MXU_WIDTH):
        r = _dot_nt(xb, w_ref[t * MXU_WIDTH:(t + 1) * MXU_WIDTH, :])
        for half, blk in enumerate((2 * t, 2 * t + 1)):
            o_ref, k = (o16_ref, blk) if blk < NBLK16 else (o32_ref, blk - NBLK16)
            o_ref[k] = r[:, half * LANE:(half + 1) * LANE].astype(o_ref.dtype)


def _inproj(x2d, wt_p, layer):
    return pl.pallas_call(
        _inproj_kernel,
        out_shape=(jax.ShapeDtypeStruct((NBLK16, ROWS, LANE), BF16),
                   jax.ShapeDtypeStruct((NBLK32, ROWS, LANE), F32)),
        grid=(ROWS // TM_IN,),
        in_specs=[pl.BlockSpec((TM_IN, D_MODEL), lambda i: (i, 0)),
                  pl.BlockSpec((None, NBLK * LANE, D_MODEL), lambda i: (layer, 0, 0),
                               pipeline_mode=pl.Buffered(1))],
        out_specs=(pl.BlockSpec((NBLK16, TM_IN, LANE), lambda i: (0, i, 0)),
                   pl.BlockSpec((NBLK32, TM_IN, LANE), lambda i: (0, i, 0))),
        compiler_params=_cparams(("parallel",)),
        name="inproj",
    )(x2d, wt_p)


def _w_in_groups():
    off = np.concatenate([[0], np.cumsum(IN_SPLITS)]).tolist()
    (o_rq, o_rk, o_rv, o_cq, _, o_kpe, o_lx, o_lv, o_lo, o_li, _, o_mq, o_z, _) = off
    groups = [o_rq, o_rk, o_rv, o_lv, o_lo, o_mq, o_z, o_z + GROUP_WIDTH, o_z + 3 * GROUP_WIDTH,
              o_cq, o_lx, o_z + 2 * GROUP_WIDTH]
    return groups, o_kpe, o_li


def _w_in_kernel(tbl_ref, w_ref, kpe_ref, gate_ref, o_ref):
    del tbl_ref
    ngroup = NBLK // HEADS

    @pl.when(pl.program_id(1) < ngroup)
    def _():
        o_ref[0] = w_ref[0].astype(BF16)

    @pl.when(pl.program_id(1) == ngroup)
    def _():
        half = MLA_ROPE // 2
        zeros = lambda n: jnp.zeros((n, D_MODEL), F32)
        kpe = kpe_ref[0]
        blk = jnp.concatenate([kpe[0:half], gate_ref[0], zeros(64 - half - 2 * HEADS),
                               kpe[half:2 * half], zeros(64 - half)], axis=0)
        o_ref[0, 0:LANE, :] = blk.astype(BF16)
        o_ref[0, LANE:, :] = jnp.zeros((GROUP_WIDTH - LANE, D_MODEL), BF16)


def _prep_w_in(w_in):
    wt = jnp.swapaxes(w_in, 1, 2)
    groups, o_kpe, o_li = _w_in_groups()
    assert len(groups) * HEADS + 2 == NBLK and o_kpe % LANE == 0 and o_li % (2 * HEADS) == 0
    sub = 8
    assert all(g % sub == 0 for g in groups)
    table = jnp.asarray([g // sub for g in groups] + [0], jnp.int32)
    grid_spec = pltpu.PrefetchScalarGridSpec(
        num_scalar_prefetch=1,
        grid=(DEPTH, len(groups) + 1),
        in_specs=[pl.BlockSpec((pl.Element(1), pl.Element(GROUP_WIDTH), pl.Element(D_MODEL)),
                               lambda l, j, tbl: (l, pl.multiple_of(tbl[j] * sub, sub), 0)),
                  pl.BlockSpec((1, LANE, D_MODEL), lambda l, j, tbl: (l, o_kpe // LANE, 0)),
                  pl.BlockSpec((1, 2 * HEADS, D_MODEL), lambda l, j, tbl: (l, o_li // (2 * HEADS), 0))],
        out_specs=pl.BlockSpec((1, GROUP_WIDTH, D_MODEL), lambda l, j, tbl: (l, j, 0)),
    )
    return pl.pallas_call(
        _w_in_kernel,
        out_shape=jax.ShapeDtypeStruct((DEPTH, NBLK * LANE, D_MODEL), BF16),
        grid_spec=grid_spec,
        compiler_params=_cparams(("parallel", "arbitrary")),
        name="w_in_relayout",
    )(table, wt, wt, wt)


def _group_spec(base, rows):
    nchunk = SEQ // rows
    return pl.BlockSpec((HEADS, rows, LANE), lambda b, c: (base // HEADS, b * nchunk + c, 0))


def _layer_spec(shape, layer):
    nd = len(shape) - 1
    return pl.BlockSpec((None,) + tuple(shape[1:]), lambda *_: (layer,) + (0,) * nd)


def _const_spec(shape):
    nd = len(shape)
    return pl.BlockSpec(tuple(shape), lambda *_: (0,) * nd)


def _ret_kernel(q_ref, k_ref, v_ref, z_ref, cos_ref, sin_ref, gq_ref, gk_ref, gc_ref, g_ref, o_ref, st_ref):
    @pl.when(pl.program_id(1) == 0)
    def _():
        st_ref[...] = jnp.zeros_like(st_ref)

    cs = cos_ref[0]
    sn = sin_ref[0]
    row = lax.broadcasted_iota(jnp.int32, (RCHUNK, RCHUNK), 0)
    col = lax.broadcasted_iota(jnp.int32, (RCHUNK, RCHUNK), 1)
    causal = col <= row
    qbs, kts, vbs, scs = [], [], [], []
    for h in range(HEADS):
        q = q_ref[h].astype(F32)
        q = (q * cs + pltpu.roll(q, HEAD_DIM // 2, 1) * sn) * gq_ref[h]
        k = k_ref[h].astype(F32)
        k = (k * cs + pltpu.roll(k, HEAD_DIM // 2, 1) * sn) * gk_ref[h]
        qb = q.astype(BF16)
        scs.append(_dot_nt(qb, k.astype(BF16)))
        qbs.append(qb)
        kts.append(k.T.astype(BF16))
        vbs.append(v_ref[h])
    outs = []
    for h in range(HEADS):
        st = st_ref[h]
        sc = jnp.where(causal, scs[h], 0.0).astype(BF16)
        outs.append(_dot(sc, vbs[h]) + _dot(qbs[h], st.astype(BF16)))
        st_ref[h] = gc_ref[h] * (st + _dot(kts[h], vbs[h]))
    for h in range(HEADS):
        out = outs[h]
        mu = jnp.mean(out, axis=-1, keepdims=True)
        dev = out - mu
        var = jnp.mean(dev * dev, axis=-1, keepdims=True)
        hn = dev * lax.rsqrt(var + LN_EPS) * g_ref[h]
        o_ref[h] = (hn * _silu(z_ref[h].astype(F32))).astype(BF16)


def _ret_consts():
    log_g = jnp.log1p(-jnp.exp2(-5.0 - jnp.arange(HEADS, dtype=F32)))
    idx = jnp.arange(RCHUNK, dtype=F32)
    full = (HEADS, RCHUNK, HEAD_DIM)
    gq = jnp.broadcast_to(jnp.exp(log_g[:, None] * (idx + 1.0))[..., None], full)
    gk = jnp.broadcast_to((jnp.exp(-log_g[:, None] * (idx + 1.0)) * HEAD_DIM ** -0.5)[..., None], full)
    gc = jnp.broadcast_to(jnp.exp(log_g * RCHUNK)[:, None, None], (HEADS, 1, HEAD_DIM))
    return gq, gk, gc


def _mlstm_kernel(lx_ref, lv_ref, lo_ref, gt_ref, z_ref, cw_ref, cb_ref, wq_ref, wk_ref, gb_ref,
                  skip_ref, g_ref, o_ref, xp_ref, st_ref, m_ref, s_ref, p_ref):
    @pl.when(pl.program_id(1) == 0)
    def _():
        xp_ref[:, 0:CONV_PAD, :] = jnp.zeros((HEADS, CONV_PAD, HEAD_DIM), F32)
        st_ref[...] = jnp.zeros_like(st_ref)
        m_ref[...] = jnp.zeros_like(m_ref)

    n = RCHUNK
    ones_nd = jnp.ones((n, HEAD_DIM), BF16)
    mean_dd = jnp.full((HEAD_DIM, HEAD_DIM), 1.0 / HEAD_DIM, BF16)
    row = lax.broadcasted_iota(jnp.int32, (n, n), 0)
    col = lax.broadcasted_iota(jnp.int32, (n, n), 1)
    tril = jnp.where(col <= row, 1.0, 0.0).astype(BF16)
    r128 = lax.broadcasted_iota(jnp.int32, (LANE, LANE), 0)
    c128 = lax.broadcasted_iota(jnp.int32, (LANE, LANE), 1)
    ident = jnp.where(r128 == c128, 1.0, 0.0).astype(BF16)
    srow = lax.broadcasted_iota(jnp.int32, (STRIP, LANE), 0)
    scol = lax.broadcasted_iota(jnp.int32, (STRIP, LANE), 1)

    lane = lax.broadcasted_iota(jnp.int32, (n, LANE), 1)
    x = gt_ref[0] + gb_ref[...]
    x = jnp.where((lane >= GATE_F_LANE) & (lane < GATE_F_LANE + HEADS), _log_sigmoid(x), x) * LOG2E
    x1, x2, x3 = _split3(x)
    cum = _dot(tril, x1) + _dot(tril, x2) + _dot(tril, x3)
    rt = x - pltpu.roll(cum, LANE - (GATE_F_LANE - GATE_I_LANE), 1)
    r1, r2, r3 = _split3(rt)
    rtt = _dot_nt(ident, r1) + _dot_nt(ident, r2) + _dot_nt(ident, r3)

    xcs, ks, qbs, vbs = [], [], [], []
    for h in range(HEADS):
        xp_ref[h, CONV_PAD:, :] = lx_ref[h]
        acc = jnp.zeros((n, HEAD_DIM), F32) + cb_ref[h]
        for j in range(CONV_WIDTH):
            off = CONV_PAD - (CONV_WIDTH - 1) + j
            acc = acc + xp_ref[h, off:off + n, :] * cw_ref[h, j:j + 1, :]
        xp_ref[h, 0:CONV_PAD, :] = lx_ref[h, n - CONV_PAD:n, :]
        xc = _silu(acc)
        xcb = xc.astype(BF16)
        k = _dot(xcb, wk_ref[h]) * HEAD_DIM ** -0.5
        qb = _dot(xcb, wq_ref[h]).astype(BF16)
        s_ref[h] = _dot_nt(qb, k.astype(BF16))
        xcs.append(xc)
        ks.append(k)
        qbs.append(qb)
        vbs.append(jnp.concatenate([lv_ref[h], ones_nd], axis=-1))

    r_rows = [rtt[GATE_I_LANE + h:GATE_I_LANE + h + 1, :] for h in range(HEADS)]
    bases = [m_ref[h] for h in range(HEADS)]
    strips = [[] for _ in range(HEADS)]
    u_lasts = [None] * HEADS
    for i in range(n // STRIP):
        rows = slice(i * STRIP, (i + 1) * STRIP)
        d0 = (i * STRIP // LANE) * LANE
        w = d0 + LANE
        mask = (scol + d0) <= (srow + i * STRIP)
        for h in range(HEADS):
            r_row = r_rows[h]
            if d0 > 0 and (i * STRIP) % LANE == 0:
                bases[h] = jnp.maximum(bases[h], jnp.max(r_row[:, d0 - LANE:d0], axis=-1, keepdims=True))
            rmd = jnp.where(mask, r_row[:, d0:w], -jnp.inf)
            u_col = jnp.maximum(jnp.max(rmd, axis=-1, keepdims=True), bases[h])
            u = jnp.broadcast_to(u_col, (STRIP, LANE))
            p_ref[h, rows, d0:w] = (s_ref[h, rows, d0:w] * jnp.exp2(rmd - u)).astype(BF16)
            for c in range(d0 // LANE):
                cols = slice(c * LANE, (c + 1) * LANE)
                p_ref[h, rows, cols] = (s_ref[h, rows, cols] * jnp.exp2(r_row[:, cols] - u)).astype(BF16)
            if w < n:
                p_ref[h, rows, w:n] = jnp.zeros((STRIP, n - w), BF16)
            strips[h].append(u)
            u_lasts[h] = u_col[STRIP - 1:STRIP, :]
    us = [jnp.concatenate(st_h, axis=0) for st_h in strips]

    cells = []
    for h in range(HEADS):
        li = GATE_I_LANE + h
        lf = GATE_F_LANE + h
        m_st = m_ref[h]
        u, u_last, vb = us[h], u_lasts[h], vbs[h]
        w_inter = jnp.exp2(m_st - u)
        st = st_ref[h]
        intra = _dot(p_ref[h], vb)
        inter = _dot(qbs[h], st.astype(BF16))
        num = intra[:, :LANE] + w_inter * inter[:, :LANE]
        den = intra[:, LANE:] + w_inter * inter[:, LANE:]
        cum_f = jnp.broadcast_to(cum[:, lf:lf + 1], (n, LANE))
        cells.append(num / jnp.maximum(jnp.abs(den), jnp.exp2(-(cum_f + u))))

        decay = jnp.exp2(m_st - u_last)
        kw = ks[h] * jnp.exp2(jnp.broadcast_to(rt[:, li:li + 1], (n, LANE)) - u_last)
        upd = _dot(kw.T.astype(BF16), vb)
        st_ref[h] = decay * st + upd
        m_ref[h] = cum[n - 1:n, lf:lf + 1] + u_last

    for h in range(HEADS):
        cell = cells[h] * jax.nn.sigmoid(lo_ref[h].astype(F32))
        mu = _rowsum_mxu(cell, mean_dd)
        dev = cell - mu
        var = _rowsum_mxu(dev * dev, mean_dd)
        hn = dev * lax.rsqrt(var + LN_EPS) * g_ref[h]
        o_ref[h] = ((hn + skip_ref[h] * xcs[h]) * _silu(z_ref[h])).astype(BF16)


def _prep_mlstm(conv_w, conv_b, w_q, w_k, i_bias, f_bias, skip, norm_g):
    nl = conv_w.shape[0]
    cw = conv_w.reshape(nl, CONV_WIDTH, HEADS, HEAD_DIM).transpose(0, 2, 1, 3)
    cb = conv_b.reshape(nl, HEADS, 1, HEAD_DIM)
    zeros = lambda w: jnp.zeros((nl, w), F32)
    gb = jnp.concatenate([zeros(GATE_I_LANE), i_bias, f_bias, zeros(LANE - GATE_F_LANE - HEADS)], axis=-1)
    return (cw, cb, w_q.astype(BF16), w_k.astype(BF16), gb.reshape(nl, 1, LANE),
            skip.reshape(nl, HEADS, 1, HEAD_DIM), norm_g.reshape(nl, HEADS, 1, HEAD_DIM))


def _mlstm(hb16, hb32, params, layer):
    nchunk = SEQ // RCHUNK
    return pl.pallas_call(
        _mlstm_kernel,
        out_shape=jax.ShapeDtypeStruct((HEADS, ROWS, LANE), BF16),
        grid=(BATCH, nchunk),
        in_specs=[_group_spec(BLK_LX, RCHUNK), _group_spec(BLK_LV, RCHUNK), _group_spec(BLK_LO, RCHUNK),
                  pl.BlockSpec((1, RCHUNK, LANE), lambda b, c: (BLK_KPE, b * nchunk + c, 0)),
                  _group_spec(BLK_ZC, RCHUNK)] + [_layer_spec(p.shape, layer) for p in params],
        out_specs=pl.BlockSpec((HEADS, RCHUNK, LANE), lambda b, c: (0, b * nchunk + c, 0)),
        scratch_shapes=[pltpu.VMEM((HEADS, RCHUNK + CONV_PAD, HEAD_DIM), F32),
                        pltpu.VMEM((HEADS, HEAD_DIM, 2 * HEAD_DIM), F32),
                        pltpu.VMEM((HEADS, 1, 1), F32),
                        pltpu.VMEM((HEADS, RCHUNK, RCHUNK), F32),
                        pltpu.VMEM((HEADS, RCHUNK, RCHUNK), BF16)],
        compiler_params=_cparams(("parallel", "arbitrary")),
        name="mlstm",
    )(hb32, hb16, hb16, hb32, hb32, *params)


def _mla_prep_kernel(lat_ref, kpe_ref, cm_ref, sm_ref, qg_ref, wuq_ref, kvg_ref, wukv_ref,
                     q_out, k_out, v_out):
    scale = (MLA_NOPE + MLA_ROPE) ** -0.5 * LOG2E
    cs = cm_ref[0]
    sn = sm_ref[0]
    cq = jnp.concatenate([lat_ref[0], lat_ref[1], lat_ref[2]], axis=-1).astype(F32)
    qn = cq * lax.rsqrt(jnp.mean(jnp.square(cq), axis=-1, keepdims=True) + RMS_EPS) * qg_ref[...]
    q = _dot(qn.astype(BF16), wuq_ref[...])
    ckv = lat_ref[3].astype(F32)
    kvn = ckv * lax.rsqrt(jnp.mean(jnp.square(ckv), axis=-1, keepdims=True) + RMS_EPS) * kvg_ref[...]
    kv = _dot(kvn.astype(BF16), wukv_ref[...])
    kpe = kpe_ref[0]
    krot = (kpe * cs + pltpu.roll(kpe, LANE // 2, 1) * sn).astype(BF16)
    for h in range(HEADS):
        qr = q[:, GROUP_WIDTH + h * LANE:GROUP_WIDTH + (h + 1) * LANE]
        qr = qr * cs + pltpu.roll(qr, LANE // 2, 1) * sn
        q_out[0, h, :, 0:LANE] = (q[:, h * LANE:(h + 1) * LANE] * scale).astype(BF16)
        q_out[0, h, :, LANE:2 * LANE] = (qr * scale).astype(BF16)
        k_out[0, h, :, 0:LANE] = kv[:, h * LANE:(h + 1) * LANE].astype(BF16)
        k_out[0, h, :, LANE:2 * LANE] = krot
        v_out[0, h] = kv[:, GROUP_WIDTH + h * LANE:GROUP_WIDTH + (h + 1) * LANE].astype(BF16)


def _prep_mla_weights(w_uq, w_ukv):
    lead = w_uq.shape[:-1]
    wq = w_uq.reshape(lead + (HEADS, MLA_NOPE + MLA_ROPE))
    nope = wq[..., :MLA_NOPE].reshape(lead + (GROUP_WIDTH,))
    half = MLA_ROPE // 2
    zeros = jnp.zeros(lead + (HEADS, 64 - half), w_uq.dtype)
    rope = jnp.concatenate([wq[..., MLA_NOPE:MLA_NOPE + half], zeros, wq[..., MLA_NOPE + half:], zeros], axis=-1)
    wq_p = jnp.concatenate([nope, rope.reshape(lead + (HEADS * LANE,))], axis=-1).astype(BF16)
    lead = w_ukv.shape[:-1]
    wkv = w_ukv.reshape(lead + (HEADS, MLA_NOPE + HEAD_DIM))
    wkv_p = jnp.concatenate([wkv[..., :MLA_NOPE].reshape(lead + (GROUP_WIDTH,)),
                             wkv[..., MLA_NOPE:].reshape(lead + (GROUP_WIDTH,))], axis=-1).astype(BF16)
    return wq_p, wkv_p


def _attn_tile(nfull, q_ref, k_ref, v_ref, z_ref, o_ref, s_ref, p_ref, m_ref):
    kvlen = (nfull + 1) * TQ_ATT
    d0 = nfull * TQ_ATT
    hq = TQ_ATT // 2
    srow = lax.broadcasted_iota(jnp.int32, (STRIP, LANE), 0)
    scol = lax.broadcasted_iota(jnp.int32, (STRIP, LANE), 1)

    def strip_blocks(g, i):
        rows = slice(i * STRIP, (i + 1) * STRIP)
        wd = -(-(i + 1) * STRIP // LANE) * LANE
        ncol = (d0 + wd) // LANE
        blks = [s_ref[g, rows, c * LANE:(c + 1) * LANE] for c in range(ncol)]
        blks[-1] = jnp.where(scol + (wd - LANE) <= srow + i * STRIP, blks[-1], -jnp.inf)
        return rows, ncol, blks

    qrows = slice(d0, d0 + TQ_ATT)
    for g in range(ATT_HEADS):
        q = q_ref[0, g, qrows, :]
        for j in range(nfull):
            s_ref[g, :, j * TQ_ATT:(j + 1) * TQ_ATT] = _dot_nt(q, k_ref[0, g, j * TQ_ATT:(j + 1) * TQ_ATT, :])
        s_ref[g, 0:hq, d0:d0 + hq] = _dot_nt(q[0:hq], k_ref[0, g, d0:d0 + hq, :])
        s_ref[g, hq:TQ_ATT, d0:kvlen] = _dot_nt(q[hq:TQ_ATT], k_ref[0, g, d0:kvlen, :])
    for g in range(ATT_HEADS):
        for i in range(TQ_ATT // STRIP):
            rows, ncol, blks = strip_blocks(g, i)
            mx = blks[0]
            for blk in blks[1:]:
                mx = jnp.maximum(mx, blk)
            m_ref[g, rows, :] = jnp.broadcast_to(jnp.max(mx, axis=-1, keepdims=True), (STRIP, LANE))
        for i in range(TQ_ATT // STRIP):
            rows, ncol, blks = strip_blocks(g, i)
            m = m_ref[g, rows, :]
            for c, blk in enumerate(blks):
                p_ref[g, rows, c * LANE:(c + 1) * LANE] = jnp.exp2(blk - m).astype(BF16)
            kl = d0 + hq if (i + 1) * STRIP <= hq else kvlen
            if ncol * LANE < kl:
                p_ref[g, rows, ncol * LANE:kl] = jnp.zeros((STRIP, kl - ncol * LANE), BF16)
        v1 = jnp.concatenate([v_ref[0, g, 0:kvlen, :], jnp.ones((kvlen, LANE), BF16)], axis=-1)
        for r0, r1, kl in ((0, hq, d0 + hq), (hq, TQ_ATT, kvlen)):
            pv = _dot(p_ref[g, r0:r1, 0:kl], v1[0:kl])
            gate = _silu(z_ref[g, d0 + r0:d0 + r1, :].astype(F32))
            o_ref[g, d0 + r0:d0 + r1, :] = (pv[:, :LANE] / pv[:, LANE:] * gate).astype(BF16)


def _mla_attn_kernel(q_ref, k_ref, v_ref, z_ref, o_ref, s_ref, p_ref, m_ref):
    for nfull in range(SEQ // TQ_ATT):
        _attn_tile(nfull, q_ref, k_ref, v_ref, z_ref, o_ref, s_ref, p_ref, m_ref)


def _mla_attn(q, k, v, hb16):
    return pl.pallas_call(
        _mla_attn_kernel,
        out_shape=jax.ShapeDtypeStruct((HEADS, ROWS, LANE), BF16),
        grid=(BATCH, HEADS // ATT_HEADS),
        in_specs=[pl.BlockSpec((1, ATT_HEADS, SEQ, 2 * LANE), lambda b, h: (b, h, 0, 0)),
                  pl.BlockSpec((1, ATT_HEADS, SEQ, 2 * LANE), lambda b, h: (b, h, 0, 0)),
                  pl.BlockSpec((1, ATT_HEADS, SEQ, LANE), lambda b, h: (b, h, 0, 0)),
                  pl.BlockSpec((ATT_HEADS, SEQ, LANE), lambda b, h: (BLK_ZB // ATT_HEADS + h, b, 0))],
        out_specs=pl.BlockSpec((ATT_HEADS, SEQ, LANE), lambda b, h: (h, b, 0)),
        scratch_shapes=[pltpu.VMEM((ATT_HEADS, TQ_ATT, SEQ), F32), pltpu.VMEM((ATT_HEADS, TQ_ATT, SEQ), BF16),
                        pltpu.VMEM((ATT_HEADS, TQ_ATT, LANE), F32)],
        compiler_params=_cparams(("parallel", "parallel")),
        name="mla_attn",
    )(q, k, v, hb16)


def _mem_attn_kernel(q_ref, k_ref, v_ref, z_ref, o_ref):
    ones = jnp.ones((MEM_LEN, LANE), BF16)
    scores = []
    for h in range(HEADS):
        q = (q_ref[h].astype(F32) * (HEAD_DIM ** -0.5 * LOG2E)).astype(BF16)
        scores.append(_dot_nt(q, k_ref[:, h * LANE:(h + 1) * LANE]))
    for h in range(HEADS):
        s = scores[h]
        p = jnp.exp2(s - jnp.max(s, axis=-1, keepdims=True)).astype(BF16)
        pv = _dot(p, jnp.concatenate([v_ref[:, h * LANE:(h + 1) * LANE], ones], axis=-1))
        o_ref[h] = (pv[:, :LANE] / pv[:, LANE:] * _silu(z_ref[h].astype(F32))).astype(BF16)


def _light_kernel(rq, rk, rv, rz, cr, sr, gq, gk, gc, rg, mq, mk, mv, mz, lat, kpe, cm, sm, qg, wuq, kvg, wukv,
                  ya, yd, q_out, k_out, v_out, st_ref):
    _ret_kernel(rq, rk, rv, rz, cr, sr, gq, gk, gc, rg, ya, st_ref)
    _mem_attn_kernel(mq, mk, mv, mz, yd)
    _mla_prep_kernel(lat, kpe, cm, sm, qg, wuq, kvg, wukv, q_out, k_out, v_out)


def _light_mixers(hb16, hb32, tables, ret_consts, ret_g, kvm, qg, wuq_p, kvg, wukv_p, layer):
    cos_r, sin_r, cos_m, sin_m = tables
    gq, gk, gc = ret_consts
    nchunk = SEQ // RCHUNK
    tab = pl.BlockSpec((1, RCHUNK, LANE), lambda b, c: (b, c, 0))
    row_blk = lambda blk: pl.BlockSpec((1, RCHUNK, LANE), lambda b, c: (blk, b * nchunk + c, 0))
    y_shape = jax.ShapeDtypeStruct((HEADS, ROWS, LANE), BF16)
    y_spec = pl.BlockSpec((HEADS, RCHUNK, LANE), lambda b, c: (0, b * nchunk + c, 0))
    qk_shape = jax.ShapeDtypeStruct((BATCH, HEADS, SEQ, 2 * LANE), BF16)
    v_shape = jax.ShapeDtypeStruct((BATCH, HEADS, SEQ, LANE), BF16)
    qk_spec = pl.BlockSpec((1, HEADS, RCHUNK, 2 * LANE), lambda b, c: (b, 0, c, 0))
    v_spec = pl.BlockSpec((1, HEADS, RCHUNK, LANE), lambda b, c: (b, 0, c, 0))
    return pl.pallas_call(
        _light_kernel,
        out_shape=(y_shape, y_shape, qk_shape, qk_shape, v_shape),
        grid=(BATCH, nchunk),
        in_specs=[_group_spec(BLK_RQ, RCHUNK), _group_spec(BLK_RK, RCHUNK), _group_spec(BLK_RV, RCHUNK),
                  _group_spec(BLK_ZA, RCHUNK), tab, tab,
                  _const_spec(gq.shape), _const_spec(gk.shape), _const_spec(gc.shape),
                  _layer_spec(ret_g.shape, layer),
                  _group_spec(BLK_MQ, RCHUNK),
                  pl.BlockSpec((MEM_LEN, GROUP_WIDTH), lambda b, c: (b, 2 * layer)),
                  pl.BlockSpec((MEM_LEN, GROUP_WIDTH), lambda b, c: (b, 2 * layer + 1)),
                  _group_spec(BLK_ZD, RCHUNK),
                  _group_spec(BLK_CQ, RCHUNK), row_blk(BLK_KPE), tab, tab,
                  _layer_spec(qg.shape, layer), _layer_spec(wuq_p.shape, layer),
                  _layer_spec(kvg.shape, layer), _layer_spec(wukv_p.shape, layer)],
        out_specs=(y_spec, y_spec, qk_spec, qk_spec, v_spec),
        scratch_shapes=[pltpu.VMEM((HEADS, HEAD_DIM, HEAD_DIM), F32)],
        compiler_params=_cparams(("parallel", "arbitrary")),
        name="light_mixers",
    )(hb16, hb16, hb16, hb16, cos_r, sin_r, gq, gk, gc, ret_g, hb16, kvm, kvm, hb16, hb16, hb32, cos_m, sin_m,
      qg, wuq_p, kvg, wukv_p)


def _out_kernel(ya_ref, yb_ref, yc_ref, yd_ref, w_ref, x_ref, g_ref, b_ref, o_ref, *rest):
    *bf16_out, wb_ref = rest

    @pl.when(pl.program_id(0) == 0)
    def _():
        wb_ref[...] = w_ref[...].astype(BF16)

    for t in range(TM_OUT // SUB_OUT):
        rows = slice(t * SUB_OUT, (t + 1) * SUB_OUT)
        parts = [ref[h, rows, :] for ref in (ya_ref, yb_ref, yc_ref, yd_ref) for h in range(HEADS)]
        y = jnp.concatenate(parts, axis=-1)
        for s in range(D_MODEL // MXU_WIDTH):
            cols = slice(s * MXU_WIDTH, (s + 1) * MXU_WIDTH)
            o_ref[rows, cols] = DEEPNORM_ALPHA * x_ref[rows, cols] + _dot(y, wb_ref[:, cols])
        r = o_ref[rows, :]
        mu = jnp.mean(r, axis=-1, keepdims=True)
        var = jnp.mean(jnp.square(r - mu), axis=-1, keepdims=True)
        out = (r - mu) * lax.rsqrt(var + LN_EPS) * g_ref[...] + b_ref[...]
        o_ref[rows, :] = out
        for ob_ref in bf16_out:
            ob_ref[rows, :] = out.astype(BF16)


def _outproj(ya, yb, yc, yd, w_out, x2d, ln_g, ln_b, layer, with_bf16):
    yspec = pl.BlockSpec((HEADS, TM_OUT, LANE), lambda i: (0, i, 0))
    xspec = pl.BlockSpec((TM_OUT, D_MODEL), lambda i: (i, 0))
    dtypes = (F32, BF16) if with_bf16 else (F32,)
    return pl.pallas_call(
        _out_kernel,
        out_shape=tuple(jax.ShapeDtypeStruct((ROWS, D_MODEL), dt) for dt in dtypes),
        grid=(ROWS // TM_OUT,),
        in_specs=[yspec, yspec, yspec, yspec,
                  pl.BlockSpec((None,) + tuple(w_out.shape[1:]), lambda i: (layer, 0, 0),
                               pipeline_mode=pl.Buffered(1)),
                  xspec, _layer_spec(ln_g.shape, layer), _layer_spec(ln_b.shape, layer)],
        out_specs=tuple(xspec for _ in dtypes),
        scratch_shapes=[pltpu.VMEM(tuple(w_out.shape[1:]), BF16)],
        compiler_params=_cparams(("arbitrary",)),
        name="outproj_ln",
    )(ya, yb, yc, yd, w_out, x2d, ln_g, ln_b)


def kernel(x, mem, positions, w_in, ret_norm_g, mla_q_norm_g, mla_w_uq, mla_kv_norm_g, mla_w_ukv, ml_conv_w, ml_conv_b, ml_w_q, ml_w_k, ml_i_bias, ml_f_bias, ml_skip, ml_norm_g, w_mem_kv, w_out, ln_g, ln_b):
    assert x.shape == (BATCH, SEQ, D_MODEL) and mem.shape == (BATCH, MEM_LEN, D_MODEL)
    tables = _rope_tables(positions)
    kvm = _mem_kv(mem, w_mem_kv)
    w_in_p = _prep_w_in(w_in)
    wuq_p, wukv_p = _prep_mla_weights(mla_w_uq, mla_w_ukv)
    ret_consts = _ret_consts()
    ret_g = ret_norm_g.reshape(DEPTH, HEADS, 1, HEAD_DIM)
    mlstm_params = _prep_mlstm(ml_conv_w, ml_conv_b, ml_w_q, ml_w_k, ml_i_bias, ml_f_bias, ml_skip, ml_norm_g)
    qg = mla_q_norm_g.reshape(DEPTH, 1, MLA_Q_RANK)
    kvg = mla_kv_norm_g.reshape(DEPTH, 1, MLA_KV_RANK)
    lng = ln_g.reshape(DEPTH, 1, D_MODEL)
    lnb = ln_b.reshape(DEPTH, 1, D_MODEL)

    x2d = x.reshape(ROWS, D_MODEL)
    xb = []
    for l in range(DEPTH):
        hb16, hb32 = _inproj(xb[0] if xb else x2d, w_in_p, l)
        ya, yd, q, k, v = _light_mixers(hb16, hb32, tables, ret_consts, ret_g, kvm, qg, wuq_p, kvg, wukv_p, l)
        yb = _mla_attn(q, k, v, hb16)
        yc = _mlstm(hb16, hb32, mlstm_params, l)
        x2d, *xb = _outproj(ya, yb, yc, yd, w_out, x2d, lng, lnb, l, with_bf16=l + 1 < DEPTH)
    return x2d.reshape(BATCH, SEQ, D_MODEL)
```

```python
import numpy as np
import jax
import jax.numpy as jnp
from jax import lax
from jax.experimental import pallas as pl
from jax.experimental.pallas import tpu as pltpu

F32 = jnp.float32
BF16 = jnp.bfloat16

D_MODEL = 2048
BATCH = 8
SEQ = 2048
DEPTH = 4
MEM_LEN = 256
HEAD_DIM = 128
HEADS = 4
GROUP_WIDTH = HEADS * HEAD_DIM
MLA_NOPE = 128
MLA_ROPE = 64
MLA_Q_RANK = 384
MLA_KV_RANK = 128
CONV_WIDTH = 4
MIX_WIDTH = 4 * GROUP_WIDTH
ROPE_THETA = 10000.0
LN_EPS = 1e-5
RMS_EPS = 1e-6
DEEPNORM_ALPHA = (2 * DEPTH) ** 0.25
IN_SPLITS = (512, 512, 512, MLA_Q_RANK, MLA_KV_RANK, MLA_ROPE, 512, 512, 512, HEADS, HEADS, 512, MIX_WIDTH)

LANE = 128
MXU_WIDTH = 256
ROWS = BATCH * SEQ

BLK_RQ, BLK_RK, BLK_RV = 0, 4, 8
BLK_LV, BLK_LO = 12, 16
BLK_MQ = 20
BLK_ZA, BLK_ZB, BLK_ZD = 24, 28, 32
BLK_CQ, BLK_CKV = 36, 39
NBLK16 = 40
BLK_LX, BLK_ZC = 0, 4
BLK_KPE = 8
NBLK32 = 10
NBLK = NBLK16 + NBLK32
GATE_I_LANE, GATE_F_LANE = 32, 36

TM_IN = 512
TM_OUT = 512
SUB_OUT = 256
TQ_ATT = 512
ATT_HEADS = 2
RCHUNK = 512
MLSTM_SUBS = 2
CONV_PAD = 8
STRIP = 64
LOG2E = 1.4426950408889634
VMEM_LIMIT = 56 * 1024 * 1024


def _cparams(sem):
    return pltpu.CompilerParams(dimension_semantics=sem, vmem_limit_bytes=VMEM_LIMIT)


def _silu(z):
    return z * jax.nn.sigmoid(z)


def _log_sigmoid(x):
    return jnp.minimum(x, 0.0) - jnp.log1p(jnp.exp(-jnp.abs(x)))


def _split3(x):
    x1 = x.astype(BF16)
    r1 = x - x1.astype(F32)
    x2 = r1.astype(BF16)
    x3 = (r1 - x2.astype(F32)).astype(BF16)
    return x1, x2, x3


def _dot(a, b):
    return jnp.dot(a, b, preferred_element_type=F32)


def _rowsum_mxu(x, w):
    hi = x.astype(BF16)
    lo = (x - hi.astype(F32)).astype(BF16)
    return _dot(hi, w) + _dot(lo, w)


def _dot_nt(a, b):
    return lax.dot_general(a, b, (((1,), (1,)), ((), ())), preferred_element_type=F32)


def _tables_kernel(pos_ref, c_ref, cr_ref, sr_ref, cm_ref, sm_ref):
    pos = pos_ref[0].astype(F32)
    ang = pos * c_ref[0:1, :]
    cs = jnp.cos(ang)
    sn = jnp.sin(ang)
    cs_sw = pltpu.roll(cs, LANE // 2, 1)
    sn_sw = pltpu.roll(sn, LANE // 2, 1)
    lane = lax.broadcasted_iota(jnp.int32, cs.shape, 1)
    half_m = MLA_ROPE // 2
    first = lane < LANE // 2
    m_lo = lane < half_m
    m_hi = (lane >= LANE // 2) & (lane < LANE // 2 + half_m)
    cr_ref[0] = jnp.where(first, cs, cs_sw)
    sr_ref[0] = jnp.where(first, -sn, sn_sw)
    cm_ref[0] = jnp.where(m_lo, cs_sw, jnp.where(m_hi, cs, 0.0))
    sm_ref[0] = jnp.where(m_lo, -sn_sw, jnp.where(m_hi, sn, 0.0))


def _rope_tables(positions):
    half_r = HEAD_DIM // 2
    fr = ROPE_THETA ** (-jnp.arange(half_r, dtype=F32) / half_r)
    half_m = MLA_ROPE // 2
    fm = ROPE_THETA ** (-jnp.arange(half_m, dtype=F32) / half_m)
    freqs = jnp.concatenate([fr, fm, jnp.zeros((LANE - half_r - half_m,), F32)])
    consts = jnp.concatenate([freqs[None, :], jnp.zeros((7, LANE), F32)], axis=0)
    ts = RCHUNK
    tab = jax.ShapeDtypeStruct((BATCH, SEQ, LANE), F32)
    spec = pl.BlockSpec((1, ts, LANE), lambda b, i: (b, i, 0))
    return pl.pallas_call(
        _tables_kernel,
        out_shape=(tab, tab, tab, tab),
        grid=(BATCH, SEQ // ts),
        in_specs=[pl.BlockSpec((1, ts, 1), lambda b, i: (b, i, 0)),
                  pl.BlockSpec((8, LANE), lambda b, i: (0, 0))],
        out_specs=(spec, spec, spec, spec),
        compiler_params=_cparams(("parallel", "parallel")),
        name="rope_tables",
    )(positions.reshape(BATCH, SEQ, 1), consts)


def _mem_kv_kernel(a_ref, w_ref, o_ref, wb_ref):
    @pl.when(pl.program_id(1) == 0)
    def _():
        wb_ref[...] = w_ref[...].astype(BF16)

    o_ref[...] = _dot(a_ref[...].astype(BF16), wb_ref[...]).astype(BF16)


def _mem_kv(mem, w_mem_kv):
    a = mem.reshape(BATCH * MEM_LEN, D_MODEL)
    tm, tn = 1024, 2 * GROUP_WIDTH
    return pl.pallas_call(
        _mem_kv_kernel,
        out_shape=jax.ShapeDtypeStruct((BATCH * MEM_LEN, DEPTH * tn), BF16),
        grid=(DEPTH, BATCH * MEM_LEN // tm),
        in_specs=[pl.BlockSpec((tm, D_MODEL), lambda l, i: (i, 0)),
                  pl.BlockSpec((None, D_MODEL, tn), lambda l, i: (l, 0, 0))],
        out_specs=pl.BlockSpec((tm, tn), lambda l, i: (i, l)),
        scratch_shapes=[pltpu.VMEM((D_MODEL, tn), BF16)],
        compiler_params=_cparams(("parallel", "arbitrary")),
        name="mem_kv",
    )(a, w_mem_kv)


def _inproj_kernel(x_ref, w_ref, o16_ref, o32_ref):
    xb = x_ref[...].astype(BF16)
    for t in range(NBLK * LANE // MXU_WIDTH):
        r = _dot_nt(xb, w_ref[t * MXU_WIDTH:(t + 1) * MXU_WIDTH, :])
        for half, blk in enumerate((2 * t, 2 * t + 1)):
            o_ref, k = (o16_ref, blk) if blk < NBLK16 else (o32_ref, blk - NBLK16)
            o_ref[k] = r[:, half * LANE:(half + 1) * LANE].astype(o_ref.dtype)


def _inproj(x2d, wt_p, layer):
    return pl.pallas_call(
        _inproj_kernel,
        out_shape=(jax.ShapeDtypeStruct((NBLK16, ROWS, LANE), BF16),
                   jax.ShapeDtypeStruct((NBLK32, ROWS, LANE), F32)),
        grid=(ROWS // TM_IN,),
        in_specs=[pl.BlockSpec((TM_IN, D_MODEL), lambda i: (i, 0)),
                  pl.BlockSpec((None, NBLK * LANE, D_MODEL), lambda i: (layer, 0, 0),
                               pipeline_mode=pl.Buffered(1))],
        out_specs=(pl.BlockSpec((NBLK16, TM_IN, LANE), lambda i: (0, i, 0)),
                   pl.BlockSpec((NBLK32, TM_IN, LANE), lambda i: (0, i, 0))),
        compiler_params=_cparams(("parallel",)),
        name="inproj",
    )(x2d, wt_p)


def _w_in_groups():
    off = np.concatenate([[0], np.cumsum(IN_SPLITS)]).tolist()
    (o_rq, o_rk, o_rv, o_cq, _, o_kpe, o_lx, o_lv, o_lo, o_li, _, o_mq, o_z, _) = off
    groups = [o_rq, o_rk, o_rv, o_lv, o_lo, o_mq, o_z, o_z + GROUP_WIDTH, o_z + 3 * GROUP_WIDTH,
              o_cq, o_lx, o_z + 2 * GROUP_WIDTH]
    return groups, o_kpe, o_li


def _w_in_kernel(tbl_ref, w_ref, kpe_ref, gate_ref, o_ref):
    del tbl_ref
    ngroup = NBLK // HEADS

    @pl.when(pl.program_id(1) < ngroup)
    def _():
        o_ref[0] = w_ref[0].astype(BF16)

    @pl.when(pl.program_id(1) == ngroup)
    def _():
        half = MLA_ROPE // 2
        zeros = lambda n: jnp.zeros((n, D_MODEL), F32)
        kpe = kpe_ref[0]
        blk = jnp.concatenate([kpe[0:half], gate_ref[0], zeros(64 - half - 2 * HEADS),
                               kpe[half:2 * half], zeros(64 - half)], axis=0)
        o_ref[0, 0:LANE, :] = blk.astype(BF16)
        o_ref[0, LANE:, :] = jnp.zeros((GROUP_WIDTH - LANE, D_MODEL), BF16)


def _prep_w_in(w_in):
    wt = jnp.swapaxes(w_in, 1, 2)
    groups, o_kpe, o_li = _w_in_groups()
    assert len(groups) * HEADS + 2 == NBLK and o_kpe % LANE == 0 and o_li % (2 * HEADS) == 0
    sub = 8
    assert all(g % sub == 0 for g in groups)
    table = jnp.asarray([g // sub for g in groups] + [0], jnp.int32)
    grid_spec = pltpu.PrefetchScalarGridSpec(
        num_scalar_prefetch=1,
        grid=(DEPTH, len(groups) + 1),
        in_specs=[pl.BlockSpec((pl.Element(1), pl.Element(GROUP_WIDTH), pl.Element(D_MODEL)),
                               lambda l, j, tbl: (l, pl.multiple_of(tbl[j] * sub, sub), 0)),
                  pl.BlockSpec((1, LANE, D_MODEL), lambda l, j, tbl: (l, o_kpe // LANE, 0)),
                  pl.BlockSpec((1, 2 * HEADS, D_MODEL), lambda l, j, tbl: (l, o_li // (2 * HEADS), 0))],
        out_specs=pl.BlockSpec((1, GROUP_WIDTH, D_MODEL), lambda l, j, tbl: (l, j, 0)),
    )
    return pl.pallas_call(
        _w_in_kernel,
        out_shape=jax.ShapeDtypeStruct((DEPTH, NBLK * LANE, D_MODEL), BF16),
        grid_spec=grid_spec,
        compiler_params=_cparams(("parallel", "arbitrary")),
        name="w_in_relayout",
    )(table, wt, wt, wt)


def _group_spec(base, rows):
    nchunk = SEQ // rows
    return pl.BlockSpec((HEADS, rows, LANE), lambda b, c: (base // HEADS, b * nchunk + c, 0))


def _layer_spec(shape, layer):
    nd = len(shape) - 1
    return pl.BlockSpec((None,) + tuple(shape[1:]), lambda *_: (layer,) + (0,) * nd)


def _const_spec(shape):
    nd = len(shape)
    return pl.BlockSpec(tuple(shape), lambda *_: (0,) * nd)


def _ret_kernel(q_ref, k_ref, v_ref, z_ref, cos_ref, sin_ref, gq_ref, gk_ref, gc_ref, g_ref, o_ref, st_ref):
    @pl.when(pl.program_id(1) == 0)
    def _():
        st_ref[...] = jnp.zeros_like(st_ref)

    cs = cos_ref[0]
    sn = sin_ref[0]
    row = lax.broadcasted_iota(jnp.int32, (RCHUNK, RCHUNK), 0)
    col = lax.broadcasted_iota(jnp.int32, (RCHUNK, RCHUNK), 1)
    causal = col <= row
    qbs, kts, vbs, scs = [], [], [], []
    for h in range(HEADS):
        q = q_ref[h].astype(F32)
        q = (q * cs + pltpu.roll(q, HEAD_DIM // 2, 1) * sn) * gq_ref[h]
        k = k_ref[h].astype(F32)
        k = (k * cs + pltpu.roll(k, HEAD_DIM // 2, 1) * sn) * gk_ref[h]
        qb = q.astype(BF16)
        scs.append(_dot_nt(qb, k.astype(BF16)))
        qbs.append(qb)
        kts.append(k.T.astype(BF16))
        vbs.append(v_ref[h])
    outs = []
    for h in range(HEADS):
        st = st_ref[h]
        sc = jnp.where(causal, scs[h], 0.0).astype(BF16)
        outs.append(_dot(sc, vbs[h]) + _dot(qbs[h], st.astype(BF16)))
        st_ref[h] = gc_ref[h] * (st + _dot(kts[h], vbs[h]))
    for h in range(HEADS):
        out = outs[h]
        mu = jnp.mean(out, axis=-1, keepdims=True)
        dev = out - mu
        var = jnp.mean(dev * dev, axis=-1, keepdims=True)
        hn = dev * lax.rsqrt(var + LN_EPS) * g_ref[h]
        o_ref[h] = (hn * _silu(z_ref[h].astype(F32))).astype(BF16)


def _ret_consts():
    log_g = jnp.log1p(-jnp.exp2(-5.0 - jnp.arange(HEADS, dtype=F32)))
    idx = jnp.arange(RCHUNK, dtype=F32)
    full = (HEADS, RCHUNK, HEAD_DIM)
    gq = jnp.broadcast_to(jnp.exp(log_g[:, None] * (idx + 1.0))[..., None], full)
    gk = jnp.broadcast_to((jnp.exp(-log_g[:, None] * (idx + 1.0)) * HEAD_DIM ** -0.5)[..., None], full)
    gc = jnp.broadcast_to(jnp.exp(log_g * RCHUNK)[:, None, None], (HEADS, 1, HEAD_DIM))
    return gq, gk, gc


def _mlstm_kernel(lx_ref, lv_ref, lo_ref, gt_ref, z_ref, cw_ref, cb_ref, wq_ref, wk_ref, gb_ref,
                  skip_ref, g_ref, o_ref, xp_ref, st_ref, m_ref, s_ref, p_ref, *, first=True):
    if first:
        @pl.when(pl.program_id(1) == 0)
        def _():
            xp_ref[:, 0:CONV_PAD, :] = jnp.zeros((HEADS, CONV_PAD, HEAD_DIM), F32)
            st_ref[...] = jnp.zeros_like(st_ref)
            m_ref[...] = jnp.zeros_like(m_ref)

    n = RCHUNK
    ones_nd = jnp.ones((n, HEAD_DIM), BF16)
    mean_dd = jnp.full((HEAD_DIM, HEAD_DIM), 1.0 / HEAD_DIM, BF16)
    row = lax.broadcasted_iota(jnp.int32, (n, n), 0)
    col = lax.broadcasted_iota(jnp.int32, (n, n), 1)
    tril = jnp.where(col <= row, 1.0, 0.0).astype(BF16)
    r128 = lax.broadcasted_iota(jnp.int32, (LANE, LANE), 0)
    c128 = lax.broadcasted_iota(jnp.int32, (LANE, LANE), 1)
    ident = jnp.where(r128 == c128, 1.0, 0.0).astype(BF16)
    srow = lax.broadcasted_iota(jnp.int32, (STRIP, LANE), 0)
    scol = lax.broadcasted_iota(jnp.int32, (STRIP, LANE), 1)

    lane = lax.broadcasted_iota(jnp.int32, (n, LANE), 1)
    x = gt_ref[0] + gb_ref[...]
    x = jnp.where((lane >= GATE_F_LANE) & (lane < GATE_F_LANE + HEADS), _log_sigmoid(x), x) * LOG2E
    x1, x2, x3 = _split3(x)
    cum = _dot(tril, x1) + _dot(tril, x2) + _dot(tril, x3)
    rt = x - pltpu.roll(cum, LANE - (GATE_F_LANE - GATE_I_LANE), 1)
    r1, r2, r3 = _split3(rt)
    rtt = _dot_nt(ident, r1) + _dot_nt(ident, r2) + _dot_nt(ident, r3)

    xcs, ks, qbs, vbs = [], [], [], []
    for h in range(HEADS):
        xp_ref[h, CONV_PAD:, :] = lx_ref[h]
        acc = jnp.zeros((n, HEAD_DIM), F32) + cb_ref[h]
        for j in range(CONV_WIDTH):
            off = CONV_PAD - (CONV_WIDTH - 1) + j
            acc = acc + xp_ref[h, off:off + n, :] * cw_ref[h, j:j + 1, :]
        xp_ref[h, 0:CONV_PAD, :] = lx_ref[h, n - CONV_PAD:n, :]
        xc = _silu(acc)
        xcb = xc.astype(BF16)
        k = _dot(xcb, wk_ref[h]) * HEAD_DIM ** -0.5
        qb = _dot(xcb, wq_ref[h]).astype(BF16)
        s_ref[h] = _dot_nt(qb, k.astype(BF16))
        xcs.append(xc)
        ks.append(k)
        qbs.append(qb)
        vbs.append(jnp.concatenate([lv_ref[h], ones_nd], axis=-1))

    r_rows = [rtt[GATE_I_LANE + h:GATE_I_LANE + h + 1, :] for h in range(HEADS)]
    bases = [m_ref[h] for h in range(HEADS)]
    strips = [[] for _ in range(HEADS)]
    u_lasts = [None] * HEADS
    for i in range(n // STRIP):
        rows = slice(i * STRIP, (i + 1) * STRIP)
        d0 = (i * STRIP // LANE) * LANE
        w = d0 + LANE
        mask = (scol + d0) <= (srow + i * STRIP)
        for h in range(HEADS):
            r_row = r_rows[h]
            if d0 > 0 and (i * STRIP) % LANE == 0:
                bases[h] = jnp.maximum(bases[h], jnp.max(r_row[:, d0 - LANE:d0], axis=-1, keepdims=True))
            rmd = jnp.where(mask, r_row[:, d0:w], -jnp.inf)
            u_col = jnp.maximum(jnp.max(rmd, axis=-1, keepdims=True), bases[h])
            u = jnp.broadcast_to(u_col, (STRIP, LANE))
            p_ref[h, rows, d0:w] = (s_ref[h, rows, d0:w] * jnp.exp2(rmd - u)).astype(BF16)
            for c in range(d0 // LANE):
                cols = slice(c * LANE, (c + 1) * LANE)
                p_ref[h, rows, cols] = (s_ref[h, rows, cols] * jnp.exp2(r_row[:, cols] - u)).astype(BF16)
            if w < n:
                p_ref[h, rows, w:n] = jnp.zeros((STRIP, n - w), BF16)
            strips[h].append(u)
            u_lasts[h] = u_col[STRIP - 1:STRIP, :]
    us = [jnp.concatenate(st_h, axis=0) for st_h in strips]

    cells = []
    for h in range(HEADS):
        li = GATE_I_LANE + h
        lf = GATE_F_LANE + h
        m_st = m_ref[h]
        u, u_last, vb = us[h], u_lasts[h], vbs[h]
        w_inter = jnp.exp2(m_st - u)
        st = st_ref[h]
        intra = _dot(p_ref[h], vb)
        inter = _dot(qbs[h], st.astype(BF16))
        num = intra[:, :LANE] + w_inter * inter[:, :LANE]
        den = intra[:, LANE:] + w_inter * inter[:, LANE:]
        cum_f = jnp.broadcast_to(cum[:, lf:lf + 1], (n, LANE))
        cells.append(num / jnp.maximum(jnp.abs(den), jnp.exp2(-(cum_f + u))))

        decay = jnp.exp2(m_st - u_last)
        kw = ks[h] * jnp.exp2(jnp.broadcast_to(rt[:, li:li + 1], (n, LANE)) - u_last)
        upd = _dot(kw.T.astype(BF16), vb)
        st_ref[h] = decay * st + upd
        m_ref[h] = cum[n - 1:n, lf:lf + 1] + u_last

    for h in range(HEADS):
        cell = cells[h] * jax.nn.sigmoid(lo_ref[h].astype(F32))
        mu = _rowsum_mxu(cell, mean_dd)
        dev = cell - mu
        var = _rowsum_mxu(dev * dev, mean_dd)
        hn = dev * lax.rsqrt(var + LN_EPS) * g_ref[h]
        o_ref[h] = ((hn + skip_ref[h] * xcs[h]) * _silu(z_ref[h])).astype(BF16)


def _prep_mlstm(conv_w, conv_b, w_q, w_k, i_bias, f_bias, skip, norm_g):
    nl = conv_w.shape[0]
    cw = conv_w.reshape(nl, CONV_WIDTH, HEADS, HEAD_DIM).transpose(0, 2, 1, 3)
    cb = conv_b.reshape(nl, HEADS, 1, HEAD_DIM)
    zeros = lambda w: jnp.zeros((nl, w), F32)
    gb = jnp.concatenate([zeros(GATE_I_LANE), i_bias, f_bias, zeros(LANE - GATE_F_LANE - HEADS)], axis=-1)
    return (cw, cb, w_q.astype(BF16), w_k.astype(BF16), gb.reshape(nl, 1, LANE),
            skip.reshape(nl, HEADS, 1, HEAD_DIM), norm_g.reshape(nl, HEADS, 1, HEAD_DIM))


def _mlstm_step_kernel(lx_ref, lv_ref, lo_ref, gt_ref, z_ref, *rest):
    *params, o_ref, xp_ref, st_ref, m_ref, s_ref, p_ref = rest
    for sub in range(MLSTM_SUBS):
        rows = pl.ds(sub * RCHUNK, RCHUNK)
        view = lambda ref: ref.at[:, rows, :]
        _mlstm_kernel(view(lx_ref), view(lv_ref), view(lo_ref), view(gt_ref), view(z_ref), *params,
                      view(o_ref), xp_ref, st_ref, m_ref, s_ref, p_ref, first=sub == 0)


def _mlstm(hb16, hb32, params, layer):
    rows = MLSTM_SUBS * RCHUNK
    nchunk = SEQ // rows
    return pl.pallas_call(
        _mlstm_step_kernel,
        out_shape=jax.ShapeDtypeStruct((HEADS, ROWS, LANE), BF16),
        grid=(BATCH, nchunk),
        in_specs=[_group_spec(BLK_LX, rows), _group_spec(BLK_LV, rows), _group_spec(BLK_LO, rows),
                  pl.BlockSpec((1, rows, LANE), lambda b, c: (BLK_KPE, b * nchunk + c, 0)),
                  _group_spec(BLK_ZC, rows)] + [_layer_spec(p.shape, layer) for p in params],
        out_specs=pl.BlockSpec((HEADS, rows, LANE), lambda b, c: (0, b * nchunk + c, 0)),
        scratch_shapes=[pltpu.VMEM((HEADS, RCHUNK + CONV_PAD, HEAD_DIM), F32),
                        pltpu.VMEM((HEADS, HEAD_DIM, 2 * HEAD_DIM), F32),
                        pltpu.VMEM((HEADS, 1, 1), F32),
                        pltpu.VMEM((HEADS, RCHUNK, RCHUNK), F32),
                        pltpu.VMEM((HEADS, RCHUNK, RCHUNK), BF16)],
        compiler_params=_cparams(("parallel", "arbitrary")),
        name="mlstm",
    )(hb32, hb16, hb16, hb32, hb32, *params)


def _mla_prep_kernel(lat_ref, kpe_ref, cm_ref, sm_ref, qg_ref, wuq_ref, kvg_ref, wukv_ref,
                     q_out, k_out, v_out):
    scale = (MLA_NOPE + MLA_ROPE) ** -0.5 * LOG2E
    cs = cm_ref[0]
    sn = sm_ref[0]
    cq = jnp.concatenate([lat_ref[0], lat_ref[1], lat_ref[2]], axis=-1).astype(F32)
    qn = cq * lax.rsqrt(jnp.mean(jnp.square(cq), axis=-1, keepdims=True) + RMS_EPS) * qg_ref[...]
    q = _dot(qn.astype(BF16), wuq_ref[...])
    ckv = lat_ref[3].astype(F32)
    kvn = ckv * lax.rsqrt(jnp.mean(jnp.square(ckv), axis=-1, keepdims=True) + RMS_EPS) * kvg_ref[...]
    kv = _dot(kvn.astype(BF16), wukv_ref[...])
    kpe = kpe_ref[0]
    krot = (kpe * cs + pltpu.roll(kpe, LANE // 2, 1) * sn).astype(BF16)
    for h in range(HEADS):
        qr = q[:, GROUP_WIDTH + h * LANE:GROUP_WIDTH + (h + 1) * LANE]
        qr = qr * cs + pltpu.roll(qr, LANE // 2, 1) * sn
        q_out[0, h, :, 0:LANE] = (q[:, h * LANE:(h + 1) * LANE] * scale).astype(BF16)
        q_out[0, h, :, LANE:2 * LANE] = (qr * scale).astype(BF16)
        k_out[0, h, :, 0:LANE] = kv[:, h * LANE:(h + 1) * LANE].astype(BF16)
        k_out[0, h, :, LANE:2 * LANE] = krot
        v_out[0, h] = kv[:, GROUP_WIDTH + h * LANE:GROUP_WIDTH + (h + 1) * LANE].astype(BF16)


def _prep_mla_weights(w_uq, w_ukv):
    lead = w_uq.shape[:-1]
    wq = w_uq.reshape(lead + (HEADS, MLA_NOPE + MLA_ROPE))
    nope = wq[..., :MLA_NOPE].reshape(lead + (GROUP_WIDTH,))
    half = MLA_ROPE // 2
    zeros = jnp.zeros(lead + (HEADS, 64 - half), w_uq.dtype)
    rope = jnp.concatenate([wq[..., MLA_NOPE:MLA_NOPE + half], zeros, wq[..., MLA_NOPE + half:], zeros], axis=-1)
    wq_p = jnp.concatenate([nope, rope.reshape(lead + (HEADS * LANE,))], axis=-1).astype(BF16)
    lead = w_ukv.shape[:-1]
    wkv = w_ukv.reshape(lead + (HEADS, MLA_NOPE + HEAD_DIM))
    wkv_p = jnp.concatenate([wkv[..., :MLA_NOPE].reshape(lead + (GROUP_WIDTH,)),
                             wkv[..., MLA_NOPE:].reshape(lead + (GROUP_WIDTH,))], axis=-1).astype(BF16)
    return wq_p, wkv_p


def _attn_tile(nfull, q_ref, k_ref, v_ref, z_ref, o_ref, s_ref, p_ref, m_ref):
    kvlen = (nfull + 1) * TQ_ATT
    d0 = nfull * TQ_ATT
    hq = TQ_ATT // 2
    srow = lax.broadcasted_iota(jnp.int32, (STRIP, LANE), 0)
    scol = lax.broadcasted_iota(jnp.int32, (STRIP, LANE), 1)

    def strip_blocks(g, i):
        rows = slice(i * STRIP, (i + 1) * STRIP)
        wd = -(-(i + 1) * STRIP // LANE) * LANE
        ncol = (d0 + wd) // LANE
        blks = [s_ref[g, rows, c * LANE:(c + 1) * LANE] for c in range(ncol)]
        blks[-1] = jnp.where(scol + (wd - LANE) <= srow + i * STRIP, blks[-1], -jnp.inf)
        return rows, ncol, blks

    qrows = slice(d0, d0 + TQ_ATT)
    for g in range(ATT_HEADS):
        q = q_ref[0, g, qrows, :]
        for j in range(nfull):
            s_ref[g, :, j * TQ_ATT:(j + 1) * TQ_ATT] = _dot_nt(q, k_ref[0, g, j * TQ_ATT:(j + 1) * TQ_ATT, :])
        s_ref[g, 0:hq, d0:d0 + hq] = _dot_nt(q[0:hq], k_ref[0, g, d0:d0 + hq, :])
        s_ref[g, hq:TQ_ATT, d0:kvlen] = _dot_nt(q[hq:TQ_ATT], k_ref[0, g, d0:kvlen, :])
    for g in range(ATT_HEADS):
        for i in range(TQ_ATT // STRIP):
            rows, ncol, blks = strip_blocks(g, i)
            mx = blks[0]
            for blk in blks[1:]:
                mx = jnp.maximum(mx, blk)
            m_ref[g, rows, :] = jnp.broadcast_to(jnp.max(mx, axis=-1, keepdims=True), (STRIP, LANE))
        for i in range(TQ_ATT // STRIP):
            rows, ncol, blks = strip_blocks(g, i)
            m = m_ref[g, rows, :]
            for c, blk in enumerate(blks):
                p_ref[g, rows, c * LANE:(c + 1) * LANE] = jnp.exp2(blk - m).astype(BF16)
            kl = d0 + hq if (i + 1) * STRIP <= hq else kvlen
            if ncol * LANE < kl:
                p_ref[g, rows, ncol * LANE:kl] = jnp.zeros((STRIP, kl - ncol * LANE), BF16)
        v1 = jnp.concatenate([v_ref[0, g, 0:kvlen, :], jnp.ones((kvlen, LANE), BF16)], axis=-1)
        for r0, r1, kl in ((0, hq, d0 + hq), (hq, TQ_ATT, kvlen)):
            pv = _dot(p_ref[g, r0:r1, 0:kl], v1[0:kl])
            gate = _silu(z_ref[g, d0 + r0:d0 + r1, :].astype(F32))
            o_ref[g, d0 + r0:d0 + r1, :] = (pv[:, :LANE] / pv[:, LANE:] * gate).astype(BF16)


def _mla_attn_kernel(q_ref, k_ref, v_ref, z_ref, o_ref, s_ref, p_ref, m_ref):
    for nfull in range(SEQ // TQ_ATT):
        _attn_tile(nfull, q_ref, k_ref, v_ref, z_ref, o_ref, s_ref, p_ref, m_ref)


def _mla_attn(q, k, v, hb16):
    return pl.pallas_call(
        _mla_attn_kernel,
        out_shape=jax.ShapeDtypeStruct((HEADS, ROWS, LANE), BF16),
        grid=(BATCH, HEADS // ATT_HEADS),
        in_specs=[pl.BlockSpec((1, ATT_HEADS, SEQ, 2 * LANE), lambda b, h: (b, h, 0, 0)),
                  pl.BlockSpec((1, ATT_HEADS, SEQ, 2 * LANE), lambda b, h: (b, h, 0, 0)),
                  pl.BlockSpec((1, ATT_HEADS, SEQ, LANE), lambda b, h: (b, h, 0, 0)),
                  pl.BlockSpec((ATT_HEADS, SEQ, LANE), lambda b, h: (BLK_ZB // ATT_HEADS + h, b, 0))],
        out_specs=pl.BlockSpec((ATT_HEADS, SEQ, LANE), lambda b, h: (h, b, 0)),
        scratch_shapes=[pltpu.VMEM((ATT_HEADS, TQ_ATT, SEQ), F32), pltpu.VMEM((ATT_HEADS, TQ_ATT, SEQ), BF16),
                        pltpu.VMEM((ATT_HEADS, TQ_ATT, LANE), F32)],
        compiler_params=_cparams(("parallel", "parallel")),
        name="mla_attn",
    )(q, k, v, hb16)


def _mem_attn_kernel(q_ref, k_ref, v_ref, z_ref, o_ref):
    ones = jnp.ones((MEM_LEN, LANE), BF16)
    scores = []
    for h in range(HEADS):
        q = (q_ref[h].astype(F32) * (HEAD_DIM ** -0.5 * LOG2E)).astype(BF16)
        scores.append(_dot_nt(q, k_ref[:, h * LANE:(h + 1) * LANE]))
    for h in range(HEADS):
        s = scores[h]
        p = jnp.exp2(s - jnp.max(s, axis=-1, keepdims=True)).astype(BF16)
        pv = _dot(p, jnp.concatenate([v_ref[:, h * LANE:(h + 1) * LANE], ones], axis=-1))
        o_ref[h] = (pv[:, :LANE] / pv[:, LANE:] * _silu(z_ref[h].astype(F32))).astype(BF16)


def _light_kernel(rq, rk, rv, rz, cr, sr, gq, gk, gc, rg, mq, mk, mv, mz, lat, kpe, cm, sm, qg, wuq, kvg, wukv,
                  ya, yd, q_out, k_out, v_out, st_ref):
    _ret_kernel(rq, rk, rv, rz, cr, sr, gq, gk, gc, rg, ya, st_ref)
    _mem_attn_kernel(mq, mk, mv, mz, yd)
    _mla_prep_kernel(lat, kpe, cm, sm, qg, wuq, kvg, wukv, q_out, k_out, v_out)


def _light_mixers(hb16, hb32, tables, ret_consts, ret_g, kvm, qg, wuq_p, kvg, wukv_p, layer):
    cos_r, sin_r, cos_m, sin_m = tables
    gq, gk, gc = ret_consts
    nchunk = SEQ // RCHUNK
    tab = pl.BlockSpec((1, RCHUNK, LANE), lambda b, c: (b, c, 0))
    row_blk = lambda blk: pl.BlockSpec((1, RCHUNK, LANE), lambda b, c: (blk, b * nchunk + c, 0))
    y_shape = jax.ShapeDtypeStruct((HEADS, ROWS, LANE), BF16)
    y_spec = pl.BlockSpec((HEADS, RCHUNK, LANE), lambda b, c: (0, b * nchunk + c, 0))
    qk_shape = jax.ShapeDtypeStruct((BATCH, HEADS, SEQ, 2 * LANE), BF16)
    v_shape = jax.ShapeDtypeStruct((BATCH, HEADS, SEQ, LANE), BF16)
    qk_spec = pl.BlockSpec((1, HEADS, RCHUNK, 2 * LANE), lambda b, c: (b, 0, c, 0))
    v_spec = pl.BlockSpec((1, HEADS, RCHUNK, LANE), lambda b, c: (b, 0, c, 0))
    return pl.pallas_call(
        _light_kernel,
        out_shape=(y_shape, y_shape, qk_shape, qk_shape, v_shape),
        grid=(BATCH, nchunk),
        in_specs=[_group_spec(BLK_RQ, RCHUNK), _group_spec(BLK_RK, RCHUNK), _group_spec(BLK_RV, RCHUNK),
                  _group_spec(BLK_ZA, RCHUNK), tab, tab,
                  _const_spec(gq.shape), _const_spec(gk.shape), _const_spec(gc.shape),
                  _layer_spec(ret_g.shape, layer),
                  _group_spec(BLK_MQ, RCHUNK),
                  pl.BlockSpec((MEM_LEN, GROUP_WIDTH), lambda b, c: (b, 2 * layer)),
                  pl.BlockSpec((MEM_LEN, GROUP_WIDTH), lambda b, c: (b, 2 * layer + 1)),
                  _group_spec(BLK_ZD, RCHUNK),
                  _group_spec(BLK_CQ, RCHUNK), row_blk(BLK_KPE), tab, tab,
                  _layer_spec(qg.shape, layer), _layer_spec(wuq_p.shape, layer),
                  _layer_spec(kvg.shape, layer), _layer_spec(wukv_p.shape, layer)],
        out_specs=(y_spec, y_spec, qk_spec, qk_spec, v_spec),
        scratch_shapes=[pltpu.VMEM((HEADS, HEAD_DIM, HEAD_DIM), F32)],
        compiler_params=_cparams(("parallel", "arbitrary")),
        name="light_mixers",
    )(hb16, hb16, hb16, hb16, cos_r, sin_r, gq, gk, gc, ret_g, hb16, kvm, kvm, hb16, hb16, hb32, cos_m, sin_m,
      qg, wuq_p, kvg, wukv_p)


def _out_kernel(ya_ref, yb_ref, yc_ref, yd_ref, w_ref, x_ref, g_ref, b_ref, o_ref, *rest):
    *bf16_out, wb_ref = rest

    @pl.when(pl.program_id(0) == 0)
    def _():
        wb_ref[...] = w_ref[...].astype(BF16)

    for t in range(TM_OUT // SUB_OUT):
        rows = slice(t * SUB_OUT, (t + 1) * SUB_OUT)
        parts = [ref[h, rows, :] for ref in (ya_ref, yb_ref, yc_ref, yd_ref) for h in range(HEADS)]
        y = jnp.concatenate(parts, axis=-1)
        for s in range(D_MODEL // MXU_WIDTH):
            cols = slice(s * MXU_WIDTH, (s + 1) * MXU_WIDTH)
            o_ref[rows, cols] = DEEPNORM_ALPHA * x_ref[rows, cols] + _dot(y, wb_ref[:, cols])
        r = o_ref[rows, :]
        mu = jnp.mean(r, axis=-1, keepdims=True)
        var = jnp.mean(jnp.square(r - mu), axis=-1, keepdims=True)
        out = (r - mu) * lax.rsqrt(var + LN_EPS) * g_ref[...] + b_ref[...]
        o_ref[rows, :] = out
        for ob_ref in bf16_out:
            ob_ref[rows, :] = out.astype(BF16)


def _outproj(ya, yb, yc, yd, w_out, x2d, ln_g, ln_b, layer, with_bf16):
    yspec = pl.BlockSpec((HEADS, TM_OUT, LANE), lambda i: (0, i, 0))
    xspec = pl.BlockSpec((TM_OUT, D_MODEL), lambda i: (i, 0))
    dtypes = (F32, BF16) if with_bf16 else (F32,)
    return pl.pallas_call(
        _out_kernel,
        out_shape=tuple(jax.ShapeDtypeStruct((ROWS, D_MODEL), dt) for dt in dtypes),
        grid=(ROWS // TM_OUT,),
        in_specs=[yspec, yspec, yspec, yspec,
                  pl.BlockSpec((None,) + tuple(w_out.shape[1:]), lambda i: (layer, 0, 0),
                               pipeline_mode=pl.Buffered(1)),
                  xspec, _layer_spec(ln_g.shape, layer), _layer_spec(ln_b.shape, layer)],
        out_specs=tuple(xspec for _ in dtypes),
        scratch_shapes=[pltpu.VMEM(tuple(w_out.shape[1:]), BF16)],
        compiler_params=_cparams(("arbitrary",)),
        name="outproj_ln",
    )(ya, yb, yc, yd, w_out, x2d, ln_g, ln_b)


def kernel(x, mem, positions, w_in, ret_norm_g, mla_q_norm_g, mla_w_uq, mla_kv_norm_g, mla_w_ukv, ml_conv_w, ml_conv_b, ml_w_q, ml_w_k, ml_i_bias, ml_f_bias, ml_skip, ml_norm_g, w_mem_kv, w_out, ln_g, ln_b):
    assert x.shape == (BATCH, SEQ, D_MODEL) and mem.shape == (BATCH, MEM_LEN, D_MODEL)
    tables = _rope_tables(positions)
    kvm = _mem_kv(mem, w_mem_kv)
    w_in_p = _prep_w_in(w_in)
    wuq_p, wukv_p = _prep_mla_weights(mla_w_uq, mla_w_ukv)
    ret_consts = _ret_consts()
    ret_g = ret_norm_g.reshape(DEPTH, HEADS, 1, HEAD_DIM)
    mlstm_params = _prep_mlstm(ml_conv_w, ml_conv_b, ml_w_q, ml_w_k, ml_i_bias, ml_f_bias, ml_skip, ml_norm_g)
    qg = mla_q_norm_g.reshape(DEPTH, 1, MLA_Q_RANK)
    kvg = mla_kv_norm_g.reshape(DEPTH, 1, MLA_KV_RANK)
    lng = ln_g.reshape(DEPTH, 1, D_MODEL)
    lnb = ln_b.reshape(DEPTH, 1, D_MODEL)

    x2d = x.reshape(ROWS, D_MODEL)
    xb = []
    for l in range(DEPTH):
        hb16, hb32 = _inproj(xb[0] if xb else x2d, w_in_p, l)
        ya, yd, q, k, v = _light_mixers(hb16, hb32, tables, ret_consts, ret_g, kvm, qg, wuq_p, kvg, wukv_p, l)
        yb = _mla_attn(q, k, v, hb16)
        yc = _mlstm(hb16, hb32, mlstm_params, l)
        x2d, *xb = _outproj(ya, yb, yc, yd, w_out, x2d, lng, lnb, l, with_bf16=l + 1 < DEPTH)
    return x2d.reshape(BATCH, SEQ, D_MODEL)
```

```python
import numpy as np
import jax
import jax.numpy as jnp
from jax import lax
from jax.experimental import pallas as pl
from jax.experimental.pallas import tpu as pltpu

F32 = jnp.float32
BF16 = jnp.bfloat16

D_MODEL = 2048
BATCH = 8
SEQ = 2048
DEPTH = 4
MEM_LEN = 256
HEAD_DIM = 128
HEADS = 4
GROUP_WIDTH = HEADS * HEAD_DIM
MLA_NOPE = 128
MLA_ROPE = 64
MLA_Q_RANK = 384
MLA_KV_RANK = 128
CONV_WIDTH = 4
MIX_WIDTH = 4 * GROUP_WIDTH
ROPE_THETA = 10000.0
LN_EPS = 1e-5
RMS_EPS = 1e-6
DEEPNORM_ALPHA = (2 * DEPTH) ** 0.25
IN_SPLITS = (512, 512, 512, MLA_Q_RANK, MLA_KV_RANK, MLA_ROPE, 512, 512, 512, HEADS, HEADS, 512, MIX_WIDTH)

LANE = 128
MXU_WIDTH = 256
ROWS = BATCH * SEQ

BLK_RQ, BLK_RK, BLK_RV = 0, 4, 8
BLK_LV, BLK_LO = 12, 16
BLK_MQ = 20
BLK_ZA, BLK_ZB, BLK_ZD = 24, 28, 32
BLK_CQ, BLK_CKV = 36, 39
NBLK16 = 40
BLK_LX, BLK_ZC = 0, 4
BLK_KPE = 8
NBLK32 = 10
NBLK = NBLK16 + NBLK32
GATE_I_LANE, GATE_F_LANE = 32, 36

TM_IN = 512
TM_OUT = 512
SUB_OUT = 256
TQ_ATT = 512
ATT_HEADS = 2
RCHUNK = 512
CONV_PAD = 8
STRIP = 64
LOG2E = 1.4426950408889634
VMEM_LIMIT = 56 * 1024 * 1024


def _cparams(sem):
    return pltpu.CompilerParams(dimension_semantics=sem, vmem_limit_bytes=VMEM_LIMIT)


def _silu(z):
    return z * jax.nn.sigmoid(z)


def _log_sigmoid(x):
    return jnp.minimum(x, 0.0) - jnp.log1p(jnp.exp(-jnp.abs(x)))


def _split3(x):
    x1 = x.astype(BF16)
    r1 = x - x1.astype(F32)
    x2 = r1.astype(BF16)
    x3 = (r1 - x2.astype(F32)).astype(BF16)
    return x1, x2, x3


def _dot(a, b):
    return jnp.dot(a, b, preferred_element_type=F32)


def _rowsum_mxu(x, w):
    hi = x.astype(BF16)
    lo = (x - hi.astype(F32)).astype(BF16)
    return _dot(hi, w) + _dot(lo, w)


def _dot_nt(a, b):
    return lax.dot_general(a, b, (((1,), (1,)), ((), ())), preferred_element_type=F32)


def _tables_kernel(pos_ref, c_ref, cr_ref, sr_ref, cm_ref, sm_ref):
    pos = pos_ref[0].astype(F32)
    ang = pos * c_ref[0:1, :]
    cs = jnp.cos(ang)
    sn = jnp.sin(ang)
    cs_sw = pltpu.roll(cs, LANE // 2, 1)
    sn_sw = pltpu.roll(sn, LANE // 2, 1)
    lane = lax.broadcasted_iota(jnp.int32, cs.shape, 1)
    half_m = MLA_ROPE // 2
    first = lane < LANE // 2
    m_lo = lane < half_m
    m_hi = (lane >= LANE // 2) & (lane < LANE // 2 + half_m)
    cr_ref[0] = jnp.where(first, cs, cs_sw)
    sr_ref[0] = jnp.where(first, -sn, sn_sw)
    cm_ref[0] = jnp.where(m_lo, cs_sw, jnp.where(m_hi, cs, 0.0))
    sm_ref[0] = jnp.where(m_lo, -sn_sw, jnp.where(m_hi, sn, 0.0))


def _rope_tables(positions):
    half_r = HEAD_DIM // 2
    fr = ROPE_THETA ** (-jnp.arange(half_r, dtype=F32) / half_r)
    half_m = MLA_ROPE // 2
    fm = ROPE_THETA ** (-jnp.arange(half_m, dtype=F32) / half_m)
    freqs = jnp.concatenate([fr, fm, jnp.zeros((LANE - half_r - half_m,), F32)])
    consts = jnp.concatenate([freqs[None, :], jnp.zeros((7, LANE), F32)], axis=0)
    ts = RCHUNK
    tab = jax.ShapeDtypeStruct((BATCH, SEQ, LANE), F32)
    spec = pl.BlockSpec((1, ts, LANE), lambda b, i: (b, i, 0))
    return pl.pallas_call(
        _tables_kernel,
        out_shape=(tab, tab, tab, tab),
        grid=(BATCH, SEQ // ts),
        in_specs=[pl.BlockSpec((1, ts, 1), lambda b, i: (b, i, 0)),
                  pl.BlockSpec((8, LANE), lambda b, i: (0, 0))],
        out_specs=(spec, spec, spec, spec),
        compiler_params=_cparams(("parallel", "parallel")),
        name="rope_tables",
    )(positions.reshape(BATCH, SEQ, 1), consts)


def _mem_kv_kernel(a_ref, w_ref, o_ref, wb_ref):
    @pl.when(pl.program_id(1) == 0)
    def _():
        wb_ref[...] = w_ref[...].astype(BF16)

    o_ref[...] = _dot(a_ref[...].astype(BF16), wb_ref[...]).astype(BF16)


def _mem_kv(mem, w_mem_kv):
    a = mem.reshape(BATCH * MEM_LEN, D_MODEL)
    tm, tn = 1024, 2 * GROUP_WIDTH
    return pl.pallas_call(
        _mem_kv_kernel,
        out_shape=jax.ShapeDtypeStruct((BATCH * MEM_LEN, DEPTH * tn), BF16),
        grid=(DEPTH, BATCH * MEM_LEN // tm),
        in_specs=[pl.BlockSpec((tm, D_MODEL), lambda l, i: (i, 0)),
                  pl.BlockSpec((None, D_MODEL, tn), lambda l, i: (l, 0, 0))],
        out_specs=pl.BlockSpec((tm, tn), lambda l, i: (i, l)),
        scratch_shapes=[pltpu.VMEM((D_MODEL, tn), BF16)],
        compiler_params=_cparams(("parallel", "arbitrary")),
        name="mem_kv",
    )(a, w_mem_kv)


def _inproj_kernel(x_ref, w_ref, o16_ref, o32_ref):
    xb = x_ref[...].astype(BF16)
    for t in range(NBLK * LANE // MXU_WIDTH):
        r = _dot_nt(xb, w_ref[t * MXU_WIDTH:(t + 1) * MXU_WIDTH, :])
        for half, blk in enumerate((2 * t, 2 * t + 1)):
            o_ref, k = (o16_ref, blk) if blk < NBLK16 else (o32_ref, blk - NBLK16)
            o_ref[k] = r[:, half * LANE:(half + 1) * LANE].astype(o_ref.dtype)


def _inproj(x2d, wt_p, layer):
    return pl.pallas_call(
        _inproj_kernel,
        out_shape=(jax.ShapeDtypeStruct((NBLK16, ROWS, LANE), BF16),
                   jax.ShapeDtypeStruct((NBLK32, ROWS, LANE), F32)),
        grid=(ROWS // TM_IN,),
        in_specs=[pl.BlockSpec((TM_IN, D_MODEL), lambda i: (i, 0)),
                  pl.BlockSpec((None, NBLK * LANE, D_MODEL), lambda i: (layer, 0, 0),
                               pipeline_mode=pl.Buffered(1))],
        out_specs=(pl.BlockSpec((NBLK16, TM_IN, LANE), lambda i: (0, i, 0)),
                   pl.BlockSpec((NBLK32, TM_IN, LANE), lambda i: (0, i, 0))),
        compiler_params=_cparams(("parallel",)),
        name="inproj",
    )(x2d, wt_p)


def _w_in_groups():
    off = np.concatenate([[0], np.cumsum(IN_SPLITS)]).tolist()
    (o_rq, o_rk, o_rv, o_cq, _, o_kpe, o_lx, o_lv, o_lo, o_li, _, o_mq, o_z, _) = off
    groups = [o_rq, o_rk, o_rv, o_lv, o_lo, o_mq, o_z, o_z + GROUP_WIDTH, o_z + 3 * GROUP_WIDTH,
              o_cq, o_lx, o_z + 2 * GROUP_WIDTH]
    return groups, o_kpe, o_li


def _w_in_kernel(tbl_ref, w_ref, kpe_ref, gate_ref, o_ref):
    del tbl_ref
    ngroup = NBLK // HEADS

    @pl.when(pl.program_id(1) < ngroup)
    def _():
        o_ref[0] = w_ref[0].astype(BF16)

    @pl.when(pl.program_id(1) == ngroup)
    def _():
        half = MLA_ROPE // 2
        zeros = lambda n: jnp.zeros((n, D_MODEL), F32)
        kpe = kpe_ref[0]
        blk = jnp.concatenate([kpe[0:half], gate_ref[0], zeros(64 - half - 2 * HEADS),
                               kpe[half:2 * half], zeros(64 - half)], axis=0)
        o_ref[0, 0:LANE, :] = blk.astype(BF16)
        o_ref[0, LANE:, :] = jnp.zeros((GROUP_WIDTH - LANE, D_MODEL), BF16)


def _prep_w_in(w_in):
    wt = jnp.swapaxes(w_in, 1, 2)
    groups, o_kpe, o_li = _w_in_groups()
    assert len(groups) * HEADS + 2 == NBLK and o_kpe % LANE == 0 and o_li % (2 * HEADS) == 0
    sub = 8
    assert all(g % sub == 0 for g in groups)
    table = jnp.asarray([g // sub for g in groups] + [0], jnp.int32)
    grid_spec = pltpu.PrefetchScalarGridSpec(
        num_scalar_prefetch=1,
        grid=(DEPTH, len(groups) + 1),
        in_specs=[pl.BlockSpec((pl.Element(1), pl.Element(GROUP_WIDTH), pl.Element(D_MODEL)),
                               lambda l, j, tbl: (l, pl.multiple_of(tbl[j] * sub, sub), 0)),
                  pl.BlockSpec((1, LANE, D_MODEL), lambda l, j, tbl: (l, o_kpe // LANE, 0)),
                  pl.BlockSpec((1, 2 * HEADS, D_MODEL), lambda l, j, tbl: (l, o_li // (2 * HEADS), 0))],
        out_specs=pl.BlockSpec((1, GROUP_WIDTH, D_MODEL), lambda l, j, tbl: (l, j, 0)),
    )
    return pl.pallas_call(
        _w_in_kernel,
        out_shape=jax.ShapeDtypeStruct((DEPTH, NBLK * LANE, D_MODEL), BF16),
        grid_spec=grid_spec,
        compiler_params=_cparams(("parallel", "arbitrary")),
        name="w_in_relayout",
    )(table, wt, wt, wt)


def _group_spec(base, rows):
    nchunk = SEQ // rows
    return pl.BlockSpec((HEADS, rows, LANE), lambda b, c: (base // HEADS, b * nchunk + c, 0))


def _layer_spec(shape, layer):
    nd = len(shape) - 1
    return pl.BlockSpec((None,) + tuple(shape[1:]), lambda *_: (layer,) + (0,) * nd)


def _const_spec(shape):
    nd = len(shape)
    return pl.BlockSpec(tuple(shape), lambda *_: (0,) * nd)


def _ret_kernel(q_ref, k_ref, v_ref, z_ref, cos_ref, sin_ref, gq_ref, gk_ref, gc_ref, g_ref, o_ref, st_ref):
    @pl.when(pl.program_id(1) == 0)
    def _():
        st_ref[...] = jnp.zeros_like(st_ref)

    cs = cos_ref[0]
    sn = sin_ref[0]
    row = lax.broadcasted_iota(jnp.int32, (RCHUNK, RCHUNK), 0)
    col = lax.broadcasted_iota(jnp.int32, (RCHUNK, RCHUNK), 1)
    causal = col <= row
    qbs, kts, vbs, scs = [], [], [], []
    for h in range(HEADS):
        q = q_ref[h].astype(F32)
        q = (q * cs + pltpu.roll(q, HEAD_DIM // 2, 1) * sn) * gq_ref[h]
        k = k_ref[h].astype(F32)
        k = (k * cs + pltpu.roll(k, HEAD_DIM // 2, 1) * sn) * gk_ref[h]
        qb = q.astype(BF16)
        scs.append(_dot_nt(qb, k.astype(BF16)))
        qbs.append(qb)
        kts.append(k.T.astype(BF16))
        vbs.append(v_ref[h])
    outs = []
    for h in range(HEADS):
        st = st_ref[h]
        intra = []
        for i in range(RCHUNK // LANE):
            r0, r1 = i * LANE, (i + 1) * LANE
            blk = jnp.where(causal[r0:r1, 0:r1], scs[h][r0:r1, 0:r1], 0.0).astype(BF16)
            intra.append(_dot(blk, vbs[h][0:r1]))
        outs.append(jnp.concatenate(intra, axis=0) + _dot(qbs[h], st.astype(BF16)))
        st_ref[h] = gc_ref[h] * (st + _dot(kts[h], vbs[h]))
    for h in range(HEADS):
        out = outs[h]
        mu = jnp.mean(out, axis=-1, keepdims=True)
        dev = out - mu
        var = jnp.mean(dev * dev, axis=-1, keepdims=True)
        hn = dev * lax.rsqrt(var + LN_EPS) * g_ref[h]
        o_ref[h] = (hn * _silu(z_ref[h].astype(F32))).astype(BF16)


def _ret_consts():
    log_g = jnp.log1p(-jnp.exp2(-5.0 - jnp.arange(HEADS, dtype=F32)))
    idx = jnp.arange(RCHUNK, dtype=F32)
    full = (HEADS, RCHUNK, HEAD_DIM)
    gq = jnp.broadcast_to(jnp.exp(log_g[:, None] * (idx + 1.0))[..., None], full)
    gk = jnp.broadcast_to((jnp.exp(-log_g[:, None] * (idx + 1.0)) * HEAD_DIM ** -0.5)[..., None], full)
    gc = jnp.broadcast_to(jnp.exp(log_g * RCHUNK)[:, None, None], (HEADS, 1, HEAD_DIM))
    return gq, gk, gc


def _mlstm_kernel(lx_ref, lv_ref, lo_ref, gt_ref, z_ref, cw_ref, cb_ref, wq_ref, wk_ref, gb_ref,
                  skip_ref, g_ref, o_ref, xp_ref, st_ref, m_ref, s_ref, p_ref):
    @pl.when(pl.program_id(1) == 0)
    def _():
        xp_ref[:, 0:CONV_PAD, :] = jnp.zeros((HEADS, CONV_PAD, HEAD_DIM), F32)
        st_ref[...] = jnp.zeros_like(st_ref)
        m_ref[...] = jnp.zeros_like(m_ref)

    n = RCHUNK
    ones_nd = jnp.ones((n, HEAD_DIM), BF16)
    mean_dd = jnp.full((HEAD_DIM, HEAD_DIM), 1.0 / HEAD_DIM, BF16)
    row = lax.broadcasted_iota(jnp.int32, (n, n), 0)
    col = lax.broadcasted_iota(jnp.int32, (n, n), 1)
    tril = jnp.where(col <= row, 1.0, 0.0).astype(BF16)
    r128 = lax.broadcasted_iota(jnp.int32, (LANE, LANE), 0)
    c128 = lax.broadcasted_iota(jnp.int32, (LANE, LANE), 1)
    ident = jnp.where(r128 == c128, 1.0, 0.0).astype(BF16)
    srow = lax.broadcasted_iota(jnp.int32, (STRIP, LANE), 0)
    scol = lax.broadcasted_iota(jnp.int32, (STRIP, LANE), 1)

    lane = lax.broadcasted_iota(jnp.int32, (n, LANE), 1)
    x = gt_ref[0] + gb_ref[...]
    x = jnp.where((lane >= GATE_F_LANE) & (lane < GATE_F_LANE + HEADS), _log_sigmoid(x), x) * LOG2E
    x1, x2, x3 = _split3(x)
    cum = _dot(tril, x1) + _dot(tril, x2) + _dot(tril, x3)
    rt = x - pltpu.roll(cum, LANE - (GATE_F_LANE - GATE_I_LANE), 1)
    r1, r2, r3 = _split3(rt)
    rtt = _dot_nt(ident, r1) + _dot_nt(ident, r2) + _dot_nt(ident, r3)

    xcs, ks, qbs, vbs = [], [], [], []
    for h in range(HEADS):
        xp_ref[h, CONV_PAD:, :] = lx_ref[h]
        acc = jnp.zeros((n, HEAD_DIM), F32) + cb_ref[h]
        for j in range(CONV_WIDTH):
            off = CONV_PAD - (CONV_WIDTH - 1) + j
            acc = acc + xp_ref[h, off:off + n, :] * cw_ref[h, j:j + 1, :]
        xp_ref[h, 0:CONV_PAD, :] = lx_ref[h, n - CONV_PAD:n, :]
        xc = _silu(acc)
        xcb = xc.astype(BF16)
        k = _dot(xcb, wk_ref[h]) * HEAD_DIM ** -0.5
        qb = _dot(xcb, wq_ref[h]).astype(BF16)
        s_ref[h] = _dot_nt(qb, k.astype(BF16))
        xcs.append(xc)
        ks.append(k)
        qbs.append(qb)
        vbs.append(jnp.concatenate([lv_ref[h], ones_nd], axis=-1))

    r_rows = [rtt[GATE_I_LANE + h:GATE_I_LANE + h + 1, :] for h in range(HEADS)]
    bases = [m_ref[h] for h in range(HEADS)]
    strips = [[] for _ in range(HEADS)]
    u_lasts = [None] * HEADS
    for i in range(n // STRIP):
        rows = slice(i * STRIP, (i + 1) * STRIP)
        d0 = (i * STRIP // LANE) * LANE
        w = d0 + LANE
        mask = (scol + d0) <= (srow + i * STRIP)
        for h in range(HEADS):
            r_row = r_rows[h]
            if d0 > 0 and (i * STRIP) % LANE == 0:
                bases[h] = jnp.maximum(bases[h], jnp.max(r_row[:, d0 - LANE:d0], axis=-1, keepdims=True))
            rmd = jnp.where(mask, r_row[:, d0:w], -jnp.inf)
            u_col = jnp.maximum(jnp.max(rmd, axis=-1, keepdims=True), bases[h])
            u = jnp.broadcast_to(u_col, (STRIP, LANE))
            p_ref[h, rows, d0:w] = (s_ref[h, rows, d0:w] * jnp.exp2(rmd - u)).astype(BF16)
            for c in range(d0 // LANE):
                cols = slice(c * LANE, (c + 1) * LANE)
                p_ref[h, rows, cols] = (s_ref[h, rows, cols] * jnp.exp2(r_row[:, cols] - u)).astype(BF16)
            if w < n:
                p_ref[h, rows, w:n] = jnp.zeros((STRIP, n - w), BF16)
            strips[h].append(u)
            u_lasts[h] = u_col[STRIP - 1:STRIP, :]
    us = [jnp.concatenate(st_h, axis=0) for st_h in strips]

    cells = []
    for h in range(HEADS):
        li = GATE_I_LANE + h
        lf = GATE_F_LANE + h
        m_st = m_ref[h]
        u, u_last, vb = us[h], u_lasts[h], vbs[h]
        w_inter = jnp.exp2(m_st - u)
        st = st_ref[h]
        intra = _dot(p_ref[h], vb)
        inter = _dot(qbs[h], st.astype(BF16))
        num = intra[:, :LANE] + w_inter * inter[:, :LANE]
        den = intra[:, LANE:] + w_inter * inter[:, LANE:]
        cum_f = jnp.broadcast_to(cum[:, lf:lf + 1], (n, LANE))
        cells.append(num / jnp.maximum(jnp.abs(den), jnp.exp2(-(cum_f + u))))

        decay = jnp.exp2(m_st - u_last)
        kw = ks[h] * jnp.exp2(jnp.broadcast_to(rt[:, li:li + 1], (n, LANE)) - u_last)
        upd = _dot(kw.T.astype(BF16), vb)
        st_ref[h] = decay * st + upd
        m_ref[h] = cum[n - 1:n, lf:lf + 1] + u_last

    for h in range(HEADS):
        cell = cells[h] * jax.nn.sigmoid(lo_ref[h].astype(F32))
        mu = _rowsum_mxu(cell, mean_dd)
        dev = cell - mu
        var = _rowsum_mxu(dev * dev, mean_dd)
        hn = dev * lax.rsqrt(var + LN_EPS) * g_ref[h]
        o_ref[h] = ((hn + skip_ref[h] * xcs[h]) * _silu(z_ref[h])).astype(BF16)


def _prep_mlstm(conv_w, conv_b, w_q, w_k, i_bias, f_bias, skip, norm_g):
    nl = conv_w.shape[0]
    cw = conv_w.reshape(nl, CONV_WIDTH, HEADS, HEAD_DIM).transpose(0, 2, 1, 3)
    cb = conv_b.reshape(nl, HEADS, 1, HEAD_DIM)
    zeros = lambda w: jnp.zeros((nl, w), F32)
    gb = jnp.concatenate([zeros(GATE_I_LANE), i_bias, f_bias, zeros(LANE - GATE_F_LANE - HEADS)], axis=-1)
    return (cw, cb, w_q.astype(BF16), w_k.astype(BF16), gb.reshape(nl, 1, LANE),
            skip.reshape(nl, HEADS, 1, HEAD_DIM), norm_g.reshape(nl, HEADS, 1, HEAD_DIM))


def _mlstm(hb16, hb32, params, layer):
    nchunk = SEQ // RCHUNK
    return pl.pallas_call(
        _mlstm_kernel,
        out_shape=jax.ShapeDtypeStruct((HEADS, ROWS, LANE), BF16),
        grid=(BATCH, nchunk),
        in_specs=[_group_spec(BLK_LX, RCHUNK), _group_spec(BLK_LV, RCHUNK), _group_spec(BLK_LO, RCHUNK),
                  pl.BlockSpec((1, RCHUNK, LANE), lambda b, c: (BLK_KPE, b * nchunk + c, 0)),
                  _group_spec(BLK_ZC, RCHUNK)] + [_layer_spec(p.shape, layer) for p in params],
        out_specs=pl.BlockSpec((HEADS, RCHUNK, LANE), lambda b, c: (0, b * nchunk + c, 0)),
        scratch_shapes=[pltpu.VMEM((HEADS, RCHUNK + CONV_PAD, HEAD_DIM), F32),
                        pltpu.VMEM((HEADS, HEAD_DIM, 2 * HEAD_DIM), F32),
                        pltpu.VMEM((HEADS, 1, 1), F32),
                        pltpu.VMEM((HEADS, RCHUNK, RCHUNK), F32),
                        pltpu.VMEM((HEADS, RCHUNK, RCHUNK), BF16)],
        compiler_params=_cparams(("parallel", "arbitrary")),
        name="mlstm",
    )(hb32, hb16, hb16, hb32, hb32, *params)


def _mla_prep_kernel(lat_ref, kpe_ref, cm_ref, sm_ref, qg_ref, wuq_ref, kvg_ref, wukv_ref,
                     q_out, k_out, v_out):
    scale = (MLA_NOPE + MLA_ROPE) ** -0.5 * LOG2E
    cs = cm_ref[0]
    sn = sm_ref[0]
    cq = jnp.concatenate([lat_ref[0], lat_ref[1], lat_ref[2]], axis=-1).astype(F32)
    qn = cq * lax.rsqrt(jnp.mean(jnp.square(cq), axis=-1, keepdims=True) + RMS_EPS) * qg_ref[...]
    q = _dot(qn.astype(BF16), wuq_ref[...])
    ckv = lat_ref[3].astype(F32)
    kvn = ckv * lax.rsqrt(jnp.mean(jnp.square(ckv), axis=-1, keepdims=True) + RMS_EPS) * kvg_ref[...]
    kv = _dot(kvn.astype(BF16), wukv_ref[...])
    kpe = kpe_ref[0]
    krot = (kpe * cs + pltpu.roll(kpe, LANE // 2, 1) * sn).astype(BF16)
    for h in range(HEADS):
        qr = q[:, GROUP_WIDTH + h * LANE:GROUP_WIDTH + (h + 1) * LANE]
        qr = qr * cs + pltpu.roll(qr, LANE // 2, 1) * sn
        q_out[0, h, :, 0:LANE] = (q[:, h * LANE:(h + 1) * LANE] * scale).astype(BF16)
        q_out[0, h, :, LANE:2 * LANE] = (qr * scale).astype(BF16)
        k_out[0, h, :, 0:LANE] = kv[:, h * LANE:(h + 1) * LANE].astype(BF16)
        k_out[0, h, :, LANE:2 * LANE] = krot
        v_out[0, h] = kv[:, GROUP_WIDTH + h * LANE:GROUP_WIDTH + (h + 1) * LANE].astype(BF16)


def _prep_mla_weights(w_uq, w_ukv):
    lead = w_uq.shape[:-1]
    wq = w_uq.reshape(lead + (HEADS, MLA_NOPE + MLA_ROPE))
    nope = wq[..., :MLA_NOPE].reshape(lead + (GROUP_WIDTH,))
    half = MLA_ROPE // 2
    zeros = jnp.zeros(lead + (HEADS, 64 - half), w_uq.dtype)
    rope = jnp.concatenate([wq[..., MLA_NOPE:MLA_NOPE + half], zeros, wq[..., MLA_NOPE + half:], zeros], axis=-1)
    wq_p = jnp.concatenate([nope, rope.reshape(lead + (HEADS * LANE,))], axis=-1).astype(BF16)
    lead = w_ukv.shape[:-1]
    wkv = w_ukv.reshape(lead + (HEADS, MLA_NOPE + HEAD_DIM))
    wkv_p = jnp.concatenate([wkv[..., :MLA_NOPE].reshape(lead + (GROUP_WIDTH,)),
                             wkv[..., MLA_NOPE:].reshape(lead + (GROUP_WIDTH,))], axis=-1).astype(BF16)
    return wq_p, wkv_p


def _attn_tile(nfull, q_ref, k_ref, v_ref, z_ref, o_ref, s_ref, p_ref, m_ref):
    kvlen = (nfull + 1) * TQ_ATT
    d0 = nfull * TQ_ATT
    hq = TQ_ATT // 2
    srow = lax.broadcasted_iota(jnp.int32, (STRIP, LANE), 0)
    scol = lax.broadcasted_iota(jnp.int32, (STRIP, LANE), 1)

    def strip_blocks(g, i):
        rows = slice(i * STRIP, (i + 1) * STRIP)
        wd = -(-(i + 1) * STRIP // LANE) * LANE
        ncol = (d0 + wd) // LANE
        blks = [s_ref[g, rows, c * LANE:(c + 1) * LANE] for c in range(ncol)]
        blks[-1] = jnp.where(scol + (wd - LANE) <= srow + i * STRIP, blks[-1], -jnp.inf)
        return rows, ncol, blks

    qrows = slice(d0, d0 + TQ_ATT)
    for g in range(ATT_HEADS):
        q = q_ref[0, g, qrows, :]
        for j in range(nfull):
            s_ref[g, :, j * TQ_ATT:(j + 1) * TQ_ATT] = _dot_nt(q, k_ref[0, g, j * TQ_ATT:(j + 1) * TQ_ATT, :])
        s_ref[g, 0:hq, d0:d0 + hq] = _dot_nt(q[0:hq], k_ref[0, g, d0:d0 + hq, :])
        s_ref[g, hq:TQ_ATT, d0:kvlen] = _dot_nt(q[hq:TQ_ATT], k_ref[0, g, d0:kvlen, :])
    for g in range(ATT_HEADS):
        for i in range(TQ_ATT // STRIP):
            rows, ncol, blks = strip_blocks(g, i)
            mx = blks[0]
            for blk in blks[1:]:
                mx = jnp.maximum(mx, blk)
            m_ref[g, rows, :] = jnp.broadcast_to(jnp.max(mx, axis=-1, keepdims=True), (STRIP, LANE))
        for i in range(TQ_ATT // STRIP):
            rows, ncol, blks = strip_blocks(g, i)
            m = m_ref[g, rows, :]
            for c, blk in enumerate(blks):
                p_ref[g, rows, c * LANE:(c + 1) * LANE] = jnp.exp2(blk - m).astype(BF16)
            kl = d0 + hq if (i + 1) * STRIP <= hq else kvlen
            if ncol * LANE < kl:
                p_ref[g, rows, ncol * LANE:kl] = jnp.zeros((STRIP, kl - ncol * LANE), BF16)
        v1 = jnp.concatenate([v_ref[0, g, 0:kvlen, :], jnp.ones((kvlen, LANE), BF16)], axis=-1)
        for r0, r1, kl in ((0, hq, d0 + hq), (hq, TQ_ATT, kvlen)):
            pv = _dot(p_ref[g, r0:r1, 0:kl], v1[0:kl])
            gate = _silu(z_ref[g, d0 + r0:d0 + r1, :].astype(F32))
            o_ref[g, d0 + r0:d0 + r1, :] = (pv[:, :LANE] / pv[:, LANE:] * gate).astype(BF16)


def _mla_attn_kernel(q_ref, k_ref, v_ref, z_ref, o_ref, s_ref, p_ref, m_ref):
    for nfull in range(SEQ // TQ_ATT):
        _attn_tile(nfull, q_ref, k_ref, v_ref, z_ref, o_ref, s_ref, p_ref, m_ref)


def _mla_attn(q, k, v, hb16):
    return pl.pallas_call(
        _mla_attn_kernel,
        out_shape=jax.ShapeDtypeStruct((HEADS, ROWS, LANE), BF16),
        grid=(BATCH, HEADS // ATT_HEADS),
        in_specs=[pl.BlockSpec((1, ATT_HEADS, SEQ, 2 * LANE), lambda b, h: (b, h, 0, 0)),
                  pl.BlockSpec((1, ATT_HEADS, SEQ, 2 * LANE), lambda b, h: (b, h, 0, 0)),
                  pl.BlockSpec((1, ATT_HEADS, SEQ, LANE), lambda b, h: (b, h, 0, 0)),
                  pl.BlockSpec((ATT_HEADS, SEQ, LANE), lambda b, h: (BLK_ZB // ATT_HEADS + h, b, 0))],
        out_specs=pl.BlockSpec((ATT_HEADS, SEQ, LANE), lambda b, h: (h, b, 0)),
        scratch_shapes=[pltpu.VMEM((ATT_HEADS, TQ_ATT, SEQ), F32), pltpu.VMEM((ATT_HEADS, TQ_ATT, SEQ), BF16),
                        pltpu.VMEM((ATT_HEADS, TQ_ATT, LANE), F32)],
        compiler_params=_cparams(("parallel", "parallel")),
        name="mla_attn",
    )(q, k, v, hb16)


def _mem_attn_kernel(q_ref, k_ref, v_ref, z_ref, o_ref):
    ones = jnp.ones((MEM_LEN, LANE), BF16)
    scores = []
    for h in range(HEADS):
        q = (q_ref[h].astype(F32) * (HEAD_DIM ** -0.5 * LOG2E)).astype(BF16)
        scores.append(_dot_nt(q, k_ref[:, h * LANE:(h + 1) * LANE]))
    for h in range(HEADS):
        s = scores[h]
        p = jnp.exp2(s - jnp.max(s, axis=-1, keepdims=True)).astype(BF16)
        pv = _dot(p, jnp.concatenate([v_ref[:, h * LANE:(h + 1) * LANE], ones], axis=-1))
        o_ref[h] = (pv[:, :LANE] / pv[:, LANE:] * _silu(z_ref[h].astype(F32))).astype(BF16)


def _light_kernel(rq, rk, rv, rz, cr, sr, gq, gk, gc, rg, mq, mk, mv, mz, lat, kpe, cm, sm, qg, wuq, kvg, wukv,
                  ya, yd, q_out, k_out, v_out, st_ref):
    _ret_kernel(rq, rk, rv, rz, cr, sr, gq, gk, gc, rg, ya, st_ref)
    _mem_attn_kernel(mq, mk, mv, mz, yd)
    _mla_prep_kernel(lat, kpe, cm, sm, qg, wuq, kvg, wukv, q_out, k_out, v_out)


def _light_mixers(hb16, hb32, tables, ret_consts, ret_g, kvm, qg, wuq_p, kvg, wukv_p, layer):
    cos_r, sin_r, cos_m, sin_m = tables
    gq, gk, gc = ret_consts
    nchunk = SEQ // RCHUNK
    tab = pl.BlockSpec((1, RCHUNK, LANE), lambda b, c: (b, c, 0))
    row_blk = lambda blk: pl.BlockSpec((1, RCHUNK, LANE), lambda b, c: (blk, b * nchunk + c, 0))
    y_shape = jax.ShapeDtypeStruct((HEADS, ROWS, LANE), BF16)
    y_spec = pl.BlockSpec((HEADS, RCHUNK, LANE), lambda b, c: (0, b * nchunk + c, 0))
    qk_shape = jax.ShapeDtypeStruct((BATCH, HEADS, SEQ, 2 * LANE), BF16)
    v_shape = jax.ShapeDtypeStruct((BATCH, HEADS, SEQ, LANE), BF16)
    qk_spec = pl.BlockSpec((1, HEADS, RCHUNK, 2 * LANE), lambda b, c: (b, 0, c, 0))
    v_spec = pl.BlockSpec((1, HEADS, RCHUNK, LANE), lambda b, c: (b, 0, c, 0))
    return pl.pallas_call(
        _light_kernel,
        out_shape=(y_shape, y_shape, qk_shape, qk_shape, v_shape),
        grid=(BATCH, nchunk),
        in_specs=[_group_spec(BLK_RQ, RCHUNK), _group_spec(BLK_RK, RCHUNK), _group_spec(BLK_RV, RCHUNK),
                  _group_spec(BLK_ZA, RCHUNK), tab, tab,
                  _const_spec(gq.shape), _const_spec(gk.shape), _const_spec(gc.shape),
                  _layer_spec(ret_g.shape, layer),
                  _group_spec(BLK_MQ, RCHUNK),
                  pl.BlockSpec((MEM_LEN, GROUP_WIDTH), lambda b, c: (b, 2 * layer)),
                  pl.BlockSpec((MEM_LEN, GROUP_WIDTH), lambda b, c: (b, 2 * layer + 1)),
                  _group_spec(BLK_ZD, RCHUNK),
                  _group_spec(BLK_CQ, RCHUNK), row_blk(BLK_KPE), tab, tab,
                  _layer_spec(qg.shape, layer), _layer_spec(wuq_p.shape, layer),
                  _layer_spec(kvg.shape, layer), _layer_spec(wukv_p.shape, layer)],
        out_specs=(y_spec, y_spec, qk_spec, qk_spec, v_spec),
        scratch_shapes=[pltpu.VMEM((HEADS, HEAD_DIM, HEAD_DIM), F32)],
        compiler_params=_cparams(("parallel", "arbitrary")),
        name="light_mixers",
    )(hb16, hb16, hb16, hb16, cos_r, sin_r, gq, gk, gc, ret_g, hb16, kvm, kvm, hb16, hb16, hb32, cos_m, sin_m,
      qg, wuq_p, kvg, wukv_p)


def _out_kernel(ya_ref, yb_ref, yc_ref, yd_ref, w_ref, x_ref, g_ref, b_ref, o_ref, *rest):
    *bf16_out, wb_ref = rest

    @pl.when(pl.program_id(0) == 0)
    def _():
        wb_ref[...] = w_ref[...].astype(BF16)

    for t in range(TM_OUT // SUB_OUT):
        rows = slice(t * SUB_OUT, (t + 1) * SUB_OUT)
        parts = [ref[h, rows, :] for ref in (ya_ref, yb_ref, yc_ref, yd_ref) for h in range(HEADS)]
        y = jnp.concatenate(parts, axis=-1)
        for s in range(D_MODEL // MXU_WIDTH):
            cols = slice(s * MXU_WIDTH, (s + 1) * MXU_WIDTH)
            o_ref[rows, cols] = DEEPNORM_ALPHA * x_ref[rows, cols] + _dot(y, wb_ref[:, cols])
        r = o_ref[rows, :]
        mu = jnp.mean(r, axis=-1, keepdims=True)
        var = jnp.mean(jnp.square(r - mu), axis=-1, keepdims=True)
        out = (r - mu) * lax.rsqrt(var + LN_EPS) * g_ref[...] + b_ref[...]
        o_ref[rows, :] = out
        for ob_ref in bf16_out:
            ob_ref[rows, :] = out.astype(BF16)


def _outproj(ya, yb, yc, yd, w_out, x2d, ln_g, ln_b, layer, with_bf16):
    yspec = pl.BlockSpec((HEADS, TM_OUT, LANE), lambda i: (0, i, 0))
    xspec = pl.BlockSpec((TM_OUT, D_MODEL), lambda i: (i, 0))
    dtypes = (F32, BF16) if with_bf16 else (F32,)
    return pl.pallas_call(
        _out_kernel,
        out_shape=tuple(jax.ShapeDtypeStruct((ROWS, D_MODEL), dt) for dt in dtypes),
        grid=(ROWS // TM_OUT,),
        in_specs=[yspec, yspec, yspec, yspec,
                  pl.BlockSpec((None,) + tuple(w_out.shape[1:]), lambda i: (layer, 0, 0),
                               pipeline_mode=pl.Buffered(1)),
                  xspec, _layer_spec(ln_g.shape, layer), _layer_spec(ln_b.shape, layer)],
        out_specs=tuple(xspec for _ in dtypes),
        scratch_shapes=[pltpu.VMEM(tuple(w_out.shape[1:]), BF16)],
        compiler_params=_cparams(("arbitrary",)),
        name="outproj_ln",
    )(ya, yb, yc, yd, w_out, x2d, ln_g, ln_b)


def kernel(x, mem, positions, w_in, ret_norm_g, mla_q_norm_g, mla_w_uq, mla_kv_norm_g, mla_w_ukv, ml_conv_w, ml_conv_b, ml_w_q, ml_w_k, ml_i_bias, ml_f_bias, ml_skip, ml_norm_g, w_mem_kv, w_out, ln_g, ln_b):
    assert x.shape == (BATCH, SEQ, D_MODEL) and mem.shape == (BATCH, MEM_LEN, D_MODEL)
    tables = _rope_tables(positions)
    kvm = _mem_kv(mem, w_mem_kv)
    w_in_p = _prep_w_in(w_in)
    wuq_p, wukv_p = _prep_mla_weights(mla_w_uq, mla_w_ukv)
    ret_consts = _ret_consts()
    ret_g = ret_norm_g.reshape(DEPTH, HEADS, 1, HEAD_DIM)
    mlstm_params = _prep_mlstm(ml_conv_w, ml_conv_b, ml_w_q, ml_w_k, ml_i_bias, ml_f_bias, ml_skip, ml_norm_g)
    qg = mla_q_norm_g.reshape(DEPTH, 1, MLA_Q_RANK)
    kvg = mla_kv_norm_g.reshape(DEPTH, 1, MLA_KV_RANK)
    lng = ln_g.reshape(DEPTH, 1, D_MODEL)
    lnb = ln_b.reshape(DEPTH, 1, D_MODEL)

    x2d = x.reshape(ROWS, D_MODEL)
    xb = []
    for l in range(DEPTH):
        hb16, hb32 = _inproj(xb[0] if xb else x2d, w_in_p, l)
        ya, yd, q, k, v = _light_mixers(hb16, hb32, tables, ret_consts, ret_g, kvm, qg, wuq_p, kvg, wukv_p, l)
        yb = _mla_attn(q, k, v, hb16)
        yc = _mlstm(hb16, hb32, mlstm_params, l)
        x2d, *xb = _outproj(ya, yb, yc, yd, w_out, x2d, lng, lnb, l, with_bf16=l + 1 < DEPTH)
    return x2d.reshape(BATCH, SEQ, D_MODEL)
```

```python
import functools

import numpy as np
import jax
import jax.numpy as jnp
from jax import lax
from jax.experimental import pallas as pl
from jax.experimental.pallas import tpu as pltpu

F32 = jnp.float32
BF16 = jnp.bfloat16

D_MODEL = 2048
BATCH = 8
SEQ = 2048
DEPTH = 4
MEM_LEN = 256
HEAD_DIM = 128
HEADS = 4
GROUP_WIDTH = HEADS * HEAD_DIM
MLA_NOPE = 128
MLA_ROPE = 64
MLA_Q_RANK = 384
MLA_KV_RANK = 128
CONV_WIDTH = 4
MIX_WIDTH = 4 * GROUP_WIDTH
ROPE_THETA = 10000.0
LN_EPS = 1e-5
RMS_EPS = 1e-6
DEEPNORM_ALPHA = (2 * DEPTH) ** 0.25
IN_SPLITS = (512, 512, 512, MLA_Q_RANK, MLA_KV_RANK, MLA_ROPE, 512, 512, 512, HEADS, HEADS, 512, MIX_WIDTH)

LANE = 128
MXU_WIDTH = 256
ROWS = BATCH * SEQ

BLK_RQ, BLK_RK, BLK_RV = 0, 4, 8
BLK_LV, BLK_LO = 12, 16
BLK_MQ = 20
BLK_ZA, BLK_ZB, BLK_ZD = 24, 28, 32
BLK_CQ, BLK_CKV = 36, 39
NBLK16 = 40
BLK_LX, BLK_ZC = 0, 4
BLK_KPE = 8
NBLK32 = 10
NBLK = NBLK16 + NBLK32
GATE_I_LANE, GATE_F_LANE = 32, 36

TM_IN = 512
TM_OUT = 512
SUB_OUT = 256
TQ_ATT = 512
ATT_HEADS = 2
RCHUNK = 512
CONV_PAD = 8
STRIP = 64
LOG2E = 1.4426950408889634
VMEM_LIMIT = 56 * 1024 * 1024


def _cparams(sem):
    return pltpu.CompilerParams(dimension_semantics=sem, vmem_limit_bytes=VMEM_LIMIT)


def _silu(z):
    return z * jax.nn.sigmoid(z)


def _log_sigmoid(x):
    return jnp.minimum(x, 0.0) - jnp.log1p(jnp.exp(-jnp.abs(x)))


def _split3(x):
    x1 = x.astype(BF16)
    r1 = x - x1.astype(F32)
    x2 = r1.astype(BF16)
    x3 = (r1 - x2.astype(F32)).astype(BF16)
    return x1, x2, x3


def _dot(a, b):
    return jnp.dot(a, b, preferred_element_type=F32)


def _rowsum_mxu(x, w):
    hi = x.astype(BF16)
    lo = (x - hi.astype(F32)).astype(BF16)
    return _dot(hi, w) + _dot(lo, w)


def _dot_nt(a, b):
    return lax.dot_general(a, b, (((1,), (1,)), ((), ())), preferred_element_type=F32)


def _tables_kernel(pos_ref, c_ref, cr_ref, sr_ref, cm_ref, sm_ref):
    pos = pos_ref[0].astype(F32)
    ang = pos * c_ref[0:1, :]
    cs = jnp.cos(ang)
    sn = jnp.sin(ang)
    cs_sw = pltpu.roll(cs, LANE // 2, 1)
    sn_sw = pltpu.roll(sn, LANE // 2, 1)
    lane = lax.broadcasted_iota(jnp.int32, cs.shape, 1)
    half_m = MLA_ROPE // 2
    first = lane < LANE // 2
    m_lo = lane < half_m
    m_hi = (lane >= LANE // 2) & (lane < LANE // 2 + half_m)
    cr_ref[0] = jnp.where(first, cs, cs_sw)
    sr_ref[0] = jnp.where(first, -sn, sn_sw)
    cm_ref[0] = jnp.where(m_lo, cs_sw, jnp.where(m_hi, cs, 0.0))
    sm_ref[0] = jnp.where(m_lo, -sn_sw, jnp.where(m_hi, sn, 0.0))


def _rope_tables(positions):
    half_r = HEAD_DIM // 2
    fr = ROPE_THETA ** (-jnp.arange(half_r, dtype=F32) / half_r)
    half_m = MLA_ROPE // 2
    fm = ROPE_THETA ** (-jnp.arange(half_m, dtype=F32) / half_m)
    freqs = jnp.concatenate([fr, fm, jnp.zeros((LANE - half_r - half_m,), F32)])
    consts = jnp.concatenate([freqs[None, :], jnp.zeros((7, LANE), F32)], axis=0)
    ts = RCHUNK
    tab = jax.ShapeDtypeStruct((BATCH, SEQ, LANE), F32)
    spec = pl.BlockSpec((1, ts, LANE), lambda b, i: (b, i, 0))
    return pl.pallas_call(
        _tables_kernel,
        out_shape=(tab, tab, tab, tab),
        grid=(BATCH, SEQ // ts),
        in_specs=[pl.BlockSpec((1, ts, 1), lambda b, i: (b, i, 0)),
                  pl.BlockSpec((8, LANE), lambda b, i: (0, 0))],
        out_specs=(spec, spec, spec, spec),
        compiler_params=_cparams(("parallel", "parallel")),
        name="rope_tables",
    )(positions.reshape(BATCH, SEQ, 1), consts)


def _mem_kv_kernel(a_ref, w_ref, o_ref, wb_ref):
    @pl.when(pl.program_id(1) == 0)
    def _():
        wb_ref[...] = w_ref[...].astype(BF16)

    o_ref[...] = _dot(a_ref[...].astype(BF16), wb_ref[...]).astype(BF16)


def _mem_kv(mem, w_mem_kv):
    a = mem.reshape(BATCH * MEM_LEN, D_MODEL)
    tm, tn = 1024, 2 * GROUP_WIDTH
    return pl.pallas_call(
        _mem_kv_kernel,
        out_shape=jax.ShapeDtypeStruct((BATCH * MEM_LEN, DEPTH * tn), BF16),
        grid=(DEPTH, BATCH * MEM_LEN // tm),
        in_specs=[pl.BlockSpec((tm, D_MODEL), lambda l, i: (i, 0)),
                  pl.BlockSpec((None, D_MODEL, tn), lambda l, i: (l, 0, 0))],
        out_specs=pl.BlockSpec((tm, tn), lambda l, i: (i, l)),
        scratch_shapes=[pltpu.VMEM((D_MODEL, tn), BF16)],
        compiler_params=_cparams(("parallel", "arbitrary")),
        name="mem_kv",
    )(a, w_mem_kv)


W_SLABS = 5
W_SLAB_ROWS = NBLK * LANE // W_SLABS


def _inproj_kernel(layer, x_ref, w_hbm, o16_ref, o32_ref, w_ref, sem):
    first = pl.program_id(0) == 0

    def slab_copy(s):
        rows = pl.ds(s * W_SLAB_ROWS, W_SLAB_ROWS)
        return pltpu.make_async_copy(w_hbm.at[layer, rows, :], w_ref.at[rows, :], sem.at[s])

    def project(fetch):
        if fetch:
            for s in range(W_SLABS):
                slab_copy(s).start()
        xb = x_ref[...].astype(BF16)
        dots_per_slab = W_SLAB_ROWS // MXU_WIDTH
        for t in range(NBLK * LANE // MXU_WIDTH):
            if fetch and t % dots_per_slab == 0:
                slab_copy(t // dots_per_slab).wait()
            r = _dot_nt(xb, w_ref[t * MXU_WIDTH:(t + 1) * MXU_WIDTH, :])
            for half, blk in enumerate((2 * t, 2 * t + 1)):
                o_ref, k = (o16_ref, blk) if blk < NBLK16 else (o32_ref, blk - NBLK16)
                o_ref[k] = r[:, half * LANE:(half + 1) * LANE].astype(o_ref.dtype)

    @pl.when(first)
    def _():
        project(fetch=True)

    @pl.when(jnp.logical_not(first))
    def _():
        project(fetch=False)


def _inproj(x2d, wt_p, layer):
    return pl.pallas_call(
        functools.partial(_inproj_kernel, layer),
        out_shape=(jax.ShapeDtypeStruct((NBLK16, ROWS, LANE), BF16),
                   jax.ShapeDtypeStruct((NBLK32, ROWS, LANE), F32)),
        grid=(ROWS // TM_IN,),
        in_specs=[pl.BlockSpec((TM_IN, D_MODEL), lambda i: (i, 0)),
                  pl.BlockSpec(memory_space=pl.ANY)],
        out_specs=(pl.BlockSpec((NBLK16, TM_IN, LANE), lambda i: (0, i, 0)),
                   pl.BlockSpec((NBLK32, TM_IN, LANE), lambda i: (0, i, 0))),
        scratch_shapes=[pltpu.VMEM((NBLK * LANE, D_MODEL), BF16), pltpu.SemaphoreType.DMA((W_SLABS,))],
        compiler_params=_cparams(("arbitrary",)),
        name="inproj",
    )(x2d, wt_p)


def _w_in_groups():
    off = np.concatenate([[0], np.cumsum(IN_SPLITS)]).tolist()
    (o_rq, o_rk, o_rv, o_cq, _, o_kpe, o_lx, o_lv, o_lo, o_li, _, o_mq, o_z, _) = off
    groups = [o_rq, o_rk, o_rv, o_lv, o_lo, o_mq, o_z, o_z + GROUP_WIDTH, o_z + 3 * GROUP_WIDTH,
              o_cq, o_lx, o_z + 2 * GROUP_WIDTH]
    return groups, o_kpe, o_li


def _w_in_kernel(tbl_ref, w_ref, kpe_ref, gate_ref, o_ref):
    del tbl_ref
    ngroup = NBLK // HEADS

    @pl.when(pl.program_id(1) < ngroup)
    def _():
        o_ref[0] = w_ref[0].astype(BF16)

    @pl.when(pl.program_id(1) == ngroup)
    def _():
        half = MLA_ROPE // 2
        zeros = lambda n: jnp.zeros((n, D_MODEL), F32)
        kpe = kpe_ref[0]
        blk = jnp.concatenate([kpe[0:half], gate_ref[0], zeros(64 - half - 2 * HEADS),
                               kpe[half:2 * half], zeros(64 - half)], axis=0)
        o_ref[0, 0:LANE, :] = blk.astype(BF16)
        o_ref[0, LANE:, :] = jnp.zeros((GROUP_WIDTH - LANE, D_MODEL), BF16)


def _prep_w_in(w_in):
    wt = jnp.swapaxes(w_in, 1, 2)
    groups, o_kpe, o_li = _w_in_groups()
    assert len(groups) * HEADS + 2 == NBLK and o_kpe % LANE == 0 and o_li % (2 * HEADS) == 0
    sub = 8
    assert all(g % sub == 0 for g in groups)
    table = jnp.asarray([g // sub for g in groups] + [0], jnp.int32)
    grid_spec = pltpu.PrefetchScalarGridSpec(
        num_scalar_prefetch=1,
        grid=(DEPTH, len(groups) + 1),
        in_specs=[pl.BlockSpec((pl.Element(1), pl.Element(GROUP_WIDTH), pl.Element(D_MODEL)),
                               lambda l, j, tbl: (l, pl.multiple_of(tbl[j] * sub, sub), 0)),
                  pl.BlockSpec((1, LANE, D_MODEL), lambda l, j, tbl: (l, o_kpe // LANE, 0)),
                  pl.BlockSpec((1, 2 * HEADS, D_MODEL), lambda l, j, tbl: (l, o_li // (2 * HEADS), 0))],
        out_specs=pl.BlockSpec((1, GROUP_WIDTH, D_MODEL), lambda l, j, tbl: (l, j, 0)),
    )
    return pl.pallas_call(
        _w_in_kernel,
        out_shape=jax.ShapeDtypeStruct((DEPTH, NBLK * LANE, D_MODEL), BF16),
        grid_spec=grid_spec,
        compiler_params=_cparams(("parallel", "arbitrary")),
        name="w_in_relayout",
    )(table, wt, wt, wt)


def _group_spec(base, rows):
    nchunk = SEQ // rows
    return pl.BlockSpec((HEADS, rows, LANE), lambda b, c: (base // HEADS, b * nchunk + c, 0))


def _layer_spec(shape, layer):
    nd = len(shape) - 1
    return pl.BlockSpec((None,) + tuple(shape[1:]), lambda *_: (layer,) + (0,) * nd)


def _const_spec(shape):
    nd = len(shape)
    return pl.BlockSpec(tuple(shape), lambda *_: (0,) * nd)


def _ret_kernel(q_ref, k_ref, v_ref, z_ref, cos_ref, sin_ref, gq_ref, gk_ref, gc_ref, g_ref, o_ref, st_ref):
    @pl.when(pl.program_id(1) == 0)
    def _():
        st_ref[...] = jnp.zeros_like(st_ref)

    cs = cos_ref[0]
    sn = sin_ref[0]
    row = lax.broadcasted_iota(jnp.int32, (RCHUNK, RCHUNK), 0)
    col = lax.broadcasted_iota(jnp.int32, (RCHUNK, RCHUNK), 1)
    causal = col <= row
    qbs, kts, vbs, scs = [], [], [], []
    for h in range(HEADS):
        q = q_ref[h].astype(F32)
        q = (q * cs + pltpu.roll(q, HEAD_DIM // 2, 1) * sn) * gq_ref[h]
        k = k_ref[h].astype(F32)
        k = (k * cs + pltpu.roll(k, HEAD_DIM // 2, 1) * sn) * gk_ref[h]
        qb = q.astype(BF16)
        scs.append(_dot_nt(qb, k.astype(BF16)))
        qbs.append(qb)
        kts.append(k.T.astype(BF16))
        vbs.append(v_ref[h])
    outs = []
    for h in range(HEADS):
        st = st_ref[h]
        intra = []
        for i in range(RCHUNK // LANE):
            r0, r1 = i * LANE, (i + 1) * LANE
            blk = jnp.where(causal[r0:r1, 0:r1], scs[h][r0:r1, 0:r1], 0.0).astype(BF16)
            intra.append(_dot(blk, vbs[h][0:r1]))
        outs.append(jnp.concatenate(intra, axis=0) + _dot(qbs[h], st.astype(BF16)))
        st_ref[h] = gc_ref[h] * (st + _dot(kts[h], vbs[h]))
    for h in range(HEADS):
        out = outs[h]
        mu = jnp.mean(out, axis=-1, keepdims=True)
        dev = out - mu
        var = jnp.mean(dev * dev, axis=-1, keepdims=True)
        hn = dev * lax.rsqrt(var + LN_EPS) * g_ref[h]
        o_ref[h] = (hn * _silu(z_ref[h].astype(F32))).astype(BF16)


def _ret_consts():
    log_g = jnp.log1p(-jnp.exp2(-5.0 - jnp.arange(HEADS, dtype=F32)))
    idx = jnp.arange(RCHUNK, dtype=F32)
    full = (HEADS, RCHUNK, HEAD_DIM)
    gq = jnp.broadcast_to(jnp.exp(log_g[:, None] * (idx + 1.0))[..., None], full)
    gk = jnp.broadcast_to((jnp.exp(-log_g[:, None] * (idx + 1.0)) * HEAD_DIM ** -0.5)[..., None], full)
    gc = jnp.broadcast_to(jnp.exp(log_g * RCHUNK)[:, None, None], (HEADS, 1, HEAD_DIM))
    return gq, gk, gc


def _mlstm_kernel(lx_ref, lv_ref, lo_ref, gt_ref, z_ref, cw_ref, cb_ref, wq_ref, wk_ref, gb_ref,
                  skip_ref, g_ref, o_ref, xp_ref, st_ref, m_ref, s_ref, p_ref):
    @pl.when(pl.program_id(1) == 0)
    def _():
        xp_ref[:, 0:CONV_PAD, :] = jnp.zeros((HEADS, CONV_PAD, HEAD_DIM), F32)
        st_ref[...] = jnp.zeros_like(st_ref)
        m_ref[...] = jnp.zeros_like(m_ref)

    n = RCHUNK
    ones_nd = jnp.ones((n, HEAD_DIM), BF16)
    mean_dd = jnp.full((HEAD_DIM, HEAD_DIM), 1.0 / HEAD_DIM, BF16)
    row = lax.broadcasted_iota(jnp.int32, (n, n), 0)
    col = lax.broadcasted_iota(jnp.int32, (n, n), 1)
    tril = jnp.where(col <= row, 1.0, 0.0).astype(BF16)
    r128 = lax.broadcasted_iota(jnp.int32, (LANE, LANE), 0)
    c128 = lax.broadcasted_iota(jnp.int32, (LANE, LANE), 1)
    ident = jnp.where(r128 == c128, 1.0, 0.0).astype(BF16)
    srow = lax.broadcasted_iota(jnp.int32, (STRIP, LANE), 0)
    scol = lax.broadcasted_iota(jnp.int32, (STRIP, LANE), 1)

    lane = lax.broadcasted_iota(jnp.int32, (n, LANE), 1)
    x = gt_ref[0] + gb_ref[...]
    x = jnp.where((lane >= GATE_F_LANE) & (lane < GATE_F_LANE + HEADS), _log_sigmoid(x), x) * LOG2E
    x1, x2, x3 = _split3(x)
    cum = _dot(tril, x1) + _dot(tril, x2) + _dot(tril, x3)
    rt = x - pltpu.roll(cum, LANE - (GATE_F_LANE - GATE_I_LANE), 1)
    r1, r2, r3 = _split3(rt)
    rtt = _dot_nt(ident, r1) + _dot_nt(ident, r2) + _dot_nt(ident, r3)

    xcs, ks, qbs, vbs = [], [], [], []
    for h in range(HEADS):
        xp_ref[h, CONV_PAD:, :] = lx_ref[h]
        acc = jnp.zeros((n, HEAD_DIM), F32) + cb_ref[h]
        for j in range(CONV_WIDTH):
            off = CONV_PAD - (CONV_WIDTH - 1) + j
            acc = acc + xp_ref[h, off:off + n, :] * cw_ref[h, j:j + 1, :]
        xp_ref[h, 0:CONV_PAD, :] = lx_ref[h, n - CONV_PAD:n, :]
        xc = _silu(acc)
        xcb = xc.astype(BF16)
        k = _dot(xcb, wk_ref[h]) * HEAD_DIM ** -0.5
        qb = _dot(xcb, wq_ref[h]).astype(BF16)
        s_ref[h] = _dot_nt(qb, k.astype(BF16))
        xcs.append(xc)
        ks.append(k)
        qbs.append(qb)
        vbs.append(jnp.concatenate([lv_ref[h], ones_nd], axis=-1))

    r_rows = [rtt[GATE_I_LANE + h:GATE_I_LANE + h + 1, :] for h in range(HEADS)]
    bases = [m_ref[h] for h in range(HEADS)]
    strips = [[] for _ in range(HEADS)]
    u_lasts = [None] * HEADS
    for i in range(n // STRIP):
        rows = slice(i * STRIP, (i + 1) * STRIP)
        d0 = (i * STRIP // LANE) * LANE
        w = d0 + LANE
        mask = (scol + d0) <= (srow + i * STRIP)
        for h in range(HEADS):
            r_row = r_rows[h]
            if d0 > 0 and (i * STRIP) % LANE == 0:
                bases[h] = jnp.maximum(bases[h], jnp.max(r_row[:, d0 - LANE:d0], axis=-1, keepdims=True))
            rmd = jnp.where(mask, r_row[:, d0:w], -jnp.inf)
            u_col = jnp.maximum(jnp.max(rmd, axis=-1, keepdims=True), bases[h])
            u = jnp.broadcast_to(u_col, (STRIP, LANE))
            p_ref[h, rows, d0:w] = (s_ref[h, rows, d0:w] * jnp.exp2(rmd - u)).astype(BF16)
            for c in range(d0 // LANE):
                cols = slice(c * LANE, (c + 1) * LANE)
                p_ref[h, rows, cols] = (s_ref[h, rows, cols] * jnp.exp2(r_row[:, cols] - u)).astype(BF16)
            if w < n:
                p_ref[h, rows, w:n] = jnp.zeros((STRIP, n - w), BF16)
            strips[h].append(u)
            u_lasts[h] = u_col[STRIP - 1:STRIP, :]
    us = [jnp.concatenate(st_h, axis=0) for st_h in strips]

    cells = []
    for h in range(HEADS):
        li = GATE_I_LANE + h
        lf = GATE_F_LANE + h
        m_st = m_ref[h]
        u, u_last, vb = us[h], u_lasts[h], vbs[h]
        w_inter = jnp.exp2(m_st - u)
        st = st_ref[h]
        intra = _dot(p_ref[h], vb)
        inter = _dot(qbs[h], st.astype(BF16))
        num = intra[:, :LANE] + w_inter * inter[:, :LANE]
        den = intra[:, LANE:] + w_inter * inter[:, LANE:]
        cum_f = jnp.broadcast_to(cum[:, lf:lf + 1], (n, LANE))
        cells.append(num / jnp.maximum(jnp.abs(den), jnp.exp2(-(cum_f + u))))

        decay = jnp.exp2(m_st - u_last)
        kw = ks[h] * jnp.exp2(jnp.broadcast_to(rt[:, li:li + 1], (n, LANE)) - u_last)
        upd = _dot(kw.T.astype(BF16), vb)
        st_ref[h] = decay * st + upd
        m_ref[h] = cum[n - 1:n, lf:lf + 1] + u_last

    for h in range(HEADS):
        cell = cells[h] * jax.nn.sigmoid(lo_ref[h].astype(F32))
        mu = _rowsum_mxu(cell, mean_dd)
        dev = cell - mu
        var = _rowsum_mxu(dev * dev, mean_dd)
        hn = dev * lax.rsqrt(var + LN_EPS) * g_ref[h]
        o_ref[h] = ((hn + skip_ref[h] * xcs[h]) * _silu(z_ref[h])).astype(BF16)


def _prep_mlstm(conv_w, conv_b, w_q, w_k, i_bias, f_bias, skip, norm_g):
    nl = conv_w.shape[0]
    cw = conv_w.reshape(nl, CONV_WIDTH, HEADS, HEAD_DIM).transpose(0, 2, 1, 3)
    cb = conv_b.reshape(nl, HEADS, 1, HEAD_DIM)
    zeros = lambda w: jnp.zeros((nl, w), F32)
    gb = jnp.concatenate([zeros(GATE_I_LANE), i_bias, f_bias, zeros(LANE - GATE_F_LANE - HEADS)], axis=-1)
    return (cw, cb, w_q.astype(BF16), w_k.astype(BF16), gb.reshape(nl, 1, LANE),
            skip.reshape(nl, HEADS, 1, HEAD_DIM), norm_g.reshape(nl, HEADS, 1, HEAD_DIM))


def _mlstm(hb16, hb32, params, layer):
    nchunk = SEQ // RCHUNK
    return pl.pallas_call(
        _mlstm_kernel,
        out_shape=jax.ShapeDtypeStruct((HEADS, ROWS, LANE), BF16),
        grid=(BATCH, nchunk),
        in_specs=[_group_spec(BLK_LX, RCHUNK), _group_spec(BLK_LV, RCHUNK), _group_spec(BLK_LO, RCHUNK),
                  pl.BlockSpec((1, RCHUNK, LANE), lambda b, c: (BLK_KPE, b * nchunk + c, 0)),
                  _group_spec(BLK_ZC, RCHUNK)] + [_layer_spec(p.shape, layer) for p in params],
        out_specs=pl.BlockSpec((HEADS, RCHUNK, LANE), lambda b, c: (0, b * nchunk + c, 0)),
        scratch_shapes=[pltpu.VMEM((HEADS, RCHUNK + CONV_PAD, HEAD_DIM), F32),
                        pltpu.VMEM((HEADS, HEAD_DIM, 2 * HEAD_DIM), F32),
                        pltpu.VMEM((HEADS, 1, 1), F32),
                        pltpu.VMEM((HEADS, RCHUNK, RCHUNK), F32),
                        pltpu.VMEM((HEADS, RCHUNK, RCHUNK), BF16)],
        compiler_params=_cparams(("parallel", "arbitrary")),
        name="mlstm",
    )(hb32, hb16, hb16, hb32, hb32, *params)


def _mla_prep_kernel(lat_ref, kpe_ref, cm_ref, sm_ref, qg_ref, wuq_ref, kvg_ref, wukv_ref,
                     q_out, k_out, v_out):
    scale = (MLA_NOPE + MLA_ROPE) ** -0.5 * LOG2E
    cs = cm_ref[0]
    sn = sm_ref[0]
    cq = jnp.concatenate([lat_ref[0], lat_ref[1], lat_ref[2]], axis=-1).astype(F32)
    qn = cq * lax.rsqrt(jnp.mean(jnp.square(cq), axis=-1, keepdims=True) + RMS_EPS) * qg_ref[...]
    q = _dot(qn.astype(BF16), wuq_ref[...])
    ckv = lat_ref[3].astype(F32)
    kvn = ckv * lax.rsqrt(jnp.mean(jnp.square(ckv), axis=-1, keepdims=True) + RMS_EPS) * kvg_ref[...]
    kv = _dot(kvn.astype(BF16), wukv_ref[...])
    kpe = kpe_ref[0]
    krot = (kpe * cs + pltpu.roll(kpe, LANE // 2, 1) * sn).astype(BF16)
    for h in range(HEADS):
        qr = q[:, GROUP_WIDTH + h * LANE:GROUP_WIDTH + (h + 1) * LANE]
        qr = qr * cs + pltpu.roll(qr, LANE // 2, 1) * sn
        q_out[0, h, :, 0:LANE] = (q[:, h * LANE:(h + 1) * LANE] * scale).astype(BF16)
        q_out[0, h, :, LANE:2 * LANE] = (qr * scale).astype(BF16)
        k_out[0, h, :, 0:LANE] = kv[:, h * LANE:(h + 1) * LANE].astype(BF16)
        k_out[0, h, :, LANE:2 * LANE] = krot
        v_out[0, h] = kv[:, GROUP_WIDTH + h * LANE:GROUP_WIDTH + (h + 1) * LANE].astype(BF16)


def _prep_mla_weights(w_uq, w_ukv):
    lead = w_uq.shape[:-1]
    wq = w_uq.reshape(lead + (HEADS, MLA_NOPE + MLA_ROPE))
    nope = wq[..., :MLA_NOPE].reshape(lead + (GROUP_WIDTH,))
    half = MLA_ROPE // 2
    zeros = jnp.zeros(lead + (HEADS, 64 - half), w_uq.dtype)
    rope = jnp.concatenate([wq[..., MLA_NOPE:MLA_NOPE + half], zeros, wq[..., MLA_NOPE + half:], zeros], axis=-1)
    wq_p = jnp.concatenate([nope, rope.reshape(lead + (HEADS * LANE,))], axis=-1).astype(BF16)
    lead = w_ukv.shape[:-1]
    wkv = w_ukv.reshape(lead + (HEADS, MLA_NOPE + HEAD_DIM))
    wkv_p = jnp.concatenate([wkv[..., :MLA_NOPE].reshape(lead + (GROUP_WIDTH,)),
                             wkv[..., MLA_NOPE:].reshape(lead + (GROUP_WIDTH,))], axis=-1).astype(BF16)
    return wq_p, wkv_p


def _attn_tile(nfull, q_ref, k_ref, v_ref, z_ref, o_ref, s_ref, p_ref, m_ref):
    kvlen = (nfull + 1) * TQ_ATT
    d0 = nfull * TQ_ATT
    hq = TQ_ATT // 2
    srow = lax.broadcasted_iota(jnp.int32, (STRIP, LANE), 0)
    scol = lax.broadcasted_iota(jnp.int32, (STRIP, LANE), 1)

    def strip_blocks(g, i):
        rows = slice(i * STRIP, (i + 1) * STRIP)
        wd = -(-(i + 1) * STRIP // LANE) * LANE
        ncol = (d0 + wd) // LANE
        blks = [s_ref[g, rows, c * LANE:(c + 1) * LANE] for c in range(ncol)]
        blks[-1] = jnp.where(scol + (wd - LANE) <= srow + i * STRIP, blks[-1], -jnp.inf)
        return rows, ncol, blks

    qrows = slice(d0, d0 + TQ_ATT)
    for g in range(ATT_HEADS):
        q = q_ref[0, g, qrows, :]
        for j in range(nfull):
            s_ref[g, :, j * TQ_ATT:(j + 1) * TQ_ATT] = _dot_nt(q, k_ref[0, g, j * TQ_ATT:(j + 1) * TQ_ATT, :])
        s_ref[g, 0:hq, d0:d0 + hq] = _dot_nt(q[0:hq], k_ref[0, g, d0:d0 + hq, :])
        s_ref[g, hq:TQ_ATT, d0:kvlen] = _dot_nt(q[hq:TQ_ATT], k_ref[0, g, d0:kvlen, :])
    for g in range(ATT_HEADS):
        for i in range(TQ_ATT // STRIP):
            rows, ncol, blks = strip_blocks(g, i)
            mx = blks[0]
            for blk in blks[1:]:
                mx = jnp.maximum(mx, blk)
            m_ref[g, rows, :] = jnp.broadcast_to(jnp.max(mx, axis=-1, keepdims=True), (STRIP, LANE))
        for i in range(TQ_ATT // STRIP):
            rows, ncol, blks = strip_blocks(g, i)
            m = m_ref[g, rows, :]
            for c, blk in enumerate(blks):
                p_ref[g, rows, c * LANE:(c + 1) * LANE] = jnp.exp2(blk - m).astype(BF16)
            kl = d0 + hq if (i + 1) * STRIP <= hq else kvlen
            if ncol * LANE < kl:
                p_ref[g, rows, ncol * LANE:kl] = jnp.zeros((STRIP, kl - ncol * LANE), BF16)
        v1 = jnp.concatenate([v_ref[0, g, 0:kvlen, :], jnp.ones((kvlen, LANE), BF16)], axis=-1)
        for r0, r1, kl in ((0, hq, d0 + hq), (hq, TQ_ATT, kvlen)):
            pv = _dot(p_ref[g, r0:r1, 0:kl], v1[0:kl])
            gate = _silu(z_ref[g, d0 + r0:d0 + r1, :].astype(F32))
            o_ref[g, d0 + r0:d0 + r1, :] = (pv[:, :LANE] / pv[:, LANE:] * gate).astype(BF16)


def _mla_attn_kernel(q_ref, k_ref, v_ref, z_ref, o_ref, s_ref, p_ref, m_ref):
    for nfull in range(SEQ // TQ_ATT):
        _attn_tile(nfull, q_ref, k_ref, v_ref, z_ref, o_ref, s_ref, p_ref, m_ref)


def _mla_attn(q, k, v, hb16):
    return pl.pallas_call(
        _mla_attn_kernel,
        out_shape=jax.ShapeDtypeStruct((HEADS, ROWS, LANE), BF16),
        grid=(BATCH, HEADS // ATT_HEADS),
        in_specs=[pl.BlockSpec((1, ATT_HEADS, SEQ, 2 * LANE), lambda b, h: (b, h, 0, 0)),
                  pl.BlockSpec((1, ATT_HEADS, SEQ, 2 * LANE), lambda b, h: (b, h, 0, 0)),
                  pl.BlockSpec((1, ATT_HEADS, SEQ, LANE), lambda b, h: (b, h, 0, 0)),
                  pl.BlockSpec((ATT_HEADS, SEQ, LANE), lambda b, h: (BLK_ZB // ATT_HEADS + h, b, 0))],
        out_specs=pl.BlockSpec((ATT_HEADS, SEQ, LANE), lambda b, h: (h, b, 0)),
        scratch_shapes=[pltpu.VMEM((ATT_HEADS, TQ_ATT, SEQ), F32), pltpu.VMEM((ATT_HEADS, TQ_ATT, SEQ), BF16),
                        pltpu.VMEM((ATT_HEADS, TQ_ATT, LANE), F32)],
        compiler_params=_cparams(("parallel", "parallel")),
        name="mla_attn",
    )(q, k, v, hb16)


def _mem_attn_kernel(q_ref, k_ref, v_ref, z_ref, o_ref):
    ones = jnp.ones((MEM_LEN, LANE), BF16)
    scores = []
    for h in range(HEADS):
        q = (q_ref[h].astype(F32) * (HEAD_DIM ** -0.5 * LOG2E)).astype(BF16)
        scores.append(_dot_nt(q, k_ref[:, h * LANE:(h + 1) * LANE]))
    for h in range(HEADS):
        s = scores[h]
        p = jnp.exp2(s - jnp.max(s, axis=-1, keepdims=True)).astype(BF16)
        pv = _dot(p, jnp.concatenate([v_ref[:, h * LANE:(h + 1) * LANE], ones], axis=-1))
        o_ref[h] = (pv[:, :LANE] / pv[:, LANE:] * _silu(z_ref[h].astype(F32))).astype(BF16)


def _light_kernel(rq, rk, rv, rz, cr, sr, gq, gk, gc, rg, mq, mk, mv, mz, lat, kpe, cm, sm, qg, wuq, kvg, wukv,
                  ya, yd, q_out, k_out, v_out, st_ref):
    _ret_kernel(rq, rk, rv, rz, cr, sr, gq, gk, gc, rg, ya, st_ref)
    _mem_attn_kernel(mq, mk, mv, mz, yd)
    _mla_prep_kernel(lat, kpe, cm, sm, qg, wuq, kvg, wukv, q_out, k_out, v_out)


def _light_mixers(hb16, hb32, tables, ret_consts, ret_g, kvm, qg, wuq_p, kvg, wukv_p, layer):
    cos_r, sin_r, cos_m, sin_m = tables
    gq, gk, gc = ret_consts
    nchunk = SEQ // RCHUNK
    tab = pl.BlockSpec((1, RCHUNK, LANE), lambda b, c: (b, c, 0))
    row_blk = lambda blk: pl.BlockSpec((1, RCHUNK, LANE), lambda b, c: (blk, b * nchunk + c, 0))
    y_shape = jax.ShapeDtypeStruct((HEADS, ROWS, LANE), BF16)
    y_spec = pl.BlockSpec((HEADS, RCHUNK, LANE), lambda b, c: (0, b * nchunk + c, 0))
    qk_shape = jax.ShapeDtypeStruct((BATCH, HEADS, SEQ, 2 * LANE), BF16)
    v_shape = jax.ShapeDtypeStruct((BATCH, HEADS, SEQ, LANE), BF16)
    qk_spec = pl.BlockSpec((1, HEADS, RCHUNK, 2 * LANE), lambda b, c: (b, 0, c, 0))
    v_spec = pl.BlockSpec((1, HEADS, RCHUNK, LANE), lambda b, c: (b, 0, c, 0))
    return pl.pallas_call(
        _light_kernel,
        out_shape=(y_shape, y_shape, qk_shape, qk_shape, v_shape),
        grid=(BATCH, nchunk),
        in_specs=[_group_spec(BLK_RQ, RCHUNK), _group_spec(BLK_RK, RCHUNK), _group_spec(BLK_RV, RCHUNK),
                  _group_spec(BLK_ZA, RCHUNK), tab, tab,
                  _const_spec(gq.shape), _const_spec(gk.shape), _const_spec(gc.shape),
                  _layer_spec(ret_g.shape, layer),
                  _group_spec(BLK_MQ, RCHUNK),
                  pl.BlockSpec((MEM_LEN, GROUP_WIDTH), lambda b, c: (b, 2 * layer)),
                  pl.BlockSpec((MEM_LEN, GROUP_WIDTH), lambda b, c: (b, 2 * layer + 1)),
                  _group_spec(BLK_ZD, RCHUNK),
                  _group_spec(BLK_CQ, RCHUNK), row_blk(BLK_KPE), tab, tab,
                  _layer_spec(qg.shape, layer), _layer_spec(wuq_p.shape, layer),
                  _layer_spec(kvg.shape, layer), _layer_spec(wukv_p.shape, layer)],
        out_specs=(y_spec, y_spec, qk_spec, qk_spec, v_spec),
        scratch_shapes=[pltpu.VMEM((HEADS, HEAD_DIM, HEAD_DIM), F32)],
        compiler_params=_cparams(("parallel", "arbitrary")),
        name="light_mixers",
    )(hb16, hb16, hb16, hb16, cos_r, sin_r, gq, gk, gc, ret_g, hb16, kvm, kvm, hb16, hb16, hb32, cos_m, sin_m,
      qg, wuq_p, kvg, wukv_p)


def _out_kernel(ya_ref, yb_ref, yc_ref, yd_ref, w_ref, x_ref, g_ref, b_ref, o_ref, *rest):
    *bf16_out, wb_ref = rest

    @pl.when(pl.program_id(0) == 0)
    def _():
        wb_ref[...] = w_ref[...].astype(BF16)

    for t in range(TM_OUT // SUB_OUT):
        rows = slice(t * SUB_OUT, (t + 1) * SUB_OUT)
        parts = [ref[h, rows, :] for ref in (ya_ref, yb_ref, yc_ref, yd_ref) for h in range(HEADS)]
        y = jnp.concatenate(parts, axis=-1)
        for s in range(D_MODEL // MXU_WIDTH):
            cols = slice(s * MXU_WIDTH, (s + 1) * MXU_WIDTH)
            o_ref[rows, cols] = DEEPNORM_ALPHA * x_ref[rows, cols] + _dot(y, wb_ref[:, cols])
        r = o_ref[rows, :]
        mu = jnp.mean(r, axis=-1, keepdims=True)
        var = jnp.mean(jnp.square(r - mu), axis=-1, keepdims=True)
        out = (r - mu) * lax.rsqrt(var + LN_EPS) * g_ref[...] + b_ref[...]
        o_ref[rows, :] = out
        for ob_ref in bf16_out:
            ob_ref[rows, :] = out.astype(BF16)


def _outproj(ya, yb, yc, yd, w_out, x2d, ln_g, ln_b, layer, with_bf16):
    yspec = pl.BlockSpec((HEADS, TM_OUT, LANE), lambda i: (0, i, 0))
    xspec = pl.BlockSpec((TM_OUT, D_MODEL), lambda i: (i, 0))
    dtypes = (F32, BF16) if with_bf16 else (F32,)
    return pl.pallas_call(
        _out_kernel,
        out_shape=tuple(jax.ShapeDtypeStruct((ROWS, D_MODEL), dt) for dt in dtypes),
        grid=(ROWS // TM_OUT,),
        in_specs=[yspec, yspec, yspec, yspec,
                  pl.BlockSpec((None,) + tuple(w_out.shape[1:]), lambda i: (layer, 0, 0),
                               pipeline_mode=pl.Buffered(1)),
                  xspec, _layer_spec(ln_g.shape, layer), _layer_spec(ln_b.shape, layer)],
        out_specs=tuple(xspec for _ in dtypes),
        scratch_shapes=[pltpu.VMEM(tuple(w_out.shape[1:]), BF16)],
        compiler_params=_cparams(("arbitrary",)),
        name="outproj_ln",
    )(ya, yb, yc, yd, w_out, x2d, ln_g, ln_b)


def kernel(x, mem, positions, w_in, ret_norm_g, mla_q_norm_g, mla_w_uq, mla_kv_norm_g, mla_w_ukv, ml_conv_w, ml_conv_b, ml_w_q, ml_w_k, ml_i_bias, ml_f_bias, ml_skip, ml_norm_g, w_mem_kv, w_out, ln_g, ln_b):
    assert x.shape == (BATCH, SEQ, D_MODEL) and mem.shape == (BATCH, MEM_LEN, D_MODEL)
    tables = _rope_tables(positions)
    kvm = _mem_kv(mem, w_mem_kv)
    w_in_p = _prep_w_in(w_in)
    wuq_p, wukv_p = _prep_mla_weights(mla_w_uq, mla_w_ukv)
    ret_consts = _ret_consts()
    ret_g = ret_norm_g.reshape(DEPTH, HEADS, 1, HEAD_DIM)
    mlstm_params = _prep_mlstm(ml_conv_w, ml_conv_b, ml_w_q, ml_w_k, ml_i_bias, ml_f_bias, ml_skip, ml_norm_g)
    qg = mla_q_norm_g.reshape(DEPTH, 1, MLA_Q_RANK)
    kvg = mla_kv_norm_g.reshape(DEPTH, 1, MLA_KV_RANK)
    lng = ln_g.reshape(DEPTH, 1, D_MODEL)
    lnb = ln_b.reshape(DEPTH, 1, D_MODEL)

    x2d = x.reshape(ROWS, D_MODEL)
    xb = []
    for l in range(DEPTH):
        hb16, hb32 = _inproj(xb[0] if xb else x2d, w_in_p, l)
        ya, yd, q, k, v = _light_mixers(hb16, hb32, tables, ret_consts, ret_g, kvm, qg, wuq_p, kvg, wukv_p, l)
        yb = _mla_attn(q, k, v, hb16)
        yc = _mlstm(hb16, hb32, mlstm_params, l)
        x2d, *xb = _outproj(ya, yb, yc, yd, w_out, x2d, lng, lnb, l, with_bf16=l + 1 < DEPTH)
    return x2d.reshape(BATCH, SEQ, D_MODEL)
```
